```python
import jax, jax.numpy as jnp
from jax import lax
import numpy as np

D_MODEL = 1024
BATCH = 32
SEQ = 256
DEPTH = 2
DEC_BATCH = 8
DEC_SEQ = 1024
PAST_LEN = 512

GRID_W = 64
HEAD_DIM = 64
BRANCH_W = 512
A_HEADS = BRANCH_W // HEAD_DIM
A_KV_HEADS = 2
A_KV_W = A_KV_HEADS * HEAD_DIM
A_WINDOW = 128
A_BLOCK = 128
B_HEADS = BRANCH_W // HEAD_DIM
NB_ROWS = 8
NB_COLS = 16
NB_QCOLS = 16
NB_KCOLS = 32
LRU_WIDTH = BRANCH_W
LRU_BLOCKS = 8
LRU_BW = LRU_WIDTH // LRU_BLOCKS
LRU_C = 8.0
CONV_W = 4
N_BRANCH = 3
Q_BLOCK = 128
ROPE_BASE = 10000.0
EPS = 1e-6
NEG_INF = -1e30
IN_SPLITS = (BRANCH_W, A_KV_W, A_KV_W, BRANCH_W,
             BRANCH_W, BRANCH_W, BRANCH_W, BRANCH_W,
             LRU_WIDTH, LRU_WIDTH,
             D_MODEL, D_MODEL, D_MODEL)
IN_COLS = sum(IN_SPLITS)
IN_OFFSETS = tuple(int(v) for v in np.cumsum(IN_SPLITS)[:-1])

kernel_name = "hybrid_diffusion_prefix_trunk_step"


def rms_norm(x, g):
    x32 = x.astype(jnp.float32)
    y = x32 * lax.rsqrt(jnp.mean(x32 * x32, axis=-1, keepdims=True) + EPS)
    return (y * g.astype(jnp.float32)).astype(x.dtype)


def mixer_inputs(x, cond, norm_g, w_ada, b_ada, w_in):
    mod = jax.nn.silu(cond) @ w_ada + b_ada
    if mod.ndim == 2:
        mod = mod[:, None, :]
    shift, scale, gate = jnp.split(mod, 3, axis=-1)
    h = rms_norm(x, norm_g) * (1 + scale) + shift
    return jnp.split(h @ w_in, IN_OFFSETS, axis=-1), gate


def rope_axis(x, pos):
    m = x.shape[-1] // 2
    freqs = ROPE_BASE ** (-jnp.arange(m, dtype=jnp.float32) / m)
    ang = pos.astype(jnp.float32)[:, None] * freqs[None, :]
    cos = jnp.cos(ang)[None, :, None, :].astype(x.dtype)
    sin = jnp.sin(ang)[None, :, None, :].astype(x.dtype)
    x1, x2 = x[..., :m], x[..., m:]
    return jnp.concatenate([x1 * cos - x2 * sin, x2 * cos + x1 * sin], axis=-1)


def rope_2d(x):
    t = jnp.arange(x.shape[1])
    half = x.shape[-1] // 2
    return jnp.concatenate([rope_axis(x[..., :half], t // GRID_W),
                            rope_axis(x[..., half:], t % GRID_W)], axis=-1)


def context_attention(q, k, v, sink):
    bsz, s_len, hq, dh = q.shape
    hkv = k.shape[2]
    g = hq // hkv
    nb = s_len // Q_BLOCK
    scale = dh ** -0.5
    qb = jnp.moveaxis(q.reshape(bsz, nb, Q_BLOCK, hkv, g, dh), 1, 0)

    def block(qi):
        s = jnp.einsum('bqhgd,bkhd->bhgqk', qi, k).astype(jnp.float32) * scale
        if sink is not None:
            sk = jnp.broadcast_to(sink.astype(jnp.float32).reshape(1, hkv, g, 1, 1), s.shape[:-1] + (1,))
            s = jnp.concatenate([s, sk], axis=-1)
        p = jax.nn.softmax(s, axis=-1)[..., :s_len].astype(v.dtype)
        return jnp.einsum('bhgqk,bkhd->bqhgd', p, v)

    o = lax.map(block, qb)
    return jnp.moveaxis(o, 0, 1).reshape(bsz, s_len, hq * dh)


def window_attention(q, k, v, kc, vc, sink):
    bsz, t_len, hq, dh = q.shape
    hkv = k.shape[2]
    g = hq // hkv
    nb = t_len // A_BLOCK
    nloc = 3 * A_BLOCK
    scale = dh ** -0.5
    qb = jnp.moveaxis(q.reshape(bsz, nb, A_BLOCK, hkv, g, dh), 1, 0)

    def band(x):
        xp = jnp.pad(x.reshape(bsz, nb, A_BLOCK, hkv, dh), ((0, 0), (1, 1), (0, 0), (0, 0), (0, 0)))
        xb = jnp.concatenate([xp[:, :-2], xp[:, 1:-1], xp[:, 2:]], axis=2)
        return jnp.moveaxis(xb, 1, 0)

    kb, vb = band(k), band(v)
    blk = jnp.arange(nb)[:, None, None]
    qpos = blk * A_BLOCK + jnp.arange(A_BLOCK)[None, :, None]
    kpos = (blk - 1) * A_BLOCK + jnp.arange(nloc)[None, None, :]
    valid = (jnp.abs(qpos - kpos) <= A_WINDOW) & (kpos >= 0) & (kpos < t_len)
    sk = sink.astype(jnp.float32).reshape(1, hkv, g, 1, 1)

    def block(args):
        qi, ki, vi, mi = args
        s_loc = jnp.einsum('bqhgd,bkhd->bhgqk', qi, ki).astype(jnp.float32) * scale
        s_loc = jnp.where(mi, s_loc, NEG_INF)
        s_ctx = jnp.einsum('bqhgd,bphd->bhgqp', qi, kc).astype(jnp.float32) * scale
        s_snk = jnp.broadcast_to(sk, s_loc.shape[:-1] + (1,))
        p = jax.nn.softmax(jnp.concatenate([s_loc, s_ctx, s_snk], axis=-1), axis=-1).astype(v.dtype)
        return (jnp.einsum('bhgqk,bkhd->bqhgd', p[..., :nloc], vi)
                + jnp.einsum('bhgqp,bphd->bqhgd', p[..., nloc:-1], vc))

    o = lax.map(block, (qb, kb, vb, valid))
    return jnp.moveaxis(o, 0, 1).reshape(bsz, t_len, hq * dh)


def neighbourhood_attention(q, k, v, kc, vc, rpb):
    bsz, t_len, h, dh = q.shape
    rows = t_len // GRID_W
    kh = min(NB_ROWS, rows)
    ncb = GRID_W // NB_QCOLS
    nkeys = kh * NB_KCOLS
    scale = dh ** -0.5
    row_start = jnp.clip(jnp.arange(rows) - kh // 2, 0, rows - kh)
    kcol0 = jnp.clip(jnp.arange(ncb) * NB_QCOLS - (NB_KCOLS - NB_QCOLS) // 2, 0, GRID_W - NB_KCOLS)
    kcol = kcol0[:, None] + jnp.arange(NB_KCOLS)[None, :]
    qcol = jnp.arange(ncb)[:, None] * NB_QCOLS + jnp.arange(NB_QCOLS)[None, :]
    cs = jnp.clip(qcol - NB_COLS // 2, 0, GRID_W - NB_COLS)
    col_ok = (kcol[:, None, :] >= cs[:, :, None]) & (kcol[:, None, :] < cs[:, :, None] + NB_COLS)
    mask = jnp.broadcast_to(col_ok[:, :, None, :], (ncb, NB_QCOLS, kh, NB_KCOLS)).reshape(ncb, NB_QCOLS, nkeys)
    dc_idx = jnp.clip(kcol[:, None, :] - qcol[:, :, None] + NB_COLS - 1, 0, 2 * NB_COLS - 2)
    qr = jnp.moveaxis(q.reshape(bsz, rows, ncb, NB_QCOLS, h, dh), 1, 0)

    def block(args):
        qi, r = args
        krow = row_start[r] + jnp.arange(kh)
        idx = (krow[None, :, None] * GRID_W + kcol[:, None, :]).reshape(-1)
        ki = jnp.take(k, idx, axis=1).reshape(bsz, ncb, nkeys, h, dh)
        vi = jnp.take(v, idx, axis=1).reshape(bsz, ncb, nkeys, h, dh)
        dr_idx = krow - r + NB_ROWS - 1
        bias = rpb[:, dr_idx[None, None, :, None], dc_idx[:, :, None, :]]
        bias = jnp.moveaxis(bias.reshape(h, ncb, NB_QCOLS, nkeys), 0, 1)
        s_loc = jnp.einsum('bjqhd,bjkhd->bjhqk', qi, ki).astype(jnp.float32) * scale + bias.astype(jnp.float32)
        s_loc = jnp.where(mask[:, None], s_loc, NEG_INF)
        s_ctx = jnp.einsum('bjqhd,bphd->bjhqp', qi, kc).astype(jnp.float32) * scale
        p = jax.nn.softmax(jnp.concatenate([s_loc, s_ctx], axis=-1), axis=-1).astype(v.dtype)
        o = (jnp.einsum('bjhqk,bjkhd->bjqhd', p[..., :nkeys], vi)
             + jnp.einsum('bjhqp,bphd->bjqhd', p[..., nkeys:], vc))
        return o.reshape(bsz, GRID_W, h * dh)

    o = lax.map(block, (qr, jnp.arange(rows)))
    return jnp.moveaxis(o, 0, 1).reshape(bsz, t_len, h * dh)


def depthwise_conv(x, w, b):
    t_len = x.shape[1]
    lo = CONV_W // 2
    xp = jnp.pad(x, ((0, 0), (lo, CONV_W - 1 - lo), (0, 0)))
    y = b
    for j in range(CONV_W):
        y = y + xp[:, j:j + t_len] * w[j]
    return y


def rglru_coeffs(xc, wa, ba, wx, bx, lam):
    bsz, t_len, _ = xc.shape
    xb = xc.reshape(bsz, t_len, LRU_BLOCKS, LRU_BW)
    r = jax.nn.sigmoid((jnp.einsum('btnk,dnkj->dbtnj', xb, wa).reshape(2, bsz, t_len, LRU_WIDTH)
                        + ba[:, None, None, :]).astype(jnp.float32))
    i = jax.nn.sigmoid((jnp.einsum('btnk,dnkj->dbtnj', xb, wx).reshape(2, bsz, t_len, LRU_WIDTH)
                        + bx[:, None, None, :]).astype(jnp.float32))
    log_a = -LRU_C * jax.nn.softplus(-lam.astype(jnp.float32))[:, None, None, :] * r
    a = jnp.exp(log_a)
    u = jnp.sqrt(-jnp.expm1(2.0 * log_a)) * i * xc.astype(jnp.float32)[None]
    return a, u


def _combine(left, right):
    a1, b1 = left
    a2, b2 = right
    return a1 * a2, a2 * b1 + b2


def linear_scan(a, u, h0, reverse):
    acum, ucum = lax.associative_scan(_combine, (a, u), axis=1, reverse=reverse)
    return acum * h0[:, None, :] + ucum


def merge_branches(ya, yb, yc, ga, gb, gc, w_br, w_out):
    z = (jax.nn.sigmoid(ga) * (ya @ w_br[0]) + jax.nn.sigmoid(gb) * (yb @ w_br[1])
         + jax.nn.sigmoid(gc) * (yc @ w_br[2]))
    return z @ w_out


def context_layer(x, c_ctx, norm_g, w_ada, b_ada, w_in, a_qn, a_kn, a_sink, b_qn, b_kn,
                  conv_w, conv_b, wa, ba, wx, bx, lam, w_br, w_out):
    (aq, ak, av, ag, bq, bk, bv, bg, cx, cg, ga, gb, gc), gate = mixer_inputs(x, c_ctx, norm_g, w_ada, b_ada, w_in)
    bsz, s_len, _ = x.shape
    qa = rms_norm(aq.reshape(bsz, s_len, A_HEADS, HEAD_DIM), a_qn)
    ka = rms_norm(ak.reshape(bsz, s_len, A_KV_HEADS, HEAD_DIM), a_kn)
    va = av.reshape(bsz, s_len, A_KV_HEADS, HEAD_DIM)
    ya = context_attention(qa, ka, va, a_sink) * jax.nn.silu(ag)
    qb = rms_norm(bq.reshape(bsz, s_len, B_HEADS, HEAD_DIM), b_qn)
    kb = rms_norm(bk.reshape(bsz, s_len, B_HEADS, HEAD_DIM), b_kn)
    vb = bv.reshape(bsz, s_len, B_HEADS, HEAD_DIM)
    yb = context_attention(qb, kb, vb, None) * jax.nn.silu(bg)
    xc = depthwise_conv(cx, conv_w, conv_b)
    a, u = rglru_coeffs(xc, wa, ba, wx, bx, lam)
    h0 = jnp.zeros((bsz, LRU_WIDTH), jnp.float32)
    hf = linear_scan(a[0], u[0], h0, False)
    hb = linear_scan(a[1], u[1], h0, True)
    yc = (hf + hb).astype(x.dtype) * jax.nn.silu(cg)
    lru_state = jnp.stack([hf[:, -1], hb[:, 0]], axis=1)
    out = merge_branches(ya, yb, yc, ga, gb, gc, w_br, w_out)
    return x + gate * out, ka, va, kb, vb, lru_state


def latent_layer(x, c, ka_c, va_c, kb_c, vb_c, st, norm_g, w_ada, b_ada, w_in, a_qn, a_kn, a_sink,
                 b_qn, b_kn, b_rpb, conv_w, conv_b, wa, ba, wx, bx, lam, w_br, w_out):
    (aq, ak, av, ag, bq, bk, bv, bg, cx, cg, ga, gb, gc), gate = mixer_inputs(x, c, norm_g, w_ada, b_ada, w_in)
    bsz, t_len, _ = x.shape
    qa = rope_2d(rms_norm(aq.reshape(bsz, t_len, A_HEADS, HEAD_DIM), a_qn))
    ka = rope_2d(rms_norm(ak.reshape(bsz, t_len, A_KV_HEADS, HEAD_DIM), a_kn))
    va = av.reshape(bsz, t_len, A_KV_HEADS, HEAD_DIM)
    ya = window_attention(qa, ka, va, ka_c, va_c, a_sink) * jax.nn.silu(ag)
    qb = rms_norm(bq.reshape(bsz, t_len, B_HEADS, HEAD_DIM), b_qn)
    kb = rms_norm(bk.reshape(bsz, t_len, B_HEADS, HEAD_DIM), b_kn)
    vb = bv.reshape(bsz, t_len, B_HEADS, HEAD_DIM)
    yb = neighbourhood_attention(qb, kb, vb, kb_c, vb_c, b_rpb) * jax.nn.silu(bg)
    xc = depthwise_conv(cx, conv_w, conv_b)
    a, u = rglru_coeffs(xc, wa, ba, wx, bx, lam)
    st32 = st.astype(jnp.float32)
    hf = linear_scan(a[0], u[0], st32[:, 0], False)
    hb = linear_scan(a[1], u[1], st32[:, 1], True)
    yc = (hf + hb).astype(x.dtype) * jax.nn.silu(cg)
    out = merge_branches(ya, yb, yc, ga, gb, gc, w_br, w_out)
    return x + gate * out


def setup_inputs(seed: int = 0) -> dict:
    key = jax.random.key(seed)
    ks = jax.random.split(key, 32)
    f32 = jnp.float32
    nrm = lambda k, shape, s: jax.random.normal(k, shape, f32) * s
    lam_u = jax.random.uniform(ks[24], (DEPTH, 2, LRU_WIDTH), f32, minval=0.9, maxval=0.999)
    return {
        "x_prompt": nrm(ks[0], (BATCH, SEQ, D_MODEL), 1.0),
        "x_sample": nrm(ks[1], (DEC_BATCH, DEC_SEQ, D_MODEL), 1.0),
        "cache_ka": nrm(ks[2], (DEC_BATCH, DEPTH, PAST_LEN, A_KV_HEADS, HEAD_DIM), 1.0),
        "cache_va": nrm(ks[3], (DEC_BATCH, DEPTH, PAST_LEN, A_KV_HEADS, HEAD_DIM), 1.0),
        "cache_kb": nrm(ks[4], (DEC_BATCH, DEPTH, PAST_LEN, B_HEADS, HEAD_DIM), 1.0),
        "cache_vb": nrm(ks[5], (DEC_BATCH, DEPTH, PAST_LEN, B_HEADS, HEAD_DIM), 1.0),
        "state_lru": nrm(ks[6], (DEC_BATCH, DEPTH, 2, LRU_WIDTH), 0.5),
        "c": nrm(ks[7], (DEC_BATCH, D_MODEL), 1.0),
        "c_ctx": nrm(ks[8], (D_MODEL,), 1.0),
        "norm_g": 1.0 + nrm(ks[9], (DEPTH, D_MODEL), 0.02),
        "w_ada": nrm(ks[10], (DEPTH, D_MODEL, 3 * D_MODEL), D_MODEL ** -0.5),
        "b_ada": nrm(ks[11], (DEPTH, 3 * D_MODEL), 0.02),
        "w_in": nrm(ks[12], (DEPTH, D_MODEL, IN_COLS), D_MODEL ** -0.5),
        "a_q_norm": 1.0 + nrm(ks[13], (DEPTH, HEAD_DIM), 0.02),
        "a_k_norm": 1.0 + nrm(ks[14], (DEPTH, HEAD_DIM), 0.02),
        "a_sink": nrm(ks[15], (DEPTH, A_HEADS), 0.5),
        "b_q_norm": 1.0 + nrm(ks[16], (DEPTH, HEAD_DIM), 0.02),
        "b_k_norm": 1.0 + nrm(ks[17], (DEPTH, HEAD_DIM), 0.02),
        "b_rpb": nrm(ks[18], (DEPTH, B_HEADS, 2 * NB_ROWS - 1, 2 * NB_COLS - 1), 0.1),
        "lru_conv_w": nrm(ks[19], (DEPTH, CONV_W, LRU_WIDTH), CONV_W ** -0.5),
        "lru_conv_b": nrm(ks[20], (DEPTH, LRU_WIDTH), 0.02),
        "lru_wa": nrm(ks[21], (DEPTH, 2, LRU_BLOCKS, LRU_BW, LRU_BW), LRU_BW ** -0.5),
        "lru_ba": nrm(ks[22], (DEPTH, 2, LRU_WIDTH), 0.02),
        "lru_wx": nrm(ks[23], (DEPTH, 2, LRU_BLOCKS, LRU_BW, LRU_BW), LRU_BW ** -0.5),
        "lru_bx": nrm(ks[25], (DEPTH, 2, LRU_WIDTH), 0.02),
        "lru_lambda": jnp.log(lam_u) - jnp.log1p(-lam_u),
        "w_branch": nrm(ks[26], (DEPTH, N_BRANCH, BRANCH_W, D_MODEL), BRANCH_W ** -0.5),
        "w_out": nrm(ks[27], (DEPTH, D_MODEL, D_MODEL), D_MODEL ** -0.5),
    }


def reference(x_prompt, x_sample, cache_ka, cache_va, cache_kb, cache_vb, state_lru, c, c_ctx,
              norm_g, w_ada, b_ada, w_in, a_q_norm, a_k_norm, a_sink, b_q_norm, b_k_norm, b_rpb,
              lru_conv_w, lru_conv_b, lru_wa, lru_ba, lru_wx, lru_bx, lru_lambda, w_branch, w_out):
    y_prompt = x_prompt
    y_sample = x_sample
    new_ka, new_va, new_kb, new_vb, new_lru = [], [], [], [], []
    for l in range(DEPTH):
        y_prompt, ka, va, kb, vb, st = context_layer(
            y_prompt, c_ctx, norm_g[l], w_ada[l], b_ada[l], w_in[l], a_q_norm[l], a_k_norm[l], a_sink[l],
            b_q_norm[l], b_k_norm[l], lru_conv_w[l], lru_conv_b[l], lru_wa[l], lru_ba[l], lru_wx[l],
            lru_bx[l], lru_lambda[l], w_branch[l], w_out[l])
        new_ka.append(ka)
        new_va.append(va)
        new_kb.append(kb)
        new_vb.append(vb)
        new_lru.append(st)
        y_sample = latent_layer(
            y_sample, c, cache_ka[:, l], cache_va[:, l], cache_kb[:, l], cache_vb[:, l], state_lru[:, l],
            norm_g[l], w_ada[l], b_ada[l], w_in[l], a_q_norm[l], a_k_norm[l], a_sink[l],
            b_q_norm[l], b_k_norm[l], b_rpb[l], lru_conv_w[l], lru_conv_b[l], lru_wa[l], lru_ba[l],
            lru_wx[l], lru_bx[l], lru_lambda[l], w_branch[l], w_out[l])
    new_cache_ka = jnp.stack(new_ka, axis=1)
    new_cache_va = jnp.stack(new_va, axis=1)
    new_cache_kb = jnp.stack(new_kb, axis=1)
    new_cache_vb = jnp.stack(new_vb, axis=1)
    new_state_lru = jnp.stack(new_lru, axis=1)
    return (y_prompt, y_sample, new_cache_ka, new_cache_va, new_cache_kb, new_cache_vb, new_state_lru)
```

```python
import functools

import numpy as np
import jax
import jax.numpy as jnp
from jax import lax
from jax.experimental import pallas as pl
from jax.experimental.pallas import tpu as pltpu

F32 = jnp.float32
BF16 = jnp.bfloat16

D_MODEL = 1024
DEPTH = 2
GRID_W = 64
HEAD_DIM = 64
BRANCH_W = 512
A_HEADS = 8
A_KV_HEADS = 2
A_WINDOW = 128
A_BLOCK = 128
B_HEADS = 8
NB_ROWS = 8
NB_COLS = 16
LRU_WIDTH = 512
LRU_BLOCKS = 8
LRU_BW = LRU_WIDTH // LRU_BLOCKS
LRU_C = 8.0
CONV_W = 4
ROPE_BASE = 10000.0
EPS = 1e-6
NEG_INF = -1e30
QK_SCALE = HEAD_DIM ** -0.5

LANE = 128
SUBLANE = 8
VMEM_LIMIT = 56 * 1024 * 1024

_ORIG_SPLITS = (512, 128, 128, 512, 512, 512, 512, 512, 512, 512, 1024, 1024, 1024)
_ORIG_OFFS = tuple(int(v) for v in np.cumsum((0,) + _ORIG_SPLITS)[:-1])
_PERM = (0, 3, 4, 5, 6, 7, 8, 9, 10, 11, 12, 1, 2)
IN_COLS = sum(_ORIG_SPLITS)
C_AQ, C_AG, C_BQ, C_BK, C_BV, C_BG, C_CX, C_CG, C_GA, C_GB, C_GC, C_AK, C_AV = (
    0, 4, 8, 12, 16, 20, 24, 28, 32, 40, 48, 56, 57)

_NT = (((1,), (1,)), ((), ()))


def _cparams(sem):
    return pltpu.CompilerParams(dimension_semantics=sem, vmem_limit_bytes=VMEM_LIMIT)


def _silu(x):
    return x * jax.nn.sigmoid(x)


def _head_rms(x, g):
    ms = jnp.mean(x * x, axis=-1, keepdims=True)
    return x * lax.rsqrt(ms + EPS) * g


def _mods_kernel(c_ref, w_ref, b_ref, o_ref):
    c = c_ref[...]
    s = _silu(c).astype(BF16)
    o_ref[0] = jnp.dot(s, w_ref[0].astype(BF16), preferred_element_type=F32) + b_ref[0]


def _mods(cond, w_ada, b_ada):
    n = cond.shape[0]
    tn = D_MODEL
    return pl.pallas_call(
        _mods_kernel,
        grid=(DEPTH, 3 * D_MODEL // tn),
        in_specs=[
            pl.BlockSpec((n, D_MODEL), lambda l, j: (0, 0)),
            pl.BlockSpec((1, D_MODEL, tn), lambda l, j: (l, 0, j)),
            pl.BlockSpec((1, 1, tn), lambda l, j: (l, 0, j)),
        ],
        out_specs=pl.BlockSpec((1, n, tn), lambda l, j: (l, 0, j)),
        out_shape=jax.ShapeDtypeStruct((DEPTH, n, 3 * D_MODEL), F32),
        compiler_params=_cparams(("arbitrary", "arbitrary")),
        name="mods",
    )(cond, w_ada, b_ada.reshape(DEPTH, 1, 3 * D_MODEL))


def _inproj_kernel(x_ref, mod_ref, g_ref, w_ref, o_ref):
    x = x_ref[...]
    y = x * lax.rsqrt(jnp.mean(x * x, axis=-1, keepdims=True) + EPS)
    y = y * g_ref[...]
    shift = mod_ref[0, 0:1, :]
    scale = mod_ref[0, 1:2, :]
    h = (y * (1.0 + scale) + shift).astype(BF16)
    o_ref[...] = jnp.dot(h, w_ref[...], preferred_element_type=F32)


def _inproj(x2d, mod3, norm_g, w_perm, rows_per_mod, mod_row0):
    tokens = x2d.shape[0]
    tm = 512
    tn = IN_COLS // 2
    tiles_per_mod = rows_per_mod // tm if rows_per_mod else 0

    def mod_idx(j, i):
        if rows_per_mod:
            return (mod_row0 + i // tiles_per_mod, 0, 0)
        return (mod_row0, 0, 0)

    return pl.pallas_call(
        _inproj_kernel,
        grid=(IN_COLS // tn, tokens // tm),
        in_specs=[
            pl.BlockSpec((tm, D_MODEL), lambda j, i: (i, 0)),
            pl.BlockSpec((1, 3, D_MODEL), mod_idx),
            pl.BlockSpec((1, D_MODEL), lambda j, i: (0, 0)),
            pl.BlockSpec((D_MODEL, tn), lambda j, i: (0, j)),
        ],
        out_specs=pl.BlockSpec((tm, tn), lambda j, i: (i, j)),
        out_shape=jax.ShapeDtypeStruct((tokens, IN_COLS), F32),
        compiler_params=_cparams(("arbitrary", "arbitrary")),
        name="inproj",
    )(x2d, mod3, norm_g.reshape(1, D_MODEL), w_perm)


def _ctx_attn_kernel(q_ref, k_ref, v_ref, g_ref, qn_ref, kn_ref, sink_ref,
                     y_ref, ko_ref, vo_ref, *, n_q, n_kv, use_sink):
    grp = n_q // n_kv
    s_len = q_ref.shape[0]
    q = q_ref[...]
    k = k_ref[...]
    v = v_ref[...]
    vo_ref[...] = v
    qn = qn_ref[...]
    kn = kn_ref[...]
    k_heads = []
    outs = [None] * n_q
    for kv in range(n_kv):
        kh = _head_rms(k[:, kv * HEAD_DIM:(kv + 1) * HEAD_DIM], kn)
        k_heads.append(kh)
        khb = kh.astype(BF16)
        vhb = v[:, kv * HEAD_DIM:(kv + 1) * HEAD_DIM].astype(BF16)
        qs, sinks = [], []
        for gi in range(grp):
            h = kv * grp + gi
            qh = _head_rms(q[:, h * HEAD_DIM:(h + 1) * HEAD_DIM], qn) * QK_SCALE
            qs.append(qh.astype(BF16))
            if use_sink:
                sinks.append(jnp.broadcast_to(sink_ref[0:1, h:h + 1], (s_len, 1)))
        qst = jnp.concatenate(qs, axis=0) if grp > 1 else qs[0]
        s = lax.dot_general(qst, khb, _NT, preferred_element_type=F32)
        m = jnp.max(s, axis=-1, keepdims=True)
        if use_sink:
            snk = jnp.concatenate(sinks, axis=0) if grp > 1 else sinks[0]
            m = jnp.maximum(m, snk)
        p = jnp.exp(s - m)
        l = jnp.sum(p, axis=-1, keepdims=True)
        if use_sink:
            l = l + jnp.exp(snk - m)
        o = jnp.dot(p.astype(BF16), vhb, preferred_element_type=F32) / l
        for gi in range(grp):
            outs[kv * grp + gi] = o[gi * s_len:(gi + 1) * s_len]
    ko_ref[...] = jnp.concatenate(k_heads, axis=-1)
    y = jnp.concatenate(outs, axis=-1) * _silu(g_ref[...])
    y_ref[...] = y.astype(y_ref.dtype)


def _ctx_attn(proj, bsz, s_len, qn, kn, sink, *, n_q, n_kv, use_sink, cq, ck, cv, cg):
    kvw = n_kv * HEAD_DIM
    kvb = kvw // LANE
    kern = functools.partial(_ctx_attn_kernel, n_q=n_q, n_kv=n_kv, use_sink=use_sink)
    return pl.pallas_call(
        kern,
        grid=(bsz,),
        in_specs=[
            pl.BlockSpec((s_len, BRANCH_W), lambda b: (b, cq // 4)),
            pl.BlockSpec((s_len, kvw), lambda b: (b, ck // kvb)),
            pl.BlockSpec((s_len, kvw), lambda b: (b, cv // kvb)),
            pl.BlockSpec((s_len, BRANCH_W), lambda b: (b, cg // 4)),
            pl.BlockSpec((1, HEAD_DIM), lambda b: (0, 0)),
            pl.BlockSpec((1, HEAD_DIM), lambda b: (0, 0)),
            pl.BlockSpec((1, n_q), lambda b: (0, 0)),
        ],
        out_specs=[
            pl.BlockSpec((s_len, BRANCH_W), lambda b: (b, 0)),
            pl.BlockSpec((s_len, kvw), lambda b: (b, 0)),
            pl.BlockSpec((s_len, kvw), lambda b: (b, 0)),
        ],
        out_shape=[
            jax.ShapeDtypeStruct((bsz * s_len, BRANCH_W), BF16),
            jax.ShapeDtypeStruct((bsz * s_len, kvw), F32),
            jax.ShapeDtypeStruct((bsz * s_len, kvw), F32),
        ],
        compiler_params=_cparams(("arbitrary",)),
        name="ctx_attn",
    )(proj, proj, proj, proj, qn.reshape(1, HEAD_DIM), kn.reshape(1, HEAD_DIM), sink.reshape(1, n_q))


def _rope(x, cos, sin_signed):
    w = x.shape[-1]
    lane = lax.broadcasted_iota(jnp.int32, x.shape, 1)
    up = pltpu.roll(x, w - 16, axis=1)
    dn = pltpu.roll(x, 16, axis=1)
    partner = jnp.where((lane & 16) == 0, up, dn)
    return x * cos + partner * sin_signed


def _win_attn_kernel(q_ref, k_ref, v_ref, g_ref, kc_ref, vc_ref, cos_ref, sin_ref,
                     qn_ref, kn_ref, sink_ref, y_ref, kpad, vpad):
    j = pl.program_id(1)
    t_len = k_ref.shape[0]
    grp = A_HEADS // A_KV_HEADS
    kvw = A_KV_HEADS * HEAD_DIM

    @pl.when(j == 0)
    def _():
        k = k_ref[...]
        kn = kn_ref[...]
        kh = jnp.concatenate(
            [_head_rms(k[:, i * HEAD_DIM:(i + 1) * HEAD_DIM], kn) for i in range(A_KV_HEADS)], axis=-1)
        kh = _rope(kh, cos_ref[:, 0:kvw], sin_ref[:, 0:kvw])
        zeros = jnp.zeros((A_BLOCK, kvw), BF16)
        kpad[0:A_BLOCK, :] = zeros
        kpad[A_BLOCK + t_len:2 * A_BLOCK + t_len, :] = zeros
        vpad[0:A_BLOCK, :] = zeros
        vpad[A_BLOCK + t_len:2 * A_BLOCK + t_len, :] = zeros
        kpad[A_BLOCK:A_BLOCK + t_len, :] = kh.astype(BF16)
        vpad[A_BLOCK:A_BLOCK + t_len, :] = v_ref[...].astype(BF16)

    r0 = pl.multiple_of(j * A_BLOCK, A_BLOCK)
    q = q_ref[...]
    qn = qn_ref[...]
    qh = jnp.concatenate(
        [_head_rms(q[:, h * HEAD_DIM:(h + 1) * HEAD_DIM], qn) for h in range(A_HEADS)], axis=-1)
    qh = _rope(qh, cos_ref[pl.ds(r0, A_BLOCK), :], sin_ref[pl.ds(r0, A_BLOCK), :]) * QK_SCALE
    qb = qh.astype(BF16)

    nloc = 3 * A_BLOCK
    rows = grp * A_BLOCK
    ri = lax.broadcasted_iota(jnp.int32, (rows, nloc), 0) & (A_BLOCK - 1)
    ci = lax.broadcasted_iota(jnp.int32, (rows, nloc), 1)
    kpos = ci + (j - 1) * A_BLOCK
    valid = (jnp.abs(ri + A_BLOCK - ci) <= A_WINDOW) & (kpos >= 0) & (kpos < t_len)

    kband = kpad[pl.ds(r0, nloc), :]
    vband = vpad[pl.ds(r0, nloc), :]
    kc = kc_ref[0, 0].astype(BF16)
    vc = vc_ref[0, 0].astype(BF16)
    outs = [None] * A_HEADS
    for kv in range(A_KV_HEADS):
        sl = slice(kv * HEAD_DIM, (kv + 1) * HEAD_DIM)
        qst = jnp.concatenate([qb[:, (kv * grp + gi) * HEAD_DIM:(kv * grp + gi + 1) * HEAD_DIM]
                               for gi in range(grp)], axis=0)
        snk = jnp.concatenate([jnp.broadcast_to(sink_ref[0:1, kv * grp + gi:kv * grp + gi + 1], (A_BLOCK, 1))
                               for gi in range(grp)], axis=0)
        s_loc = lax.dot_general(qst, kband[:, sl], _NT, preferred_element_type=F32)
        s_loc = jnp.where(valid, s_loc, NEG_INF)
        s_ctx = lax.dot_general(qst, kc[:, sl], _NT, preferred_element_type=F32)
        m = jnp.maximum(jnp.maximum(jnp.max(s_loc, axis=-1, keepdims=True),
                                    jnp.max(s_ctx, axis=-1, keepdims=True)), snk)
        p_loc = jnp.exp(s_loc - m)
        p_ctx = jnp.exp(s_ctx - m)
        l = (jnp.sum(p_loc, axis=-1, keepdims=True) + jnp.sum(p_ctx, axis=-1, keepdims=True)
             + jnp.exp(snk - m))
        o = (jnp.dot(p_loc.astype(BF16), vband[:, sl], preferred_element_type=F32)
             + jnp.dot(p_ctx.astype(BF16), vc[:, sl], preferred_element_type=F32)) / l
        for gi in range(grp):
            outs[kv * grp + gi] = o[gi * A_BLOCK:(gi + 1) * A_BLOCK]
    y = jnp.concatenate(outs, axis=-1) * _silu(g_ref[...])
    y_ref[...] = y.astype(y_ref.dtype)


def _win_attn(proj, cache_k, cache_v, layer, cos, sin_signed, qn, kn, sink, bsz, t_len):
    nb = t_len // A_BLOCK
    kvw = A_KV_HEADS * HEAD_DIM
    p_len = cache_k.shape[2]
    return pl.pallas_call(
        _win_attn_kernel,
        grid=(bsz, nb),
        in_specs=[
            pl.BlockSpec((A_BLOCK, BRANCH_W), lambda b, j: (b * nb + j, C_AQ // 4)),
            pl.BlockSpec((t_len, kvw), lambda b, j: (b, C_AK)),
            pl.BlockSpec((t_len, kvw), lambda b, j: (b, C_AV)),
            pl.BlockSpec((A_BLOCK, BRANCH_W), lambda b, j: (b * nb + j, C_AG // 4)),
            pl.BlockSpec((1, 1, p_len, kvw), lambda b, j: (b, layer, 0, 0)),
            pl.BlockSpec((1, 1, p_len, kvw), lambda b, j: (b, layer, 0, 0)),
            pl.BlockSpec((t_len, BRANCH_W), lambda b, j: (0, 0)),
            pl.BlockSpec((t_len, BRANCH_W), lambda b, j: (0, 0)),
            pl.BlockSpec((1, HEAD_DIM), lambda b, j: (0, 0)),
            pl.BlockSpec((1, HEAD_DIM), lambda b, j: (0, 0)),
            pl.BlockSpec((1, A_HEADS), lambda b, j: (0, 0)),
        ],
        out_specs=pl.BlockSpec((A_BLOCK, BRANCH_W), lambda b, j: (b * nb + j, 0)),
        out_shape=jax.ShapeDtypeStruct((bsz * t_len, BRANCH_W), BF16),
        scratch_shapes=[pltpu.VMEM((t_len + 2 * A_BLOCK, kvw), BF16),
                        pltpu.VMEM((t_len + 2 * A_BLOCK, kvw), BF16)],
        compiler_params=_cparams(("arbitrary", "arbitrary")),
        name="win_attn",
    )(proj, proj, proj, proj, cache_k, cache_v, cos, sin_signed,
      qn.reshape(1, HEAD_DIM), kn.reshape(1, HEAD_DIM), sink.reshape(1, A_HEADS))


NB_QCHUNK = 256


def _nbr_attn_kernel(q_ref, k_ref, v_ref, g_ref, kc_ref, vc_ref, bias_ref, qn_ref, kn_ref, y_ref):
    t_len = q_ref.shape[0]
    heads = q_ref.shape[1] // HEAD_DIM
    q = q_ref[...]
    k = k_ref[...]
    v = v_ref[...]
    qn = qn_ref[...]
    kn = kn_ref[...]
    kc = kc_ref[0, 0]
    vc = vc_ref[0, 0]
    head_out = []
    for h in range(heads):
        sl = slice(h * HEAD_DIM, (h + 1) * HEAD_DIM)
        qh = (_head_rms(q[:, sl], qn) * QK_SCALE).astype(BF16)
        kh = _head_rms(k[:, sl], kn).astype(BF16)
        vh = v[:, sl].astype(BF16)
        kch = kc[:, sl].astype(BF16)
        vch = vc[:, sl].astype(BF16)
        chunks = []
        for c in range(t_len // NB_QCHUNK):
            rs = slice(c * NB_QCHUNK, (c + 1) * NB_QCHUNK)
            s_loc = lax.dot_general(qh[rs], kh, _NT, preferred_element_type=F32) + bias_ref[h, rs, :]
            s_ctx = lax.dot_general(qh[rs], kch, _NT, preferred_element_type=F32)
            m = jnp.maximum(jnp.max(s_loc, axis=-1, keepdims=True), jnp.max(s_ctx, axis=-1, keepdims=True))
            p_loc = jnp.exp(s_loc - m)
            p_ctx = jnp.exp(s_ctx - m)
            l = jnp.sum(p_loc, axis=-1, keepdims=True) + jnp.sum(p_ctx, axis=-1, keepdims=True)
            o = (jnp.dot(p_loc.astype(BF16), vh, preferred_element_type=F32)
                 + jnp.dot(p_ctx.astype(BF16), vch, preferred_element_type=F32)) / l
            chunks.append(o)
        head_out.append(jnp.concatenate(chunks, axis=0))
    y = jnp.concatenate(head_out, axis=-1) * _silu(g_ref[...])
    y_ref[...] = y.astype(y_ref.dtype)


def _nbr_bias(rpb, t_len):
    rows = t_len // GRID_W
    kh = min(NB_ROWS, rows)
    heads = rpb.shape[0]
    r = jnp.arange(rows)
    c = jnp.arange(GRID_W)
    rs = jnp.clip(r - kh // 2, 0, rows - kh)
    cs = jnp.clip(c - NB_COLS // 2, 0, GRID_W - NB_COLS)
    row_ok = (r[None, :] >= rs[:, None]) & (r[None, :] < rs[:, None] + kh)
    col_ok = (c[None, :] >= cs[:, None]) & (c[None, :] < cs[:, None] + NB_COLS)
    dr = jnp.clip(r[None, :] - r[:, None] + NB_ROWS - 1, 0, 2 * NB_ROWS - 2)
    dc = jnp.clip(c[None, :] - c[:, None] + NB_COLS - 1, 0, 2 * NB_COLS - 2)
    by_col = jnp.where(col_ok[None, None], rpb[:, :, dc], NEG_INF)
    full = jnp.take(by_col, dr.reshape(-1), axis=1).reshape(heads, rows, rows, GRID_W, GRID_W)
    full = jnp.where(row_ok[None, :, :, None, None], full, NEG_INF)
    return full.transpose(0, 1, 3, 2, 4).reshape(heads, t_len, t_len).astype(F32)


def _nbr_attn(proj, cache_k, cache_v, layer, bias, qn, kn, bsz, t_len):
    hp = 2
    nhp = B_HEADS // hp
    p_len = cache_k.shape[2]
    return pl.pallas_call(
        _nbr_attn_kernel,
        grid=(nhp, bsz),
        in_specs=[
            pl.BlockSpec((t_len, LANE), lambda h, b: (b, C_BQ + h)),
            pl.BlockSpec((t_len, LANE), lambda h, b: (b, C_BK + h)),
            pl.BlockSpec((t_len, LANE), lambda h, b: (b, C_BV + h)),
            pl.BlockSpec((t_len, LANE), lambda h, b: (b, C_BG + h)),
            pl.BlockSpec((1, 1, p_len, LANE), lambda h, b: (b, layer, 0, h)),
            pl.BlockSpec((1, 1, p_len, LANE), lambda h, b: (b, layer, 0, h)),
            pl.BlockSpec((hp, t_len, t_len), lambda h, b: (h, 0, 0)),
            pl.BlockSpec((1, HEAD_DIM), lambda h, b: (0, 0)),
            pl.BlockSpec((1, HEAD_DIM), lambda h, b: (0, 0)),
        ],
        out_specs=pl.BlockSpec((t_len, LANE), lambda h, b: (b, h)),
        out_shape=jax.ShapeDtypeStruct((bsz * t_len, BRANCH_W), BF16),
        compiler_params=_cparams(("arbitrary", "arbitrary")),
        name="nbr_attn",
    )(proj, proj, proj, proj, cache_k, cache_v, bias, qn.reshape(1, HEAD_DIM), kn.reshape(1, HEAD_DIM))


def _lru_kernel(cx_ref, cg_ref, h0_ref, cw_ref, cb_ref, wg_ref, bg_ref, lam_ref,
                y_ref, st_ref, a_f, u_f, a_b, u_b):
    t_len = cx_ref.shape[0]
    w = LRU_WIDTH
    cx = cx_ref[...]
    row = lax.broadcasted_iota(jnp.int32, (t_len, w), 0)
    xc = cb_ref[...] + cx * cw_ref[2:3, :]
    xc = xc + jnp.where(row >= 2, pltpu.roll(cx, 2, axis=0), 0.0) * cw_ref[0:1, :]
    xc = xc + jnp.where(row >= 1, pltpu.roll(cx, 1, axis=0), 0.0) * cw_ref[1:2, :]
    xc = xc + jnp.where(row < t_len - 1, pltpu.roll(cx, t_len - 1, axis=0), 0.0) * cw_ref[3:4, :]

    gates = jnp.dot(xc.astype(BF16), wg_ref[...], preferred_element_type=F32) + bg_ref[...]
    for d, (a_s, u_s) in enumerate(((a_f, u_f), (a_b, u_b))):
        r = jax.nn.sigmoid(gates[:, d * w:(d + 1) * w])
        i = jax.nn.sigmoid(gates[:, (2 + d) * w:(3 + d) * w])
        nl = -lam_ref[d:d + 1, :]
        softplus = jnp.maximum(nl, 0.0) + jnp.log1p(jnp.exp(-jnp.abs(nl)))
        log_a = (-LRU_C * softplus) * r
        a_s[...] = jnp.exp(log_a)
        th = jnp.tanh(log_a)
        u_s[...] = jnp.sqrt(-2.0 * th / (1.0 - th)) * i * xc

    n_grp = t_len // SUBLANE

    def body(it, carry):
        hf, hb = carry
        r0 = pl.multiple_of(it * SUBLANE, SUBLANE)
        av = a_f[pl.ds(r0, SUBLANE), :]
        uv = u_f[pl.ds(r0, SUBLANE), :]
        rows_f = []
        for s in range(SUBLANE):
            hf = av[s:s + 1, :] * hf + uv[s:s + 1, :]
            rows_f.append(hf)
        u_f[pl.ds(r0, SUBLANE), :] = jnp.concatenate(rows_f, axis=0)
        r1 = pl.multiple_of((n_grp - 1 - it) * SUBLANE, SUBLANE)
        av = a_b[pl.ds(r1, SUBLANE), :]
        uv = u_b[pl.ds(r1, SUBLANE), :]
        rows_b = [None] * SUBLANE
        for s in range(SUBLANE - 1, -1, -1):
            hb = av[s:s + 1, :] * hb + uv[s:s + 1, :]
            rows_b[s] = hb
        u_b[pl.ds(r1, SUBLANE), :] = jnp.concatenate(rows_b, axis=0)
        return hf, hb

    h0 = h0_ref[0]
    hf, hb = lax.fori_loop(0, n_grp, body, (h0[0:1, :], h0[1:2, :]))
    st_ref[0] = jnp.concatenate([hf, hb], axis=0)
    y = (u_f[...] + u_b[...]) * _silu(cg_ref[...])
    y_ref[...] = y.astype(y_ref.dtype)


def _lru(proj, h0, conv_w, conv_b, w_gates, b_gates, lam, bsz, t_len):
    w = LRU_WIDTH
    return pl.pallas_call(
        _lru_kernel,
        grid=(bsz,),
        in_specs=[
            pl.BlockSpec((t_len, w), lambda b: (b, C_CX // 4)),
            pl.BlockSpec((t_len, w), lambda b: (b, C_CG // 4)),
            pl.BlockSpec((1, 2, w), lambda b: (b, 0, 0)),
            pl.BlockSpec((CONV_W, w), lambda b: (0, 0)),
            pl.BlockSpec((1, w), lambda b: (0, 0)),
            pl.BlockSpec((w, 4 * w), lambda b: (0, 0)),
            pl.BlockSpec((1, 4 * w), lambda b: (0, 0)),
            pl.BlockSpec((2, w), lambda b: (0, 0)),
        ],
        out_specs=[
            pl.BlockSpec((t_len, w), lambda b: (b, 0)),
            pl.BlockSpec((1, 2, w), lambda b: (b, 0, 0)),
        ],
        out_shape=[
            jax.ShapeDtypeStruct((bsz * t_len, w), BF16),
            jax.ShapeDtypeStruct((bsz, 2, w), F32),
        ],
        scratch_shapes=[pltpu.VMEM((t_len, w), F32) for _ in range(4)],
        compiler_params=_cparams(("arbitrary",)),
        name="lru",
    )(proj, proj, h0, conv_w, conv_b.reshape(1, w), w_gates, b_gates, lam)


def _lru_gate_weights(wa, ba, wx, bx):
    def dense(wblk):
        eye = jnp.eye(LRU_BLOCKS, dtype=wblk.dtype)
        return jnp.einsum('nkj,nm->nkmj', wblk, eye).reshape(LRU_WIDTH, LRU_WIDTH)
    wg = jnp.concatenate([dense(wa[0]), dense(wa[1]), dense(wx[0]), dense(wx[1])], axis=1).astype(BF16)
    bg = jnp.concatenate([ba[0], ba[1], bx[0], bx[1]]).reshape(1, 4 * LRU_WIDTH)
    return wg, bg


def _merge_kernel(x_ref, ya_ref, yb_ref, yc_ref, ga_ref, gb_ref, gc_ref, mod_ref, wbr_ref, wout_ref, o_ref):
    z = (jax.nn.sigmoid(ga_ref[...]) * jnp.dot(ya_ref[...], wbr_ref[0], preferred_element_type=F32)
         + jax.nn.sigmoid(gb_ref[...]) * jnp.dot(yb_ref[...], wbr_ref[1], preferred_element_type=F32)
         + jax.nn.sigmoid(gc_ref[...]) * jnp.dot(yc_ref[...], wbr_ref[2], preferred_element_type=F32))
    out = jnp.dot(z.astype(BF16), wout_ref[...], preferred_element_type=F32)
    o_ref[...] = x_ref[...] + mod_ref[0, 2:3, :] * out


def _merge(x2d, ya, yb, yc, proj, mod3, w_br, w_out, rows_per_mod, mod_row0):
    tokens = x2d.shape[0]
    tm = 512
    tiles_per_mod = rows_per_mod // tm if rows_per_mod else 0

    def mod_idx(i):
        if rows_per_mod:
            return (mod_row0 + i // tiles_per_mod, 0, 0)
        return (mod_row0, 0, 0)

    return pl.pallas_call(
        _merge_kernel,
        grid=(tokens // tm,),
        in_specs=[
            pl.BlockSpec((tm, D_MODEL), lambda i: (i, 0)),
            pl.BlockSpec((tm, BRANCH_W), lambda i: (i, 0)),
            pl.BlockSpec((tm, BRANCH_W), lambda i: (i, 0)),
            pl.BlockSpec((tm, BRANCH_W), lambda i: (i, 0)),
            pl.BlockSpec((tm, D_MODEL), lambda i: (i, C_GA // 8)),
            pl.BlockSpec((tm, D_MODEL), lambda i: (i, C_GB // 8)),
            pl.BlockSpec((tm, D_MODEL), lambda i: (i, C_GC // 8)),
            pl.BlockSpec((1, 3, D_MODEL), mod_idx),
            pl.BlockSpec((3, BRANCH_W, D_MODEL), lambda i: (0, 0, 0)),
            pl.BlockSpec((D_MODEL, D_MODEL), lambda i: (0, 0)),
        ],
        out_specs=pl.BlockSpec((tm, D_MODEL), lambda i: (i, 0)),
        out_shape=jax.ShapeDtypeStruct((tokens, D_MODEL), F32),
        compiler_params=_cparams(("arbitrary",)),
        name="merge",
    )(x2d, ya, yb, yc, proj, proj, proj, mod3, w_br, w_out)


def _rope_tables(t_len):
    t = jnp.arange(t_len)
    m = HEAD_DIM // 4
    freqs = ROPE_BASE ** (-jnp.arange(m, dtype=F32) / m)
    ang_r = (t // GRID_W).astype(F32)[:, None] * freqs[None, :]
    ang_c = (t % GRID_W).astype(F32)[:, None] * freqs[None, :]
    cos = jnp.concatenate([jnp.cos(ang_r), jnp.cos(ang_r), jnp.cos(ang_c), jnp.cos(ang_c)], axis=-1)
    sin = jnp.concatenate([-jnp.sin(ang_r), jnp.sin(ang_r), -jnp.sin(ang_c), jnp.sin(ang_c)], axis=-1)
    reps = BRANCH_W // HEAD_DIM
    return jnp.tile(cos, (1, reps)), jnp.tile(sin, (1, reps))


def kernel(x_prompt, x_sample, cache_ka, cache_va, cache_kb, cache_vb, state_lru, c, c_ctx,
           norm_g, w_ada, b_ada, w_in, a_q_norm, a_k_norm, a_sink, b_q_norm, b_k_norm, b_rpb,
           lru_conv_w, lru_conv_b, lru_wa, lru_ba, lru_wx, lru_bx, lru_lambda, w_branch, w_out):
    bsz, s_len, _ = x_prompt.shape
    dbsz, t_len, _ = x_sample.shape
    p_len = cache_ka.shape[2]

    n_mod = 16
    ctx_row = dbsz
    cond = jnp.zeros((n_mod, D_MODEL), F32).at[:dbsz].set(c).at[ctx_row].set(c_ctx)
    mods = _mods(cond, w_ada, b_ada).reshape(DEPTH, n_mod, 3, D_MODEL)

    w_perm = jnp.concatenate(
        [w_in[:, :, _ORIG_OFFS[k]:_ORIG_OFFS[k] + _ORIG_SPLITS[k]] for k in _PERM], axis=-1).astype(BF16)
    w_br = w_branch.astype(BF16)
    w_o = w_out.astype(BF16)
    cos, sin_signed = _rope_tables(t_len)

    cka = cache_ka.reshape(dbsz, DEPTH, p_len, A_KV_HEADS * HEAD_DIM)
    cva = cache_va.reshape(dbsz, DEPTH, p_len, A_KV_HEADS * HEAD_DIM)
    ckb = cache_kb.reshape(dbsz, DEPTH, p_len, B_HEADS * HEAD_DIM)
    cvb = cache_vb.reshape(dbsz, DEPTH, p_len, B_HEADS * HEAD_DIM)

    yp = x_prompt.reshape(bsz * s_len, D_MODEL)
    ys = x_sample.reshape(dbsz * t_len, D_MODEL)
    zero_state = jnp.zeros((bsz, 2, LRU_WIDTH), F32)
    new_ka, new_va, new_kb, new_vb, new_lru = [], [], [], [], []
    for l in range(DEPTH):
        wg, bg = _lru_gate_weights(lru_wa[l], lru_ba[l], lru_wx[l], lru_bx[l])
        bias = _nbr_bias(b_rpb[l], t_len)

        proj = _inproj(yp, mods[l], norm_g[l], w_perm[l], 0, ctx_row)
        ya, ka, va = _ctx_attn(proj, bsz, s_len, a_q_norm[l], a_k_norm[l], a_sink[l],
                               n_q=A_HEADS, n_kv=A_KV_HEADS, use_sink=True,
                               cq=C_AQ, ck=C_AK, cv=C_AV, cg=C_AG)
        yb, kb, vb = _ctx_attn(proj, bsz, s_len, b_q_norm[l], b_k_norm[l], jnp.zeros((B_HEADS,), F32),
                               n_q=B_HEADS, n_kv=B_HEADS, use_sink=False,
                               cq=C_BQ, ck=C_BK, cv=C_BV, cg=C_BG)
        yc, st = _lru(proj, zero_state, lru_conv_w[l], lru_conv_b[l], wg, bg, lru_lambda[l], bsz, s_len)
        yp = _merge(yp, ya, yb, yc, proj, mods[l], w_br[l], w_o[l], 0, ctx_row)
        new_ka.append(ka.reshape(bsz, s_len, A_KV_HEADS, HEAD_DIM))
        new_va.append(va.reshape(bsz, s_len, A_KV_HEADS, HEAD_DIM))
        new_kb.append(kb.reshape(bsz, s_len, B_HEADS, HEAD_DIM))
        new_vb.append(vb.reshape(bsz, s_len, B_HEADS, HEAD_DIM))
        new_lru.append(st)

        proj = _inproj(ys, mods[l], norm_g[l], w_perm[l], t_len, 0)
        ya = _win_attn(proj, cka, cva, l, cos, sin_signed, a_q_norm[l], a_k_norm[l], a_sink[l], dbsz, t_len)
        yb = _nbr_attn(proj, ckb, cvb, l, bias, b_q_norm[l], b_k_norm[l], dbsz, t_len)
        yc, _ = _lru(proj, state_lru[:, l], lru_conv_w[l], lru_conv_b[l], wg, bg, lru_lambda[l], dbsz, t_len)
        ys = _merge(ys, ya, yb, yc, proj, mods[l], w_br[l], w_o[l], t_len, 0)

    return (yp.reshape(bsz, s_len, D_MODEL), ys.reshape(dbsz, t_len, D_MODEL),
            jnp.stack(new_ka, axis=1), jnp.stack(new_va, axis=1),
            jnp.stack(new_kb, axis=1), jnp.stack(new_vb, axis=1),
            jnp.stack(new_lru, axis=1))
```

```python
import functools

import numpy as np
import jax
import jax.numpy as jnp
from jax import lax
from jax.experimental import pallas as pl
from jax.experimental.pallas import tpu as pltpu

F32 = jnp.float32
BF16 = jnp.bfloat16

D_MODEL = 1024
DEPTH = 2
GRID_W = 64
HEAD_DIM = 64
BRANCH_W = 512
A_HEADS = 8
A_KV_HEADS = 2
A_WINDOW = 128
A_BLOCK = 128
B_HEADS = 8
NB_ROWS = 8
NB_COLS = 16
LRU_WIDTH = 512
LRU_BLOCKS = 8
LRU_BW = LRU_WIDTH // LRU_BLOCKS
LRU_C = 8.0
CONV_W = 4
ROPE_BASE = 10000.0
EPS = 1e-6
NEG_INF = -1e30
QK_SCALE = HEAD_DIM ** -0.5

LANE = 128
SUBLANE = 8
MXU_DIM = 256
VMEM_LIMIT = 56 * 1024 * 1024

_ORIG_SPLITS = (512, 128, 128, 512, 512, 512, 512, 512, 512, 512, 1024, 1024, 1024)
_ORIG_OFFS = tuple(int(v) for v in np.cumsum((0,) + _ORIG_SPLITS)[:-1])
_PERM = (0, 3, 4, 5, 6, 7, 8, 9, 10, 11, 12, 1, 2)
IN_COLS = sum(_ORIG_SPLITS)
C_AQ, C_AG, C_BQ, C_BK, C_BV, C_BG, C_CX, C_CG, C_GA, C_GB, C_GC, C_AK, C_AV = (
    0, 4, 8, 12, 16, 20, 24, 28, 32, 40, 48, 56, 57)

_NT = (((1,), (1,)), ((), ()))


def _cparams(sem):
    return pltpu.CompilerParams(dimension_semantics=sem, vmem_limit_bytes=VMEM_LIMIT)


def _silu(x):
    return x * jax.nn.sigmoid(x)


def _head_rms(x, g):
    ms = jnp.mean(x * x, axis=-1, keepdims=True)
    return x * lax.rsqrt(ms + EPS) * g


def _head_mean_matrix(width):
    idx = np.arange(width) // HEAD_DIM
    return jnp.asarray((idx[:, None] == idx[None, :]).astype(np.float32) / HEAD_DIM, dtype=BF16)


def _heads_rms(x, bd, g):
    x2 = x * x
    hi = x2.astype(BF16)
    lo = (x2 - hi.astype(F32)).astype(BF16)
    ms = jnp.dot(hi, bd, preferred_element_type=F32) + jnp.dot(lo, bd, preferred_element_type=F32)
    return x * lax.rsqrt(ms + EPS) * g


def _head_lane_mask(width, h):
    lane = lax.broadcasted_iota(jnp.int32, (1, width), 1)
    return (lane >= h * HEAD_DIM) & (lane < (h + 1) * HEAD_DIM)


def _mods_kernel(c_ref, w_ref, b_ref, o_ref):
    c = c_ref[...]
    s = _silu(c).astype(BF16)
    o_ref[0] = jnp.dot(s, w_ref[0].astype(BF16), preferred_element_type=F32) + b_ref[0]


def _mods(cond, w_ada, b_ada):
    n = cond.shape[0]
    tn = D_MODEL
    return pl.pallas_call(
        _mods_kernel,
        grid=(DEPTH, 3 * D_MODEL // tn),
        in_specs=[
            pl.BlockSpec((n, D_MODEL), lambda l, j: (0, 0)),
            pl.BlockSpec((1, D_MODEL, tn), lambda l, j: (l, 0, j)),
            pl.BlockSpec((1, 1, tn), lambda l, j: (l, 0, j)),
        ],
        out_specs=pl.BlockSpec((1, n, tn), lambda l, j: (l, 0, j)),
        out_shape=jax.ShapeDtypeStruct((DEPTH, n, 3 * D_MODEL), F32),
        compiler_params=_cparams(("arbitrary", "arbitrary")),
        name="mods",
    )(cond, w_ada, b_ada.reshape(DEPTH, 1, 3 * D_MODEL))


def _inproj_kernel(x_ref, mod_ref, g_ref, w_ref, o_ref):
    x = x_ref[...]
    y = x * lax.rsqrt(jnp.mean(x * x, axis=-1, keepdims=True) + EPS)
    y = y * g_ref[...]
    shift = mod_ref[0, 0:1, :]
    scale = mod_ref[0, 1:2, :]
    h = (y * (1.0 + scale) + shift).astype(BF16)
    o_ref[...] = jnp.dot(h, w_ref[...], preferred_element_type=F32)


def _inproj(x2d, mod3, norm_g, w_perm, rows_per_mod, mod_row0):
    tokens = x2d.shape[0]
    tm = 512
    tn = IN_COLS // 2
    tiles_per_mod = rows_per_mod // tm if rows_per_mod else 0

    def mod_idx(j, i):
        if rows_per_mod:
            return (mod_row0 + i // tiles_per_mod, 0, 0)
        return (mod_row0, 0, 0)

    return pl.pallas_call(
        _inproj_kernel,
        grid=(IN_COLS // tn, tokens // tm),
        in_specs=[
            pl.BlockSpec((tm, D_MODEL), lambda j, i: (i, 0)),
            pl.BlockSpec((1, 3, D_MODEL), mod_idx),
            pl.BlockSpec((1, D_MODEL), lambda j, i: (0, 0)),
            pl.BlockSpec((D_MODEL, tn), lambda j, i: (0, j)),
        ],
        out_specs=pl.BlockSpec((tm, tn), lambda j, i: (i, j)),
        out_shape=jax.ShapeDtypeStruct((tokens, IN_COLS), F32),
        compiler_params=_cparams(("arbitrary", "arbitrary")),
        name="inproj",
    )(x2d, mod3, norm_g.reshape(1, D_MODEL), w_perm)


def _ctx_attn_kernel(q_ref, k_ref, v_ref, g_ref, gq_ref, gk_ref, sink_ref, bdq_ref, bdk_ref, exp_ref,
                     y_ref, ko_ref, vo_ref, *, n_q, n_kv, use_sink):
    s_len = q_ref.shape[0]
    v = v_ref[...]
    vo_ref[...] = v
    kn = _heads_rms(k_ref[...], bdk_ref[...], gk_ref[...])
    ko_ref[...] = kn
    qn = (_heads_rms(q_ref[...], bdq_ref[...], gq_ref[...]) * QK_SCALE).astype(BF16)
    knb = kn.astype(BF16)
    vb = v.astype(BF16)
    if n_kv < n_q:
        knb = jnp.dot(knb, exp_ref[...], preferred_element_type=F32).astype(BF16)
        vb = jnp.dot(vb, exp_ref[...], preferred_element_type=F32).astype(BF16)
    ones = jnp.ones((s_len, MXU_DIM), BF16)
    heads_per_half = MXU_DIM // HEAD_DIM
    halves = []
    for hh in range(n_q // heads_per_half):
        cols = slice(hh * MXU_DIM, (hh + 1) * MXU_DIM)
        qh, kh, vh = qn[:, cols], knb[:, cols], vb[:, cols]
        acc = jnp.zeros((s_len, MXU_DIM), F32)
        for hi in range(heads_per_half):
            hm = _head_lane_mask(MXU_DIM, hi)
            km = jnp.where(hm, kh, jnp.zeros_like(kh))
            vm = jnp.where(hm, vh, jnp.zeros_like(vh))
            s = lax.dot_general(qh, km, _NT, preferred_element_type=F32)
            m = jnp.max(s, axis=-1, keepdims=True)
            if use_sink:
                h = hh * heads_per_half + hi
                snk = sink_ref[0:1, h:h + 1]
                m = jnp.maximum(m, snk)
            p = jnp.exp(s - m).astype(BF16)
            l = jnp.dot(p, ones, preferred_element_type=F32)
            if use_sink:
                l = l + jnp.exp(snk - m)
            acc = acc + jnp.dot(p, vm, preferred_element_type=F32) / l
        halves.append(acc)
    y = jnp.concatenate(halves, axis=-1) * _silu(g_ref[...])
    y_ref[...] = y.astype(y_ref.dtype)


def _ctx_attn(proj, bsz, s_len, gq, gk, sink, *, n_q, n_kv, use_sink, cq, ck, cv, cg):
    kvw = n_kv * HEAD_DIM
    kvb = kvw // LANE
    grp = n_q // n_kv
    expand = np.zeros((kvw, BRANCH_W), np.float32)
    for h in range(n_q):
        for d in range(HEAD_DIM):
            expand[(h // grp) * HEAD_DIM + d, h * HEAD_DIM + d] = 1.0
    const = lambda b: (0, 0)
    kern = functools.partial(_ctx_attn_kernel, n_q=n_q, n_kv=n_kv, use_sink=use_sink)
    return pl.pallas_call(
        kern,
        grid=(bsz,),
        in_specs=[
            pl.BlockSpec((s_len, BRANCH_W), lambda b: (b, cq // 4)),
            pl.BlockSpec((s_len, kvw), lambda b: (b, ck // kvb)),
            pl.BlockSpec((s_len, kvw), lambda b: (b, cv // kvb)),
            pl.BlockSpec((s_len, BRANCH_W), lambda b: (b, cg // 4)),
            pl.BlockSpec((1, BRANCH_W), const),
            pl.BlockSpec((1, kvw), const),
            pl.BlockSpec((1, n_q), const),
            pl.BlockSpec((BRANCH_W, BRANCH_W), const),
            pl.BlockSpec((kvw, kvw), const),
            pl.BlockSpec((kvw, BRANCH_W), const),
        ],
        out_specs=[
            pl.BlockSpec((s_len, BRANCH_W), lambda b: (b, 0)),
            pl.BlockSpec((s_len, kvw), lambda b: (b, 0)),
            pl.BlockSpec((s_len, kvw), lambda b: (b, 0)),
        ],
        out_shape=[
            jax.ShapeDtypeStruct((bsz * s_len, BRANCH_W), BF16),
            jax.ShapeDtypeStruct((bsz * s_len, kvw), F32),
            jax.ShapeDtypeStruct((bsz * s_len, kvw), F32),
        ],
        compiler_params=_cparams(("arbitrary",)),
        name="ctx_attn",
    )(proj, proj, proj, proj,
      jnp.tile(gq, n_q).reshape(1, BRANCH_W), jnp.tile(gk, n_kv).reshape(1, kvw), sink.reshape(1, n_q),
      _head_mean_matrix(BRANCH_W), _head_mean_matrix(kvw), jnp.asarray(expand, dtype=BF16))


def _rope(x, cos, sin_signed):
    w = x.shape[-1]
    lane = lax.broadcasted_iota(jnp.int32, x.shape, 1)
    up = pltpu.roll(x, w - 16, axis=1)
    dn = pltpu.roll(x, 16, axis=1)
    partner = jnp.where((lane & 16) == 0, up, dn)
    return x * cos + partner * sin_signed


def _win_attn_kernel(q_ref, k_ref, v_ref, g_ref, kc_ref, vc_ref, cos_ref, sin_ref,
                     qn_ref, kn_ref, sink_ref, y_ref, kpad, vpad):
    j = pl.program_id(1)
    t_len = k_ref.shape[0]
    grp = A_HEADS // A_KV_HEADS
    kvw = A_KV_HEADS * HEAD_DIM

    @pl.when(j == 0)
    def _():
        k = k_ref[...]
        kn = kn_ref[...]
        kh = jnp.concatenate(
            [_head_rms(k[:, i * HEAD_DIM:(i + 1) * HEAD_DIM], kn) for i in range(A_KV_HEADS)], axis=-1)
        kh = _rope(kh, cos_ref[:, 0:kvw], sin_ref[:, 0:kvw])
        zeros = jnp.zeros((A_BLOCK, kvw), BF16)
        kpad[0:A_BLOCK, :] = zeros
        kpad[A_BLOCK + t_len:2 * A_BLOCK + t_len, :] = zeros
        vpad[0:A_BLOCK, :] = zeros
        vpad[A_BLOCK + t_len:2 * A_BLOCK + t_len, :] = zeros
        kpad[A_BLOCK:A_BLOCK + t_len, :] = kh.astype(BF16)
        vpad[A_BLOCK:A_BLOCK + t_len, :] = v_ref[...].astype(BF16)

    r0 = pl.multiple_of(j * A_BLOCK, A_BLOCK)
    q = q_ref[...]
    qn = qn_ref[...]
    qh = jnp.concatenate(
        [_head_rms(q[:, h * HEAD_DIM:(h + 1) * HEAD_DIM], qn) for h in range(A_HEADS)], axis=-1)
    qh = _rope(qh, cos_ref[pl.ds(r0, A_BLOCK), :], sin_ref[pl.ds(r0, A_BLOCK), :]) * QK_SCALE
    qb = qh.astype(BF16)

    nloc = 3 * A_BLOCK
    rows = grp * A_BLOCK
    ri = lax.broadcasted_iota(jnp.int32, (rows, nloc), 0) & (A_BLOCK - 1)
    ci = lax.broadcasted_iota(jnp.int32, (rows, nloc), 1)
    kpos = ci + (j - 1) * A_BLOCK
    valid = (jnp.abs(ri + A_BLOCK - ci) <= A_WINDOW) & (kpos >= 0) & (kpos < t_len)

    kband = kpad[pl.ds(r0, nloc), :]
    vband = vpad[pl.ds(r0, nloc), :]
    kc = kc_ref[0, 0].astype(BF16)
    vc = vc_ref[0, 0].astype(BF16)
    outs = [None] * A_HEADS
    for kv in range(A_KV_HEADS):
        sl = slice(kv * HEAD_DIM, (kv + 1) * HEAD_DIM)
        qst = jnp.concatenate([qb[:, (kv * grp + gi) * HEAD_DIM:(kv * grp + gi + 1) * HEAD_DIM]
                               for gi in range(grp)], axis=0)
        snk = jnp.concatenate([jnp.broadcast_to(sink_ref[0:1, kv * grp + gi:kv * grp + gi + 1], (A_BLOCK, 1))
                               for gi in range(grp)], axis=0)
        s_loc = lax.dot_general(qst, kband[:, sl], _NT, preferred_element_type=F32)
        s_loc = jnp.where(valid, s_loc, NEG_INF)
        s_ctx = lax.dot_general(qst, kc[:, sl], _NT, preferred_element_type=F32)
        m = jnp.maximum(jnp.maximum(jnp.max(s_loc, axis=-1, keepdims=True),
                                    jnp.max(s_ctx, axis=-1, keepdims=True)), snk)
        p_loc = jnp.exp(s_loc - m)
        p_ctx = jnp.exp(s_ctx - m)
        l = (jnp.sum(p_loc, axis=-1, keepdims=True) + jnp.sum(p_ctx, axis=-1, keepdims=True)
             + jnp.exp(snk - m))
        o = (jnp.dot(p_loc.astype(BF16), vband[:, sl], preferred_element_type=F32)
             + jnp.dot(p_ctx.astype(BF16), vc[:, sl], preferred_element_type=F32)) / l
        for gi in range(grp):
            outs[kv * grp + gi] = o[gi * A_BLOCK:(gi + 1) * A_BLOCK]
    y = jnp.concatenate(outs, axis=-1) * _silu(g_ref[...])
    y_ref[...] = y.astype(y_ref.dtype)


def _win_attn(proj, cache_k, cache_v, layer, cos, sin_signed, qn, kn, sink, bsz, t_len):
    nb = t_len // A_BLOCK
    kvw = A_KV_HEADS * HEAD_DIM
    p_len = cache_k.shape[2]
    return pl.pallas_call(
        _win_attn_kernel,
        grid=(bsz, nb),
        in_specs=[
            pl.BlockSpec((A_BLOCK, BRANCH_W), lambda b, j: (b * nb + j, C_AQ // 4)),
            pl.BlockSpec((t_len, kvw), lambda b, j: (b, C_AK)),
            pl.BlockSpec((t_len, kvw), lambda b, j: (b, C_AV)),
            pl.BlockSpec((A_BLOCK, BRANCH_W), lambda b, j: (b * nb + j, C_AG // 4)),
            pl.BlockSpec((1, 1, p_len, kvw), lambda b, j: (b, layer, 0, 0)),
            pl.BlockSpec((1, 1, p_len, kvw), lambda b, j: (b, layer, 0, 0)),
            pl.BlockSpec((t_len, BRANCH_W), lambda b, j: (0, 0)),
            pl.BlockSpec((t_len, BRANCH_W), lambda b, j: (0, 0)),
            pl.BlockSpec((1, HEAD_DIM), lambda b, j: (0, 0)),
            pl.BlockSpec((1, HEAD_DIM), lambda b, j: (0, 0)),
            pl.BlockSpec((1, A_HEADS), lambda b, j: (0, 0)),
        ],
        out_specs=pl.BlockSpec((A_BLOCK, BRANCH_W), lambda b, j: (b * nb + j, 0)),
        out_shape=jax.ShapeDtypeStruct((bsz * t_len, BRANCH_W), BF16),
        scratch_shapes=[pltpu.VMEM((t_len + 2 * A_BLOCK, kvw), BF16),
                        pltpu.VMEM((t_len + 2 * A_BLOCK, kvw), BF16)],
        compiler_params=_cparams(("arbitrary", "arbitrary")),
        name="win_attn",
    )(proj, proj, proj, proj, cache_k, cache_v, cos, sin_signed,
      qn.reshape(1, HEAD_DIM), kn.reshape(1, HEAD_DIM), sink.reshape(1, A_HEADS))


NBR_QB = 2 * GRID_W
NBR_BAND = 5


def _nbr_plan(rows):
    kh = min(NB_ROWS, rows)
    nblk = rows // 2
    specs, plan, starts = {}, [], []
    for i in range(nblk):
        s0 = min(max(i - 2, 0), nblk - NBR_BAND)
        starts.append(s0)
        blk = []
        for a in range(2):
            qr = 2 * i + a
            rs = min(max(qr - kh // 2, 0), rows - kh)
            assert 2 * s0 <= rs and rs + kh <= 2 * (s0 + NBR_BAND)
            row = []
            for p in range(NBR_BAND):
                pair = tuple(kr - qr + NB_ROWS - 1 if rs <= kr < rs + kh else None
                             for kr in (2 * (s0 + p), 2 * (s0 + p) + 1))
                row.append(specs.setdefault(pair, len(specs)))
            blk.append(row)
        plan.append(blk)
    return tuple(starts), plan, list(specs)


def _nbr_table(rpb, specs):
    heads = rpb.shape[0]
    c = np.arange(GRID_W)
    cs = np.clip(c - NB_COLS // 2, 0, GRID_W - NB_COLS)
    col_ok = (c[None, :] >= cs[:, None]) & (c[None, :] < cs[:, None] + NB_COLS)
    dc = np.clip(c[None, :] - c[:, None] + NB_COLS - 1, 0, 2 * NB_COLS - 2)
    by_col = jnp.where(col_ok[None, None], rpb[:, :, dc], NEG_INF).astype(F32)
    neg = jnp.full((heads, GRID_W, GRID_W), NEG_INF, F32)
    blocks = [jnp.concatenate([neg if d is None else by_col[:, d] for d in spec], axis=-1) for spec in specs]
    return jnp.stack(blocks, axis=1)


def _nbr_attn_kernel(q_ref, k_ref, v_ref, g_ref, kc_ref, vc_ref, tb_ref, gq_ref, gk_ref, bd_ref, y_ref,
                     *, starts, plan):
    nband = NBR_BAND * NBR_QB
    bd = bd_ref[...]
    qn = (_heads_rms(q_ref[...], bd, gq_ref[...]) * QK_SCALE).astype(BF16)
    kn = _heads_rms(k_ref[...], bd, gk_ref[...]).astype(BF16)
    vb = v_ref[...].astype(BF16)
    kcb = kc_ref[0, 0].astype(BF16)
    vcb = vc_ref[0, 0].astype(BF16)
    acc = [jnp.zeros((NBR_QB, LANE), F32) for _ in starts]
    for h in range(LANE // HEAD_DIM):
        hm = _head_lane_mask(LANE, h)
        km = jnp.where(hm, kn, jnp.zeros_like(kn))
        kcm = jnp.where(hm, kcb, jnp.zeros_like(kcb))
        vm = jnp.concatenate([jnp.where(hm, vb, jnp.zeros_like(vb)), jnp.ones_like(vb)], axis=1)
        vcm = jnp.concatenate([jnp.where(hm, vcb, jnp.zeros_like(vcb)), jnp.ones_like(vcb)], axis=1)
        for i, s0 in enumerate(starts):
            qi = qn[i * NBR_QB:(i + 1) * NBR_QB]
            ks = slice(s0 * NBR_QB, s0 * NBR_QB + nband)
            s_raw = lax.dot_general(qi, km[ks], _NT, preferred_element_type=F32)
            s_loc = jnp.concatenate(
                [jnp.concatenate([s_raw[a * GRID_W:(a + 1) * GRID_W, p * LANE:(p + 1) * LANE]
                                  + tb_ref[h, plan[i][a][p]] for p in range(NBR_BAND)], axis=1)
                 for a in range(2)], axis=0)
            s_ctx = lax.dot_general(qi, kcm, _NT, preferred_element_type=F32)
            m = jnp.maximum(jnp.max(s_loc, axis=-1, keepdims=True), jnp.max(s_ctx, axis=-1, keepdims=True))
            p_loc = jnp.exp(s_loc - m).astype(BF16)
            p_ctx = jnp.exp(s_ctx - m).astype(BF16)
            oe = (jnp.dot(p_loc, vm[ks], preferred_element_type=F32)
                  + jnp.dot(p_ctx, vcm, preferred_element_type=F32))
            acc[i] = acc[i] + oe[:, :LANE] / oe[:, LANE:]
    y = jnp.concatenate(acc, axis=0) * _silu(g_ref[...])
    y_ref[...] = y.astype(y_ref.dtype)


def _nbr_attn(proj, cache_k, cache_v, layer, rpb, gq, gk, bsz, t_len):
    hp = LANE // HEAD_DIM
    nhp = B_HEADS // hp
    p_len = cache_k.shape[2]
    rows = t_len // GRID_W
    assert rows % 2 == 0 and rows // 2 >= NBR_BAND
    starts, plan, specs = _nbr_plan(rows)
    table = _nbr_table(rpb, specs)
    const = lambda h, b: (0, 0)
    kern = functools.partial(_nbr_attn_kernel, starts=starts, plan=plan)
    return pl.pallas_call(
        kern,
        grid=(nhp, bsz),
        in_specs=[
            pl.BlockSpec((t_len, LANE), lambda h, b: (b, C_BQ + h)),
            pl.BlockSpec((t_len, LANE), lambda h, b: (b, C_BK + h)),
            pl.BlockSpec((t_len, LANE), lambda h, b: (b, C_BV + h)),
            pl.BlockSpec((t_len, LANE), lambda h, b: (b, C_BG + h)),
            pl.BlockSpec((1, 1, p_len, LANE), lambda h, b: (b, layer, 0, h)),
            pl.BlockSpec((1, 1, p_len, LANE), lambda h, b: (b, layer, 0, h)),
            pl.BlockSpec((hp, len(specs), GRID_W, 2 * GRID_W), lambda h, b: (h, 0, 0, 0)),
            pl.BlockSpec((1, LANE), const),
            pl.BlockSpec((1, LANE), const),
            pl.BlockSpec((LANE, LANE), const),
        ],
        out_specs=pl.BlockSpec((t_len, LANE), lambda h, b: (b, h)),
        out_shape=jax.ShapeDtypeStruct((bsz * t_len, BRANCH_W), BF16),
        compiler_params=_cparams(("arbitrary", "arbitrary")),
        name="nbr_attn",
    )(proj, proj, proj, proj, cache_k, cache_v, table,
      jnp.tile(gq, hp).reshape(1, LANE), jnp.tile(gk, hp).reshape(1, LANE), _head_mean_matrix(LANE))


def _lru_kernel(cx_ref, cg_ref, h0_ref, cw_ref, cb_ref, wg_ref, bg_ref, lam_ref,
                y_ref, st_ref, a_f, u_f, a_b, u_b):
    t_len = cx_ref.shape[0]
    w = LRU_WIDTH
    cx = cx_ref[...]
    row = lax.broadcasted_iota(jnp.int32, (t_len, w), 0)
    xc = cb_ref[...] + cx * cw_ref[2:3, :]
    xc = xc + jnp.where(row >= 2, pltpu.roll(cx, 2, axis=0), 0.0) * cw_ref[0:1, :]
    xc = xc + jnp.where(row >= 1, pltpu.roll(cx, 1, axis=0), 0.0) * cw_ref[1:2, :]
    xc = xc + jnp.where(row < t_len - 1, pltpu.roll(cx, t_len - 1, axis=0), 0.0) * cw_ref[3:4, :]

    gates = jnp.dot(xc.astype(BF16), wg_ref[...], preferred_element_type=F32) + bg_ref[...]
    for d, (a_s, u_s) in enumerate(((a_f, u_f), (a_b, u_b))):
        r = jax.nn.sigmoid(gates[:, d * w:(d + 1) * w])
        i = jax.nn.sigmoid(gates[:, (2 + d) * w:(3 + d) * w])
        nl = -lam_ref[d:d + 1, :]
        softplus = jnp.maximum(nl, 0.0) + jnp.log1p(jnp.exp(-jnp.abs(nl)))
        log_a = (-LRU_C * softplus) * r
        a_s[...] = jnp.exp(log_a)
        th = jnp.tanh(log_a)
        u_s[...] = jnp.sqrt(-2.0 * th / (1.0 - th)) * i * xc

    n_grp = t_len // SUBLANE

    def body(it, carry):
        hf, hb = carry
        r0 = pl.multiple_of(it * SUBLANE, SUBLANE)
        av = a_f[pl.ds(r0, SUBLANE), :]
        uv = u_f[pl.ds(r0, SUBLANE), :]
        rows_f = []
        for s in range(SUBLANE):
            hf = av[s:s + 1, :] * hf + uv[s:s + 1, :]
            rows_f.append(hf)
        u_f[pl.ds(r0, SUBLANE), :] = jnp.concatenate(rows_f, axis=0)
        r1 = pl.multiple_of((n_grp - 1 - it) * SUBLANE, SUBLANE)
        av = a_b[pl.ds(r1, SUBLANE), :]
        uv = u_b[pl.ds(r1, SUBLANE), :]
        rows_b = [None] * SUBLANE
        for s in range(SUBLANE - 1, -1, -1):
            hb = av[s:s + 1, :] * hb + uv[s:s + 1, :]
            rows_b[s] = hb
        u_b[pl.ds(r1, SUBLANE), :] = jnp.concatenate(rows_b, axis=0)
        return hf, hb

    h0 = h0_ref[0]
    hf, hb = lax.fori_loop(0, n_grp, body, (h0[0:1, :], h0[1:2, :]))
    st_ref[0] = jnp.concatenate([hf, hb], axis=0)
    y = (u_f[...] + u_b[...]) * _silu(cg_ref[...])
    y_ref[...] = y.astype(y_ref.dtype)


def _lru(proj, h0, conv_w, conv_b, w_gates, b_gates, lam, bsz, t_len):
    w = LRU_WIDTH
    return pl.pallas_call(
        _lru_kernel,
        grid=(bsz,),
        in_specs=[
            pl.BlockSpec((t_len, w), lambda b: (b, C_CX // 4)),
            pl.BlockSpec((t_len, w), lambda b: (b, C_CG // 4)),
            pl.BlockSpec((1, 2, w), lambda b: (b, 0, 0)),
            pl.BlockSpec((CONV_W, w), lambda b: (0, 0)),
            pl.BlockSpec((1, w), lambda b: (0, 0)),
            pl.BlockSpec((w, 4 * w), lambda b: (0, 0)),
            pl.BlockSpec((1, 4 * w), lambda b: (0, 0)),
            pl.BlockSpec((2, w), lambda b: (0, 0)),
        ],
        out_specs=[
            pl.BlockSpec((t_len, w), lambda b: (b, 0)),
            pl.BlockSpec((1, 2, w), lambda b: (b, 0, 0)),
        ],
        out_shape=[
            jax.ShapeDtypeStruct((bsz * t_len, w), BF16),
            jax.ShapeDtypeStruct((bsz, 2, w), F32),
        ],
        scratch_shapes=[pltpu.VMEM((t_len, w), F32) for _ in range(4)],
        compiler_params=_cparams(("arbitrary",)),
        name="lru",
    )(proj, proj, h0, conv_w, conv_b.reshape(1, w), w_gates, b_gates, lam)


def _lru_gate_weights(wa, ba, wx, bx):
    def dense(wblk):
        eye = jnp.eye(LRU_BLOCKS, dtype=wblk.dtype)
        return jnp.einsum('nkj,nm->nkmj', wblk, eye).reshape(LRU_WIDTH, LRU_WIDTH)
    wg = jnp.concatenate([dense(wa[0]), dense(wa[1]), dense(wx[0]), dense(wx[1])], axis=1).astype(BF16)
    bg = jnp.concatenate([ba[0], ba[1], bx[0], bx[1]]).reshape(1, 4 * LRU_WIDTH)
    return wg, bg


def _merge_kernel(x_ref, ya_ref, yb_ref, yc_ref, ga_ref, gb_ref, gc_ref, mod_ref, wbr_ref, wout_ref, o_ref):
    z = (jax.nn.sigmoid(ga_ref[...]) * jnp.dot(ya_ref[...], wbr_ref[0], preferred_element_type=F32)
         + jax.nn.sigmoid(gb_ref[...]) * jnp.dot(yb_ref[...], wbr_ref[1], preferred_element_type=F32)
         + jax.nn.sigmoid(gc_ref[...]) * jnp.dot(yc_ref[...], wbr_ref[2], preferred_element_type=F32))
    out = jnp.dot(z.astype(BF16), wout_ref[...], preferred_element_type=F32)
    o_ref[...] = x_ref[...] + mod_ref[0, 2:3, :] * out


def _merge(x2d, ya, yb, yc, proj, mod3, w_br, w_out, rows_per_mod, mod_row0):
    tokens = x2d.shape[0]
    tm = 512
    tiles_per_mod = rows_per_mod // tm if rows_per_mod else 0

    def mod_idx(i):
        if rows_per_mod:
            return (mod_row0 + i // tiles_per_mod, 0, 0)
        return (mod_row0, 0, 0)

    return pl.pallas_call(
        _merge_kernel,
        grid=(tokens // tm,),
        in_specs=[
            pl.BlockSpec((tm, D_MODEL), lambda i: (i, 0)),
            pl.BlockSpec((tm, BRANCH_W), lambda i: (i, 0)),
            pl.BlockSpec((tm, BRANCH_W), lambda i: (i, 0)),
            pl.BlockSpec((tm, BRANCH_W), lambda i: (i, 0)),
            pl.BlockSpec((tm, D_MODEL), lambda i: (i, C_GA // 8)),
            pl.BlockSpec((tm, D_MODEL), lambda i: (i, C_GB // 8)),
            pl.BlockSpec((tm, D_MODEL), lambda i: (i, C_GC // 8)),
            pl.BlockSpec((1, 3, D_MODEL), mod_idx),
            pl.BlockSpec((3, BRANCH_W, D_MODEL), lambda i: (0, 0, 0)),
            pl.BlockSpec((D_MODEL, D_MODEL), lambda i: (0, 0)),
        ],
        out_specs=pl.BlockSpec((tm, D_MODEL), lambda i: (i, 0)),
        out_shape=jax.ShapeDtypeStruct((tokens, D_MODEL), F32),
        compiler_params=_cparams(("arbitrary",)),
        name="merge",
    )(x2d, ya, yb, yc, proj, proj, proj, mod3, w_br, w_out)


def _rope_tables(t_len):
    t = jnp.arange(t_len)
    m = HEAD_DIM // 4
    freqs = ROPE_BASE ** (-jnp.arange(m, dtype=F32) / m)
    ang_r = (t // GRID_W).astype(F32)[:, None] * freqs[None, :]
    ang_c = (t % GRID_W).astype(F32)[:, None] * freqs[None, :]
    cos = jnp.concatenate([jnp.cos(ang_r), jnp.cos(ang_r), jnp.cos(ang_c), jnp.cos(ang_c)], axis=-1)
    sin = jnp.concatenate([-jnp.sin(ang_r), jnp.sin(ang_r), -jnp.sin(ang_c), jnp.sin(ang_c)], axis=-1)
    reps = BRANCH_W // HEAD_DIM
    return jnp.tile(cos, (1, reps)), jnp.tile(sin, (1, reps))


def kernel(x_prompt, x_sample, cache_ka, cache_va, cache_kb, cache_vb, state_lru, c, c_ctx,
           norm_g, w_ada, b_ada, w_in, a_q_norm, a_k_norm, a_sink, b_q_norm, b_k_norm, b_rpb,
           lru_conv_w, lru_conv_b, lru_wa, lru_ba, lru_wx, lru_bx, lru_lambda, w_branch, w_out):
    bsz, s_len, _ = x_prompt.shape
    dbsz, t_len, _ = x_sample.shape
    p_len = cache_ka.shape[2]

    n_mod = 16
    ctx_row = dbsz
    cond = jnp.zeros((n_mod, D_MODEL), F32).at[:dbsz].set(c).at[ctx_row].set(c_ctx)
    mods = _mods(cond, w_ada, b_ada).reshape(DEPTH, n_mod, 3, D_MODEL)

    w_perm = jnp.concatenate(
        [w_in[:, :, _ORIG_OFFS[k]:_ORIG_OFFS[k] + _ORIG_SPLITS[k]] for k in _PERM], axis=-1).astype(BF16)
    w_br = w_branch.astype(BF16)
    w_o = w_out.astype(BF16)
    cos, sin_signed = _rope_tables(t_len)

    cka = cache_ka.reshape(dbsz, DEPTH, p_len, A_KV_HEADS * HEAD_DIM)
    cva = cache_va.reshape(dbsz, DEPTH, p_len, A_KV_HEADS * HEAD_DIM)
    ckb = cache_kb.reshape(dbsz, DEPTH, p_len, B_HEADS * HEAD_DIM)
    cvb = cache_vb.reshape(dbsz, DEPTH, p_len, B_HEADS * HEAD_DIM)

    yp = x_prompt.reshape(bsz * s_len, D_MODEL)
    ys = x_sample.reshape(dbsz * t_len, D_MODEL)
    zero_state = jnp.zeros((bsz, 2, LRU_WIDTH), F32)
    new_ka, new_va, new_kb, new_vb, new_lru = [], [], [], [], []
    for l in range(DEPTH):
        wg, bg = _lru_gate_weights(lru_wa[l], lru_ba[l], lru_wx[l], lru_bx[l])

        proj = _inproj(yp, mods[l], norm_g[l], w_perm[l], 0, ctx_row)
        ya, ka, va = _ctx_attn(proj, bsz, s_len, a_q_norm[l], a_k_norm[l], a_sink[l],
                               n_q=A_HEADS, n_kv=A_KV_HEADS, use_sink=True,
                               cq=C_AQ, ck=C_AK, cv=C_AV, cg=C_AG)
        yb, kb, vb = _ctx_attn(proj, bsz, s_len, b_q_norm[l], b_k_norm[l], jnp.zeros((B_HEADS,), F32),
                               n_q=B_HEADS, n_kv=B_HEADS, use_sink=False,
                               cq=C_BQ, ck=C_BK, cv=C_BV, cg=C_BG)
        yc, st = _lru(proj, zero_state, lru_conv_w[l], lru_conv_b[l], wg, bg, lru_lambda[l], bsz, s_len)
        yp = _merge(yp, ya, yb, yc, proj, mods[l], w_br[l], w_o[l], 0, ctx_row)
        new_ka.append(ka.reshape(bsz, s_len, A_KV_HEADS, HEAD_DIM))
        new_va.append(va.reshape(bsz, s_len, A_KV_HEADS, HEAD_DIM))
        new_kb.append(kb.reshape(bsz, s_len, B_HEADS, HEAD_DIM))
        new_vb.append(vb.reshape(bsz, s_len, B_HEADS, HEAD_DIM))
        new_lru.append(st)

        proj = _inproj(ys, mods[l], norm_g[l], w_perm[l], t_len, 0)
        ya = _win_attn(proj, cka, cva, l, cos, sin_signed, a_q_norm[l], a_k_norm[l], a_sink[l], dbsz, t_len)
        yb = _nbr_attn(proj, ckb, cvb, l, b_rpb[l], b_q_norm[l], b_k_norm[l], dbsz, t_len)
        yc, _ = _lru(proj, state_lru[:, l], lru_conv_w[l], lru_conv_b[l], wg, bg, lru_lambda[l], dbsz, t_len)
        ys = _merge(ys, ya, yb, yc, proj, mods[l], w_br[l], w_o[l], t_len, 0)

    return (yp.reshape(bsz, s_len, D_MODEL), ys.reshape(dbsz, t_len, D_MODEL),
            jnp.stack(new_ka, axis=1), jnp.stack(new_va, axis=1),
            jnp.stack(new_kb, axis=1), jnp.stack(new_vb, axis=1),
            jnp.stack(new_lru, axis=1))
```

```python
import functools

import numpy as np
import jax
import jax.numpy as jnp
from jax import lax
from jax.experimental import pallas as pl
from jax.experimental.pallas import tpu as pltpu

F32 = jnp.float32
BF16 = jnp.bfloat16

D_MODEL = 1024
DEPTH = 2
GRID_W = 64
HEAD_DIM = 64
BRANCH_W = 512
A_HEADS = 8
A_KV_HEADS = 2
A_WINDOW = 128
A_BLOCK = 128
B_HEADS = 8
NB_ROWS = 8
NB_COLS = 16
LRU_WIDTH = 512
LRU_BLOCKS = 8
LRU_BW = LRU_WIDTH // LRU_BLOCKS
LRU_C = 8.0
CONV_W = 4
ROPE_BASE = 10000.0
EPS = 1e-6
NEG_INF = -1e30
QK_SCALE = HEAD_DIM ** -0.5

LANE = 128
SUBLANE = 8
MXU_DIM = 256
VMEM_LIMIT = 56 * 1024 * 1024

_ORIG_SPLITS = (512, 128, 128, 512, 512, 512, 512, 512, 512, 512, 1024, 1024, 1024)
_ORIG_OFFS = tuple(int(v) for v in np.cumsum((0,) + _ORIG_SPLITS)[:-1])
_PERM = (0, 3, 4, 5, 6, 7, 8, 9, 10, 11, 12, 1, 2)
IN_COLS = sum(_ORIG_SPLITS)
C_AQ, C_AG, C_BQ, C_BK, C_BV, C_BG, C_CX, C_CG, C_GA, C_GB, C_GC, C_AK, C_AV = (
    0, 4, 8, 12, 16, 20, 24, 28, 32, 40, 48, 56, 57)

_NT = (((1,), (1,)), ((), ()))


def _cparams(sem):
    return pltpu.CompilerParams(dimension_semantics=sem, vmem_limit_bytes=VMEM_LIMIT)


def _sigmoid(x):
    return 0.5 + 0.5 * jnp.tanh(0.5 * x)


def _silu(x):
    return x * _sigmoid(x)


def _head_rms(x, g):
    ms = jnp.mean(x * x, axis=-1, keepdims=True)
    return x * lax.rsqrt(ms + EPS) * g


def _head_mean_matrix(width):
    idx = np.arange(width) // HEAD_DIM
    return jnp.asarray((idx[:, None] == idx[None, :]).astype(np.float32) / HEAD_DIM, dtype=BF16)


def _heads_rms(x, bd, g):
    x2 = x * x
    hi = x2.astype(BF16)
    lo = (x2 - hi.astype(F32)).astype(BF16)
    ms = jnp.dot(hi, bd, preferred_element_type=F32) + jnp.dot(lo, bd, preferred_element_type=F32)
    return x * lax.rsqrt(ms + EPS) * g


def _head_lane_mask(width, h):
    lane = lax.broadcasted_iota(jnp.int32, (1, width), 1)
    return (lane >= h * HEAD_DIM) & (lane < (h + 1) * HEAD_DIM)


def _mods_kernel(c_ref, w_ref, b_ref, o_ref):
    c = c_ref[...]
    s = _silu(c).astype(BF16)
    o_ref[0] = jnp.dot(s, w_ref[0].astype(BF16), preferred_element_type=F32) + b_ref[0]


def _mods(cond, w_ada, b_ada):
    n = cond.shape[0]
    tn = D_MODEL
    return pl.pallas_call(
        _mods_kernel,
        grid=(DEPTH, 3 * D_MODEL // tn),
        in_specs=[
            pl.BlockSpec((n, D_MODEL), lambda l, j: (0, 0)),
            pl.BlockSpec((1, D_MODEL, tn), lambda l, j: (l, 0, j)),
            pl.BlockSpec((1, 1, tn), lambda l, j: (l, 0, j)),
        ],
        out_specs=pl.BlockSpec((1, n, tn), lambda l, j: (l, 0, j)),
        out_shape=jax.ShapeDtypeStruct((DEPTH, n, 3 * D_MODEL), F32),
        compiler_params=_cparams(("arbitrary", "arbitrary")),
        name="mods",
    )(cond, w_ada, b_ada.reshape(DEPTH, 1, 3 * D_MODEL))


def _inproj_kernel(x_ref, mod_ref, g_ref, w_ref, o_ref):
    x = x_ref[...]
    y = x * lax.rsqrt(jnp.mean(x * x, axis=-1, keepdims=True) + EPS)
    y = y * g_ref[...]
    shift = mod_ref[0, 0:1, :]
    scale = mod_ref[0, 1:2, :]
    h = (y * (1.0 + scale) + shift).astype(BF16)
    o_ref[...] = jnp.dot(h, w_ref[...], preferred_element_type=F32)


def _inproj(x2d, mod3, norm_g, w_perm, rows_per_mod, mod_row0):
    tokens = x2d.shape[0]
    tm = 512
    tn = IN_COLS // 2
    tiles_per_mod = rows_per_mod // tm if rows_per_mod else 0

    def mod_idx(j, i):
        if rows_per_mod:
            return (mod_row0 + i // tiles_per_mod, 0, 0)
        return (mod_row0, 0, 0)

    return pl.pallas_call(
        _inproj_kernel,
        grid=(IN_COLS // tn, tokens // tm),
        in_specs=[
            pl.BlockSpec((tm, D_MODEL), lambda j, i: (i, 0)),
            pl.BlockSpec((1, 3, D_MODEL), mod_idx),
            pl.BlockSpec((1, D_MODEL), lambda j, i: (0, 0)),
            pl.BlockSpec((D_MODEL, tn), lambda j, i: (0, j)),
        ],
        out_specs=pl.BlockSpec((tm, tn), lambda j, i: (i, j)),
        out_shape=jax.ShapeDtypeStruct((tokens, IN_COLS), F32),
        compiler_params=_cparams(("arbitrary", "arbitrary")),
        name="inproj",
    )(x2d, mod3, norm_g.reshape(1, D_MODEL), w_perm)


def _ctx_attn_kernel(q_ref, k_ref, v_ref, g_ref, gq_ref, gk_ref, sink_ref, bdq_ref, bdk_ref, exp_ref, ones_ref,
                     y_ref, ko_ref, vo_ref, *, n_q, n_kv, use_sink):
    s_len = q_ref.shape[0]
    v = v_ref[...]
    vo_ref[...] = v
    kn = _heads_rms(k_ref[...], bdk_ref[...], gk_ref[...])
    ko_ref[...] = kn
    qn = (_heads_rms(q_ref[...], bdq_ref[...], gq_ref[...]) * QK_SCALE).astype(BF16)
    knb = kn.astype(BF16)
    vb = v.astype(BF16)
    if n_kv < n_q:
        knb = jnp.dot(knb, exp_ref[...], preferred_element_type=F32).astype(BF16)
        vb = jnp.dot(vb, exp_ref[...], preferred_element_type=F32).astype(BF16)
    heads_per_half = MXU_DIM // HEAD_DIM
    masks = [_head_lane_mask(MXU_DIM, hi) for hi in range(heads_per_half)]
    ones_stack = ones_ref[...]
    halves = []
    for hh in range(n_q // heads_per_half):
        cols = slice(hh * MXU_DIM, (hh + 1) * MXU_DIM)
        qh, kh, vh = qn[:, cols], knb[:, cols], vb[:, cols]
        kstack = jnp.concatenate([jnp.where(mk, kh, jnp.zeros_like(kh)) for mk in masks], axis=0)
        vstack = jnp.concatenate([jnp.where(mk, vh, jnp.zeros_like(vh)) for mk in masks], axis=0)
        s = lax.dot_general(qh, kstack, _NT, preferred_element_type=F32)
        ps, sink_terms = [], []
        for hi in range(heads_per_half):
            si = s[:, hi * s_len:(hi + 1) * s_len]
            m = jnp.max(si, axis=-1, keepdims=True)
            if use_sink:
                h = hh * heads_per_half + hi
                snk = sink_ref[0:1, h:h + 1]
                m = jnp.maximum(m, snk)
                sink_terms.append(jnp.where(masks[hi], jnp.exp(snk - m), 0.0))
            ps.append(jnp.exp(si - m).astype(BF16))
        p = jnp.concatenate(ps, axis=1)
        rhs = jnp.concatenate([vstack, ones_stack], axis=1)
        oe = jnp.dot(p, rhs, preferred_element_type=F32)
        l = oe[:, MXU_DIM:]
        for term in sink_terms:
            l = l + term
        halves.append(oe[:, :MXU_DIM] / l)
    y = jnp.concatenate(halves, axis=-1) * _silu(g_ref[...])
    y_ref[...] = y.astype(y_ref.dtype)


def _ctx_attn(proj, bsz, s_len, gq, gk, sink, *, n_q, n_kv, use_sink, cq, ck, cv, cg):
    kvw = n_kv * HEAD_DIM
    kvb = kvw // LANE
    grp = n_q // n_kv
    expand = np.zeros((kvw, BRANCH_W), np.float32)
    for h in range(n_q):
        for d in range(HEAD_DIM):
            expand[(h // grp) * HEAD_DIM + d, h * HEAD_DIM + d] = 1.0
    heads_per_half = MXU_DIM // HEAD_DIM
    ones_stack = np.zeros((heads_per_half * s_len, MXU_DIM), np.float32)
    for hi in range(heads_per_half):
        ones_stack[hi * s_len:(hi + 1) * s_len, hi * HEAD_DIM:(hi + 1) * HEAD_DIM] = 1.0
    const = lambda b: (0, 0)
    kern = functools.partial(_ctx_attn_kernel, n_q=n_q, n_kv=n_kv, use_sink=use_sink)
    return pl.pallas_call(
        kern,
        grid=(bsz,),
        in_specs=[
            pl.BlockSpec((s_len, BRANCH_W), lambda b: (b, cq // 4)),
            pl.BlockSpec((s_len, kvw), lambda b: (b, ck // kvb)),
            pl.BlockSpec((s_len, kvw), lambda b: (b, cv // kvb)),
            pl.BlockSpec((s_len, BRANCH_W), lambda b: (b, cg // 4)),
            pl.BlockSpec((1, BRANCH_W), const),
            pl.BlockSpec((1, kvw), const),
            pl.BlockSpec((1, n_q), const),
            pl.BlockSpec((BRANCH_W, BRANCH_W), const),
            pl.BlockSpec((kvw, kvw), const),
            pl.BlockSpec((kvw, BRANCH_W), const),
            pl.BlockSpec((heads_per_half * s_len, MXU_DIM), const),
        ],
        out_specs=[
            pl.BlockSpec((s_len, BRANCH_W), lambda b: (b, 0)),
            pl.BlockSpec((s_len, kvw), lambda b: (b, 0)),
            pl.BlockSpec((s_len, kvw), lambda b: (b, 0)),
        ],
        out_shape=[
            jax.ShapeDtypeStruct((bsz * s_len, BRANCH_W), BF16),
            jax.ShapeDtypeStruct((bsz * s_len, kvw), F32),
            jax.ShapeDtypeStruct((bsz * s_len, kvw), F32),
        ],
        compiler_params=_cparams(("arbitrary",)),
        name="ctx_attn",
    )(proj, proj, proj, proj,
      jnp.tile(gq, n_q).reshape(1, BRANCH_W), jnp.tile(gk, n_kv).reshape(1, kvw), sink.reshape(1, n_q),
      _head_mean_matrix(BRANCH_W), _head_mean_matrix(kvw), jnp.asarray(expand, dtype=BF16),
      jnp.asarray(ones_stack, dtype=BF16))


def _rope(x, cos, sin_signed):
    w = x.shape[-1]
    lane = lax.broadcasted_iota(jnp.int32, x.shape, 1)
    up = pltpu.roll(x, w - 16, axis=1)
    dn = pltpu.roll(x, 16, axis=1)
    partner = jnp.where((lane & 16) == 0, up, dn)
    return x * cos + partner * sin_signed


def _win_attn_kernel(q_ref, k_ref, v_ref, g_ref, kc_ref, vc_ref, cos_ref, sin_ref,
                     qn_ref, kn_ref, sink_ref, y_ref, kpad, vpad):
    j = pl.program_id(1)
    t_len = k_ref.shape[0]
    grp = A_HEADS // A_KV_HEADS
    kvw = A_KV_HEADS * HEAD_DIM

    @pl.when(j == 0)
    def _():
        k = k_ref[...]
        kn = kn_ref[...]
        kh = jnp.concatenate(
            [_head_rms(k[:, i * HEAD_DIM:(i + 1) * HEAD_DIM], kn) for i in range(A_KV_HEADS)], axis=-1)
        kh = _rope(kh, cos_ref[:, 0:kvw], sin_ref[:, 0:kvw])
        zeros = jnp.zeros((A_BLOCK, kvw), BF16)
        kpad[0:A_BLOCK, :] = zeros
        kpad[A_BLOCK + t_len:2 * A_BLOCK + t_len, :] = zeros
        vpad[0:A_BLOCK, :] = zeros
        vpad[A_BLOCK + t_len:2 * A_BLOCK + t_len, :] = zeros
        kpad[A_BLOCK:A_BLOCK + t_len, :] = kh.astype(BF16)
        vpad[A_BLOCK:A_BLOCK + t_len, :] = v_ref[...].astype(BF16)

    r0 = pl.multiple_of(j * A_BLOCK, A_BLOCK)
    q = q_ref[...]
    qn = qn_ref[...]
    qh = jnp.concatenate(
        [_head_rms(q[:, h * HEAD_DIM:(h + 1) * HEAD_DIM], qn) for h in range(A_HEADS)], axis=-1)
    qh = _rope(qh, cos_ref[pl.ds(r0, A_BLOCK), :], sin_ref[pl.ds(r0, A_BLOCK), :]) * QK_SCALE
    qb = qh.astype(BF16)

    nloc = 3 * A_BLOCK
    rows = grp * A_BLOCK
    ri = lax.broadcasted_iota(jnp.int32, (rows, nloc), 0) & (A_BLOCK - 1)
    ci = lax.broadcasted_iota(jnp.int32, (rows, nloc), 1)
    kpos = ci + (j - 1) * A_BLOCK
    valid = (jnp.abs(ri + A_BLOCK - ci) <= A_WINDOW) & (kpos >= 0) & (kpos < t_len)

    kband = kpad[pl.ds(r0, nloc), :]
    vband = vpad[pl.ds(r0, nloc), :]
    kc = kc_ref[0, 0].astype(BF16)
    vc = vc_ref[0, 0].astype(BF16)
    outs = [None] * A_HEADS
    for kv in range(A_KV_HEADS):
        sl = slice(kv * HEAD_DIM, (kv + 1) * HEAD_DIM)
        qst = jnp.concatenate([qb[:, (kv * grp + gi) * HEAD_DIM:(kv * grp + gi + 1) * HEAD_DIM]
                               for gi in range(grp)], axis=0)
        snk = jnp.concatenate([jnp.broadcast_to(sink_ref[0:1, kv * grp + gi:kv * grp + gi + 1], (A_BLOCK, 1))
                               for gi in range(grp)], axis=0)
        s_loc = lax.dot_general(qst, kband[:, sl], _NT, preferred_element_type=F32)
        s_loc = jnp.where(valid, s_loc, NEG_INF)
        s_ctx = lax.dot_general(qst, kc[:, sl], _NT, preferred_element_type=F32)
        m = jnp.maximum(jnp.maximum(jnp.max(s_loc, axis=-1, keepdims=True),
                                    jnp.max(s_ctx, axis=-1, keepdims=True)), snk)
        p_loc = jnp.exp(s_loc - m)
        p_ctx = jnp.exp(s_ctx - m)
        l = (jnp.sum(p_loc, axis=-1, keepdims=True) + jnp.sum(p_ctx, axis=-1, keepdims=True)
             + jnp.exp(snk - m))
        o = (jnp.dot(p_loc.astype(BF16), vband[:, sl], preferred_element_type=F32)
             + jnp.dot(p_ctx.astype(BF16), vc[:, sl], preferred_element_type=F32)) / l
        for gi in range(grp):
            outs[kv * grp + gi] = o[gi * A_BLOCK:(gi + 1) * A_BLOCK]
    y = jnp.concatenate(outs, axis=-1) * _silu(g_ref[...])
    y_ref[...] = y.astype(y_ref.dtype)


def _win_attn(proj, cache_k, cache_v, layer, cos, sin_signed, qn, kn, sink, bsz, t_len):
    nb = t_len // A_BLOCK
    kvw = A_KV_HEADS * HEAD_DIM
    p_len = cache_k.shape[2]
    return pl.pallas_call(
        _win_attn_kernel,
        grid=(bsz, nb),
        in_specs=[
            pl.BlockSpec((A_BLOCK, BRANCH_W), lambda b, j: (b * nb + j, C_AQ // 4)),
            pl.BlockSpec((t_len, kvw), lambda b, j: (b, C_AK)),
            pl.BlockSpec((t_len, kvw), lambda b, j: (b, C_AV)),
            pl.BlockSpec((A_BLOCK, BRANCH_W), lambda b, j: (b * nb + j, C_AG // 4)),
            pl.BlockSpec((1, 1, p_len, kvw), lambda b, j: (b, layer, 0, 0)),
            pl.BlockSpec((1, 1, p_len, kvw), lambda b, j: (b, layer, 0, 0)),
            pl.BlockSpec((t_len, BRANCH_W), lambda b, j: (0, 0)),
            pl.BlockSpec((t_len, BRANCH_W), lambda b, j: (0, 0)),
            pl.BlockSpec((1, HEAD_DIM), lambda b, j: (0, 0)),
            pl.BlockSpec((1, HEAD_DIM), lambda b, j: (0, 0)),
            pl.BlockSpec((1, A_HEADS), lambda b, j: (0, 0)),
        ],
        out_specs=pl.BlockSpec((A_BLOCK, BRANCH_W), lambda b, j: (b * nb + j, 0)),
        out_shape=jax.ShapeDtypeStruct((bsz * t_len, BRANCH_W), BF16),
        scratch_shapes=[pltpu.VMEM((t_len + 2 * A_BLOCK, kvw), BF16),
                        pltpu.VMEM((t_len + 2 * A_BLOCK, kvw), BF16)],
        compiler_params=_cparams(("arbitrary", "arbitrary")),
        name="win_attn",
    )(proj, proj, proj, proj, cache_k, cache_v, cos, sin_signed,
      qn.reshape(1, HEAD_DIM), kn.reshape(1, HEAD_DIM), sink.reshape(1, A_HEADS))


NBR_QB = 2 * GRID_W
NBR_BAND = 5


def _nbr_plan(rows):
    kh = min(NB_ROWS, rows)
    nblk = rows // 2
    specs, plan, starts = {}, [], []
    for i in range(nblk):
        s0 = min(max(i - 2, 0), nblk - NBR_BAND)
        starts.append(s0)
        blk = []
        for a in range(2):
            qr = 2 * i + a
            rs = min(max(qr - kh // 2, 0), rows - kh)
            assert 2 * s0 <= rs and rs + kh <= 2 * (s0 + NBR_BAND)
            row = []
            for p in range(NBR_BAND):
                pair = tuple(kr - qr + NB_ROWS - 1 if rs <= kr < rs + kh else None
                             for kr in (2 * (s0 + p), 2 * (s0 + p) + 1))
                row.append(specs.setdefault(pair, len(specs)))
            blk.append(row)
        plan.append(blk)
    return tuple(starts), plan, list(specs)


def _nbr_table(rpb, specs):
    heads = rpb.shape[0]
    c = np.arange(GRID_W)
    cs = np.clip(c - NB_COLS // 2, 0, GRID_W - NB_COLS)
    col_ok = (c[None, :] >= cs[:, None]) & (c[None, :] < cs[:, None] + NB_COLS)
    edge = GRID_W - NB_COLS
    padded = jnp.pad(rpb.astype(F32), ((0, 0), (0, 0), (edge, edge)))
    shifted = jnp.stack([padded[:, :, GRID_W - 1 - qc:2 * GRID_W - 1 - qc] for qc in range(GRID_W)], axis=2)
    by_col = jnp.where(col_ok[None, None], shifted, NEG_INF)
    neg = jnp.full((heads, GRID_W, GRID_W), NEG_INF, F32)
    blocks = [jnp.concatenate([neg if d is None else by_col[:, d] for d in spec], axis=-1) for spec in specs]
    return jnp.stack(blocks, axis=1)


def _nbr_attn_kernel(q_ref, k_ref, v_ref, g_ref, kc_ref, vc_ref, tb_ref, gq_ref, gk_ref, bd_ref, y_ref,
                     *, starts, plan):
    nband = NBR_BAND * NBR_QB
    bd = bd_ref[...]
    qn = (_heads_rms(q_ref[...], bd, gq_ref[...]) * QK_SCALE).astype(BF16)
    kn = _heads_rms(k_ref[...], bd, gk_ref[...]).astype(BF16)
    vb = v_ref[...].astype(BF16)
    kcb = kc_ref[0, 0].astype(BF16)
    vcb = vc_ref[0, 0].astype(BF16)
    acc = [jnp.zeros((NBR_QB, LANE), F32) for _ in starts]
    for h in range(LANE // HEAD_DIM):
        hm = _head_lane_mask(LANE, h)
        km = jnp.where(hm, kn, jnp.zeros_like(kn))
        kcm = jnp.where(hm, kcb, jnp.zeros_like(kcb))
        vm = jnp.concatenate([jnp.where(hm, vb, jnp.zeros_like(vb)), jnp.ones_like(vb)], axis=1)
        vcm = jnp.concatenate([jnp.where(hm, vcb, jnp.zeros_like(vcb)), jnp.ones_like(vcb)], axis=1)
        for i, s0 in enumerate(starts):
            qi = qn[i * NBR_QB:(i + 1) * NBR_QB]
            ks = slice(s0 * NBR_QB, s0 * NBR_QB + nband)
            s_raw = lax.dot_general(qi, km[ks], _NT, preferred_element_type=F32)
            s_loc = jnp.concatenate(
                [jnp.concatenate([s_raw[a * GRID_W:(a + 1) * GRID_W, p * LANE:(p + 1) * LANE]
                                  + tb_ref[h, plan[i][a][p]] for p in range(NBR_BAND)], axis=1)
                 for a in range(2)], axis=0)
            s_ctx = lax.dot_general(qi, kcm, _NT, preferred_element_type=F32)
            m = jnp.maximum(jnp.max(s_loc, axis=-1, keepdims=True), jnp.max(s_ctx, axis=-1, keepdims=True))
            p_loc = jnp.exp(s_loc - m).astype(BF16)
            p_ctx = jnp.exp(s_ctx - m).astype(BF16)
            oe = (jnp.dot(p_loc, vm[ks], preferred_element_type=F32)
                  + jnp.dot(p_ctx, vcm, preferred_element_type=F32))
            acc[i] = acc[i] + oe[:, :LANE] / oe[:, LANE:]
    y = jnp.concatenate(acc, axis=0) * _silu(g_ref[...])
    y_ref[...] = y.astype(y_ref.dtype)


def _nbr_attn(proj, cache_k, cache_v, layer, rpb, gq, gk, bsz, t_len):
    hp = LANE // HEAD_DIM
    nhp = B_HEADS // hp
    p_len = cache_k.shape[2]
    rows = t_len // GRID_W
    assert rows % 2 == 0 and rows // 2 >= NBR_BAND
    starts, plan, specs = _nbr_plan(rows)
    table = _nbr_table(rpb, specs)
    const = lambda h, b: (0, 0)
    kern = functools.partial(_nbr_attn_kernel, starts=starts, plan=plan)
    return pl.pallas_call(
        kern,
        grid=(nhp, bsz),
        in_specs=[
            pl.BlockSpec((t_len, LANE), lambda h, b: (b, C_BQ + h)),
            pl.BlockSpec((t_len, LANE), lambda h, b: (b, C_BK + h)),
            pl.BlockSpec((t_len, LANE), lambda h, b: (b, C_BV + h)),
            pl.BlockSpec((t_len, LANE), lambda h, b: (b, C_BG + h)),
            pl.BlockSpec((1, 1, p_len, LANE), lambda h, b: (b, layer, 0, h)),
            pl.BlockSpec((1, 1, p_len, LANE), lambda h, b: (b, layer, 0, h)),
            pl.BlockSpec((hp, len(specs), GRID_W, 2 * GRID_W), lambda h, b: (h, 0, 0, 0)),
            pl.BlockSpec((1, LANE), const),
            pl.BlockSpec((1, LANE), const),
            pl.BlockSpec((LANE, LANE), const),
        ],
        out_specs=pl.BlockSpec((t_len, LANE), lambda h, b: (b, h)),
        out_shape=jax.ShapeDtypeStruct((bsz * t_len, BRANCH_W), BF16),
        compiler_params=_cparams(("arbitrary", "arbitrary")),
        name="nbr_attn",
    )(proj, proj, proj, proj, cache_k, cache_v, table,
      jnp.tile(gq, hp).reshape(1, LANE), jnp.tile(gk, hp).reshape(1, LANE), _head_mean_matrix(LANE))


def _lru_kernel(cx_ref, cg_ref, h0_ref, cw_ref, cb_ref, wg_ref, bg_ref, lam_ref,
                y_ref, st_ref, a_f, u_f, a_b, u_b):
    t_len = cx_ref.shape[0]
    w = LRU_WIDTH
    cx = cx_ref[...]
    row = lax.broadcasted_iota(jnp.int32, (t_len, w), 0)
    xc = cb_ref[...] + cx * cw_ref[2:3, :]
    xc = xc + jnp.where(row >= 2, pltpu.roll(cx, 2, axis=0), 0.0) * cw_ref[0:1, :]
    xc = xc + jnp.where(row >= 1, pltpu.roll(cx, 1, axis=0), 0.0) * cw_ref[1:2, :]
    xc = xc + jnp.where(row < t_len - 1, pltpu.roll(cx, t_len - 1, axis=0), 0.0) * cw_ref[3:4, :]

    gates = jnp.dot(xc.astype(BF16), wg_ref[...], preferred_element_type=F32) + bg_ref[...]
    for d, (a_s, u_s) in enumerate(((a_f, u_f), (a_b, u_b))):
        r = _sigmoid(gates[:, d * w:(d + 1) * w])
        i = _sigmoid(gates[:, (2 + d) * w:(3 + d) * w])
        nl = -lam_ref[d:d + 1, :]
        softplus = jnp.maximum(nl, 0.0) + jnp.log1p(jnp.exp(-jnp.abs(nl)))
        log_a = (-LRU_C * softplus) * r
        t = jnp.tanh(0.5 * log_a)
        rc = 1.0 / (1.0 - t)
        a_s[...] = (1.0 + t) * rc
        u_s[...] = (2.0 * jnp.sqrt(-t)) * rc * i * xc

    n_grp = t_len // SUBLANE

    def body(it, carry):
        hf, hb = carry
        r0 = pl.multiple_of(it * SUBLANE, SUBLANE)
        av = a_f[pl.ds(r0, SUBLANE), :]
        uv = u_f[pl.ds(r0, SUBLANE), :]
        rows_f = []
        for s in range(SUBLANE):
            hf = av[s:s + 1, :] * hf + uv[s:s + 1, :]
            rows_f.append(hf)
        u_f[pl.ds(r0, SUBLANE), :] = jnp.concatenate(rows_f, axis=0)
        r1 = pl.multiple_of((n_grp - 1 - it) * SUBLANE, SUBLANE)
        av = a_b[pl.ds(r1, SUBLANE), :]
        uv = u_b[pl.ds(r1, SUBLANE), :]
        rows_b = [None] * SUBLANE
        for s in range(SUBLANE - 1, -1, -1):
            hb = av[s:s + 1, :] * hb + uv[s:s + 1, :]
            rows_b[s] = hb
        u_b[pl.ds(r1, SUBLANE), :] = jnp.concatenate(rows_b, axis=0)
        return hf, hb

    h0 = h0_ref[0]
    hf, hb = lax.fori_loop(0, n_grp, body, (h0[0:1, :], h0[1:2, :]))
    st_ref[0] = jnp.concatenate([hf, hb], axis=0)
    y = (u_f[...] + u_b[...]) * _silu(cg_ref[...])
    y_ref[...] = y.astype(y_ref.dtype)


def _lru(proj, h0, conv_w, conv_b, w_gates, b_gates, lam, bsz, t_len):
    w = LRU_WIDTH
    return pl.pallas_call(
        _lru_kernel,
        grid=(bsz,),
        in_specs=[
            pl.BlockSpec((t_len, w), lambda b: (b, C_CX // 4)),
            pl.BlockSpec((t_len, w), lambda b: (b, C_CG // 4)),
            pl.BlockSpec((1, 2, w), lambda b: (b, 0, 0)),
            pl.BlockSpec((CONV_W, w), lambda b: (0, 0)),
            pl.BlockSpec((1, w), lambda b: (0, 0)),
            pl.BlockSpec((w, 4 * w), lambda b: (0, 0)),
            pl.BlockSpec((1, 4 * w), lambda b: (0, 0)),
            pl.BlockSpec((2, w), lambda b: (0, 0)),
        ],
        out_specs=[
            pl.BlockSpec((t_len, w), lambda b: (b, 0)),
            pl.BlockSpec((1, 2, w), lambda b: (b, 0, 0)),
        ],
        out_shape=[
            jax.ShapeDtypeStruct((bsz * t_len, w), BF16),
            jax.ShapeDtypeStruct((bsz, 2, w), F32),
        ],
        scratch_shapes=[pltpu.VMEM((t_len, w), F32) for _ in range(4)],
        compiler_params=_cparams(("arbitrary",)),
        name="lru",
    )(proj, proj, h0, conv_w, conv_b.reshape(1, w), w_gates, b_gates, lam)


def _lru_gate_weights(wa, ba, wx, bx):
    def dense(wblk):
        eye = jnp.eye(LRU_BLOCKS, dtype=wblk.dtype)
        return jnp.einsum('nkj,nm->nkmj', wblk, eye).reshape(LRU_WIDTH, LRU_WIDTH)
    wg = jnp.concatenate([dense(wa[0]), dense(wa[1]), dense(wx[0]), dense(wx[1])], axis=1).astype(BF16)
    bg = jnp.concatenate([ba[0], ba[1], bx[0], bx[1]]).reshape(1, 4 * LRU_WIDTH)
    return wg, bg


def _merge_kernel(x_ref, ya_ref, yb_ref, yc_ref, ga_ref, gb_ref, gc_ref, mod_ref, wbr_ref, wout_ref, o_ref):
    z = (_sigmoid(ga_ref[...]) * jnp.dot(ya_ref[...], wbr_ref[0], preferred_element_type=F32)
         + _sigmoid(gb_ref[...]) * jnp.dot(yb_ref[...], wbr_ref[1], preferred_element_type=F32)
         + _sigmoid(gc_ref[...]) * jnp.dot(yc_ref[...], wbr_ref[2], preferred_element_type=F32))
    out = jnp.dot(z.astype(BF16), wout_ref[...], preferred_element_type=F32)
    o_ref[...] = x_ref[...] + mod_ref[0, 2:3, :] * out


def _merge(x2d, ya, yb, yc, proj, mod3, w_br, w_out, rows_per_mod, mod_row0):
    tokens = x2d.shape[0]
    tm = 512
    tiles_per_mod = rows_per_mod // tm if rows_per_mod else 0

    def mod_idx(i):
        if rows_per_mod:
            return (mod_row0 + i // tiles_per_mod, 0, 0)
        return (mod_row0, 0, 0)

    return pl.pallas_call(
        _merge_kernel,
        grid=(tokens // tm,),
        in_specs=[
            pl.BlockSpec((tm, D_MODEL), lambda i: (i, 0)),
            pl.BlockSpec((tm, BRANCH_W), lambda i: (i, 0)),
            pl.BlockSpec((tm, BRANCH_W), lambda i: (i, 0)),
            pl.BlockSpec((tm, BRANCH_W), lambda i: (i, 0)),
            pl.BlockSpec((tm, D_MODEL), lambda i: (i, C_GA // 8)),
            pl.BlockSpec((tm, D_MODEL), lambda i: (i, C_GB // 8)),
            pl.BlockSpec((tm, D_MODEL), lambda i: (i, C_GC // 8)),
            pl.BlockSpec((1, 3, D_MODEL), mod_idx),
            pl.BlockSpec((3, BRANCH_W, D_MODEL), lambda i: (0, 0, 0)),
            pl.BlockSpec((D_MODEL, D_MODEL), lambda i: (0, 0)),
        ],
        out_specs=pl.BlockSpec((tm, D_MODEL), lambda i: (i, 0)),
        out_shape=jax.ShapeDtypeStruct((tokens, D_MODEL), F32),
        compiler_params=_cparams(("arbitrary",)),
        name="merge",
    )(x2d, ya, yb, yc, proj, proj, proj, mod3, w_br, w_out)


def _rope_tables(t_len):
    t = jnp.arange(t_len)
    m = HEAD_DIM // 4
    freqs = ROPE_BASE ** (-jnp.arange(m, dtype=F32) / m)
    ang_r = (t // GRID_W).astype(F32)[:, None] * freqs[None, :]
    ang_c = (t % GRID_W).astype(F32)[:, None] * freqs[None, :]
    cos = jnp.concatenate([jnp.cos(ang_r), jnp.cos(ang_r), jnp.cos(ang_c), jnp.cos(ang_c)], axis=-1)
    sin = jnp.concatenate([-jnp.sin(ang_r), jnp.sin(ang_r), -jnp.sin(ang_c), jnp.sin(ang_c)], axis=-1)
    reps = BRANCH_W // HEAD_DIM
    return jnp.tile(cos, (1, reps)), jnp.tile(sin, (1, reps))


def kernel(x_prompt, x_sample, cache_ka, cache_va, cache_kb, cache_vb, state_lru, c, c_ctx,
           norm_g, w_ada, b_ada, w_in, a_q_norm, a_k_norm, a_sink, b_q_norm, b_k_norm, b_rpb,
           lru_conv_w, lru_conv_b, lru_wa, lru_ba, lru_wx, lru_bx, lru_lambda, w_branch, w_out):
    bsz, s_len, _ = x_prompt.shape
    dbsz, t_len, _ = x_sample.shape
    p_len = cache_ka.shape[2]

    n_mod = 16
    ctx_row = dbsz
    cond = jnp.zeros((n_mod, D_MODEL), F32).at[:dbsz].set(c).at[ctx_row].set(c_ctx)
    mods = _mods(cond, w_ada, b_ada).reshape(DEPTH, n_mod, 3, D_MODEL)

    w_perm = jnp.concatenate(
        [w_in[:, :, _ORIG_OFFS[k]:_ORIG_OFFS[k] + _ORIG_SPLITS[k]] for k in _PERM], axis=-1).astype(BF16)
    w_br = w_branch.astype(BF16)
    w_o = w_out.astype(BF16)
    cos, sin_signed = _rope_tables(t_len)

    cka = cache_ka.reshape(dbsz, DEPTH, p_len, A_KV_HEADS * HEAD_DIM)
    cva = cache_va.reshape(dbsz, DEPTH, p_len, A_KV_HEADS * HEAD_DIM)
    ckb = cache_kb.reshape(dbsz, DEPTH, p_len, B_HEADS * HEAD_DIM)
    cvb = cache_vb.reshape(dbsz, DEPTH, p_len, B_HEADS * HEAD_DIM)

    yp = x_prompt.reshape(bsz * s_len, D_MODEL)
    ys = x_sample.reshape(dbsz * t_len, D_MODEL)
    zero_state = jnp.zeros((bsz, 2, LRU_WIDTH), F32)
    new_ka, new_va, new_kb, new_vb, new_lru = [], [], [], [], []
    for l in range(DEPTH):
        wg, bg = _lru_gate_weights(lru_wa[l], lru_ba[l], lru_wx[l], lru_bx[l])

        proj = _inproj(yp, mods[l], norm_g[l], w_perm[l], 0, ctx_row)
        ya, ka, va = _ctx_attn(proj, bsz, s_len, a_q_norm[l], a_k_norm[l], a_sink[l],
                               n_q=A_HEADS, n_kv=A_KV_HEADS, use_sink=True,
                               cq=C_AQ, ck=C_AK, cv=C_AV, cg=C_AG)
        yb, kb, vb = _ctx_attn(proj, bsz, s_len, b_q_norm[l], b_k_norm[l], jnp.zeros((B_HEADS,), F32),
                               n_q=B_HEADS, n_kv=B_HEADS, use_sink=False,
                               cq=C_BQ, ck=C_BK, cv=C_BV, cg=C_BG)
        yc, st = _lru(proj, zero_state, lru_conv_w[l], lru_conv_b[l], wg, bg, lru_lambda[l], bsz, s_len)
        yp = _merge(yp, ya, yb, yc, proj, mods[l], w_br[l], w_o[l], 0, ctx_row)
        new_ka.append(ka.reshape(bsz, s_len, A_KV_HEADS, HEAD_DIM))
        new_va.append(va.reshape(bsz, s_len, A_KV_HEADS, HEAD_DIM))
        new_kb.append(kb.reshape(bsz, s_len, B_HEADS, HEAD_DIM))
        new_vb.append(vb.reshape(bsz, s_len, B_HEADS, HEAD_DIM))
        new_lru.append(st)

        proj = _inproj(ys, mods[l], norm_g[l], w_perm[l], t_len, 0)
        ya = _win_attn(proj, cka, cva, l, cos, sin_signed, a_q_norm[l], a_k_norm[l], a_sink[l], dbsz, t_len)
        yb = _nbr_attn(proj, ckb, cvb, l, b_rpb[l], b_q_norm[l], b_k_norm[l], dbsz, t_len)
        yc, _ = _lru(proj, state_lru[:, l], lru_conv_w[l], lru_conv_b[l], wg, bg, lru_lambda[l], dbsz, t_len)
        ys = _merge(ys, ya, yb, yc, proj, mods[l], w_br[l], w_o[l], t_len, 0)

    return (yp.reshape(bsz, s_len, D_MODEL), ys.reshape(dbsz, t_len, D_MODEL),
            jnp.stack(new_ka, axis=1), jnp.stack(new_va, axis=1),
            jnp.stack(new_kb, axis=1), jnp.stack(new_vb, axis=1),
            jnp.stack(new_lru, axis=1))
```

```python
import functools

import numpy as np
import jax
import jax.numpy as jnp
from jax import lax
from jax.experimental import pallas as pl
from jax.experimental.pallas import tpu as pltpu

F32 = jnp.float32
BF16 = jnp.bfloat16

D_MODEL = 1024
DEPTH = 2
GRID_W = 64
HEAD_DIM = 64
BRANCH_W = 512
A_HEADS = 8
A_KV_HEADS = 2
A_WINDOW = 128
A_BLOCK = 128
B_HEADS = 8
NB_ROWS = 8
NB_COLS = 16
LRU_WIDTH = 512
LRU_BLOCKS = 8
LRU_BW = LRU_WIDTH // LRU_BLOCKS
LRU_C = 8.0
CONV_W = 4
ROPE_BASE = 10000.0
EPS = 1e-6
NEG_INF = -1e30
QK_SCALE = HEAD_DIM ** -0.5

LANE = 128
SUBLANE = 8
MXU_DIM = 256
VMEM_LIMIT = 56 * 1024 * 1024

_ORIG_SPLITS = (512, 128, 128, 512, 512, 512, 512, 512, 512, 512, 1024, 1024, 1024)
_ORIG_OFFS = tuple(int(v) for v in np.cumsum((0,) + _ORIG_SPLITS)[:-1])
_PERM = (0, 3, 4, 5, 6, 7, 8, 9, 10, 11, 12, 1, 2)
IN_COLS = sum(_ORIG_SPLITS)
C_AQ, C_AG, C_BQ, C_BK, C_BV, C_BG, C_CX, C_CG, C_GA, C_GB, C_GC, C_AK, C_AV = (
    0, 4, 8, 12, 16, 20, 24, 28, 32, 40, 48, 56, 57)

_NT = (((1,), (1,)), ((), ()))


def _cparams(sem):
    return pltpu.CompilerParams(dimension_semantics=sem, vmem_limit_bytes=VMEM_LIMIT)


def _sigmoid(x):
    return 0.5 + 0.5 * jnp.tanh(0.5 * x)


def _silu(x):
    return x * _sigmoid(x)


def _head_mean_matrix(width):
    idx = np.arange(width) // HEAD_DIM
    return jnp.asarray((idx[:, None] == idx[None, :]).astype(np.float32) / HEAD_DIM, dtype=BF16)


def _heads_rms(x, bd, g):
    x2 = x * x
    hi = x2.astype(BF16)
    lo = (x2 - hi.astype(F32)).astype(BF16)
    ms = jnp.dot(hi, bd, preferred_element_type=F32) + jnp.dot(lo, bd, preferred_element_type=F32)
    return x * lax.rsqrt(ms + EPS) * g


def _head_lane_mask(width, h):
    lane = lax.broadcasted_iota(jnp.int32, (1, width), 1)
    return (lane >= h * HEAD_DIM) & (lane < (h + 1) * HEAD_DIM)


def _mods_kernel(c_ref, w_ref, b_ref, o_ref):
    c = c_ref[...]
    s = _silu(c).astype(BF16)
    o_ref[0] = jnp.dot(s, w_ref[0].astype(BF16), preferred_element_type=F32) + b_ref[0]


def _mods(cond, w_ada, b_ada):
    n = cond.shape[0]
    tn = D_MODEL
    return pl.pallas_call(
        _mods_kernel,
        grid=(DEPTH, 3 * D_MODEL // tn),
        in_specs=[
            pl.BlockSpec((n, D_MODEL), lambda l, j: (0, 0)),
            pl.BlockSpec((1, D_MODEL, tn), lambda l, j: (l, 0, j)),
            pl.BlockSpec((1, 1, tn), lambda l, j: (l, 0, j)),
        ],
        out_specs=pl.BlockSpec((1, n, tn), lambda l, j: (l, 0, j)),
        out_shape=jax.ShapeDtypeStruct((DEPTH, n, 3 * D_MODEL), F32),
        compiler_params=_cparams(("arbitrary", "arbitrary")),
        name="mods",
    )(cond, w_ada, b_ada.reshape(DEPTH, 1, 3 * D_MODEL))


def _inproj_kernel(x_ref, mod_ref, g_ref, w_ref, o_ref):
    x = x_ref[...]
    y = x * lax.rsqrt(jnp.mean(x * x, axis=-1, keepdims=True) + EPS)
    y = y * g_ref[0]
    shift = mod_ref[0, 0, 0:1, :]
    scale = mod_ref[0, 0, 1:2, :]
    h = (y * (1.0 + scale) + shift).astype(BF16)
    o_ref[...] = jnp.dot(h, w_ref[0], preferred_element_type=F32)


def _inproj(x2d, mods, norm_g, w_perm, layer, rows_per_mod, mod_row0):
    tokens = x2d.shape[0]
    tm = 512
    tn = IN_COLS // 2
    tiles_per_mod = rows_per_mod // tm if rows_per_mod else 0

    def mod_idx(j, i):
        if rows_per_mod:
            return (layer, mod_row0 + i // tiles_per_mod, 0, 0)
        return (layer, mod_row0, 0, 0)

    return pl.pallas_call(
        _inproj_kernel,
        grid=(IN_COLS // tn, tokens // tm),
        in_specs=[
            pl.BlockSpec((tm, D_MODEL), lambda j, i: (i, 0)),
            pl.BlockSpec((1, 1, 3, D_MODEL), mod_idx),
            pl.BlockSpec((1, 1, D_MODEL), lambda j, i: (layer, 0, 0)),
            pl.BlockSpec((1, D_MODEL, tn), lambda j, i: (layer, 0, j)),
        ],
        out_specs=pl.BlockSpec((tm, tn), lambda j, i: (i, j)),
        out_shape=jax.ShapeDtypeStruct((tokens, IN_COLS), F32),
        compiler_params=_cparams(("arbitrary", "arbitrary")),
        name="inproj",
    )(x2d, mods, norm_g.reshape(DEPTH, 1, D_MODEL), w_perm)


def _ctx_attn_kernel(q_ref, k_ref, v_ref, g_ref, gq_ref, gk_ref, sink_ref, bdq_ref, bdk_ref, exp_ref, ones_ref,
                     y_ref, ko_ref, vo_ref, *, n_q, n_kv, use_sink):
    s_len = q_ref.shape[0]
    v = v_ref[...]
    vo_ref[...] = v
    kn = _heads_rms(k_ref[...], bdk_ref[...], gk_ref[...])
    ko_ref[...] = kn
    qn = (_heads_rms(q_ref[...], bdq_ref[...], gq_ref[...]) * QK_SCALE).astype(BF16)
    knb = kn.astype(BF16)
    vb = v.astype(BF16)
    if n_kv < n_q:
        knb = jnp.dot(knb, exp_ref[...], preferred_element_type=F32).astype(BF16)
        vb = jnp.dot(vb, exp_ref[...], preferred_element_type=F32).astype(BF16)
    heads_per_half = MXU_DIM // HEAD_DIM
    masks = [_head_lane_mask(MXU_DIM, hi) for hi in range(heads_per_half)]
    ones_stack = ones_ref[...]
    halves = []
    for hh in range(n_q // heads_per_half):
        cols = slice(hh * MXU_DIM, (hh + 1) * MXU_DIM)
        qh, kh, vh = qn[:, cols], knb[:, cols], vb[:, cols]
        kstack = jnp.concatenate([jnp.where(mk, kh, jnp.zeros_like(kh)) for mk in masks], axis=0)
        vstack = jnp.concatenate([jnp.where(mk, vh, jnp.zeros_like(vh)) for mk in masks], axis=0)
        s = lax.dot_general(qh, kstack, _NT, preferred_element_type=F32)
        ps, sink_terms = [], []
        for hi in range(heads_per_half):
            si = s[:, hi * s_len:(hi + 1) * s_len]
            m = jnp.max(si, axis=-1, keepdims=True)
            if use_sink:
                h = hh * heads_per_half + hi
                snk = sink_ref[0:1, h:h + 1]
                m = jnp.maximum(m, snk)
                sink_terms.append(jnp.where(masks[hi], jnp.exp(snk - m), 0.0))
            ps.append(jnp.exp(si - m).astype(BF16))
        p = jnp.concatenate(ps, axis=1)
        rhs = jnp.concatenate([vstack, ones_stack], axis=1)
        oe = jnp.dot(p, rhs, preferred_element_type=F32)
        l = oe[:, MXU_DIM:]
        for term in sink_terms:
            l = l + term
        halves.append(oe[:, :MXU_DIM] / l)
    y = jnp.concatenate(halves, axis=-1) * _silu(g_ref[...])
    y_ref[...] = y.astype(y_ref.dtype)


def _ctx_attn(proj, bsz, s_len, gq, gk, sink, *, n_q, n_kv, use_sink, cq, ck, cv, cg):
    kvw = n_kv * HEAD_DIM
    kvb = kvw // LANE
    grp = n_q // n_kv
    expand = np.zeros((kvw, BRANCH_W), np.float32)
    for h in range(n_q):
        for d in range(HEAD_DIM):
            expand[(h // grp) * HEAD_DIM + d, h * HEAD_DIM + d] = 1.0
    heads_per_half = MXU_DIM // HEAD_DIM
    ones_stack = np.zeros((heads_per_half * s_len, MXU_DIM), np.float32)
    for hi in range(heads_per_half):
        ones_stack[hi * s_len:(hi + 1) * s_len, hi * HEAD_DIM:(hi + 1) * HEAD_DIM] = 1.0
    const = lambda b: (0, 0)
    kern = functools.partial(_ctx_attn_kernel, n_q=n_q, n_kv=n_kv, use_sink=use_sink)
    return pl.pallas_call(
        kern,
        grid=(bsz,),
        in_specs=[
            pl.BlockSpec((s_len, BRANCH_W), lambda b: (b, cq // 4)),
            pl.BlockSpec((s_len, kvw), lambda b: (b, ck // kvb)),
            pl.BlockSpec((s_len, kvw), lambda b: (b, cv // kvb)),
            pl.BlockSpec((s_len, BRANCH_W), lambda b: (b, cg // 4)),
            pl.BlockSpec((1, BRANCH_W), const),
            pl.BlockSpec((1, kvw), const),
            pl.BlockSpec((1, n_q), const),
            pl.BlockSpec((BRANCH_W, BRANCH_W), const),
            pl.BlockSpec((kvw, kvw), const),
            pl.BlockSpec((kvw, BRANCH_W), const),
            pl.BlockSpec((heads_per_half * s_len, MXU_DIM), const),
        ],
        out_specs=[
            pl.BlockSpec((s_len, BRANCH_W), lambda b: (b, 0)),
            pl.BlockSpec((s_len, kvw), lambda b: (b, 0)),
            pl.BlockSpec((s_len, kvw), lambda b: (b, 0)),
        ],
        out_shape=[
            jax.ShapeDtypeStruct((bsz * s_len, BRANCH_W), BF16),
            jax.ShapeDtypeStruct((bsz * s_len, kvw), F32),
            jax.ShapeDtypeStruct((bsz * s_len, kvw), F32),
        ],
        compiler_params=_cparams(("arbitrary",)),
        name="ctx_attn",
    )(proj, proj, proj, proj,
      jnp.tile(gq, n_q).reshape(1, BRANCH_W), jnp.tile(gk, n_kv).reshape(1, kvw), sink.reshape(1, n_q),
      _head_mean_matrix(BRANCH_W), _head_mean_matrix(kvw), jnp.asarray(expand, dtype=BF16),
      jnp.asarray(ones_stack, dtype=BF16))


def _rope(x, cos, sin_signed):
    w = x.shape[-1]
    lane = lax.broadcasted_iota(jnp.int32, x.shape, 1)
    up = pltpu.roll(x, w - 16, axis=1)
    dn = pltpu.roll(x, 16, axis=1)
    partner = jnp.where((lane & 16) == 0, up, dn)
    return x * cos + partner * sin_signed


def _win_attn_kernel(q_ref, k_ref, v_ref, g_ref, kc_ref, vc_ref, cos_ref, sin_ref, gq_ref, gk_ref, sink_ref,
                     bdq_ref, bdk_ref, dup_ref, mask_ref, y_ref, kpad, vpad, kcx, vcx):
    j = pl.program_id(1)
    nb = pl.num_programs(1)
    t_len = k_ref.shape[0]
    grp = A_HEADS // A_KV_HEADS
    kvw = A_KV_HEADS * HEAD_DIM
    xw = 2 * kvw

    @pl.when(j == 0)
    def _():
        dup = dup_ref[...]
        kn = _heads_rms(k_ref[...], bdk_ref[...], gk_ref[...])
        kn = _rope(kn, cos_ref[:, 0:kvw], sin_ref[:, 0:kvw]).astype(BF16)
        zeros = jnp.zeros((A_BLOCK, xw), BF16)
        kpad[0:A_BLOCK, :] = zeros
        kpad[A_BLOCK + t_len:2 * A_BLOCK + t_len, :] = zeros
        vpad[0:A_BLOCK, :] = zeros
        vpad[A_BLOCK + t_len:2 * A_BLOCK + t_len, :] = zeros
        kpad[A_BLOCK:A_BLOCK + t_len, :] = jnp.dot(kn, dup, preferred_element_type=F32).astype(BF16)
        vpad[A_BLOCK:A_BLOCK + t_len, :] = jnp.dot(
            v_ref[...].astype(BF16), dup, preferred_element_type=F32).astype(BF16)
        kcx[...] = jnp.dot(kc_ref[0, 0].astype(BF16), dup, preferred_element_type=F32).astype(BF16)
        vcx[...] = jnp.dot(vc_ref[0, 0].astype(BF16), dup, preferred_element_type=F32).astype(BF16)

    r0 = pl.multiple_of(j * A_BLOCK, A_BLOCK)
    qn = _heads_rms(q_ref[...], bdq_ref[...], gq_ref[...])
    qb = (_rope(qn, cos_ref[pl.ds(r0, A_BLOCK), :], sin_ref[pl.ds(r0, A_BLOCK), :]) * QK_SCALE).astype(BF16)

    nloc = 3 * A_BLOCK
    maskadd = mask_ref[jnp.where(j == 0, 0, jnp.where(j == nb - 1, 2, 1))]
    low_half = lax.broadcasted_iota(jnp.int32, (1, LANE), 1) < HEAD_DIM
    kband = kpad[pl.ds(r0, nloc), :]
    vband = vpad[pl.ds(r0, nloc), :]
    pairs = []
    for kv in range(A_KV_HEADS):
        cols = slice(kv * LANE, (kv + 1) * LANE)
        qparts, sinks = [], []
        for gi in range(grp):
            h = kv * grp + gi
            qpair = qb[:, (h // 2) * LANE:(h // 2 + 1) * LANE]
            keep = low_half if h % 2 == 0 else jnp.logical_not(low_half)
            qparts.append(jnp.where(keep, qpair, jnp.zeros_like(qpair)))
            sinks.append(jnp.broadcast_to(sink_ref[0:1, h:h + 1], (A_BLOCK, 1)))
        qst = jnp.concatenate(qparts, axis=0)
        snk = jnp.concatenate(sinks, axis=0)
        s_loc = lax.dot_general(qst, kband[:, cols], _NT, preferred_element_type=F32) + maskadd
        s_ctx = lax.dot_general(qst, kcx[:, cols], _NT, preferred_element_type=F32)
        m = jnp.maximum(jnp.maximum(jnp.max(s_loc, axis=-1, keepdims=True),
                                    jnp.max(s_ctx, axis=-1, keepdims=True)), snk)
        p_loc = jnp.exp(s_loc - m)
        p_ctx = jnp.exp(s_ctx - m)
        l = (jnp.sum(p_loc, axis=-1, keepdims=True) + jnp.sum(p_ctx, axis=-1, keepdims=True)
             + jnp.exp(snk - m))
        o = (jnp.dot(p_loc.astype(BF16), vband[:, cols], preferred_element_type=F32)
             + jnp.dot(p_ctx.astype(BF16), vcx[:, cols], preferred_element_type=F32)) / l
        for k2 in range(grp // 2):
            even = o[(2 * k2) * A_BLOCK:(2 * k2 + 1) * A_BLOCK]
            odd = o[(2 * k2 + 1) * A_BLOCK:(2 * k2 + 2) * A_BLOCK]
            pairs.append(jnp.where(low_half, even, odd))
    y = jnp.concatenate(pairs, axis=-1) * _silu(g_ref[...])
    y_ref[...] = y.astype(y_ref.dtype)


def _win_mask(grp, nb):
    assert nb >= 2
    r = np.arange(A_BLOCK)[:, None]
    c = np.arange(3 * A_BLOCK)[None, :]
    band = np.abs(r + A_BLOCK - c) <= A_WINDOW
    variants = [band & (c >= A_BLOCK), band, band & (c < 2 * A_BLOCK)]
    return np.stack([np.tile(np.where(v, 0.0, NEG_INF).astype(np.float32), (grp, 1)) for v in variants])


def _win_attn(proj, cache_k, cache_v, layer, cos, sin_signed, gq, gk, sink, bsz, t_len):
    nb = t_len // A_BLOCK
    kvw = A_KV_HEADS * HEAD_DIM
    grp = A_HEADS // A_KV_HEADS
    p_len = cache_k.shape[2]
    dup = np.zeros((kvw, 2 * kvw), np.float32)
    for kv in range(A_KV_HEADS):
        for half in range(2):
            for d in range(HEAD_DIM):
                dup[kv * HEAD_DIM + d, kv * LANE + half * HEAD_DIM + d] = 1.0
    const2 = lambda b, j: (0, 0)
    return pl.pallas_call(
        _win_attn_kernel,
        grid=(bsz, nb),
        in_specs=[
            pl.BlockSpec((A_BLOCK, BRANCH_W), lambda b, j: (b * nb + j, C_AQ // 4)),
            pl.BlockSpec((t_len, kvw), lambda b, j: (b, C_AK)),
            pl.BlockSpec((t_len, kvw), lambda b, j: (b, C_AV)),
            pl.BlockSpec((A_BLOCK, BRANCH_W), lambda b, j: (b * nb + j, C_AG // 4)),
            pl.BlockSpec((1, 1, p_len, kvw), lambda b, j: (b, layer, 0, 0)),
            pl.BlockSpec((1, 1, p_len, kvw), lambda b, j: (b, layer, 0, 0)),
            pl.BlockSpec((t_len, BRANCH_W), const2),
            pl.BlockSpec((t_len, BRANCH_W), const2),
            pl.BlockSpec((1, BRANCH_W), const2),
            pl.BlockSpec((1, kvw), const2),
            pl.BlockSpec((1, A_HEADS), const2),
            pl.BlockSpec((BRANCH_W, BRANCH_W), const2),
            pl.BlockSpec((kvw, kvw), const2),
            pl.BlockSpec((kvw, 2 * kvw), const2),
            pl.BlockSpec((3, grp * A_BLOCK, 3 * A_BLOCK), lambda b, j: (0, 0, 0)),
        ],
        out_specs=pl.BlockSpec((A_BLOCK, BRANCH_W), lambda b, j: (b * nb + j, 0)),
        out_shape=jax.ShapeDtypeStruct((bsz * t_len, BRANCH_W), BF16),
        scratch_shapes=[pltpu.VMEM((t_len + 2 * A_BLOCK, 2 * kvw), BF16),
                        pltpu.VMEM((t_len + 2 * A_BLOCK, 2 * kvw), BF16),
                        pltpu.VMEM((p_len, 2 * kvw), BF16),
                        pltpu.VMEM((p_len, 2 * kvw), BF16)],
        compiler_params=_cparams(("arbitrary", "arbitrary")),
        name="win_attn",
    )(proj, proj, proj, proj, cache_k, cache_v, cos, sin_signed,
      jnp.tile(gq, A_HEADS).reshape(1, BRANCH_W), jnp.tile(gk, A_KV_HEADS).reshape(1, kvw),
      sink.reshape(1, A_HEADS), _head_mean_matrix(BRANCH_W), _head_mean_matrix(kvw),
      jnp.asarray(dup, dtype=BF16), jnp.asarray(_win_mask(grp, nb)))


NBR_QB = 2 * GRID_W
NBR_BAND = 5


def _nbr_plan(rows):
    kh = min(NB_ROWS, rows)
    nblk = rows // 2
    specs, plan, starts = {}, [], []
    for i in range(nblk):
        s0 = min(max(i - 2, 0), nblk - NBR_BAND)
        starts.append(s0)
        blk = []
        for a in range(2):
            qr = 2 * i + a
            rs = min(max(qr - kh // 2, 0), rows - kh)
            assert 2 * s0 <= rs and rs + kh <= 2 * (s0 + NBR_BAND)
            row = []
            for p in range(NBR_BAND):
                pair = tuple(kr - qr + NB_ROWS - 1 if rs <= kr < rs + kh else None
                             for kr in (2 * (s0 + p), 2 * (s0 + p) + 1))
                row.append(specs.setdefault(pair, len(specs)))
            blk.append(row)
        plan.append(blk)
    return tuple(starts), plan, list(specs)


def _nbr_table(rpb, specs):
    heads = rpb.shape[0]
    c = np.arange(GRID_W)
    cs = np.clip(c - NB_COLS // 2, 0, GRID_W - NB_COLS)
    col_ok = (c[None, :] >= cs[:, None]) & (c[None, :] < cs[:, None] + NB_COLS)
    edge = GRID_W - NB_COLS
    padded = jnp.pad(rpb.astype(F32), ((0, 0), (0, 0), (edge, edge)))
    shifted = jnp.stack([padded[:, :, GRID_W - 1 - qc:2 * GRID_W - 1 - qc] for qc in range(GRID_W)], axis=2)
    by_col = jnp.where(col_ok[None, None], shifted, NEG_INF)
    neg = jnp.full((heads, GRID_W, GRID_W), NEG_INF, F32)
    blocks = [jnp.concatenate([neg if d is None else by_col[:, d] for d in spec], axis=-1) for spec in specs]
    return jnp.stack(blocks, axis=1)


def _nbr_attn_kernel(q_ref, k_ref, v_ref, g_ref, kc_ref, vc_ref, tb_ref, gq_ref, gk_ref, bd_ref, y_ref,
                     *, starts, plan):
    nband = NBR_BAND * NBR_QB
    bd = bd_ref[...]
    qn = (_heads_rms(q_ref[...], bd, gq_ref[...]) * QK_SCALE).astype(BF16)
    kn = _heads_rms(k_ref[...], bd, gk_ref[...]).astype(BF16)
    vb = v_ref[...].astype(BF16)
    kcb = kc_ref[0, 0].astype(BF16)
    vcb = vc_ref[0, 0].astype(BF16)
    acc = [jnp.zeros((NBR_QB, LANE), F32) for _ in starts]
    for h in range(LANE // HEAD_DIM):
        hm = _head_lane_mask(LANE, h)
        km = jnp.where(hm, kn, jnp.zeros_like(kn))
        kcm = jnp.where(hm, kcb, jnp.zeros_like(kcb))
        vm = jnp.concatenate([jnp.where(hm, vb, jnp.zeros_like(vb)), jnp.ones_like(vb)], axis=1)
        vcm = jnp.concatenate([jnp.where(hm, vcb, jnp.zeros_like(vcb)), jnp.ones_like(vcb)], axis=1)
        for i, s0 in enumerate(starts):
            qi = qn[i * NBR_QB:(i + 1) * NBR_QB]
            ks = slice(s0 * NBR_QB, s0 * NBR_QB + nband)
            s_raw = lax.dot_general(qi, km[ks], _NT, preferred_element_type=F32)
            s_loc = jnp.concatenate(
                [jnp.concatenate([s_raw[a * GRID_W:(a + 1) * GRID_W, p * LANE:(p + 1) * LANE]
                                  + tb_ref[h, plan[i][a][p]] for p in range(NBR_BAND)], axis=1)
                 for a in range(2)], axis=0)
            s_ctx = lax.dot_general(qi, kcm, _NT, preferred_element_type=F32)
            m = jnp.maximum(jnp.max(s_loc, axis=-1, keepdims=True), jnp.max(s_ctx, axis=-1, keepdims=True))
            p_loc = jnp.exp(s_loc - m).astype(BF16)
            p_ctx = jnp.exp(s_ctx - m).astype(BF16)
            oe = (jnp.dot(p_loc, vm[ks], preferred_element_type=F32)
                  + jnp.dot(p_ctx, vcm, preferred_element_type=F32))
            acc[i] = acc[i] + oe[:, :LANE] / oe[:, LANE:]
    y = jnp.concatenate(acc, axis=0) * _silu(g_ref[...])
    y_ref[...] = y.astype(y_ref.dtype)


def _nbr_attn(proj, cache_k, cache_v, layer, rpb, gq, gk, bsz, t_len):
    hp = LANE // HEAD_DIM
    nhp = B_HEADS // hp
    p_len = cache_k.shape[2]
    rows = t_len // GRID_W
    assert rows % 2 == 0 and rows // 2 >= NBR_BAND
    starts, plan, specs = _nbr_plan(rows)
    table = _nbr_table(rpb, specs)
    const = lambda h, b: (0, 0)
    kern = functools.partial(_nbr_attn_kernel, starts=starts, plan=plan)
    return pl.pallas_call(
        kern,
        grid=(nhp, bsz),
        in_specs=[
            pl.BlockSpec((t_len, LANE), lambda h, b: (b, C_BQ + h)),
            pl.BlockSpec((t_len, LANE), lambda h, b: (b, C_BK + h)),
            pl.BlockSpec((t_len, LANE), lambda h, b: (b, C_BV + h)),
            pl.BlockSpec((t_len, LANE), lambda h, b: (b, C_BG + h)),
            pl.BlockSpec((1, 1, p_len, LANE), lambda h, b: (b, layer, 0, h)),
            pl.BlockSpec((1, 1, p_len, LANE), lambda h, b: (b, layer, 0, h)),
            pl.BlockSpec((hp, len(specs), GRID_W, 2 * GRID_W), lambda h, b: (h, 0, 0, 0)),
            pl.BlockSpec((1, LANE), const),
            pl.BlockSpec((1, LANE), const),
            pl.BlockSpec((LANE, LANE), const),
        ],
        out_specs=pl.BlockSpec((t_len, LANE), lambda h, b: (b, h)),
        out_shape=jax.ShapeDtypeStruct((bsz * t_len, BRANCH_W), BF16),
        compiler_params=_cparams(("arbitrary", "arbitrary")),
        name="nbr_attn",
    )(proj, proj, proj, proj, cache_k, cache_v, table,
      jnp.tile(gq, hp).reshape(1, LANE), jnp.tile(gk, hp).reshape(1, LANE), _head_mean_matrix(LANE))


def _lru_kernel(cx_ref, cg_ref, h0_ref, cw_ref, cb_ref, wg_ref, bg_ref, lam_ref,
                y_ref, st_ref, a_f, u_f, a_b, u_b):
    t_len = cx_ref.shape[0]
    w = LRU_WIDTH
    cx = cx_ref[...]
    row = lax.broadcasted_iota(jnp.int32, (t_len, w), 0)
    xc = cb_ref[...] + cx * cw_ref[2:3, :]
    xc = xc + jnp.where(row >= 2, pltpu.roll(cx, 2, axis=0), 0.0) * cw_ref[0:1, :]
    xc = xc + jnp.where(row >= 1, pltpu.roll(cx, 1, axis=0), 0.0) * cw_ref[1:2, :]
    xc = xc + jnp.where(row < t_len - 1, pltpu.roll(cx, t_len - 1, axis=0), 0.0) * cw_ref[3:4, :]

    gates = jnp.dot(xc.astype(BF16), wg_ref[...], preferred_element_type=F32) + bg_ref[...]
    for d, (a_s, u_s) in enumerate(((a_f, u_f), (a_b, u_b))):
        th_r = jnp.tanh(gates[:, d * w:(d + 1) * w])
        th_i = jnp.tanh(gates[:, (2 + d) * w:(3 + d) * w])
        nl = -lam_ref[d:d + 1, :]
        softplus = jnp.maximum(nl, 0.0) + jnp.log1p(jnp.exp(-jnp.abs(nl)))
        quarter_c = (-0.25 * LRU_C) * softplus
        half_log_a = quarter_c * th_r + quarter_c
        t = jnp.tanh(half_log_a)
        rc = 1.0 / (1.0 - t)
        a_s[...] = (1.0 + t) * rc
        u_s[...] = jnp.sqrt(-t) * rc * (1.0 + th_i) * xc

    n_grp = t_len // SUBLANE

    def body(it, carry):
        hf, hb = carry
        r0 = pl.multiple_of(it * SUBLANE, SUBLANE)
        av = a_f[pl.ds(r0, SUBLANE), :]
        uv = u_f[pl.ds(r0, SUBLANE), :]
        rows_f = []
        for s in range(SUBLANE):
            hf = av[s:s + 1, :] * hf + uv[s:s + 1, :]
            rows_f.append(hf)
        u_f[pl.ds(r0, SUBLANE), :] = jnp.concatenate(rows_f, axis=0)
        r1 = pl.multiple_of((n_grp - 1 - it) * SUBLANE, SUBLANE)
        av = a_b[pl.ds(r1, SUBLANE), :]
        uv = u_b[pl.ds(r1, SUBLANE), :]
        rows_b = [None] * SUBLANE
        for s in range(SUBLANE - 1, -1, -1):
            hb = av[s:s + 1, :] * hb + uv[s:s + 1, :]
            rows_b[s] = hb
        u_b[pl.ds(r1, SUBLANE), :] = jnp.concatenate(rows_b, axis=0)
        return hf, hb

    h0 = h0_ref[0]
    hf, hb = lax.fori_loop(0, n_grp, body, (h0[0:1, :], h0[1:2, :]))
    st_ref[0] = jnp.concatenate([hf, hb], axis=0)
    y = (u_f[...] + u_b[...]) * _silu(cg_ref[...])
    y_ref[...] = y.astype(y_ref.dtype)


def _lru(proj, h0, conv_w, conv_b, w_gates, b_gates, lam, bsz, t_len):
    w = LRU_WIDTH
    return pl.pallas_call(
        _lru_kernel,
        grid=(bsz,),
        in_specs=[
            pl.BlockSpec((t_len, w), lambda b: (b, C_CX // 4)),
            pl.BlockSpec((t_len, w), lambda b: (b, C_CG // 4)),
            pl.BlockSpec((1, 2, w), lambda b: (b, 0, 0)),
            pl.BlockSpec((CONV_W, w), lambda b: (0, 0)),
            pl.BlockSpec((1, w), lambda b: (0, 0)),
            pl.BlockSpec((w, 4 * w), lambda b: (0, 0)),
            pl.BlockSpec((1, 4 * w), lambda b: (0, 0)),
            pl.BlockSpec((2, w), lambda b: (0, 0)),
        ],
        out_specs=[
            pl.BlockSpec((t_len, w), lambda b: (b, 0)),
            pl.BlockSpec((1, 2, w), lambda b: (b, 0, 0)),
        ],
        out_shape=[
            jax.ShapeDtypeStruct((bsz * t_len, w), BF16),
            jax.ShapeDtypeStruct((bsz, 2, w), F32),
        ],
        scratch_shapes=[pltpu.VMEM((t_len, w), F32) for _ in range(4)],
        compiler_params=_cparams(("arbitrary",)),
        name="lru",
    )(proj, proj, h0, conv_w, conv_b.reshape(1, w), w_gates, b_gates, lam)


def _lru_gate_weights(wa, ba, wx, bx):
    def dense(wblk):
        eye = jnp.eye(LRU_BLOCKS, dtype=wblk.dtype)
        return jnp.einsum('nkj,nm->nkmj', wblk, eye).reshape(LRU_WIDTH, LRU_WIDTH)
    wg = jnp.concatenate([dense(wa[0]), dense(wa[1]), dense(wx[0]), dense(wx[1])], axis=1)
    bg = jnp.concatenate([ba[0], ba[1], bx[0], bx[1]]).reshape(1, 4 * LRU_WIDTH)
    return (0.5 * wg).astype(BF16), 0.5 * bg


def _merge_kernel(x_ref, ya_ref, yb_ref, yc_ref, ga_ref, gb_ref, gc_ref, mod_ref, wbr_ref, wout_ref, o_ref):
    z = (_sigmoid(ga_ref[...]) * jnp.dot(ya_ref[...], wbr_ref[0, 0], preferred_element_type=F32)
         + _sigmoid(gb_ref[...]) * jnp.dot(yb_ref[...], wbr_ref[0, 1], preferred_element_type=F32)
         + _sigmoid(gc_ref[...]) * jnp.dot(yc_ref[...], wbr_ref[0, 2], preferred_element_type=F32))
    out = jnp.dot(z.astype(BF16), wout_ref[0], preferred_element_type=F32)
    o_ref[...] = x_ref[...] + mod_ref[0, 0, 2:3, :] * out


def _merge(x2d, ya, yb, yc, proj, mods, w_br, w_out, layer, rows_per_mod, mod_row0):
    tokens = x2d.shape[0]
    tm = 512
    tiles_per_mod = rows_per_mod // tm if rows_per_mod else 0

    def mod_idx(i):
        if rows_per_mod:
            return (layer, mod_row0 + i // tiles_per_mod, 0, 0)
        return (layer, mod_row0, 0, 0)

    return pl.pallas_call(
        _merge_kernel,
        grid=(tokens // tm,),
        in_specs=[
            pl.BlockSpec((tm, D_MODEL), lambda i: (i, 0)),
            pl.BlockSpec((tm, BRANCH_W), lambda i: (i, 0)),
            pl.BlockSpec((tm, BRANCH_W), lambda i: (i, 0)),
            pl.BlockSpec((tm, BRANCH_W), lambda i: (i, 0)),
            pl.BlockSpec((tm, D_MODEL), lambda i: (i, C_GA // 8)),
            pl.BlockSpec((tm, D_MODEL), lambda i: (i, C_GB // 8)),
            pl.BlockSpec((tm, D_MODEL), lambda i: (i, C_GC // 8)),
            pl.BlockSpec((1, 1, 3, D_MODEL), mod_idx),
            pl.BlockSpec((1, 3, BRANCH_W, D_MODEL), lambda i: (layer, 0, 0, 0)),
            pl.BlockSpec((1, D_MODEL, D_MODEL), lambda i: (layer, 0, 0)),
        ],
        out_specs=pl.BlockSpec((tm, D_MODEL), lambda i: (i, 0)),
        out_shape=jax.ShapeDtypeStruct((tokens, D_MODEL), F32),
        compiler_params=_cparams(("arbitrary",)),
        name="merge",
    )(x2d, ya, yb, yc, proj, proj, proj, mods, w_br, w_out)


def _rope_tables(t_len):
    t = jnp.arange(t_len)
    m = HEAD_DIM // 4
    freqs = ROPE_BASE ** (-jnp.arange(m, dtype=F32) / m)
    ang_r = (t // GRID_W).astype(F32)[:, None] * freqs[None, :]
    ang_c = (t % GRID_W).astype(F32)[:, None] * freqs[None, :]
    cos = jnp.concatenate([jnp.cos(ang_r), jnp.cos(ang_r), jnp.cos(ang_c), jnp.cos(ang_c)], axis=-1)
    sin = jnp.concatenate([-jnp.sin(ang_r), jnp.sin(ang_r), -jnp.sin(ang_c), jnp.sin(ang_c)], axis=-1)
    reps = BRANCH_W // HEAD_DIM
    return jnp.tile(cos, (1, reps)), jnp.tile(sin, (1, reps))


def kernel(x_prompt, x_sample, cache_ka, cache_va, cache_kb, cache_vb, state_lru, c, c_ctx,
           norm_g, w_ada, b_ada, w_in, a_q_norm, a_k_norm, a_sink, b_q_norm, b_k_norm, b_rpb,
           lru_conv_w, lru_conv_b, lru_wa, lru_ba, lru_wx, lru_bx, lru_lambda, w_branch, w_out):
    bsz, s_len, _ = x_prompt.shape
    dbsz, t_len, _ = x_sample.shape
    p_len = cache_ka.shape[2]

    n_mod = 16
    ctx_row = dbsz
    cond = jnp.zeros((n_mod, D_MODEL), F32).at[:dbsz].set(c).at[ctx_row].set(c_ctx)
    mods = _mods(cond, w_ada, b_ada).reshape(DEPTH, n_mod, 3, D_MODEL)

    w_perm = jnp.concatenate(
        [w_in[:, :, _ORIG_OFFS[k]:_ORIG_OFFS[k] + _ORIG_SPLITS[k]] for k in _PERM], axis=-1).astype(BF16)
    w_br = w_branch.astype(BF16)
    w_o = w_out.astype(BF16)
    cos, sin_signed = _rope_tables(t_len)

    cka = cache_ka.reshape(dbsz, DEPTH, p_len, A_KV_HEADS * HEAD_DIM)
    cva = cache_va.reshape(dbsz, DEPTH, p_len, A_KV_HEADS * HEAD_DIM)
    ckb = cache_kb.reshape(dbsz, DEPTH, p_len, B_HEADS * HEAD_DIM)
    cvb = cache_vb.reshape(dbsz, DEPTH, p_len, B_HEADS * HEAD_DIM)

    yp = x_prompt.reshape(bsz * s_len, D_MODEL)
    ys = x_sample.reshape(dbsz * t_len, D_MODEL)
    zero_state = jnp.zeros((bsz, 2, LRU_WIDTH), F32)
    new_ka, new_va, new_kb, new_vb, new_lru = [], [], [], [], []
    for l in range(DEPTH):
        wg, bg = _lru_gate_weights(lru_wa[l], lru_ba[l], lru_wx[l], lru_bx[l])

        proj = _inproj(yp, mods, norm_g, w_perm, l, 0, ctx_row)
        ya, ka, va = _ctx_attn(proj, bsz, s_len, a_q_norm[l], a_k_norm[l], a_sink[l],
                               n_q=A_HEADS, n_kv=A_KV_HEADS, use_sink=True,
                               cq=C_AQ, ck=C_AK, cv=C_AV, cg=C_AG)
        yb, kb, vb = _ctx_attn(proj, bsz, s_len, b_q_norm[l], b_k_norm[l], jnp.zeros((B_HEADS,), F32),
                               n_q=B_HEADS, n_kv=B_HEADS, use_sink=False,
                               cq=C_BQ, ck=C_BK, cv=C_BV, cg=C_BG)
        yc, st = _lru(proj, zero_state, lru_conv_w[l], lru_conv_b[l], wg, bg, lru_lambda[l], bsz, s_len)
        yp = _merge(yp, ya, yb, yc, proj, mods, w_br, w_o, l, 0, ctx_row)
        new_ka.append(ka.reshape(bsz, s_len, A_KV_HEADS, HEAD_DIM))
        new_va.append(va.reshape(bsz, s_len, A_KV_HEADS, HEAD_DIM))
        new_kb.append(kb.reshape(bsz, s_len, B_HEADS, HEAD_DIM))
        new_vb.append(vb.reshape(bsz, s_len, B_HEADS, HEAD_DIM))
        new_lru.append(st)

        proj = _inproj(ys, mods, norm_g, w_perm, l, t_len, 0)
        ya = _win_attn(proj, cka, cva, l, cos, sin_signed, a_q_norm[l], a_k_norm[l], a_sink[l], dbsz, t_len)
        yb = _nbr_attn(proj, ckb, cvb, l, b_rpb[l], b_q_norm[l], b_k_norm[l], dbsz, t_len)
        yc, _ = _lru(proj, state_lru[:, l], lru_conv_w[l], lru_conv_b[l], wg, bg, lru_lambda[l], dbsz, t_len)
        ys = _merge(ys, ya, yb, yc, proj, mods, w_br, w_o, l, t_len, 0)

    return (yp.reshape(bsz, s_len, D_MODEL), ys.reshape(dbsz, t_len, D_MODEL),
            jnp.stack(new_ka, axis=1), jnp.stack(new_va, axis=1),
            jnp.stack(new_kb, axis=1), jnp.stack(new_vb, axis=1),
            jnp.stack(new_lru, axis=1))
```

```python
import functools

import numpy as np
import jax
import jax.numpy as jnp
from jax import lax
from jax.experimental import pallas as pl
from jax.experimental.pallas import tpu as pltpu

F32 = jnp.float32
BF16 = jnp.bfloat16

D_MODEL = 1024
DEPTH = 2
GRID_W = 64
HEAD_DIM = 64
BRANCH_W = 512
A_HEADS = 8
A_KV_HEADS = 2
A_WINDOW = 128
A_BLOCK = 128
B_HEADS = 8
NB_ROWS = 8
NB_COLS = 16
LRU_WIDTH = 512
LRU_BLOCKS = 8
LRU_BW = LRU_WIDTH // LRU_BLOCKS
LRU_C = 8.0
CONV_W = 4
ROPE_BASE = 10000.0
EPS = 1e-6
NEG_INF = -1e30
QK_SCALE = HEAD_DIM ** -0.5

LANE = 128
SUBLANE = 8
MXU_DIM = 256
VMEM_LIMIT = 56 * 1024 * 1024

_ORIG_SPLITS = (512, 128, 128, 512, 512, 512, 512, 512, 512, 512, 1024, 1024, 1024)
_ORIG_OFFS = tuple(int(v) for v in np.cumsum((0,) + _ORIG_SPLITS)[:-1])
_PERM = (0, 3, 4, 5, 6, 7, 8, 9, 1, 2)
_GATE_SPLITS = (10, 11, 12)
IN_COLS = sum(_ORIG_SPLITS[k] for k in _PERM)
GATE_COLS = sum(_ORIG_SPLITS[k] for k in _GATE_SPLITS)
C_AQ, C_AG, C_BQ, C_BK, C_BV, C_BG, C_CX, C_CG, C_AK, C_AV = (0, 4, 8, 12, 16, 20, 24, 28, 32, 33)

_NT = (((1,), (1,)), ((), ()))


def _cparams(sem):
    return pltpu.CompilerParams(dimension_semantics=sem, vmem_limit_bytes=VMEM_LIMIT)


def _sigmoid(x):
    return 0.5 + 0.5 * jnp.tanh(0.5 * x)


def _silu(x):
    return x * _sigmoid(x)


def _head_mean_matrix(width):
    idx = np.arange(width) // HEAD_DIM
    return jnp.asarray((idx[:, None] == idx[None, :]).astype(np.float32) / HEAD_DIM, dtype=BF16)


def _heads_rms(x, bd, g):
    x2 = x * x
    hi = x2.astype(BF16)
    lo = (x2 - hi.astype(F32)).astype(BF16)
    ms = jnp.dot(hi, bd, preferred_element_type=F32) + jnp.dot(lo, bd, preferred_element_type=F32)
    return x * lax.rsqrt(ms + EPS) * g


def _head_lane_mask(width, h):
    lane = lax.broadcasted_iota(jnp.int32, (1, width), 1)
    return (lane >= h * HEAD_DIM) & (lane < (h + 1) * HEAD_DIM)


def _mods_kernel(c_ref, w_ref, b_ref, o_ref):
    c = c_ref[...]
    s = _silu(c).astype(BF16)
    o_ref[0] = jnp.dot(s, w_ref[0].astype(BF16), preferred_element_type=F32) + b_ref[0]


def _mods(cond, w_ada, b_ada):
    n = cond.shape[0]
    tn = D_MODEL
    return pl.pallas_call(
        _mods_kernel,
        grid=(DEPTH, 3 * D_MODEL // tn),
        in_specs=[
            pl.BlockSpec((n, D_MODEL), lambda l, j: (0, 0)),
            pl.BlockSpec((1, D_MODEL, tn), lambda l, j: (l, 0, j)),
            pl.BlockSpec((1, 1, tn), lambda l, j: (l, 0, j)),
        ],
        out_specs=pl.BlockSpec((1, n, tn), lambda l, j: (l, 0, j)),
        out_shape=jax.ShapeDtypeStruct((DEPTH, n, 3 * D_MODEL), F32),
        compiler_params=_cparams(("arbitrary", "arbitrary")),
        name="mods",
    )(cond, w_ada, b_ada.reshape(DEPTH, 1, 3 * D_MODEL))


def _modulated_norm(x, g_ref, mod_ref):
    y = x * lax.rsqrt(jnp.mean(x * x, axis=-1, keepdims=True) + EPS)
    y = y * g_ref[0]
    shift = mod_ref[0, 0, 0:1, :]
    scale = mod_ref[0, 0, 1:2, :]
    return (y * (1.0 + scale) + shift).astype(BF16)


def _inproj_kernel(x_ref, mod_ref, g_ref, w_ref, o_ref):
    h = _modulated_norm(x_ref[...], g_ref, mod_ref)
    o_ref[...] = jnp.dot(h, w_ref[0], preferred_element_type=F32)


def _inproj(x2d, mods, norm_g, w_perm, layer, rows_per_mod, mod_row0):
    tokens = x2d.shape[0]
    tm = 512
    tiles_per_mod = rows_per_mod // tm if rows_per_mod else 0

    def mod_idx(i):
        if rows_per_mod:
            return (layer, mod_row0 + i // tiles_per_mod, 0, 0)
        return (layer, mod_row0, 0, 0)

    return pl.pallas_call(
        _inproj_kernel,
        grid=(tokens // tm,),
        in_specs=[
            pl.BlockSpec((tm, D_MODEL), lambda i: (i, 0)),
            pl.BlockSpec((1, 1, 3, D_MODEL), mod_idx),
            pl.BlockSpec((1, 1, D_MODEL), lambda i: (layer, 0, 0)),
            pl.BlockSpec((1, D_MODEL, IN_COLS), lambda i: (layer, 0, 0)),
        ],
        out_specs=pl.BlockSpec((tm, IN_COLS), lambda i: (i, 0)),
        out_shape=jax.ShapeDtypeStruct((tokens, IN_COLS), F32),
        compiler_params=_cparams(("arbitrary",)),
        name="inproj",
    )(x2d, mods, norm_g.reshape(DEPTH, 1, D_MODEL), w_perm)


CTX_SEQS = 2


def _ctx_mixer(q, k, v, g, gq, gk, bdq, bdk, ones_stack, expand=None, sink_ref=None):
    s_len = q.shape[0]
    n_q = q.shape[1] // HEAD_DIM
    kn = _heads_rms(k, bdk, gk)
    qn = (_heads_rms(q, bdq, gq) * QK_SCALE).astype(BF16)
    knb = kn.astype(BF16)
    vb = v.astype(BF16)
    if expand is not None:
        knb = jnp.dot(knb, expand, preferred_element_type=F32).astype(BF16)
        vb = jnp.dot(vb, expand, preferred_element_type=F32).astype(BF16)
    heads_per_half = MXU_DIM // HEAD_DIM
    masks = [_head_lane_mask(MXU_DIM, hi) for hi in range(heads_per_half)]
    halves = []
    for hh in range(n_q // heads_per_half):
        cols = slice(hh * MXU_DIM, (hh + 1) * MXU_DIM)
        qh, kh, vh = qn[:, cols], knb[:, cols], vb[:, cols]
        kstack = jnp.concatenate([jnp.where(mk, kh, jnp.zeros_like(kh)) for mk in masks], axis=0)
        vstack = jnp.concatenate([jnp.where(mk, vh, jnp.zeros_like(vh)) for mk in masks], axis=0)
        s = lax.dot_general(qh, kstack, _NT, preferred_element_type=F32)
        ps, sink_terms = [], []
        for hi in range(heads_per_half):
            si = s[:, hi * s_len:(hi + 1) * s_len]
            m = jnp.max(si, axis=-1, keepdims=True)
            if sink_ref is not None:
                h = hh * heads_per_half + hi
                snk = sink_ref[0:1, h:h + 1]
                m = jnp.maximum(m, snk)
                sink_terms.append(jnp.where(masks[hi], jnp.exp(snk - m), 0.0))
            ps.append(jnp.exp(si - m).astype(BF16))
        p = jnp.concatenate(ps, axis=1)
        rhs = jnp.concatenate([vstack, ones_stack], axis=1)
        oe = jnp.dot(p, rhs, preferred_element_type=F32)
        l = oe[:, MXU_DIM:]
        for term in sink_terms:
            l = l + term
        halves.append(oe[:, :MXU_DIM] / l)
    return jnp.concatenate(halves, axis=-1) * _silu(g), kn


def _ctx_attn_kernel(aq_ref, ak_ref, av_ref, ag_ref, bq_ref, bk_ref, bv_ref, bg_ref,
                     gqa_ref, gka_ref, sink_ref, gqb_ref, gkb_ref, bdw_ref, bdn_ref, exp_ref, ones_ref,
                     ya_ref, yb_ref, ka_ref, va_ref, kb_ref, vb_ref, *, s_len):
    ones_stack = ones_ref[...]
    bdw = bdw_ref[...]
    for n in range(aq_ref.shape[0] // s_len):
        rows = slice(n * s_len, (n + 1) * s_len)
        va = av_ref[rows, :]
        vb = bv_ref[rows, :]
        va_ref[rows, :] = va
        vb_ref[rows, :] = vb
        ya, kna = _ctx_mixer(aq_ref[rows, :], ak_ref[rows, :], va, ag_ref[rows, :], gqa_ref[...], gka_ref[...],
                             bdw, bdn_ref[...], ones_stack, expand=exp_ref[...], sink_ref=sink_ref)
        yb, knb = _ctx_mixer(bq_ref[rows, :], bk_ref[rows, :], vb, bg_ref[rows, :], gqb_ref[...], gkb_ref[...],
                             bdw, bdw, ones_stack)
        ka_ref[rows, :] = kna
        kb_ref[rows, :] = knb
        ya_ref[rows, :] = ya.astype(ya_ref.dtype)
        yb_ref[rows, :] = yb.astype(yb_ref.dtype)


def _ctx_attn(proj, bsz, s_len, a_gq, a_gk, a_sink, b_gq, b_gk):
    kvw = A_KV_HEADS * HEAD_DIM
    grp = A_HEADS // A_KV_HEADS
    expand = np.zeros((kvw, BRANCH_W), np.float32)
    for h in range(A_HEADS):
        for d in range(HEAD_DIM):
            expand[(h // grp) * HEAD_DIM + d, h * HEAD_DIM + d] = 1.0
    heads_per_half = MXU_DIM // HEAD_DIM
    ones_stack = np.zeros((heads_per_half * s_len, MXU_DIM), np.float32)
    for hi in range(heads_per_half):
        ones_stack[hi * s_len:(hi + 1) * s_len, hi * HEAD_DIM:(hi + 1) * HEAD_DIM] = 1.0
    const = lambda b: (0, 0)
    rows = CTX_SEQS * s_len
    wide = lambda c: pl.BlockSpec((rows, BRANCH_W), lambda b: (b, c // 4))
    narrow = lambda c: pl.BlockSpec((rows, kvw), lambda b: (b, c))
    return pl.pallas_call(
        functools.partial(_ctx_attn_kernel, s_len=s_len),
        grid=(bsz // CTX_SEQS,),
        in_specs=[
            wide(C_AQ), narrow(C_AK), narrow(C_AV), wide(C_AG), wide(C_BQ), wide(C_BK), wide(C_BV), wide(C_BG),
            pl.BlockSpec((1, BRANCH_W), const),
            pl.BlockSpec((1, kvw), const),
            pl.BlockSpec((1, A_HEADS), const),
            pl.BlockSpec((1, BRANCH_W), const),
            pl.BlockSpec((1, BRANCH_W), const),
            pl.BlockSpec((BRANCH_W, BRANCH_W), const),
            pl.BlockSpec((kvw, kvw), const),
            pl.BlockSpec((kvw, BRANCH_W), const),
            pl.BlockSpec((heads_per_half * s_len, MXU_DIM), const),
        ],
        out_specs=[
            pl.BlockSpec((rows, BRANCH_W), lambda b: (b, 0)),
            pl.BlockSpec((rows, BRANCH_W), lambda b: (b, 0)),
            pl.BlockSpec((rows, kvw), lambda b: (b, 0)),
            pl.BlockSpec((rows, kvw), lambda b: (b, 0)),
            pl.BlockSpec((rows, BRANCH_W), lambda b: (b, 0)),
            pl.BlockSpec((rows, BRANCH_W), lambda b: (b, 0)),
        ],
        out_shape=[
            jax.ShapeDtypeStruct((bsz * s_len, BRANCH_W), BF16),
            jax.ShapeDtypeStruct((bsz * s_len, BRANCH_W), BF16),
            jax.ShapeDtypeStruct((bsz * s_len, kvw), F32),
            jax.ShapeDtypeStruct((bsz * s_len, kvw), F32),
            jax.ShapeDtypeStruct((bsz * s_len, BRANCH_W), F32),
            jax.ShapeDtypeStruct((bsz * s_len, BRANCH_W), F32),
        ],
        compiler_params=_cparams(("arbitrary",)),
        name="ctx_attn",
    )(proj, proj, proj, proj, proj, proj, proj, proj,
      jnp.tile(a_gq, A_HEADS).reshape(1, BRANCH_W), jnp.tile(a_gk, A_KV_HEADS).reshape(1, kvw),
      a_sink.reshape(1, A_HEADS),
      jnp.tile(b_gq, B_HEADS).reshape(1, BRANCH_W), jnp.tile(b_gk, B_HEADS).reshape(1, BRANCH_W),
      _head_mean_matrix(BRANCH_W), _head_mean_matrix(kvw), jnp.asarray(expand, dtype=BF16),
      jnp.asarray(ones_stack, dtype=BF16))


def _rope(x, cos, sin_signed):
    w = x.shape[-1]
    lane = lax.broadcasted_iota(jnp.int32, x.shape, 1)
    up = pltpu.roll(x, w - 16, axis=1)
    dn = pltpu.roll(x, 16, axis=1)
    partner = jnp.where((lane & 16) == 0, up, dn)
    return x * cos + partner * sin_signed


def _win_attn_kernel(q_ref, k_ref, v_ref, g_ref, kc_ref, vc_ref, cos_ref, sin_ref, gq_ref, gk_ref, sink_ref,
                     bdq_ref, bdk_ref, dup_ref, mask_ref, y_ref, kpad, vpad, kcx, vcx):
    j = pl.program_id(1)
    nb = pl.num_programs(1)
    t_len = k_ref.shape[0]
    grp = A_HEADS // A_KV_HEADS
    kvw = A_KV_HEADS * HEAD_DIM
    xw = 2 * kvw

    @pl.when(j == 0)
    def _():
        dup = dup_ref[...]
        kn = _heads_rms(k_ref[...], bdk_ref[...], gk_ref[...])
        kn = _rope(kn, cos_ref[:, 0:kvw], sin_ref[:, 0:kvw]).astype(BF16)
        zeros = jnp.zeros((A_BLOCK, xw), BF16)
        kpad[0:A_BLOCK, :] = zeros
        kpad[A_BLOCK + t_len:2 * A_BLOCK + t_len, :] = zeros
        vpad[0:A_BLOCK, :] = zeros
        vpad[A_BLOCK + t_len:2 * A_BLOCK + t_len, :] = zeros
        kpad[A_BLOCK:A_BLOCK + t_len, :] = jnp.dot(kn, dup, preferred_element_type=F32).astype(BF16)
        vpad[A_BLOCK:A_BLOCK + t_len, :] = jnp.dot(
            v_ref[...].astype(BF16), dup, preferred_element_type=F32).astype(BF16)
        kcx[...] = jnp.dot(kc_ref[0, 0].astype(BF16), dup, preferred_element_type=F32).astype(BF16)
        vcx[...] = jnp.dot(vc_ref[0, 0].astype(BF16), dup, preferred_element_type=F32).astype(BF16)

    r0 = pl.multiple_of(j * A_BLOCK, A_BLOCK)
    qn = _heads_rms(q_ref[...], bdq_ref[...], gq_ref[...])
    qb = (_rope(qn, cos_ref[pl.ds(r0, A_BLOCK), :], sin_ref[pl.ds(r0, A_BLOCK), :]) * QK_SCALE).astype(BF16)

    nloc = 3 * A_BLOCK
    maskadd = mask_ref[jnp.where(j == 0, 0, jnp.where(j == nb - 1, 2, 1))]
    low_half = lax.broadcasted_iota(jnp.int32, (1, LANE), 1) < HEAD_DIM
    kband = kpad[pl.ds(r0, nloc), :]
    vband = vpad[pl.ds(r0, nloc), :]
    pairs = []
    for kv in range(A_KV_HEADS):
        cols = slice(kv * LANE, (kv + 1) * LANE)
        qparts, sinks = [], []
        for gi in range(grp):
            h = kv * grp + gi
            qpair = qb[:, (h // 2) * LANE:(h // 2 + 1) * LANE]
            keep = low_half if h % 2 == 0 else jnp.logical_not(low_half)
            qparts.append(jnp.where(keep, qpair, jnp.zeros_like(qpair)))
            sinks.append(jnp.broadcast_to(sink_ref[0:1, h:h + 1], (A_BLOCK, 1)))
        qst = jnp.concatenate(qparts, axis=0)
        snk = jnp.concatenate(sinks, axis=0)
        s_loc = lax.dot_general(qst, kband[:, cols], _NT, preferred_element_type=F32) + maskadd
        s_ctx = lax.dot_general(qst, kcx[:, cols], _NT, preferred_element_type=F32)
        m = jnp.maximum(jnp.maximum(jnp.max(s_loc, axis=-1, keepdims=True),
                                    jnp.max(s_ctx, axis=-1, keepdims=True)), snk)
        p_loc = jnp.exp(s_loc - m)
        p_ctx = jnp.exp(s_ctx - m)
        l = (jnp.sum(p_loc, axis=-1, keepdims=True) + jnp.sum(p_ctx, axis=-1, keepdims=True)
             + jnp.exp(snk - m))
        o = (jnp.dot(p_loc.astype(BF16), vband[:, cols], preferred_element_type=F32)
             + jnp.dot(p_ctx.astype(BF16), vcx[:, cols], preferred_element_type=F32)) / l
        for k2 in range(grp // 2):
            even = o[(2 * k2) * A_BLOCK:(2 * k2 + 1) * A_BLOCK]
            odd = o[(2 * k2 + 1) * A_BLOCK:(2 * k2 + 2) * A_BLOCK]
            pairs.append(jnp.where(low_half, even, odd))
    y = jnp.concatenate(pairs, axis=-1) * _silu(g_ref[...])
    y_ref[...] = y.astype(y_ref.dtype)


def _win_mask(grp, nb):
    assert nb >= 2
    r = np.arange(A_BLOCK)[:, None]
    c = np.arange(3 * A_BLOCK)[None, :]
    band = np.abs(r + A_BLOCK - c) <= A_WINDOW
    variants = [band & (c >= A_BLOCK), band, band & (c < 2 * A_BLOCK)]
    return np.stack([np.tile(np.where(v, 0.0, NEG_INF).astype(np.float32), (grp, 1)) for v in variants])


def _win_attn(proj, cache_k, cache_v, layer, cos, sin_signed, gq, gk, sink, bsz, t_len):
    nb = t_len // A_BLOCK
    kvw = A_KV_HEADS * HEAD_DIM
    grp = A_HEADS // A_KV_HEADS
    p_len = cache_k.shape[2]
    dup = np.zeros((kvw, 2 * kvw), np.float32)
    for kv in range(A_KV_HEADS):
        for half in range(2):
            for d in range(HEAD_DIM):
                dup[kv * HEAD_DIM + d, kv * LANE + half * HEAD_DIM + d] = 1.0
    const2 = lambda b, j: (0, 0)
    return pl.pallas_call(
        _win_attn_kernel,
        grid=(bsz, nb),
        in_specs=[
            pl.BlockSpec((A_BLOCK, BRANCH_W), lambda b, j: (b * nb + j, C_AQ // 4)),
            pl.BlockSpec((t_len, kvw), lambda b, j: (b, C_AK)),
            pl.BlockSpec((t_len, kvw), lambda b, j: (b, C_AV)),
            pl.BlockSpec((A_BLOCK, BRANCH_W), lambda b, j: (b * nb + j, C_AG // 4)),
            pl.BlockSpec((1, 1, p_len, kvw), lambda b, j: (b, layer, 0, 0)),
            pl.BlockSpec((1, 1, p_len, kvw), lambda b, j: (b, layer, 0, 0)),
            pl.BlockSpec((t_len, BRANCH_W), const2),
            pl.BlockSpec((t_len, BRANCH_W), const2),
            pl.BlockSpec((1, BRANCH_W), const2),
            pl.BlockSpec((1, kvw), const2),
            pl.BlockSpec((1, A_HEADS), const2),
            pl.BlockSpec((BRANCH_W, BRANCH_W), const2),
            pl.BlockSpec((kvw, kvw), const2),
            pl.BlockSpec((kvw, 2 * kvw), const2),
            pl.BlockSpec((3, grp * A_BLOCK, 3 * A_BLOCK), lambda b, j: (0, 0, 0)),
        ],
        out_specs=pl.BlockSpec((A_BLOCK, BRANCH_W), lambda b, j: (b * nb + j, 0)),
        out_shape=jax.ShapeDtypeStruct((bsz * t_len, BRANCH_W), BF16),
        scratch_shapes=[pltpu.VMEM((t_len + 2 * A_BLOCK, 2 * kvw), BF16),
                        pltpu.VMEM((t_len + 2 * A_BLOCK, 2 * kvw), BF16),
                        pltpu.VMEM((p_len, 2 * kvw), BF16),
                        pltpu.VMEM((p_len, 2 * kvw), BF16)],
        compiler_params=_cparams(("arbitrary", "arbitrary")),
        name="win_attn",
    )(proj, proj, proj, proj, cache_k, cache_v, cos, sin_signed,
      jnp.tile(gq, A_HEADS).reshape(1, BRANCH_W), jnp.tile(gk, A_KV_HEADS).reshape(1, kvw),
      sink.reshape(1, A_HEADS), _head_mean_matrix(BRANCH_W), _head_mean_matrix(kvw),
      jnp.asarray(dup, dtype=BF16), jnp.asarray(_win_mask(grp, nb)))


NBR_QB = 2 * GRID_W
NBR_BAND = 5


def _nbr_plan(rows):
    kh = min(NB_ROWS, rows)
    nblk = rows // 2
    specs, plan, starts = {}, [], []
    for i in range(nblk):
        s0 = min(max(i - 2, 0), nblk - NBR_BAND)
        starts.append(s0)
        blk = []
        for a in range(2):
            qr = 2 * i + a
            rs = min(max(qr - kh // 2, 0), rows - kh)
            assert 2 * s0 <= rs and rs + kh <= 2 * (s0 + NBR_BAND)
            row = []
            for p in range(NBR_BAND):
                pair = tuple(kr - qr + NB_ROWS - 1 if rs <= kr < rs + kh else None
                             for kr in (2 * (s0 + p), 2 * (s0 + p) + 1))
                row.append(specs.setdefault(pair, len(specs)))
            blk.append(row)
        plan.append(blk)
    return tuple(starts), plan, list(specs)


def _nbr_table(rpb, specs):
    heads = rpb.shape[0]
    c = np.arange(GRID_W)
    cs = np.clip(c - NB_COLS // 2, 0, GRID_W - NB_COLS)
    col_ok = (c[None, :] >= cs[:, None]) & (c[None, :] < cs[:, None] + NB_COLS)
    edge = GRID_W - NB_COLS
    period = 2 * GRID_W - 1
    padded = jnp.pad(rpb.astype(F32), ((0, 0), (0, 0), (edge, edge)))
    tiled = jnp.tile(padded, (1, 1, GRID_W + 1))[:, :, :GRID_W * (period + 1)]
    skew = tiled.reshape(heads, 2 * NB_ROWS - 1, GRID_W, period + 1)
    shifted = skew[:, :, ::-1, :GRID_W]
    by_col = jnp.where(col_ok[None, None], shifted, NEG_INF)
    neg = jnp.full((heads, GRID_W, GRID_W), NEG_INF, F32)
    blocks = [jnp.concatenate([neg if d is None else by_col[:, d] for d in spec], axis=-1) for spec in specs]
    return jnp.stack(blocks, axis=1)


def _nbr_attn_kernel(q_ref, k_ref, v_ref, g_ref, kc_ref, vc_ref, tb_ref, gq_ref, gk_ref, bd_ref, y_ref,
                     *, starts, plan):
    nband = NBR_BAND * NBR_QB
    bd = bd_ref[...]
    qn = (_heads_rms(q_ref[...], bd, gq_ref[...]) * QK_SCALE).astype(BF16)
    kn = _heads_rms(k_ref[...], bd, gk_ref[...]).astype(BF16)
    vb = v_ref[...].astype(BF16)
    kcb = kc_ref[0, 0].astype(BF16)
    vcb = vc_ref[0, 0].astype(BF16)
    acc = [jnp.zeros((NBR_QB, LANE), F32) for _ in starts]
    for h in range(LANE // HEAD_DIM):
        hm = _head_lane_mask(LANE, h)
        km = jnp.where(hm, kn, jnp.zeros_like(kn))
        kcm = jnp.where(hm, kcb, jnp.zeros_like(kcb))
        vm = jnp.concatenate([jnp.where(hm, vb, jnp.zeros_like(vb)), jnp.ones_like(vb)], axis=1)
        vcm = jnp.concatenate([jnp.where(hm, vcb, jnp.zeros_like(vcb)), jnp.ones_like(vcb)], axis=1)
        for i, s0 in enumerate(starts):
            qi = qn[i * NBR_QB:(i + 1) * NBR_QB]
            ks = slice(s0 * NBR_QB, s0 * NBR_QB + nband)
            s_raw = lax.dot_general(qi, km[ks], _NT, preferred_element_type=F32)
            s_loc = jnp.concatenate(
                [jnp.concatenate([s_raw[a * GRID_W:(a + 1) * GRID_W, p * LANE:(p + 1) * LANE]
                                  + tb_ref[h, plan[i][a][p]] for p in range(NBR_BAND)], axis=1)
                 for a in range(2)], axis=0)
            s_ctx = lax.dot_general(qi, kcm, _NT, preferred_element_type=F32)
            m = jnp.maximum(jnp.max(s_loc, axis=-1, keepdims=True), jnp.max(s_ctx, axis=-1, keepdims=True))
            p_loc = jnp.exp(s_loc - m).astype(BF16)
            p_ctx = jnp.exp(s_ctx - m).astype(BF16)
            oe = (jnp.dot(p_loc, vm[ks], preferred_element_type=F32)
                  + jnp.dot(p_ctx, vcm, preferred_element_type=F32))
            acc[i] = acc[i] + oe[:, :LANE] / oe[:, LANE:]
    y = jnp.concatenate(acc, axis=0) * _silu(g_ref[...])
    y_ref[...] = y.astype(y_ref.dtype)


def _nbr_attn(proj, cache_k, cache_v, layer, rpb, gq, gk, bsz, t_len):
    hp = LANE // HEAD_DIM
    nhp = B_HEADS // hp
    p_len = cache_k.shape[2]
    rows = t_len // GRID_W
    assert rows % 2 == 0 and rows // 2 >= NBR_BAND
    starts, plan, specs = _nbr_plan(rows)
    table = _nbr_table(rpb, specs)
    const = lambda h, b: (0, 0)
    kern = functools.partial(_nbr_attn_kernel, starts=starts, plan=plan)
    return pl.pallas_call(
        kern,
        grid=(nhp, bsz),
        in_specs=[
            pl.BlockSpec((t_len, LANE), lambda h, b: (b, C_BQ + h)),
            pl.BlockSpec((t_len, LANE), lambda h, b: (b, C_BK + h)),
            pl.BlockSpec((t_len, LANE), lambda h, b: (b, C_BV + h)),
            pl.BlockSpec((t_len, LANE), lambda h, b: (b, C_BG + h)),
            pl.BlockSpec((1, 1, p_len, LANE), lambda h, b: (b, layer, 0, h)),
            pl.BlockSpec((1, 1, p_len, LANE), lambda h, b: (b, layer, 0, h)),
            pl.BlockSpec((hp, len(specs), GRID_W, 2 * GRID_W), lambda h, b: (h, 0, 0, 0)),
            pl.BlockSpec((1, LANE), const),
            pl.BlockSpec((1, LANE), const),
            pl.BlockSpec((LANE, LANE), const),
        ],
        out_specs=pl.BlockSpec((t_len, LANE), lambda h, b: (b, h)),
        out_shape=jax.ShapeDtypeStruct((bsz * t_len, BRANCH_W), BF16),
        compiler_params=_cparams(("arbitrary", "arbitrary")),
        name="nbr_attn",
    )(proj, proj, proj, proj, cache_k, cache_v, table,
      jnp.tile(gq, hp).reshape(1, LANE), jnp.tile(gk, hp).reshape(1, LANE), _head_mean_matrix(LANE))


def _lru_kernel(cx_ref, cg_ref, h0_ref, cw_ref, cb_ref, wg_ref, bg_ref, lam_ref,
                y_ref, st_ref, a_f, u_f, a_b, u_b):
    t_len = cx_ref.shape[0]
    w = LRU_WIDTH
    cx = cx_ref[...]
    row = lax.broadcasted_iota(jnp.int32, (t_len, w), 0)
    xc = cb_ref[...] + cx * cw_ref[2:3, :]
    xc = xc + jnp.where(row >= 2, pltpu.roll(cx, 2, axis=0), 0.0) * cw_ref[0:1, :]
    xc = xc + jnp.where(row >= 1, pltpu.roll(cx, 1, axis=0), 0.0) * cw_ref[1:2, :]
    xc = xc + jnp.where(row < t_len - 1, pltpu.roll(cx, t_len - 1, axis=0), 0.0) * cw_ref[3:4, :]

    gates = jnp.dot(xc.astype(BF16), wg_ref[...], preferred_element_type=F32) + bg_ref[...]
    for d, (a_s, u_s) in enumerate(((a_f, u_f), (a_b, u_b))):
        th_r = jnp.tanh(gates[:, d * w:(d + 1) * w])
        th_i = jnp.tanh(gates[:, (2 + d) * w:(3 + d) * w])
        nl = -lam_ref[d:d + 1, :]
        softplus = jnp.maximum(nl, 0.0) + jnp.log1p(jnp.exp(-jnp.abs(nl)))
        quarter_c = (-0.25 * LRU_C) * softplus
        half_log_a = quarter_c * th_r + quarter_c
        t = jnp.tanh(half_log_a)
        rc = 1.0 / (1.0 - t)
        a_s[...] = (1.0 + t) * rc
        u_s[...] = jnp.sqrt(-t) * rc * (1.0 + th_i) * xc

    n_grp = t_len // SUBLANE

    def body(it, carry):
        hf, hb = carry
        r0 = pl.multiple_of(it * SUBLANE, SUBLANE)
        av = a_f[pl.ds(r0, SUBLANE), :]
        uv = u_f[pl.ds(r0, SUBLANE), :]
        rows_f = []
        for s in range(SUBLANE):
            hf = av[s:s + 1, :] * hf + uv[s:s + 1, :]
            rows_f.append(hf)
        u_f[pl.ds(r0, SUBLANE), :] = jnp.concatenate(rows_f, axis=0)
        r1 = pl.multiple_of((n_grp - 1 - it) * SUBLANE, SUBLANE)
        av = a_b[pl.ds(r1, SUBLANE), :]
        uv = u_b[pl.ds(r1, SUBLANE), :]
        rows_b = [None] * SUBLANE
        for s in range(SUBLANE - 1, -1, -1):
            hb = av[s:s + 1, :] * hb + uv[s:s + 1, :]
            rows_b[s] = hb
        u_b[pl.ds(r1, SUBLANE), :] = jnp.concatenate(rows_b, axis=0)
        return hf, hb

    h0 = h0_ref[0]
    hf, hb = lax.fori_loop(0, n_grp, body, (h0[0:1, :], h0[1:2, :]))
    st_ref[0] = jnp.concatenate([hf, hb], axis=0)
    y = (u_f[...] + u_b[...]) * _silu(cg_ref[...])
    y_ref[...] = y.astype(y_ref.dtype)


def _lru(proj, h0, conv_w, conv_b, w_gates, b_gates, lam, bsz, t_len):
    w = LRU_WIDTH
    return pl.pallas_call(
        _lru_kernel,
        grid=(bsz,),
        in_specs=[
            pl.BlockSpec((t_len, w), lambda b: (b, C_CX // 4)),
            pl.BlockSpec((t_len, w), lambda b: (b, C_CG // 4)),
            pl.BlockSpec((1, 2, w), lambda b: (b, 0, 0)),
            pl.BlockSpec((CONV_W, w), lambda b: (0, 0)),
            pl.BlockSpec((1, w), lambda b: (0, 0)),
            pl.BlockSpec((w, 4 * w), lambda b: (0, 0)),
            pl.BlockSpec((1, 4 * w), lambda b: (0, 0)),
            pl.BlockSpec((2, w), lambda b: (0, 0)),
        ],
        out_specs=[
            pl.BlockSpec((t_len, w), lambda b: (b, 0)),
            pl.BlockSpec((1, 2, w), lambda b: (b, 0, 0)),
        ],
        out_shape=[
            jax.ShapeDtypeStruct((bsz * t_len, w), BF16),
            jax.ShapeDtypeStruct((bsz, 2, w), F32),
        ],
        scratch_shapes=[pltpu.VMEM((t_len, w), F32) for _ in range(4)],
        compiler_params=_cparams(("arbitrary",)),
        name="lru",
    )(proj, proj, h0, conv_w, conv_b.reshape(1, w), w_gates, b_gates, lam)


def _lru_gate_weights(wa, ba, wx, bx):
    def dense(wblk):
        eye = jnp.eye(LRU_BLOCKS, dtype=wblk.dtype)
        return jnp.einsum('nkj,nm->nkmj', wblk, eye).reshape(LRU_WIDTH, LRU_WIDTH)
    wg = jnp.concatenate([dense(wa[0]), dense(wa[1]), dense(wx[0]), dense(wx[1])], axis=1)
    bg = jnp.concatenate([ba[0], ba[1], bx[0], bx[1]]).reshape(1, 4 * LRU_WIDTH)
    return (0.5 * wg).astype(BF16), 0.5 * bg


def _merge_kernel(x_ref, ya_ref, yb_ref, yc_ref, mod_ref, g_ref, wgate_ref, wbr_ref, wout_ref, o_ref):
    x = x_ref[...]
    h = _modulated_norm(x, g_ref, mod_ref)
    gates = jnp.dot(h, wgate_ref[0], preferred_element_type=F32)
    z = None
    for k, y_ref in enumerate((ya_ref, yb_ref, yc_ref)):
        term = (_sigmoid(gates[:, k * D_MODEL:(k + 1) * D_MODEL])
                * jnp.dot(y_ref[...], wbr_ref[0, k], preferred_element_type=F32))
        z = term if z is None else z + term
    out = jnp.dot(z.astype(BF16), wout_ref[0], preferred_element_type=F32)
    o_ref[...] = x + mod_ref[0, 0, 2:3, :] * out


def _merge(x2d, ya, yb, yc, mods, norm_g, w_gate, w_br, w_out, layer, rows_per_mod, mod_row0):
    tokens = x2d.shape[0]
    tm = 512
    tiles_per_mod = rows_per_mod // tm if rows_per_mod else 0

    def mod_idx(i):
        if rows_per_mod:
            return (layer, mod_row0 + i // tiles_per_mod, 0, 0)
        return (layer, mod_row0, 0, 0)

    return pl.pallas_call(
        _merge_kernel,
        grid=(tokens // tm,),
        in_specs=[
            pl.BlockSpec((tm, D_MODEL), lambda i: (i, 0)),
            pl.BlockSpec((tm, BRANCH_W), lambda i: (i, 0)),
            pl.BlockSpec((tm, BRANCH_W), lambda i: (i, 0)),
            pl.BlockSpec((tm, BRANCH_W), lambda i: (i, 0)),
            pl.BlockSpec((1, 1, 3, D_MODEL), mod_idx),
            pl.BlockSpec((1, 1, D_MODEL), lambda i: (layer, 0, 0)),
            pl.BlockSpec((1, D_MODEL, GATE_COLS), lambda i: (layer, 0, 0)),
            pl.BlockSpec((1, 3, BRANCH_W, D_MODEL), lambda i: (layer, 0, 0, 0)),
            pl.BlockSpec((1, D_MODEL, D_MODEL), lambda i: (layer, 0, 0)),
        ],
        out_specs=pl.BlockSpec((tm, D_MODEL), lambda i: (i, 0)),
        out_shape=jax.ShapeDtypeStruct((tokens, D_MODEL), F32),
        compiler_params=_cparams(("arbitrary",)),
        name="merge",
    )(x2d, ya, yb, yc, mods, norm_g.reshape(DEPTH, 1, D_MODEL), w_gate, w_br, w_out)


def _rope_tables(t_len):
    t = jnp.arange(t_len)
    m = HEAD_DIM // 4
    freqs = ROPE_BASE ** (-jnp.arange(m, dtype=F32) / m)
    ang_r = (t // GRID_W).astype(F32)[:, None] * freqs[None, :]
    ang_c = (t % GRID_W).astype(F32)[:, None] * freqs[None, :]
    cos = jnp.concatenate([jnp.cos(ang_r), jnp.cos(ang_r), jnp.cos(ang_c), jnp.cos(ang_c)], axis=-1)
    sin = jnp.concatenate([-jnp.sin(ang_r), jnp.sin(ang_r), -jnp.sin(ang_c), jnp.sin(ang_c)], axis=-1)
    reps = BRANCH_W // HEAD_DIM
    return jnp.tile(cos, (1, reps)), jnp.tile(sin, (1, reps))


def kernel(x_prompt, x_sample, cache_ka, cache_va, cache_kb, cache_vb, state_lru, c, c_ctx,
           norm_g, w_ada, b_ada, w_in, a_q_norm, a_k_norm, a_sink, b_q_norm, b_k_norm, b_rpb,
           lru_conv_w, lru_conv_b, lru_wa, lru_ba, lru_wx, lru_bx, lru_lambda, w_branch, w_out):
    bsz, s_len, _ = x_prompt.shape
    dbsz, t_len, _ = x_sample.shape
    p_len = cache_ka.shape[2]

    n_mod = 16
    ctx_row = dbsz
    cond = jnp.zeros((n_mod, D_MODEL), F32).at[:dbsz].set(c).at[ctx_row].set(c_ctx)
    mods = _mods(cond, w_ada, b_ada).reshape(DEPTH, n_mod, 3, D_MODEL)

    w_perm = jnp.concatenate(
        [w_in[:, :, _ORIG_OFFS[k]:_ORIG_OFFS[k] + _ORIG_SPLITS[k]] for k in _PERM], axis=-1).astype(BF16)
    gate0 = _ORIG_OFFS[_GATE_SPLITS[0]]
    w_gate = w_in[:, :, gate0:gate0 + GATE_COLS].astype(BF16)
    w_br = w_branch.astype(BF16)
    w_o = w_out.astype(BF16)
    cos, sin_signed = _rope_tables(t_len)

    cka = cache_ka.reshape(dbsz, DEPTH, p_len, A_KV_HEADS * HEAD_DIM)
    cva = cache_va.reshape(dbsz, DEPTH, p_len, A_KV_HEADS * HEAD_DIM)
    ckb = cache_kb.reshape(dbsz, DEPTH, p_len, B_HEADS * HEAD_DIM)
    cvb = cache_vb.reshape(dbsz, DEPTH, p_len, B_HEADS * HEAD_DIM)

    yp = x_prompt.reshape(bsz * s_len, D_MODEL)
    ys = x_sample.reshape(dbsz * t_len, D_MODEL)
    zero_state = jnp.zeros((bsz, 2, LRU_WIDTH), F32)
    new_ka, new_va, new_kb, new_vb, new_lru = [], [], [], [], []
    for l in range(DEPTH):
        wg, bg = _lru_gate_weights(lru_wa[l], lru_ba[l], lru_wx[l], lru_bx[l])

        proj = _inproj(yp, mods, norm_g, w_perm, l, 0, ctx_row)
        ya, yb, ka, va, kb, vb = _ctx_attn(proj, bsz, s_len, a_q_norm[l], a_k_norm[l], a_sink[l],
                                           b_q_norm[l], b_k_norm[l])
        yc, st = _lru(proj, zero_state, lru_conv_w[l], lru_conv_b[l], wg, bg, lru_lambda[l], bsz, s_len)
        yp = _merge(yp, ya, yb, yc, mods, norm_g, w_gate, w_br, w_o, l, 0, ctx_row)
        new_ka.append(ka.reshape(bsz, s_len, A_KV_HEADS, HEAD_DIM))
        new_va.append(va.reshape(bsz, s_len, A_KV_HEADS, HEAD_DIM))
        new_kb.append(kb.reshape(bsz, s_len, B_HEADS, HEAD_DIM))
        new_vb.append(vb.reshape(bsz, s_len, B_HEADS, HEAD_DIM))
        new_lru.append(st)

        proj = _inproj(ys, mods, norm_g, w_perm, l, t_len, 0)
        ya = _win_attn(proj, cka, cva, l, cos, sin_signed, a_q_norm[l], a_k_norm[l], a_sink[l], dbsz, t_len)
        yb = _nbr_attn(proj, ckb, cvb, l, b_rpb[l], b_q_norm[l], b_k_norm[l], dbsz, t_len)
        yc, _ = _lru(proj, state_lru[:, l], lru_conv_w[l], lru_conv_b[l], wg, bg, lru_lambda[l], dbsz, t_len)
        ys = _merge(ys, ya, yb, yc, mods, norm_g, w_gate, w_br, w_o, l, t_len, 0)

    return (yp.reshape(bsz, s_len, D_MODEL), ys.reshape(dbsz, t_len, D_MODEL),
            jnp.stack(new_ka, axis=1), jnp.stack(new_va, axis=1),
            jnp.stack(new_kb, axis=1), jnp.stack(new_vb, axis=1),
            jnp.stack(new_lru, axis=1))
```

```python
import functools

import numpy as np
import jax
import jax.numpy as jnp
from jax import lax
from jax.experimental import pallas as pl
from jax.experimental.pallas import tpu as pltpu

F32 = jnp.float32
BF16 = jnp.bfloat16

D_MODEL = 1024
DEPTH = 2
GRID_W = 64
HEAD_DIM = 64
BRANCH_W = 512
A_HEADS = 8
A_KV_HEADS = 2
A_WINDOW = 128
A_BLOCK = 128
B_HEADS = 8
NB_ROWS = 8
NB_COLS = 16
LRU_WIDTH = 512
LRU_BLOCKS = 8
LRU_BW = LRU_WIDTH // LRU_BLOCKS
LRU_C = 8.0
CONV_W = 4
ROPE_BASE = 10000.0
EPS = 1e-6
NEG_INF = -1e30
QK_SCALE = HEAD_DIM ** -0.5

LANE = 128
SUBLANE = 8
MXU_DIM = 256
VMEM_LIMIT = 56 * 1024 * 1024

_ORIG_SPLITS = (512, 128, 128, 512, 512, 512, 512, 512, 512, 512, 1024, 1024, 1024)
_ORIG_OFFS = tuple(int(v) for v in np.cumsum((0,) + _ORIG_SPLITS)[:-1])
_PERM = (0, 3, 4, 5, 6, 7, 8, 9, 1, 2)
_GATE_SPLITS = (10, 11, 12)
IN_COLS = sum(_ORIG_SPLITS[k] for k in _PERM)
GATE_COLS = sum(_ORIG_SPLITS[k] for k in _GATE_SPLITS)
C_AQ, C_AG, C_BQ, C_BK, C_BV, C_BG, C_CX, C_CG, C_AK, C_AV = (0, 4, 8, 12, 16, 20, 24, 28, 32, 33)

_NT = (((1,), (1,)), ((), ()))


def _cparams(sem):
    return pltpu.CompilerParams(dimension_semantics=sem, vmem_limit_bytes=VMEM_LIMIT)


def _sigmoid(x):
    return 0.5 + 0.5 * jnp.tanh(0.5 * x)


def _silu(x):
    return x * _sigmoid(x)


def _head_mean_matrix(width):
    idx = np.arange(width) // HEAD_DIM
    return jnp.asarray((idx[:, None] == idx[None, :]).astype(np.float32) / HEAD_DIM, dtype=BF16)


def _heads_rms(x, bd, g):
    x2 = x * x
    hi = x2.astype(BF16)
    lo = (x2 - hi.astype(F32)).astype(BF16)
    ms = jnp.dot(hi, bd, preferred_element_type=F32) + jnp.dot(lo, bd, preferred_element_type=F32)
    return x * lax.rsqrt(ms + EPS) * g


def _head_lane_mask(width, h):
    lane = lax.broadcasted_iota(jnp.int32, (1, width), 1)
    return (lane >= h * HEAD_DIM) & (lane < (h + 1) * HEAD_DIM)


def _mods_kernel(c_ref, w_ref, b_ref, o_ref):
    c = c_ref[...]
    s = _silu(c).astype(BF16)
    o_ref[0] = jnp.dot(s, w_ref[0].astype(BF16), preferred_element_type=F32) + b_ref[0]


def _mods(cond, w_ada, b_ada):
    n = cond.shape[0]
    tn = D_MODEL
    return pl.pallas_call(
        _mods_kernel,
        grid=(DEPTH, 3 * D_MODEL // tn),
        in_specs=[
            pl.BlockSpec((n, D_MODEL), lambda l, j: (0, 0)),
            pl.BlockSpec((1, D_MODEL, tn), lambda l, j: (l, 0, j)),
            pl.BlockSpec((1, 1, tn), lambda l, j: (l, 0, j)),
        ],
        out_specs=pl.BlockSpec((1, n, tn), lambda l, j: (l, 0, j)),
        out_shape=jax.ShapeDtypeStruct((DEPTH, n, 3 * D_MODEL), F32),
        compiler_params=_cparams(("arbitrary", "arbitrary")),
        name="mods",
    )(cond, w_ada, b_ada.reshape(DEPTH, 1, 3 * D_MODEL))


W_BLK = 256
_MIXER_BLKS = IN_COLS // W_BLK
_GATE_BLKS = GATE_COLS // W_BLK
W_MIXER_OFF = IN_COLS
W_PREP_COLS = 2 * IN_COLS


def _wprep_kernel(w_ref, o_ref):
    o_ref[...] = w_ref[...].astype(BF16)


def _wprep_src(n):
    m = n - _GATE_BLKS
    mixer = jnp.where(m < 2, m, jnp.where(m < _MIXER_BLKS - 1, m + 1, 2))
    return jnp.where(n < _GATE_BLKS, n + _MIXER_BLKS, mixer)


def _wprep(w_in):
    assert [_ORIG_OFFS[k] // W_BLK for k in _PERM[:-2]] == [0, 3, 5, 7, 9, 11, 13, 15] and _ORIG_OFFS[1] == 2 * W_BLK
    gap = W_MIXER_OFF // W_BLK - _GATE_BLKS
    return pl.pallas_call(
        _wprep_kernel,
        grid=(DEPTH, _GATE_BLKS + _MIXER_BLKS),
        in_specs=[pl.BlockSpec((1, D_MODEL, W_BLK), lambda l, n: (l, 0, _wprep_src(n)))],
        out_specs=pl.BlockSpec((1, D_MODEL, W_BLK), lambda l, n: (l, 0, jnp.where(n < _GATE_BLKS, n, n + gap))),
        out_shape=jax.ShapeDtypeStruct((DEPTH, D_MODEL, W_PREP_COLS), BF16),
        compiler_params=_cparams(("arbitrary", "arbitrary")),
        name="wprep",
    )(w_in)


def _modulated_norm(x, g_ref, mod_ref):
    y = x * lax.rsqrt(jnp.mean(x * x, axis=-1, keepdims=True) + EPS)
    y = y * g_ref[0]
    shift = mod_ref[0, 0, 0:1, :]
    scale = mod_ref[0, 0, 1:2, :]
    return (y * (1.0 + scale) + shift).astype(BF16)


def _inproj_kernel(x_ref, mod_ref, g_ref, w_ref, o_ref):
    h = _modulated_norm(x_ref[...], g_ref, mod_ref)
    o_ref[...] = jnp.dot(h, w_ref[0], preferred_element_type=F32)


def _inproj(x2d, mods, norm_g, w_perm, layer, rows_per_mod, mod_row0):
    tokens = x2d.shape[0]
    tm = 512
    tiles_per_mod = rows_per_mod // tm if rows_per_mod else 0

    def mod_idx(i):
        if rows_per_mod:
            return (layer, mod_row0 + i // tiles_per_mod, 0, 0)
        return (layer, mod_row0, 0, 0)

    return pl.pallas_call(
        _inproj_kernel,
        grid=(tokens // tm,),
        in_specs=[
            pl.BlockSpec((tm, D_MODEL), lambda i: (i, 0)),
            pl.BlockSpec((1, 1, 3, D_MODEL), mod_idx),
            pl.BlockSpec((1, 1, D_MODEL), lambda i: (layer, 0, 0)),
            pl.BlockSpec((1, D_MODEL, IN_COLS), lambda i: (layer, 0, W_MIXER_OFF // IN_COLS)),
        ],
        out_specs=pl.BlockSpec((tm, IN_COLS), lambda i: (i, 0)),
        out_shape=jax.ShapeDtypeStruct((tokens, IN_COLS), F32),
        compiler_params=_cparams(("arbitrary",)),
        name="inproj",
    )(x2d, mods, norm_g.reshape(DEPTH, 1, D_MODEL), w_perm)


CTX_SEQS = 2


def _ctx_mixer(q, k, v, g, gq, gk, bdq, bdk, ones_stack, expand=None, sink_ref=None):
    s_len = q.shape[0]
    n_q = q.shape[1] // HEAD_DIM
    kn = _heads_rms(k, bdk, gk)
    qn = (_heads_rms(q, bdq, gq) * QK_SCALE).astype(BF16)
    knb = kn.astype(BF16)
    vb = v.astype(BF16)
    if expand is not None:
        knb = jnp.dot(knb, expand, preferred_element_type=F32).astype(BF16)
        vb = jnp.dot(vb, expand, preferred_element_type=F32).astype(BF16)
    heads_per_half = MXU_DIM // HEAD_DIM
    masks = [_head_lane_mask(MXU_DIM, hi) for hi in range(heads_per_half)]
    halves = []
    for hh in range(n_q // heads_per_half):
        cols = slice(hh * MXU_DIM, (hh + 1) * MXU_DIM)
        qh, kh, vh = qn[:, cols], knb[:, cols], vb[:, cols]
        kstack = jnp.concatenate([jnp.where(mk, kh, jnp.zeros_like(kh)) for mk in masks], axis=0)
        vstack = jnp.concatenate([jnp.where(mk, vh, jnp.zeros_like(vh)) for mk in masks], axis=0)
        s = lax.dot_general(qh, kstack, _NT, preferred_element_type=F32)
        ps, sink_terms = [], []
        for hi in range(heads_per_half):
            si = s[:, hi * s_len:(hi + 1) * s_len]
            m = jnp.max(si, axis=-1, keepdims=True)
            if sink_ref is not None:
                h = hh * heads_per_half + hi
                snk = sink_ref[0:1, h:h + 1]
                m = jnp.maximum(m, snk)
                sink_terms.append(jnp.where(masks[hi], jnp.exp(snk - m), 0.0))
            ps.append(jnp.exp(si - m).astype(BF16))
        p = jnp.concatenate(ps, axis=1)
        rhs = jnp.concatenate([vstack, ones_stack], axis=1)
        oe = jnp.dot(p, rhs, preferred_element_type=F32)
        l = oe[:, MXU_DIM:]
        for term in sink_terms:
            l = l + term
        halves.append(oe[:, :MXU_DIM] / l)
    return jnp.concatenate(halves, axis=-1) * _silu(g), kn


def _ctx_attn_kernel(aq_ref, ak_ref, av_ref, ag_ref, bq_ref, bk_ref, bv_ref, bg_ref,
                     gqa_ref, gka_ref, sink_ref, gqb_ref, gkb_ref, bdw_ref, bdn_ref, exp_ref, ones_ref,
                     ya_ref, yb_ref, ka_ref, va_ref, kb_ref, vb_ref, *, s_len):
    ones_stack = ones_ref[...]
    bdw = bdw_ref[...]
    for n in range(aq_ref.shape[0] // s_len):
        rows = slice(n * s_len, (n + 1) * s_len)
        va = av_ref[rows, :]
        vb = bv_ref[rows, :]
        va_ref[rows, :] = va
        vb_ref[rows, :] = vb
        ya, kna = _ctx_mixer(aq_ref[rows, :], ak_ref[rows, :], va, ag_ref[rows, :], gqa_ref[...], gka_ref[...],
                             bdw, bdn_ref[...], ones_stack, expand=exp_ref[...], sink_ref=sink_ref)
        yb, knb = _ctx_mixer(bq_ref[rows, :], bk_ref[rows, :], vb, bg_ref[rows, :], gqb_ref[...], gkb_ref[...],
                             bdw, bdw, ones_stack)
        ka_ref[rows, :] = kna
        kb_ref[rows, :] = knb
        ya_ref[rows, :] = ya.astype(ya_ref.dtype)
        yb_ref[rows, :] = yb.astype(yb_ref.dtype)


def _ctx_attn(proj, bsz, s_len, a_gq, a_gk, a_sink, b_gq, b_gk):
    kvw = A_KV_HEADS * HEAD_DIM
    grp = A_HEADS // A_KV_HEADS
    expand = np.zeros((kvw, BRANCH_W), np.float32)
    for h in range(A_HEADS):
        for d in range(HEAD_DIM):
            expand[(h // grp) * HEAD_DIM + d, h * HEAD_DIM + d] = 1.0
    heads_per_half = MXU_DIM // HEAD_DIM
    ones_stack = np.zeros((heads_per_half * s_len, MXU_DIM), np.float32)
    for hi in range(heads_per_half):
        ones_stack[hi * s_len:(hi + 1) * s_len, hi * HEAD_DIM:(hi + 1) * HEAD_DIM] = 1.0
    const = lambda b: (0, 0)
    rows = CTX_SEQS * s_len
    wide = lambda c: pl.BlockSpec((rows, BRANCH_W), lambda b: (b, c // 4))
    narrow = lambda c: pl.BlockSpec((rows, kvw), lambda b: (b, c))
    return pl.pallas_call(
        functools.partial(_ctx_attn_kernel, s_len=s_len),
        grid=(bsz // CTX_SEQS,),
        in_specs=[
            wide(C_AQ), narrow(C_AK), narrow(C_AV), wide(C_AG), wide(C_BQ), wide(C_BK), wide(C_BV), wide(C_BG),
            pl.BlockSpec((1, BRANCH_W), const),
            pl.BlockSpec((1, kvw), const),
            pl.BlockSpec((1, A_HEADS), const),
            pl.BlockSpec((1, BRANCH_W), const),
            pl.BlockSpec((1, BRANCH_W), const),
            pl.BlockSpec((BRANCH_W, BRANCH_W), const),
            pl.BlockSpec((kvw, kvw), const),
            pl.BlockSpec((kvw, BRANCH_W), const),
            pl.BlockSpec((heads_per_half * s_len, MXU_DIM), const),
        ],
        out_specs=[
            pl.BlockSpec((rows, BRANCH_W), lambda b: (b, 0)),
            pl.BlockSpec((rows, BRANCH_W), lambda b: (b, 0)),
            pl.BlockSpec((rows, kvw), lambda b: (b, 0)),
            pl.BlockSpec((rows, kvw), lambda b: (b, 0)),
            pl.BlockSpec((rows, BRANCH_W), lambda b: (b, 0)),
            pl.BlockSpec((rows, BRANCH_W), lambda b: (b, 0)),
        ],
        out_shape=[
            jax.ShapeDtypeStruct((bsz * s_len, BRANCH_W), BF16),
            jax.ShapeDtypeStruct((bsz * s_len, BRANCH_W), BF16),
            jax.ShapeDtypeStruct((bsz * s_len, kvw), F32),
            jax.ShapeDtypeStruct((bsz * s_len, kvw), F32),
            jax.ShapeDtypeStruct((bsz * s_len, BRANCH_W), F32),
            jax.ShapeDtypeStruct((bsz * s_len, BRANCH_W), F32),
        ],
        compiler_params=_cparams(("arbitrary",)),
        name="ctx_attn",
    )(proj, proj, proj, proj, proj, proj, proj, proj,
      jnp.tile(a_gq, A_HEADS).reshape(1, BRANCH_W), jnp.tile(a_gk, A_KV_HEADS).reshape(1, kvw),
      a_sink.reshape(1, A_HEADS),
      jnp.tile(b_gq, B_HEADS).reshape(1, BRANCH_W), jnp.tile(b_gk, B_HEADS).reshape(1, BRANCH_W),
      _head_mean_matrix(BRANCH_W), _head_mean_matrix(kvw), jnp.asarray(expand, dtype=BF16),
      jnp.asarray(ones_stack, dtype=BF16))


def _rope(x, cos, sin_signed):
    w = x.shape[-1]
    lane = lax.broadcasted_iota(jnp.int32, x.shape, 1)
    up = pltpu.roll(x, w - 16, axis=1)
    dn = pltpu.roll(x, 16, axis=1)
    partner = jnp.where((lane & 16) == 0, up, dn)
    return x * cos + partner * sin_signed


def _win_attn_kernel(q_ref, k_ref, v_ref, g_ref, kc_ref, vc_ref, cos_ref, sin_ref, gq_ref, gk_ref, sink_ref,
                     bdq_ref, bdk_ref, dup_ref, mask_ref, y_ref, kpad, vpad, kcx, vcx):
    j = pl.program_id(1)
    nb = pl.num_programs(1)
    t_len = k_ref.shape[0]
    grp = A_HEADS // A_KV_HEADS
    kvw = A_KV_HEADS * HEAD_DIM
    xw = 2 * kvw

    @pl.when(j == 0)
    def _():
        dup = dup_ref[...]
        kn = _heads_rms(k_ref[...], bdk_ref[...], gk_ref[...])
        kn = _rope(kn, cos_ref[:, 0:kvw], sin_ref[:, 0:kvw]).astype(BF16)
        zeros = jnp.zeros((A_BLOCK, xw), BF16)
        kpad[0:A_BLOCK, :] = zeros
        kpad[A_BLOCK + t_len:2 * A_BLOCK + t_len, :] = zeros
        vpad[0:A_BLOCK, :] = zeros
        vpad[A_BLOCK + t_len:2 * A_BLOCK + t_len, :] = zeros
        kpad[A_BLOCK:A_BLOCK + t_len, :] = jnp.dot(kn, dup, preferred_element_type=F32).astype(BF16)
        vpad[A_BLOCK:A_BLOCK + t_len, :] = jnp.dot(
            v_ref[...].astype(BF16), dup, preferred_element_type=F32).astype(BF16)
        kcx[...] = jnp.dot(kc_ref[0, 0].astype(BF16), dup, preferred_element_type=F32).astype(BF16)
        vcx[...] = jnp.dot(vc_ref[0, 0].astype(BF16), dup, preferred_element_type=F32).astype(BF16)

    r0 = pl.multiple_of(j * A_BLOCK, A_BLOCK)
    qn = _heads_rms(q_ref[...], bdq_ref[...], gq_ref[...])
    qb = (_rope(qn, cos_ref[pl.ds(r0, A_BLOCK), :], sin_ref[pl.ds(r0, A_BLOCK), :]) * QK_SCALE).astype(BF16)

    nloc = 3 * A_BLOCK
    maskadd = mask_ref[jnp.where(j == 0, 0, jnp.where(j == nb - 1, 2, 1))]
    low_half = lax.broadcasted_iota(jnp.int32, (1, LANE), 1) < HEAD_DIM
    kband = kpad[pl.ds(r0, nloc), :]
    vband = vpad[pl.ds(r0, nloc), :]
    pairs = []
    for kv in range(A_KV_HEADS):
        cols = slice(kv * LANE, (kv + 1) * LANE)
        qparts, sinks = [], []
        for gi in range(grp):
            h = kv * grp + gi
            qpair = qb[:, (h // 2) * LANE:(h // 2 + 1) * LANE]
            keep = low_half if h % 2 == 0 else jnp.logical_not(low_half)
            qparts.append(jnp.where(keep, qpair, jnp.zeros_like(qpair)))
            sinks.append(jnp.broadcast_to(sink_ref[0:1, h:h + 1], (A_BLOCK, 1)))
        qst = jnp.concatenate(qparts, axis=0)
        snk = jnp.concatenate(sinks, axis=0)
        s_loc = lax.dot_general(qst, kband[:, cols], _NT, preferred_element_type=F32) + maskadd
        s_ctx = lax.dot_general(qst, kcx[:, cols], _NT, preferred_element_type=F32)
        m = jnp.maximum(jnp.maximum(jnp.max(s_loc, axis=-1, keepdims=True),
                                    jnp.max(s_ctx, axis=-1, keepdims=True)), snk)
        p_loc = jnp.exp(s_loc - m)
        p_ctx = jnp.exp(s_ctx - m)
        l = (jnp.sum(p_loc, axis=-1, keepdims=True) + jnp.sum(p_ctx, axis=-1, keepdims=True)
             + jnp.exp(snk - m))
        o = (jnp.dot(p_loc.astype(BF16), vband[:, cols], preferred_element_type=F32)
             + jnp.dot(p_ctx.astype(BF16), vcx[:, cols], preferred_element_type=F32)) / l
        for k2 in range(grp // 2):
            even = o[(2 * k2) * A_BLOCK:(2 * k2 + 1) * A_BLOCK]
            odd = o[(2 * k2 + 1) * A_BLOCK:(2 * k2 + 2) * A_BLOCK]
            pairs.append(jnp.where(low_half, even, odd))
    y = jnp.concatenate(pairs, axis=-1) * _silu(g_ref[...])
    y_ref[...] = y.astype(y_ref.dtype)


def _win_mask(grp, nb):
    assert nb >= 2
    r = np.arange(A_BLOCK)[:, None]
    c = np.arange(3 * A_BLOCK)[None, :]
    band = np.abs(r + A_BLOCK - c) <= A_WINDOW
    variants = [band & (c >= A_BLOCK), band, band & (c < 2 * A_BLOCK)]
    return np.stack([np.tile(np.where(v, 0.0, NEG_INF).astype(np.float32), (grp, 1)) for v in variants])


def _win_attn(proj, cache_k, cache_v, layer, cos, sin_signed, gq, gk, sink, bsz, t_len):
    nb = t_len // A_BLOCK
    kvw = A_KV_HEADS * HEAD_DIM
    grp = A_HEADS // A_KV_HEADS
    p_len = cache_k.shape[2]
    dup = np.zeros((kvw, 2 * kvw), np.float32)
    for kv in range(A_KV_HEADS):
        for half in range(2):
            for d in range(HEAD_DIM):
                dup[kv * HEAD_DIM + d, kv * LANE + half * HEAD_DIM + d] = 1.0
    const2 = lambda b, j: (0, 0)
    return pl.pallas_call(
        _win_attn_kernel,
        grid=(bsz, nb),
        in_specs=[
            pl.BlockSpec((A_BLOCK, BRANCH_W), lambda b, j: (b * nb + j, C_AQ // 4)),
            pl.BlockSpec((t_len, kvw), lambda b, j: (b, C_AK)),
            pl.BlockSpec((t_len, kvw), lambda b, j: (b, C_AV)),
            pl.BlockSpec((A_BLOCK, BRANCH_W), lambda b, j: (b * nb + j, C_AG // 4)),
            pl.BlockSpec((1, 1, p_len, kvw), lambda b, j: (b, layer, 0, 0)),
            pl.BlockSpec((1, 1, p_len, kvw), lambda b, j: (b, layer, 0, 0)),
            pl.BlockSpec((t_len, BRANCH_W), const2),
            pl.BlockSpec((t_len, BRANCH_W), const2),
            pl.BlockSpec((1, BRANCH_W), const2),
            pl.BlockSpec((1, kvw), const2),
            pl.BlockSpec((1, A_HEADS), const2),
            pl.BlockSpec((BRANCH_W, BRANCH_W), const2),
            pl.BlockSpec((kvw, kvw), const2),
            pl.BlockSpec((kvw, 2 * kvw), const2),
            pl.BlockSpec((3, grp * A_BLOCK, 3 * A_BLOCK), lambda b, j: (0, 0, 0)),
        ],
        out_specs=pl.BlockSpec((A_BLOCK, BRANCH_W), lambda b, j: (b * nb + j, 0)),
        out_shape=jax.ShapeDtypeStruct((bsz * t_len, BRANCH_W), BF16),
        scratch_shapes=[pltpu.VMEM((t_len + 2 * A_BLOCK, 2 * kvw), BF16),
                        pltpu.VMEM((t_len + 2 * A_BLOCK, 2 * kvw), BF16),
                        pltpu.VMEM((p_len, 2 * kvw), BF16),
                        pltpu.VMEM((p_len, 2 * kvw), BF16)],
        compiler_params=_cparams(("arbitrary", "arbitrary")),
        name="win_attn",
    )(proj, proj, proj, proj, cache_k, cache_v, cos, sin_signed,
      jnp.tile(gq, A_HEADS).reshape(1, BRANCH_W), jnp.tile(gk, A_KV_HEADS).reshape(1, kvw),
      sink.reshape(1, A_HEADS), _head_mean_matrix(BRANCH_W), _head_mean_matrix(kvw),
      jnp.asarray(dup, dtype=BF16), jnp.asarray(_win_mask(grp, nb)))


NBR_QB = 2 * GRID_W
NBR_BAND = 5


def _nbr_plan(rows):
    kh = min(NB_ROWS, rows)
    nblk = rows // 2
    specs, plan, starts = {}, [], []
    for i in range(nblk):
        s0 = min(max(i - 2, 0), nblk - NBR_BAND)
        starts.append(s0)
        blk = []
        for a in range(2):
            qr = 2 * i + a
            rs = min(max(qr - kh // 2, 0), rows - kh)
            assert 2 * s0 <= rs and rs + kh <= 2 * (s0 + NBR_BAND)
            row = []
            for p in range(NBR_BAND):
                pair = tuple(kr - qr + NB_ROWS - 1 if rs <= kr < rs + kh else None
                             for kr in (2 * (s0 + p), 2 * (s0 + p) + 1))
                row.append(specs.setdefault(pair, len(specs)))
            blk.append(row)
        plan.append(blk)
    return tuple(starts), plan, list(specs)


def _nbr_table(rpb, specs):
    heads = rpb.shape[0]
    c = np.arange(GRID_W)
    cs = np.clip(c - NB_COLS // 2, 0, GRID_W - NB_COLS)
    col_ok = (c[None, :] >= cs[:, None]) & (c[None, :] < cs[:, None] + NB_COLS)
    edge = GRID_W - NB_COLS
    period = 2 * GRID_W - 1
    padded = jnp.pad(rpb.astype(F32), ((0, 0), (0, 0), (edge, edge)))
    tiled = jnp.tile(padded, (1, 1, GRID_W + 1))[:, :, :GRID_W * (period + 1)]
    skew = tiled.reshape(heads, 2 * NB_ROWS - 1, GRID_W, period + 1)
    shifted = skew[:, :, ::-1, :GRID_W]
    by_col = jnp.where(col_ok[None, None], shifted, NEG_INF)
    neg = jnp.full((heads, GRID_W, GRID_W), NEG_INF, F32)
    blocks = [jnp.concatenate([neg if d is None else by_col[:, d] for d in spec], axis=-1) for spec in specs]
    return jnp.stack(blocks, axis=1)


def _nbr_attn_kernel(q_ref, k_ref, v_ref, g_ref, kc_ref, vc_ref, tb_ref, gq_ref, gk_ref, bd_ref, y_ref,
                     *, starts, plan):
    nband = NBR_BAND * NBR_QB
    bd = bd_ref[...]
    qn = (_heads_rms(q_ref[...], bd, gq_ref[...]) * QK_SCALE).astype(BF16)
    kn = _heads_rms(k_ref[...], bd, gk_ref[...]).astype(BF16)
    vb = v_ref[...].astype(BF16)
    kcb = kc_ref[0, 0].astype(BF16)
    vcb = vc_ref[0, 0].astype(BF16)
    acc = [jnp.zeros((NBR_QB, LANE), F32) for _ in starts]
    for h in range(LANE // HEAD_DIM):
        hm = _head_lane_mask(LANE, h)
        km = jnp.where(hm, kn, jnp.zeros_like(kn))
        kcm = jnp.where(hm, kcb, jnp.zeros_like(kcb))
        vm = jnp.concatenate([jnp.where(hm, vb, jnp.zeros_like(vb)), jnp.ones_like(vb)], axis=1)
        vcm = jnp.concatenate([jnp.where(hm, vcb, jnp.zeros_like(vcb)), jnp.ones_like(vcb)], axis=1)
        for i, s0 in enumerate(starts):
            qi = qn[i * NBR_QB:(i + 1) * NBR_QB]
            ks = slice(s0 * NBR_QB, s0 * NBR_QB + nband)
            s_raw = lax.dot_general(qi, km[ks], _NT, preferred_element_type=F32)
            s_loc = jnp.concatenate(
                [jnp.concatenate([s_raw[a * GRID_W:(a + 1) * GRID_W, p * LANE:(p + 1) * LANE]
                                  + tb_ref[h, plan[i][a][p]] for p in range(NBR_BAND)], axis=1)
                 for a in range(2)], axis=0)
            s_ctx = lax.dot_general(qi, kcm, _NT, preferred_element_type=F32)
            m = jnp.maximum(jnp.max(s_loc, axis=-1, keepdims=True), jnp.max(s_ctx, axis=-1, keepdims=True))
            p_loc = jnp.exp(s_loc - m).astype(BF16)
            p_ctx = jnp.exp(s_ctx - m).astype(BF16)
            oe = (jnp.dot(p_loc, vm[ks], preferred_element_type=F32)
                  + jnp.dot(p_ctx, vcm, preferred_element_type=F32))
            acc[i] = acc[i] + oe[:, :LANE] / oe[:, LANE:]
    y = jnp.concatenate(acc, axis=0) * _silu(g_ref[...])
    y_ref[...] = y.astype(y_ref.dtype)


def _nbr_attn(proj, cache_k, cache_v, layer, rpb, gq, gk, bsz, t_len):
    hp = LANE // HEAD_DIM
    nhp = B_HEADS // hp
    p_len = cache_k.shape[2]
    rows = t_len // GRID_W
    assert rows % 2 == 0 and rows // 2 >= NBR_BAND
    starts, plan, specs = _nbr_plan(rows)
    table = _nbr_table(rpb, specs)
    const = lambda h, b: (0, 0)
    kern = functools.partial(_nbr_attn_kernel, starts=starts, plan=plan)
    return pl.pallas_call(
        kern,
        grid=(nhp, bsz),
        in_specs=[
            pl.BlockSpec((t_len, LANE), lambda h, b: (b, C_BQ + h)),
            pl.BlockSpec((t_len, LANE), lambda h, b: (b, C_BK + h)),
            pl.BlockSpec((t_len, LANE), lambda h, b: (b, C_BV + h)),
            pl.BlockSpec((t_len, LANE), lambda h, b: (b, C_BG + h)),
            pl.BlockSpec((1, 1, p_len, LANE), lambda h, b: (b, layer, 0, h)),
            pl.BlockSpec((1, 1, p_len, LANE), lambda h, b: (b, layer, 0, h)),
            pl.BlockSpec((hp, len(specs), GRID_W, 2 * GRID_W), lambda h, b: (h, 0, 0, 0)),
            pl.BlockSpec((1, LANE), const),
            pl.BlockSpec((1, LANE), const),
            pl.BlockSpec((LANE, LANE), const),
        ],
        out_specs=pl.BlockSpec((t_len, LANE), lambda h, b: (b, h)),
        out_shape=jax.ShapeDtypeStruct((bsz * t_len, BRANCH_W), BF16),
        compiler_params=_cparams(("arbitrary", "arbitrary")),
        name="nbr_attn",
    )(proj, proj, proj, proj, cache_k, cache_v, table,
      jnp.tile(gq, hp).reshape(1, LANE), jnp.tile(gk, hp).reshape(1, LANE), _head_mean_matrix(LANE))


LRU_CHUNKS = SUBLANE
LRU_PITCH_PAD = 4


def _lru_kernel(cx_ref, cg_ref, h0_ref, cw_ref, cb_ref, wg_ref, bg_ref, lam_ref,
                y_ref, st_ref, a_s, u_s, h_s, p_s, y_s):
    t_len = cx_ref.shape[0]
    w = LRU_WIDTH
    cx = cx_ref[...]
    row = lax.broadcasted_iota(jnp.int32, (t_len, w), 0)
    xc = cb_ref[...] + cx * cw_ref[2:3, :]
    xc = xc + jnp.where(row >= 2, pltpu.roll(cx, 2, axis=0), 0.0) * cw_ref[0:1, :]
    xc = xc + jnp.where(row >= 1, pltpu.roll(cx, 1, axis=0), 0.0) * cw_ref[1:2, :]
    xc = xc + jnp.where(row < t_len - 1, pltpu.roll(cx, t_len - 1, axis=0), 0.0) * cw_ref[3:4, :]

    gates = jnp.dot(xc.astype(BF16), wg_ref[...], preferred_element_type=F32) + bg_ref[...]
    coeffs = []
    for d in range(2):
        th_r = jnp.tanh(gates[:, d * w:(d + 1) * w])
        th_i = jnp.tanh(gates[:, (2 + d) * w:(3 + d) * w])
        nl = -lam_ref[d:d + 1, :]
        softplus = jnp.maximum(nl, 0.0) + jnp.log1p(jnp.exp(-jnp.abs(nl)))
        quarter_c = (-0.25 * LRU_C) * softplus
        half_log_a = quarter_c * th_r + quarter_c
        t = jnp.tanh(half_log_a)
        rc = 1.0 / (1.0 - t)
        coeffs.append(((1.0 + t) * rc,
                       jnp.sqrt(-t) * rc * (1.0 + th_i) * xc))

    chunk_len = t_len // LRU_CHUNKS
    pitch = chunk_len + LRU_PITCH_PAD
    nt = w // LANE
    for d, (a_val, u_val) in enumerate(coeffs):
        for c in range(LRU_CHUNKS):
            for k in range(nt):
                dst = slice(c * pitch, c * pitch + chunk_len)
                src = (slice(c * chunk_len, (c + 1) * chunk_len), slice(k * LANE, (k + 1) * LANE))
                a_s[d, k, dst, :] = a_val[src]
                u_s[d, k, dst, :] = u_val[src]

    def scan_body(s, carry):
        hs, ps = carry
        new_h, new_p = [], []
        for d in range(2):
            pos = s if d == 0 else chunk_len - 1 - s
            idx = pl.ds(pos, LRU_CHUNKS, stride=pitch)
            for k in range(nt):
                av = a_s[d, k, idx, :]
                h = av * hs[d * nt + k] + u_s[d, k, idx, :]
                p = av * ps[d * nt + k]
                h_s[d, k, idx, :] = h
                p_s[d, k, idx, :] = p
                new_h.append(h)
                new_p.append(p)
        return tuple(new_h), tuple(new_p)

    zero = jnp.zeros((LRU_CHUNKS, LANE), F32)
    one = jnp.ones((LRU_CHUNKS, LANE), F32)
    h_end, p_end = lax.fori_loop(0, chunk_len, scan_body, ((zero,) * (2 * nt), (one,) * (2 * nt)), unroll=4)

    h0 = h0_ref[0]
    enter, finals = [], []
    for d in range(2):
        order = range(LRU_CHUNKS) if d == 0 else range(LRU_CHUNKS - 1, -1, -1)
        for k in range(nt):
            he, pe = h_end[d * nt + k], p_end[d * nt + k]
            e = h0[d:d + 1, k * LANE:(k + 1) * LANE]
            rows = [None] * LRU_CHUNKS
            for c in order:
                rows[c] = e
                e = pe[c:c + 1, :] * e + he[c:c + 1, :]
            enter.append(jnp.concatenate(rows, axis=0))
            finals.append(e)
    st_ref[0] = jnp.concatenate([jnp.concatenate(finals[d * nt:(d + 1) * nt], axis=1) for d in range(2)], axis=0)

    def fix_body(s, carry):
        idx = pl.ds(s, LRU_CHUNKS, stride=pitch)
        for k in range(nt):
            hf = h_s[0, k, idx, :] + p_s[0, k, idx, :] * enter[k]
            hb = h_s[1, k, idx, :] + p_s[1, k, idx, :] * enter[nt + k]
            y_s[k, idx, :] = hf + hb
        return carry

    lax.fori_loop(0, chunk_len, fix_body, 0, unroll=4)
    hsum = jnp.concatenate(
        [jnp.concatenate([y_s[k, c * pitch:c * pitch + chunk_len, :] for c in range(LRU_CHUNKS)], axis=0)
         for k in range(nt)], axis=1)
    y_ref[...] = (hsum * _silu(cg_ref[...])).astype(y_ref.dtype)


def _lru(proj, h0, conv_w, conv_b, w_gates, b_gates, lam, bsz, t_len):
    w = LRU_WIDTH
    assert t_len % (LRU_CHUNKS * SUBLANE) == 0
    rows_pad = LRU_CHUNKS * (t_len // LRU_CHUNKS + LRU_PITCH_PAD)
    return pl.pallas_call(
        _lru_kernel,
        grid=(bsz,),
        in_specs=[
            pl.BlockSpec((t_len, w), lambda b: (b, C_CX // 4)),
            pl.BlockSpec((t_len, w), lambda b: (b, C_CG // 4)),
            pl.BlockSpec((1, 2, w), lambda b: (b, 0, 0)),
            pl.BlockSpec((CONV_W, w), lambda b: (0, 0)),
            pl.BlockSpec((1, w), lambda b: (0, 0)),
            pl.BlockSpec((w, 4 * w), lambda b: (0, 0)),
            pl.BlockSpec((1, 4 * w), lambda b: (0, 0)),
            pl.BlockSpec((2, w), lambda b: (0, 0)),
        ],
        out_specs=[
            pl.BlockSpec((t_len, w), lambda b: (b, 0)),
            pl.BlockSpec((1, 2, w), lambda b: (b, 0, 0)),
        ],
        out_shape=[
            jax.ShapeDtypeStruct((bsz * t_len, w), BF16),
            jax.ShapeDtypeStruct((bsz, 2, w), F32),
        ],
        scratch_shapes=[pltpu.VMEM((2, w // LANE, rows_pad, LANE), F32) for _ in range(4)]
        + [pltpu.VMEM((w // LANE, rows_pad, LANE), F32)],
        compiler_params=_cparams(("arbitrary",)),
        name="lru",
    )(proj, proj, h0, conv_w, conv_b.reshape(1, w), w_gates, b_gates, lam)


def _lru_gate_weights(wa, ba, wx, bx):
    def dense(wblk):
        eye = jnp.eye(LRU_BLOCKS, dtype=wblk.dtype)
        return jnp.einsum('nkj,nm->nkmj', wblk, eye).reshape(LRU_WIDTH, LRU_WIDTH)
    wg = jnp.concatenate([dense(wa[0]), dense(wa[1]), dense(wx[0]), dense(wx[1])], axis=1)
    bg = jnp.concatenate([ba[0], ba[1], bx[0], bx[1]]).reshape(1, 4 * LRU_WIDTH)
    return (0.5 * wg).astype(BF16), 0.5 * bg


def _merge_kernel(x_ref, ya_ref, yb_ref, yc_ref, mod_ref, g_ref, wgate_ref, wbr_ref, wout_ref, o_ref):
    x = x_ref[...]
    h = _modulated_norm(x, g_ref, mod_ref)
    gates = jnp.dot(h, wgate_ref[0], preferred_element_type=F32)
    z = None
    for k, y_ref in enumerate((ya_ref, yb_ref, yc_ref)):
        term = (_sigmoid(gates[:, k * D_MODEL:(k + 1) * D_MODEL])
                * jnp.dot(y_ref[...], wbr_ref[0, k], preferred_element_type=F32))
        z = term if z is None else z + term
    out = jnp.dot(z.astype(BF16), wout_ref[0], preferred_element_type=F32)
    o_ref[...] = x + mod_ref[0, 0, 2:3, :] * out


def _merge(x2d, ya, yb, yc, mods, norm_g, w_gate, w_br, w_out, layer, rows_per_mod, mod_row0):
    tokens = x2d.shape[0]
    tm = 512
    tiles_per_mod = rows_per_mod // tm if rows_per_mod else 0

    def mod_idx(i):
        if rows_per_mod:
            return (layer, mod_row0 + i // tiles_per_mod, 0, 0)
        return (layer, mod_row0, 0, 0)

    return pl.pallas_call(
        _merge_kernel,
        grid=(tokens // tm,),
        in_specs=[
            pl.BlockSpec((tm, D_MODEL), lambda i: (i, 0)),
            pl.BlockSpec((tm, BRANCH_W), lambda i: (i, 0)),
            pl.BlockSpec((tm, BRANCH_W), lambda i: (i, 0)),
            pl.BlockSpec((tm, BRANCH_W), lambda i: (i, 0)),
            pl.BlockSpec((1, 1, 3, D_MODEL), mod_idx),
            pl.BlockSpec((1, 1, D_MODEL), lambda i: (layer, 0, 0)),
            pl.BlockSpec((1, D_MODEL, GATE_COLS), lambda i: (layer, 0, 0)),
            pl.BlockSpec((1, 3, BRANCH_W, D_MODEL), lambda i: (layer, 0, 0, 0)),
            pl.BlockSpec((1, D_MODEL, D_MODEL), lambda i: (layer, 0, 0)),
        ],
        out_specs=pl.BlockSpec((tm, D_MODEL), lambda i: (i, 0)),
        out_shape=jax.ShapeDtypeStruct((tokens, D_MODEL), F32),
        compiler_params=_cparams(("arbitrary",)),
        name="merge",
    )(x2d, ya, yb, yc, mods, norm_g.reshape(DEPTH, 1, D_MODEL), w_gate, w_br, w_out)


def _rope_tables(t_len):
    t = jnp.arange(t_len)
    m = HEAD_DIM // 4
    freqs = ROPE_BASE ** (-jnp.arange(m, dtype=F32) / m)
    ang_r = (t // GRID_W).astype(F32)[:, None] * freqs[None, :]
    ang_c = (t % GRID_W).astype(F32)[:, None] * freqs[None, :]
    cos = jnp.concatenate([jnp.cos(ang_r), jnp.cos(ang_r), jnp.cos(ang_c), jnp.cos(ang_c)], axis=-1)
    sin = jnp.concatenate([-jnp.sin(ang_r), jnp.sin(ang_r), -jnp.sin(ang_c), jnp.sin(ang_c)], axis=-1)
    reps = BRANCH_W // HEAD_DIM
    return jnp.tile(cos, (1, reps)), jnp.tile(sin, (1, reps))


def kernel(x_prompt, x_sample, cache_ka, cache_va, cache_kb, cache_vb, state_lru, c, c_ctx,
           norm_g, w_ada, b_ada, w_in, a_q_norm, a_k_norm, a_sink, b_q_norm, b_k_norm, b_rpb,
           lru_conv_w, lru_conv_b, lru_wa, lru_ba, lru_wx, lru_bx, lru_lambda, w_branch, w_out):
    bsz, s_len, _ = x_prompt.shape
    dbsz, t_len, _ = x_sample.shape
    p_len = cache_ka.shape[2]

    n_mod = 16
    ctx_row = dbsz
    cond = jnp.zeros((n_mod, D_MODEL), F32).at[:dbsz].set(c).at[ctx_row].set(c_ctx)
    mods = _mods(cond, w_ada, b_ada).reshape(DEPTH, n_mod, 3, D_MODEL)

    w_perm = w_gate = _wprep(w_in)
    w_br = w_branch.astype(BF16)
    w_o = w_out.astype(BF16)
    cos, sin_signed = _rope_tables(t_len)

    cka = cache_ka.reshape(dbsz, DEPTH, p_len, A_KV_HEADS * HEAD_DIM)
    cva = cache_va.reshape(dbsz, DEPTH, p_len, A_KV_HEADS * HEAD_DIM)
    ckb = cache_kb.reshape(dbsz, DEPTH, p_len, B_HEADS * HEAD_DIM)
    cvb = cache_vb.reshape(dbsz, DEPTH, p_len, B_HEADS * HEAD_DIM)

    yp = x_prompt.reshape(bsz * s_len, D_MODEL)
    ys = x_sample.reshape(dbsz * t_len, D_MODEL)
    zero_state = jnp.zeros((bsz, 2, LRU_WIDTH), F32)
    new_ka, new_va, new_kb, new_vb, new_lru = [], [], [], [], []
    for l in range(DEPTH):
        wg, bg = _lru_gate_weights(lru_wa[l], lru_ba[l], lru_wx[l], lru_bx[l])

        proj = _inproj(yp, mods, norm_g, w_perm, l, 0, ctx_row)
        ya, yb, ka, va, kb, vb = _ctx_attn(proj, bsz, s_len, a_q_norm[l], a_k_norm[l], a_sink[l],
                                           b_q_norm[l], b_k_norm[l])
        yc, st = _lru(proj, zero_state, lru_conv_w[l], lru_conv_b[l], wg, bg, lru_lambda[l], bsz, s_len)
        yp = _merge(yp, ya, yb, yc, mods, norm_g, w_gate, w_br, w_o, l, 0, ctx_row)
        new_ka.append(ka.reshape(bsz, s_len, A_KV_HEADS, HEAD_DIM))
        new_va.append(va.reshape(bsz, s_len, A_KV_HEADS, HEAD_DIM))
        new_kb.append(kb.reshape(bsz, s_len, B_HEADS, HEAD_DIM))
        new_vb.append(vb.reshape(bsz, s_len, B_HEADS, HEAD_DIM))
        new_lru.append(st)

        proj = _inproj(ys, mods, norm_g, w_perm, l, t_len, 0)
        ya = _win_attn(proj, cka, cva, l, cos, sin_signed, a_q_norm[l], a_k_norm[l], a_sink[l], dbsz, t_len)
        yb = _nbr_attn(proj, ckb, cvb, l, b_rpb[l], b_q_norm[l], b_k_norm[l], dbsz, t_len)
        yc, _ = _lru(proj, state_lru[:, l], lru_conv_w[l], lru_conv_b[l], wg, bg, lru_lambda[l], dbsz, t_len)
        ys = _merge(ys, ya, yb, yc, mods, norm_g, w_gate, w_br, w_o, l, t_len, 0)

    return (yp.reshape(bsz, s_len, D_MODEL), ys.reshape(dbsz, t_len, D_MODEL),
            jnp.stack(new_ka, axis=1), jnp.stack(new_va, axis=1),
            jnp.stack(new_kb, axis=1), jnp.stack(new_vb, axis=1),
            jnp.stack(new_lru, axis=1))
```

```python
import functools

import numpy as np
import jax
import jax.numpy as jnp
from jax import lax
from jax.experimental import pallas as pl
from jax.experimental.pallas import tpu as pltpu

F32 = jnp.float32
BF16 = jnp.bfloat16

D_MODEL = 1024
DEPTH = 2
GRID_W = 64
HEAD_DIM = 64
BRANCH_W = 512
A_HEADS = 8
A_KV_HEADS = 2
A_WINDOW = 128
A_BLOCK = 128
B_HEADS = 8
NB_ROWS = 8
NB_COLS = 16
LRU_WIDTH = 512
LRU_BLOCKS = 8
LRU_BW = LRU_WIDTH // LRU_BLOCKS
LRU_C = 8.0
CONV_W = 4
ROPE_BASE = 10000.0
EPS = 1e-6
NEG_INF = -1e30
QK_SCALE = HEAD_DIM ** -0.5

LANE = 128
SUBLANE = 8
MXU_DIM = 256
VMEM_LIMIT = 56 * 1024 * 1024

_ORIG_SPLITS = (512, 128, 128, 512, 512, 512, 512, 512, 512, 512, 1024, 1024, 1024)
_ORIG_OFFS = tuple(int(v) for v in np.cumsum((0,) + _ORIG_SPLITS)[:-1])
_PERM = (0, 3, 4, 5, 6, 7, 8, 9, 1, 2)
_GATE_SPLITS = (10, 11, 12)
IN_COLS = sum(_ORIG_SPLITS[k] for k in _PERM)
GATE_COLS = sum(_ORIG_SPLITS[k] for k in _GATE_SPLITS)
C_AQ, C_AG, C_BQ, C_BK, C_BV, C_BG, C_CX, C_CG, C_AK, C_AV = (0, 4, 8, 12, 16, 20, 24, 28, 32, 33)

_NT = (((1,), (1,)), ((), ()))


def _cparams(sem):
    return pltpu.CompilerParams(dimension_semantics=sem, vmem_limit_bytes=VMEM_LIMIT)


def _sigmoid(x):
    return 0.5 + 0.5 * jnp.tanh(0.5 * x)


def _silu(x):
    return x * _sigmoid(x)


def _head_mean_matrix(width):
    idx = np.arange(width) // HEAD_DIM
    return jnp.asarray((idx[:, None] == idx[None, :]).astype(np.float32) / HEAD_DIM, dtype=BF16)


def _heads_rms(x, bd, g):
    x2 = x * x
    hi = x2.astype(BF16)
    lo = (x2 - hi.astype(F32)).astype(BF16)
    ms = jnp.dot(hi, bd, preferred_element_type=F32) + jnp.dot(lo, bd, preferred_element_type=F32)
    return x * lax.rsqrt(ms + EPS) * g


def _head_lane_mask(width, h):
    lane = lax.broadcasted_iota(jnp.int32, (1, width), 1)
    return (lane >= h * HEAD_DIM) & (lane < (h + 1) * HEAD_DIM)


def _mods_kernel(c_ref, w_ref, b_ref, o_ref):
    c = c_ref[...]
    s = _silu(c).astype(BF16)
    o_ref[0] = jnp.dot(s, w_ref[0].astype(BF16), preferred_element_type=F32) + b_ref[0]


def _mods(cond, w_ada, b_ada):
    n = cond.shape[0]
    tn = D_MODEL
    return pl.pallas_call(
        _mods_kernel,
        grid=(DEPTH, 3 * D_MODEL // tn),
        in_specs=[
            pl.BlockSpec((n, D_MODEL), lambda l, j: (0, 0)),
            pl.BlockSpec((1, D_MODEL, tn), lambda l, j: (l, 0, j)),
            pl.BlockSpec((1, 1, tn), lambda l, j: (l, 0, j)),
        ],
        out_specs=pl.BlockSpec((1, n, tn), lambda l, j: (l, 0, j)),
        out_shape=jax.ShapeDtypeStruct((DEPTH, n, 3 * D_MODEL), F32),
        compiler_params=_cparams(("arbitrary", "arbitrary")),
        name="mods",
    )(cond, w_ada, b_ada.reshape(DEPTH, 1, 3 * D_MODEL))


W_BLK = 256
_MIXER_BLKS = IN_COLS // W_BLK
_GATE_BLKS = GATE_COLS // W_BLK
W_MIXER_OFF = IN_COLS
W_PREP_COLS = 2 * IN_COLS


def _wprep_kernel(w_ref, o_ref):
    o_ref[...] = w_ref[...].astype(BF16)


def _wprep_src(n):
    m = n - _GATE_BLKS
    mixer = jnp.where(m < 2, m, jnp.where(m < _MIXER_BLKS - 1, m + 1, 2))
    return jnp.where(n < _GATE_BLKS, n + _MIXER_BLKS, mixer)


def _wprep(w_in):
    assert [_ORIG_OFFS[k] // W_BLK for k in _PERM[:-2]] == [0, 3, 5, 7, 9, 11, 13, 15] and _ORIG_OFFS[1] == 2 * W_BLK
    gap = W_MIXER_OFF // W_BLK - _GATE_BLKS
    return pl.pallas_call(
        _wprep_kernel,
        grid=(DEPTH, _GATE_BLKS + _MIXER_BLKS),
        in_specs=[pl.BlockSpec((1, D_MODEL, W_BLK), lambda l, n: (l, 0, _wprep_src(n)))],
        out_specs=pl.BlockSpec((1, D_MODEL, W_BLK), lambda l, n: (l, 0, jnp.where(n < _GATE_BLKS, n, n + gap))),
        out_shape=jax.ShapeDtypeStruct((DEPTH, D_MODEL, W_PREP_COLS), BF16),
        compiler_params=_cparams(("arbitrary", "arbitrary")),
        name="wprep",
    )(w_in)


def _modulated_norm(x, g_ref, mod_ref):
    y = x * lax.rsqrt(jnp.mean(x * x, axis=-1, keepdims=True) + EPS)
    y = y * g_ref[0]
    shift = mod_ref[0, 0, 0:1, :]
    scale = mod_ref[0, 0, 1:2, :]
    return (y * (1.0 + scale) + shift).astype(BF16)


def _inproj_kernel(x_ref, mod_ref, g_ref, w_ref, o_ref):
    h = _modulated_norm(x_ref[...], g_ref, mod_ref)
    o_ref[...] = jnp.dot(h, w_ref[0], preferred_element_type=F32)


def _inproj(x2d, mods, norm_g, w_perm, layer, rows_per_mod, mod_row0):
    tokens = x2d.shape[0]
    tm = 512
    tiles_per_mod = rows_per_mod // tm if rows_per_mod else 0

    def mod_idx(i):
        if rows_per_mod:
            return (layer, mod_row0 + i // tiles_per_mod, 0, 0)
        return (layer, mod_row0, 0, 0)

    return pl.pallas_call(
        _inproj_kernel,
        grid=(tokens // tm,),
        in_specs=[
            pl.BlockSpec((tm, D_MODEL), lambda i: (i, 0)),
            pl.BlockSpec((1, 1, 3, D_MODEL), mod_idx),
            pl.BlockSpec((1, 1, D_MODEL), lambda i: (layer, 0, 0)),
            pl.BlockSpec((1, D_MODEL, IN_COLS), lambda i: (layer, 0, W_MIXER_OFF // IN_COLS)),
        ],
        out_specs=pl.BlockSpec((tm, IN_COLS), lambda i: (i, 0)),
        out_shape=jax.ShapeDtypeStruct((tokens, IN_COLS), F32),
        compiler_params=_cparams(("arbitrary",)),
        name="inproj",
    )(x2d, mods, norm_g.reshape(DEPTH, 1, D_MODEL), w_perm)


CTX_SEQS = 2


def _ctx_mixer(q, k, v, g, gq, gk, bdq, bdk, ones_stack, expand=None, sink_ref=None):
    s_len = q.shape[0]
    n_q = q.shape[1] // HEAD_DIM
    kn = _heads_rms(k, bdk, gk)
    qn = (_heads_rms(q, bdq, gq) * QK_SCALE).astype(BF16)
    knb = kn.astype(BF16)
    vb = v.astype(BF16)
    if expand is not None:
        knb = jnp.dot(knb, expand, preferred_element_type=F32).astype(BF16)
        vb = jnp.dot(vb, expand, preferred_element_type=F32).astype(BF16)
    heads_per_half = MXU_DIM // HEAD_DIM
    masks = [_head_lane_mask(MXU_DIM, hi) for hi in range(heads_per_half)]
    halves = []
    for hh in range(n_q // heads_per_half):
        cols = slice(hh * MXU_DIM, (hh + 1) * MXU_DIM)
        qh, kh, vh = qn[:, cols], knb[:, cols], vb[:, cols]
        kstack = jnp.concatenate([jnp.where(mk, kh, jnp.zeros_like(kh)) for mk in masks], axis=0)
        vstack = jnp.concatenate([jnp.where(mk, vh, jnp.zeros_like(vh)) for mk in masks], axis=0)
        s = lax.dot_general(qh, kstack, _NT, preferred_element_type=F32)
        ps, sink_terms = [], []
        for hi in range(heads_per_half):
            si = s[:, hi * s_len:(hi + 1) * s_len]
            m = jnp.max(si, axis=-1, keepdims=True)
            if sink_ref is not None:
                h = hh * heads_per_half + hi
                snk = sink_ref[0:1, h:h + 1]
                m = jnp.maximum(m, snk)
                sink_terms.append(jnp.where(masks[hi], jnp.exp(snk - m), 0.0))
            ps.append(jnp.exp(si - m).astype(BF16))
        p = jnp.concatenate(ps, axis=1)
        rhs = jnp.concatenate([vstack, ones_stack], axis=1)
        oe = jnp.dot(p, rhs, preferred_element_type=F32)
        l = oe[:, MXU_DIM:]
        for term in sink_terms:
            l = l + term
        halves.append(oe[:, :MXU_DIM] / l)
    return jnp.concatenate(halves, axis=-1) * _silu(g), kn


_CTX_INPUTS = 17


def _ctx_attn_kernel(*refs, s_len, n_alias):
    (aq_ref, ak_ref, av_ref, ag_ref, bq_ref, bk_ref, bv_ref, bg_ref,
     gqa_ref, gka_ref, sink_ref, gqb_ref, gkb_ref, bdw_ref, bdn_ref, exp_ref, ones_ref) = refs[:_CTX_INPUTS]
    ya_ref, yb_ref, ka_ref, va_ref, kb_ref, vb_ref = refs[_CTX_INPUTS + n_alias:]
    ones_stack = ones_ref[...]
    bdw = bdw_ref[...]
    for n in range(aq_ref.shape[0] // s_len):
        rows = slice(n * s_len, (n + 1) * s_len)
        va = av_ref[rows, :]
        vb = bv_ref[rows, :]
        va_ref[n, 0] = va.T
        vb_ref[n, 0] = vb.T
        ya, kna = _ctx_mixer(aq_ref[rows, :], ak_ref[rows, :], va, ag_ref[rows, :], gqa_ref[...], gka_ref[...],
                             bdw, bdn_ref[...], ones_stack, expand=exp_ref[...], sink_ref=sink_ref)
        yb, knb = _ctx_mixer(bq_ref[rows, :], bk_ref[rows, :], vb, bg_ref[rows, :], gqb_ref[...], gkb_ref[...],
                             bdw, bdw, ones_stack)
        ka_ref[n, 0] = kna.T
        kb_ref[n, 0] = knb.T
        ya_ref[rows, :] = ya.astype(ya_ref.dtype)
        yb_ref[rows, :] = yb.astype(yb_ref.dtype)


def _ctx_attn(proj, bsz, s_len, a_gq, a_gk, a_sink, b_gq, b_gk, layer, caches):
    kvw = A_KV_HEADS * HEAD_DIM
    grp = A_HEADS // A_KV_HEADS
    expand = np.zeros((kvw, BRANCH_W), np.float32)
    for h in range(A_HEADS):
        for d in range(HEAD_DIM):
            expand[(h // grp) * HEAD_DIM + d, h * HEAD_DIM + d] = 1.0
    heads_per_half = MXU_DIM // HEAD_DIM
    ones_stack = np.zeros((heads_per_half * s_len, MXU_DIM), np.float32)
    for hi in range(heads_per_half):
        ones_stack[hi * s_len:(hi + 1) * s_len, hi * HEAD_DIM:(hi + 1) * HEAD_DIM] = 1.0
    const = lambda b: (0, 0)
    rows = CTX_SEQS * s_len
    wide = lambda c: pl.BlockSpec((rows, BRANCH_W), lambda b: (b, c // 4))
    narrow = lambda c: pl.BlockSpec((rows, kvw), lambda b: (b, c))
    prev = () if caches is None else tuple(caches)
    n_alias = len(prev)
    cache_spec = lambda width: pl.BlockSpec((CTX_SEQS, 1, width, s_len), lambda b: (b, layer, 0, 0))
    cache_shape = lambda width: jax.ShapeDtypeStruct((bsz, DEPTH, width, s_len), F32)
    outs = pl.pallas_call(
        functools.partial(_ctx_attn_kernel, s_len=s_len, n_alias=n_alias),
        grid=(bsz // CTX_SEQS,),
        in_specs=[
            wide(C_AQ), narrow(C_AK), narrow(C_AV), wide(C_AG), wide(C_BQ), wide(C_BK), wide(C_BV), wide(C_BG),
            pl.BlockSpec((1, BRANCH_W), const),
            pl.BlockSpec((1, kvw), const),
            pl.BlockSpec((1, A_HEADS), const),
            pl.BlockSpec((1, BRANCH_W), const),
            pl.BlockSpec((1, BRANCH_W), const),
            pl.BlockSpec((BRANCH_W, BRANCH_W), const),
            pl.BlockSpec((kvw, kvw), const),
            pl.BlockSpec((kvw, BRANCH_W), const),
            pl.BlockSpec((heads_per_half * s_len, MXU_DIM), const),
        ] + [pl.BlockSpec(memory_space=pl.ANY)] * n_alias,
        out_specs=[
            pl.BlockSpec((rows, BRANCH_W), lambda b: (b, 0)),
            pl.BlockSpec((rows, BRANCH_W), lambda b: (b, 0)),
            cache_spec(kvw), cache_spec(kvw), cache_spec(BRANCH_W), cache_spec(BRANCH_W),
        ],
        out_shape=[
            jax.ShapeDtypeStruct((bsz * s_len, BRANCH_W), BF16),
            jax.ShapeDtypeStruct((bsz * s_len, BRANCH_W), BF16),
            cache_shape(kvw), cache_shape(kvw), cache_shape(BRANCH_W), cache_shape(BRANCH_W),
        ],
        input_output_aliases={_CTX_INPUTS + i: 2 + i for i in range(n_alias)},
        compiler_params=_cparams(("arbitrary",)),
        name="ctx_attn",
    )(proj, proj, proj, proj, proj, proj, proj, proj,
      jnp.tile(a_gq, A_HEADS).reshape(1, BRANCH_W), jnp.tile(a_gk, A_KV_HEADS).reshape(1, kvw),
      a_sink.reshape(1, A_HEADS),
      jnp.tile(b_gq, B_HEADS).reshape(1, BRANCH_W), jnp.tile(b_gk, B_HEADS).reshape(1, BRANCH_W),
      _head_mean_matrix(BRANCH_W), _head_mean_matrix(kvw), jnp.asarray(expand, dtype=BF16),
      jnp.asarray(ones_stack, dtype=BF16), *prev)
    return outs[0], outs[1], tuple(outs[2:])


def _rope(x, cos, sin_signed):
    w = x.shape[-1]
    lane = lax.broadcasted_iota(jnp.int32, x.shape, 1)
    up = pltpu.roll(x, w - 16, axis=1)
    dn = pltpu.roll(x, 16, axis=1)
    partner = jnp.where((lane & 16) == 0, up, dn)
    return x * cos + partner * sin_signed


def _win_attn_kernel(q_ref, k_ref, v_ref, g_ref, kc_ref, vc_ref, cos_ref, sin_ref, gq_ref, gk_ref, sink_ref,
                     bdq_ref, bdk_ref, dup_ref, mask_ref, y_ref, kpad, vpad, kcx, vcx):
    j = pl.program_id(1)
    nb = pl.num_programs(1)
    t_len = k_ref.shape[0]
    grp = A_HEADS // A_KV_HEADS
    kvw = A_KV_HEADS * HEAD_DIM
    xw = 2 * kvw

    @pl.when(j == 0)
    def _():
        dup = dup_ref[...]
        kn = _heads_rms(k_ref[...], bdk_ref[...], gk_ref[...])
        kn = _rope(kn, cos_ref[:, 0:kvw], sin_ref[:, 0:kvw]).astype(BF16)
        zeros = jnp.zeros((A_BLOCK, xw), BF16)
        kpad[0:A_BLOCK, :] = zeros
        kpad[A_BLOCK + t_len:2 * A_BLOCK + t_len, :] = zeros
        vpad[0:A_BLOCK, :] = zeros
        vpad[A_BLOCK + t_len:2 * A_BLOCK + t_len, :] = zeros
        kpad[A_BLOCK:A_BLOCK + t_len, :] = jnp.dot(kn, dup, preferred_element_type=F32).astype(BF16)
        vpad[A_BLOCK:A_BLOCK + t_len, :] = jnp.dot(
            v_ref[...].astype(BF16), dup, preferred_element_type=F32).astype(BF16)
        kcx[...] = jnp.dot(kc_ref[0, 0].T.astype(BF16), dup, preferred_element_type=F32).astype(BF16)
        vcx[...] = jnp.dot(vc_ref[0, 0].T.astype(BF16), dup, preferred_element_type=F32).astype(BF16)

    r0 = pl.multiple_of(j * A_BLOCK, A_BLOCK)
    qn = _heads_rms(q_ref[...], bdq_ref[...], gq_ref[...])
    qb = (_rope(qn, cos_ref[pl.ds(r0, A_BLOCK), :], sin_ref[pl.ds(r0, A_BLOCK), :]) * QK_SCALE).astype(BF16)

    nloc = 3 * A_BLOCK
    maskadd = mask_ref[jnp.where(j == 0, 0, jnp.where(j == nb - 1, 2, 1))]
    low_half = lax.broadcasted_iota(jnp.int32, (1, LANE), 1) < HEAD_DIM
    kband = kpad[pl.ds(r0, nloc), :]
    vband = vpad[pl.ds(r0, nloc), :]
    pairs = []
    for kv in range(A_KV_HEADS):
        cols = slice(kv * LANE, (kv + 1) * LANE)
        qparts, sinks = [], []
        for gi in range(grp):
            h = kv * grp + gi
            qpair = qb[:, (h // 2) * LANE:(h // 2 + 1) * LANE]
            keep = low_half if h % 2 == 0 else jnp.logical_not(low_half)
            qparts.append(jnp.where(keep, qpair, jnp.zeros_like(qpair)))
            sinks.append(jnp.broadcast_to(sink_ref[0:1, h:h + 1], (A_BLOCK, 1)))
        qst = jnp.concatenate(qparts, axis=0)
        snk = jnp.concatenate(sinks, axis=0)
        s_loc = lax.dot_general(qst, kband[:, cols], _NT, preferred_element_type=F32) + maskadd
        s_ctx = lax.dot_general(qst, kcx[:, cols], _NT, preferred_element_type=F32)
        m = jnp.maximum(jnp.maximum(jnp.max(s_loc, axis=-1, keepdims=True),
                                    jnp.max(s_ctx, axis=-1, keepdims=True)), snk)
        p_loc = jnp.exp(s_loc - m)
        p_ctx = jnp.exp(s_ctx - m)
        l = (jnp.sum(p_loc, axis=-1, keepdims=True) + jnp.sum(p_ctx, axis=-1, keepdims=True)
             + jnp.exp(snk - m))
        o = (jnp.dot(p_loc.astype(BF16), vband[:, cols], preferred_element_type=F32)
             + jnp.dot(p_ctx.astype(BF16), vcx[:, cols], preferred_element_type=F32)) / l
        for k2 in range(grp // 2):
            even = o[(2 * k2) * A_BLOCK:(2 * k2 + 1) * A_BLOCK]
            odd = o[(2 * k2 + 1) * A_BLOCK:(2 * k2 + 2) * A_BLOCK]
            pairs.append(jnp.where(low_half, even, odd))
    y = jnp.concatenate(pairs, axis=-1) * _silu(g_ref[...])
    y_ref[...] = y.astype(y_ref.dtype)


def _win_mask(grp, nb):
    assert nb >= 2
    r = np.arange(A_BLOCK)[:, None]
    c = np.arange(3 * A_BLOCK)[None, :]
    band = np.abs(r + A_BLOCK - c) <= A_WINDOW
    variants = [band & (c >= A_BLOCK), band, band & (c < 2 * A_BLOCK)]
    return np.stack([np.tile(np.where(v, 0.0, NEG_INF).astype(np.float32), (grp, 1)) for v in variants])


def _win_attn(proj, cache_k, cache_v, layer, cos, sin_signed, gq, gk, sink, bsz, t_len):
    nb = t_len // A_BLOCK
    kvw = A_KV_HEADS * HEAD_DIM
    grp = A_HEADS // A_KV_HEADS
    p_len = cache_k.shape[3]
    dup = np.zeros((kvw, 2 * kvw), np.float32)
    for kv in range(A_KV_HEADS):
        for half in range(2):
            for d in range(HEAD_DIM):
                dup[kv * HEAD_DIM + d, kv * LANE + half * HEAD_DIM + d] = 1.0
    const2 = lambda b, j: (0, 0)
    return pl.pallas_call(
        _win_attn_kernel,
        grid=(bsz, nb),
        in_specs=[
            pl.BlockSpec((A_BLOCK, BRANCH_W), lambda b, j: (b * nb + j, C_AQ // 4)),
            pl.BlockSpec((t_len, kvw), lambda b, j: (b, C_AK)),
            pl.BlockSpec((t_len, kvw), lambda b, j: (b, C_AV)),
            pl.BlockSpec((A_BLOCK, BRANCH_W), lambda b, j: (b * nb + j, C_AG // 4)),
            pl.BlockSpec((1, 1, kvw, p_len), lambda b, j: (b, layer, 0, 0)),
            pl.BlockSpec((1, 1, kvw, p_len), lambda b, j: (b, layer, 0, 0)),
            pl.BlockSpec((t_len, BRANCH_W), const2),
            pl.BlockSpec((t_len, BRANCH_W), const2),
            pl.BlockSpec((1, BRANCH_W), const2),
            pl.BlockSpec((1, kvw), const2),
            pl.BlockSpec((1, A_HEADS), const2),
            pl.BlockSpec((BRANCH_W, BRANCH_W), const2),
            pl.BlockSpec((kvw, kvw), const2),
            pl.BlockSpec((kvw, 2 * kvw), const2),
            pl.BlockSpec((3, grp * A_BLOCK, 3 * A_BLOCK), lambda b, j: (0, 0, 0)),
        ],
        out_specs=pl.BlockSpec((A_BLOCK, BRANCH_W), lambda b, j: (b * nb + j, 0)),
        out_shape=jax.ShapeDtypeStruct((bsz * t_len, BRANCH_W), BF16),
        scratch_shapes=[pltpu.VMEM((t_len + 2 * A_BLOCK, 2 * kvw), BF16),
                        pltpu.VMEM((t_len + 2 * A_BLOCK, 2 * kvw), BF16),
                        pltpu.VMEM((p_len, 2 * kvw), BF16),
                        pltpu.VMEM((p_len, 2 * kvw), BF16)],
        compiler_params=_cparams(("arbitrary", "arbitrary")),
        name="win_attn",
    )(proj, proj, proj, proj, cache_k, cache_v, cos, sin_signed,
      jnp.tile(gq, A_HEADS).reshape(1, BRANCH_W), jnp.tile(gk, A_KV_HEADS).reshape(1, kvw),
      sink.reshape(1, A_HEADS), _head_mean_matrix(BRANCH_W), _head_mean_matrix(kvw),
      jnp.asarray(dup, dtype=BF16), jnp.asarray(_win_mask(grp, nb)))


NBR_QB = 2 * GRID_W
NBR_BAND = 5


def _nbr_plan(rows):
    kh = min(NB_ROWS, rows)
    nblk = rows // 2
    specs, plan, starts = {}, [], []
    for i in range(nblk):
        s0 = min(max(i - 2, 0), nblk - NBR_BAND)
        starts.append(s0)
        blk = []
        for a in range(2):
            qr = 2 * i + a
            rs = min(max(qr - kh // 2, 0), rows - kh)
            assert 2 * s0 <= rs and rs + kh <= 2 * (s0 + NBR_BAND)
            row = []
            for p in range(NBR_BAND):
                pair = tuple(kr - qr + NB_ROWS - 1 if rs <= kr < rs + kh else None
                             for kr in (2 * (s0 + p), 2 * (s0 + p) + 1))
                row.append(specs.setdefault(pair, len(specs)))
            blk.append(row)
        plan.append(blk)
    return tuple(starts), plan, list(specs)


def _nbr_table(rpb, specs):
    heads = rpb.shape[0]
    c = np.arange(GRID_W)
    cs = np.clip(c - NB_COLS // 2, 0, GRID_W - NB_COLS)
    col_ok = (c[None, :] >= cs[:, None]) & (c[None, :] < cs[:, None] + NB_COLS)
    edge = GRID_W - NB_COLS
    period = 2 * GRID_W - 1
    padded = jnp.pad(rpb.astype(F32), ((0, 0), (0, 0), (edge, edge)))
    tiled = jnp.tile(padded, (1, 1, GRID_W + 1))[:, :, :GRID_W * (period + 1)]
    skew = tiled.reshape(heads, 2 * NB_ROWS - 1, GRID_W, period + 1)
    shifted = skew[:, :, ::-1, :GRID_W]
    by_col = jnp.where(col_ok[None, None], shifted, NEG_INF)
    neg = jnp.full((heads, GRID_W, GRID_W), NEG_INF, F32)
    blocks = [jnp.concatenate([neg if d is None else by_col[:, d] for d in spec], axis=-1) for spec in specs]
    return jnp.stack(blocks, axis=1)


def _nbr_attn_kernel(q_ref, k_ref, v_ref, g_ref, kc_ref, vc_ref, tb_ref, gq_ref, gk_ref, bd_ref, y_ref,
                     *, starts, plan):
    nband = NBR_BAND * NBR_QB
    bd = bd_ref[...]
    qn = (_heads_rms(q_ref[...], bd, gq_ref[...]) * QK_SCALE).astype(BF16)
    kn = _heads_rms(k_ref[...], bd, gk_ref[...]).astype(BF16)
    vb = v_ref[...].astype(BF16)
    kcb = kc_ref[0, 0].T.astype(BF16)
    vcb = vc_ref[0, 0].T.astype(BF16)
    acc = [jnp.zeros((NBR_QB, LANE), F32) for _ in starts]
    for h in range(LANE // HEAD_DIM):
        hm = _head_lane_mask(LANE, h)
        km = jnp.where(hm, kn, jnp.zeros_like(kn))
        kcm = jnp.where(hm, kcb, jnp.zeros_like(kcb))
        vm = jnp.concatenate([jnp.where(hm, vb, jnp.zeros_like(vb)), jnp.ones_like(vb)], axis=1)
        vcm = jnp.concatenate([jnp.where(hm, vcb, jnp.zeros_like(vcb)), jnp.ones_like(vcb)], axis=1)
        for i, s0 in enumerate(starts):
            qi = qn[i * NBR_QB:(i + 1) * NBR_QB]
            ks = slice(s0 * NBR_QB, s0 * NBR_QB + nband)
            s_raw = lax.dot_general(qi, km[ks], _NT, preferred_element_type=F32)
            s_loc = jnp.concatenate(
                [jnp.concatenate([s_raw[a * GRID_W:(a + 1) * GRID_W, p * LANE:(p + 1) * LANE]
                                  + tb_ref[h, plan[i][a][p]] for p in range(NBR_BAND)], axis=1)
                 for a in range(2)], axis=0)
            s_ctx = lax.dot_general(qi, kcm, _NT, preferred_element_type=F32)
            m = jnp.maximum(jnp.max(s_loc, axis=-1, keepdims=True), jnp.max(s_ctx, axis=-1, keepdims=True))
            p_loc = jnp.exp(s_loc - m).astype(BF16)
            p_ctx = jnp.exp(s_ctx - m).astype(BF16)
            oe = (jnp.dot(p_loc, vm[ks], preferred_element_type=F32)
                  + jnp.dot(p_ctx, vcm, preferred_element_type=F32))
            acc[i] = acc[i] + oe[:, :LANE] / oe[:, LANE:]
    y = jnp.concatenate(acc, axis=0) * _silu(g_ref[...])
    y_ref[...] = y.astype(y_ref.dtype)


def _nbr_attn(proj, cache_k, cache_v, layer, rpb, gq, gk, bsz, t_len):
    hp = LANE // HEAD_DIM
    nhp = B_HEADS // hp
    p_len = cache_k.shape[3]
    rows = t_len // GRID_W
    assert rows % 2 == 0 and rows // 2 >= NBR_BAND
    starts, plan, specs = _nbr_plan(rows)
    table = _nbr_table(rpb, specs)
    const = lambda h, b: (0, 0)
    kern = functools.partial(_nbr_attn_kernel, starts=starts, plan=plan)
    return pl.pallas_call(
        kern,
        grid=(nhp, bsz),
        in_specs=[
            pl.BlockSpec((t_len, LANE), lambda h, b: (b, C_BQ + h)),
            pl.BlockSpec((t_len, LANE), lambda h, b: (b, C_BK + h)),
            pl.BlockSpec((t_len, LANE), lambda h, b: (b, C_BV + h)),
            pl.BlockSpec((t_len, LANE), lambda h, b: (b, C_BG + h)),
            pl.BlockSpec((1, 1, LANE, p_len), lambda h, b: (b, layer, h, 0)),
            pl.BlockSpec((1, 1, LANE, p_len), lambda h, b: (b, layer, h, 0)),
            pl.BlockSpec((hp, len(specs), GRID_W, 2 * GRID_W), lambda h, b: (h, 0, 0, 0)),
            pl.BlockSpec((1, LANE), const),
            pl.BlockSpec((1, LANE), const),
            pl.BlockSpec((LANE, LANE), const),
        ],
        out_specs=pl.BlockSpec((t_len, LANE), lambda h, b: (b, h)),
        out_shape=jax.ShapeDtypeStruct((bsz * t_len, BRANCH_W), BF16),
        compiler_params=_cparams(("arbitrary", "arbitrary")),
        name="nbr_attn",
    )(proj, proj, proj, proj, cache_k, cache_v, table,
      jnp.tile(gq, hp).reshape(1, LANE), jnp.tile(gk, hp).reshape(1, LANE), _head_mean_matrix(LANE))


LRU_CHUNKS = SUBLANE
LRU_PITCH_PAD = 4


def _lru_kernel(cx_ref, cg_ref, h0_ref, cw_ref, cb_ref, wg_ref, bg_ref, lam_ref,
                y_ref, st_ref, a_s, u_s, h_s, p_s, y_s):
    t_len = cx_ref.shape[0]
    w = LRU_WIDTH
    cx = cx_ref[...]
    row = lax.broadcasted_iota(jnp.int32, (t_len, w), 0)
    xc = cb_ref[...] + cx * cw_ref[2:3, :]
    xc = xc + jnp.where(row >= 2, pltpu.roll(cx, 2, axis=0), 0.0) * cw_ref[0:1, :]
    xc = xc + jnp.where(row >= 1, pltpu.roll(cx, 1, axis=0), 0.0) * cw_ref[1:2, :]
    xc = xc + jnp.where(row < t_len - 1, pltpu.roll(cx, t_len - 1, axis=0), 0.0) * cw_ref[3:4, :]

    gates = jnp.dot(xc.astype(BF16), wg_ref[...], preferred_element_type=F32) + bg_ref[...]
    coeffs = []
    for d in range(2):
        th_r = jnp.tanh(gates[:, d * w:(d + 1) * w])
        th_i = jnp.tanh(gates[:, (2 + d) * w:(3 + d) * w])
        nl = -lam_ref[d:d + 1, :]
        softplus = jnp.maximum(nl, 0.0) + jnp.log1p(jnp.exp(-jnp.abs(nl)))
        quarter_c = (-0.25 * LRU_C) * softplus
        half_log_a = quarter_c * th_r + quarter_c
        t = jnp.tanh(half_log_a)
        rc = 1.0 / (1.0 - t)
        coeffs.append(((1.0 + t) * rc,
                       jnp.sqrt(-t) * rc * (1.0 + th_i) * xc))

    chunk_len = t_len // LRU_CHUNKS
    pitch = chunk_len + LRU_PITCH_PAD
    nt = w // LANE
    for d, (a_val, u_val) in enumerate(coeffs):
        for c in range(LRU_CHUNKS):
            for k in range(nt):
                dst = slice(c * pitch, c * pitch + chunk_len)
                src = (slice(c * chunk_len, (c + 1) * chunk_len), slice(k * LANE, (k + 1) * LANE))
                a_s[d, k, dst, :] = a_val[src]
                u_s[d, k, dst, :] = u_val[src]

    def scan_body(s, carry):
        hs, ps = carry
        new_h, new_p = [], []
        for d in range(2):
            pos = s if d == 0 else chunk_len - 1 - s
            idx = pl.ds(pos, LRU_CHUNKS, stride=pitch)
            for k in range(nt):
                av = a_s[d, k, idx, :]
                h = av * hs[d * nt + k] + u_s[d, k, idx, :]
                p = av * ps[d * nt + k]
                h_s[d, k, idx, :] = h
                p_s[d, k, idx, :] = p
                new_h.append(h)
                new_p.append(p)
        return tuple(new_h), tuple(new_p)

    zero = jnp.zeros((LRU_CHUNKS, LANE), F32)
    one = jnp.ones((LRU_CHUNKS, LANE), F32)
    h_end, p_end = lax.fori_loop(0, chunk_len, scan_body, ((zero,) * (2 * nt), (one,) * (2 * nt)), unroll=4)

    h0 = h0_ref[0]
    enter, finals = [], []
    for d in range(2):
        order = range(LRU_CHUNKS) if d == 0 else range(LRU_CHUNKS - 1, -1, -1)
        for k in range(nt):
            he, pe = h_end[d * nt + k], p_end[d * nt + k]
            e = h0[d:d + 1, k * LANE:(k + 1) * LANE]
            rows = [None] * LRU_CHUNKS
            for c in order:
                rows[c] = e
                e = pe[c:c + 1, :] * e + he[c:c + 1, :]
            enter.append(jnp.concatenate(rows, axis=0))
            finals.append(e)
    st_ref[0] = jnp.concatenate([jnp.concatenate(finals[d * nt:(d + 1) * nt], axis=1) for d in range(2)], axis=0)

    def fix_body(s, carry):
        idx = pl.ds(s, LRU_CHUNKS, stride=pitch)
        for k in range(nt):
            hf = h_s[0, k, idx, :] + p_s[0, k, idx, :] * enter[k]
            hb = h_s[1, k, idx, :] + p_s[1, k, idx, :] * enter[nt + k]
            y_s[k, idx, :] = hf + hb
        return carry

    lax.fori_loop(0, chunk_len, fix_body, 0, unroll=4)
    hsum = jnp.concatenate(
        [jnp.concatenate([y_s[k, c * pitch:c * pitch + chunk_len, :] for c in range(LRU_CHUNKS)], axis=0)
         for k in range(nt)], axis=1)
    y_ref[...] = (hsum * _silu(cg_ref[...])).astype(y_ref.dtype)


def _lru(proj, h0, conv_w, conv_b, w_gates, b_gates, lam, bsz, t_len):
    w = LRU_WIDTH
    assert t_len % (LRU_CHUNKS * SUBLANE) == 0
    rows_pad = LRU_CHUNKS * (t_len // LRU_CHUNKS + LRU_PITCH_PAD)
    return pl.pallas_call(
        _lru_kernel,
        grid=(bsz,),
        in_specs=[
            pl.BlockSpec((t_len, w), lambda b: (b, C_CX // 4)),
            pl.BlockSpec((t_len, w), lambda b: (b, C_CG // 4)),
            pl.BlockSpec((1, 2, w), lambda b: (b, 0, 0)),
            pl.BlockSpec((CONV_W, w), lambda b: (0, 0)),
            pl.BlockSpec((1, w), lambda b: (0, 0)),
            pl.BlockSpec((w, 4 * w), lambda b: (0, 0)),
            pl.BlockSpec((1, 4 * w), lambda b: (0, 0)),
            pl.BlockSpec((2, w), lambda b: (0, 0)),
        ],
        out_specs=[
            pl.BlockSpec((t_len, w), lambda b: (b, 0)),
            pl.BlockSpec((1, 2, w), lambda b: (b, 0, 0)),
        ],
        out_shape=[
            jax.ShapeDtypeStruct((bsz * t_len, w), BF16),
            jax.ShapeDtypeStruct((bsz, 2, w), F32),
        ],
        scratch_shapes=[pltpu.VMEM((2, w // LANE, rows_pad, LANE), F32) for _ in range(4)]
        + [pltpu.VMEM((w // LANE, rows_pad, LANE), F32)],
        compiler_params=_cparams(("arbitrary",)),
        name="lru",
    )(proj, proj, h0, conv_w, conv_b.reshape(1, w), w_gates, b_gates, lam)


def _lru_gate_weights(wa, ba, wx, bx):
    def dense(wblk):
        eye = jnp.eye(LRU_BLOCKS, dtype=wblk.dtype)
        return jnp.einsum('nkj,nm->nkmj', wblk, eye).reshape(LRU_WIDTH, LRU_WIDTH)
    wg = jnp.concatenate([dense(wa[0]), dense(wa[1]), dense(wx[0]), dense(wx[1])], axis=1)
    bg = jnp.concatenate([ba[0], ba[1], bx[0], bx[1]]).reshape(1, 4 * LRU_WIDTH)
    return (0.5 * wg).astype(BF16), 0.5 * bg


def _merge_kernel(x_ref, ya_ref, yb_ref, yc_ref, mod_ref, g_ref, wgate_ref, wbr_ref, wout_ref, o_ref):
    x = x_ref[...]
    h = _modulated_norm(x, g_ref, mod_ref)
    gates = jnp.dot(h, wgate_ref[0], preferred_element_type=F32)
    z = None
    for k, y_ref in enumerate((ya_ref, yb_ref, yc_ref)):
        term = (_sigmoid(gates[:, k * D_MODEL:(k + 1) * D_MODEL])
                * jnp.dot(y_ref[...], wbr_ref[0, k], preferred_element_type=F32))
        z = term if z is None else z + term
    out = jnp.dot(z.astype(BF16), wout_ref[0], preferred_element_type=F32)
    o_ref[...] = x + mod_ref[0, 0, 2:3, :] * out


def _merge(x2d, ya, yb, yc, mods, norm_g, w_gate, w_br, w_out, layer, rows_per_mod, mod_row0):
    tokens = x2d.shape[0]
    tm = 512
    tiles_per_mod = rows_per_mod // tm if rows_per_mod else 0

    def mod_idx(i):
        if rows_per_mod:
            return (layer, mod_row0 + i // tiles_per_mod, 0, 0)
        return (layer, mod_row0, 0, 0)

    return pl.pallas_call(
        _merge_kernel,
        grid=(tokens // tm,),
        in_specs=[
            pl.BlockSpec((tm, D_MODEL), lambda i: (i, 0)),
            pl.BlockSpec((tm, BRANCH_W), lambda i: (i, 0)),
            pl.BlockSpec((tm, BRANCH_W), lambda i: (i, 0)),
            pl.BlockSpec((tm, BRANCH_W), lambda i: (i, 0)),
            pl.BlockSpec((1, 1, 3, D_MODEL), mod_idx),
            pl.BlockSpec((1, 1, D_MODEL), lambda i: (layer, 0, 0)),
            pl.BlockSpec((1, D_MODEL, GATE_COLS), lambda i: (layer, 0, 0)),
            pl.BlockSpec((1, 3, BRANCH_W, D_MODEL), lambda i: (layer, 0, 0, 0)),
            pl.BlockSpec((1, D_MODEL, D_MODEL), lambda i: (layer, 0, 0)),
        ],
        out_specs=pl.BlockSpec((tm, D_MODEL), lambda i: (i, 0)),
        out_shape=jax.ShapeDtypeStruct((tokens, D_MODEL), F32),
        compiler_params=_cparams(("arbitrary",)),
        name="merge",
    )(x2d, ya, yb, yc, mods, norm_g.reshape(DEPTH, 1, D_MODEL), w_gate, w_br, w_out)


def _rope_tables(t_len):
    t = jnp.arange(t_len)
    m = HEAD_DIM // 4
    freqs = ROPE_BASE ** (-jnp.arange(m, dtype=F32) / m)
    ang_r = (t // GRID_W).astype(F32)[:, None] * freqs[None, :]
    ang_c = (t % GRID_W).astype(F32)[:, None] * freqs[None, :]
    cos = jnp.concatenate([jnp.cos(ang_r), jnp.cos(ang_r), jnp.cos(ang_c), jnp.cos(ang_c)], axis=-1)
    sin = jnp.concatenate([-jnp.sin(ang_r), jnp.sin(ang_r), -jnp.sin(ang_c), jnp.sin(ang_c)], axis=-1)
    reps = BRANCH_W // HEAD_DIM
    return jnp.tile(cos, (1, reps)), jnp.tile(sin, (1, reps))


def kernel(x_prompt, x_sample, cache_ka, cache_va, cache_kb, cache_vb, state_lru, c, c_ctx,
           norm_g, w_ada, b_ada, w_in, a_q_norm, a_k_norm, a_sink, b_q_norm, b_k_norm, b_rpb,
           lru_conv_w, lru_conv_b, lru_wa, lru_ba, lru_wx, lru_bx, lru_lambda, w_branch, w_out):
    bsz, s_len, _ = x_prompt.shape
    dbsz, t_len, _ = x_sample.shape
    p_len = cache_ka.shape[2]

    n_mod = 16
    ctx_row = dbsz
    cond = jnp.zeros((n_mod, D_MODEL), F32).at[:dbsz].set(c).at[ctx_row].set(c_ctx)
    mods = _mods(cond, w_ada, b_ada).reshape(DEPTH, n_mod, 3, D_MODEL)

    w_perm = w_gate = _wprep(w_in)
    w_br = w_branch.astype(BF16)
    w_o = w_out.astype(BF16)
    cos, sin_signed = _rope_tables(t_len)

    def cache_t(x):
        return x.transpose(0, 1, 3, 4, 2).reshape(x.shape[0], DEPTH, x.shape[3] * HEAD_DIM, p_len)
    cka, cva, ckb, cvb = cache_t(cache_ka), cache_t(cache_va), cache_t(cache_kb), cache_t(cache_vb)

    yp = x_prompt.reshape(bsz * s_len, D_MODEL)
    ys = x_sample.reshape(dbsz * t_len, D_MODEL)
    zero_state = jnp.zeros((bsz, 2, LRU_WIDTH), F32)
    new_caches, new_lru = None, []
    for l in range(DEPTH):
        wg, bg = _lru_gate_weights(lru_wa[l], lru_ba[l], lru_wx[l], lru_bx[l])

        proj = _inproj(yp, mods, norm_g, w_perm, l, 0, ctx_row)
        ya, yb, new_caches = _ctx_attn(proj, bsz, s_len, a_q_norm[l], a_k_norm[l], a_sink[l],
                                       b_q_norm[l], b_k_norm[l], l, new_caches)
        yc, st = _lru(proj, zero_state, lru_conv_w[l], lru_conv_b[l], wg, bg, lru_lambda[l], bsz, s_len)
        yp = _merge(yp, ya, yb, yc, mods, norm_g, w_gate, w_br, w_o, l, 0, ctx_row)
        new_lru.append(st)

        proj = _inproj(ys, mods, norm_g, w_perm, l, t_len, 0)
        ya = _win_attn(proj, cka, cva, l, cos, sin_signed, a_q_norm[l], a_k_norm[l], a_sink[l], dbsz, t_len)
        yb = _nbr_attn(proj, ckb, cvb, l, b_rpb[l], b_q_norm[l], b_k_norm[l], dbsz, t_len)
        yc, _ = _lru(proj, state_lru[:, l], lru_conv_w[l], lru_conv_b[l], wg, bg, lru_lambda[l], dbsz, t_len)
        ys = _merge(ys, ya, yb, yc, mods, norm_g, w_gate, w_br, w_o, l, t_len, 0)

    def cache_out(x):
        return x.reshape(bsz, DEPTH, x.shape[2] // HEAD_DIM, HEAD_DIM, s_len).transpose(0, 1, 4, 2, 3)

    return (yp.reshape(bsz, s_len, D_MODEL), ys.reshape(dbsz, t_len, D_MODEL),
            *(cache_out(x) for x in new_caches), jnp.stack(new_lru, axis=1))
```

```python
import functools

import numpy as np
import jax
import jax.numpy as jnp
from jax import lax
from jax.experimental import pallas as pl
from jax.experimental.pallas import tpu as pltpu

F32 = jnp.float32
BF16 = jnp.bfloat16

D_MODEL = 1024
DEPTH = 2
GRID_W = 64
HEAD_DIM = 64
BRANCH_W = 512
A_HEADS = 8
A_KV_HEADS = 2
A_WINDOW = 128
A_BLOCK = 128
B_HEADS = 8
NB_ROWS = 8
NB_COLS = 16
LRU_WIDTH = 512
LRU_BLOCKS = 8
LRU_BW = LRU_WIDTH // LRU_BLOCKS
LRU_C = 8.0
CONV_W = 4
ROPE_BASE = 10000.0
EPS = 1e-6
NEG_INF = -1e30
QK_SCALE = HEAD_DIM ** -0.5

LANE = 128
SUBLANE = 8
MXU_DIM = 256
VMEM_LIMIT = 56 * 1024 * 1024

_ORIG_SPLITS = (512, 128, 128, 512, 512, 512, 512, 512, 512, 512, 1024, 1024, 1024)
_ORIG_OFFS = tuple(int(v) for v in np.cumsum((0,) + _ORIG_SPLITS)[:-1])
_PERM = (0, 3, 4, 5, 6, 7, 8, 9, 1, 2)
_GATE_SPLITS = (10, 11, 12)
IN_COLS = sum(_ORIG_SPLITS[k] for k in _PERM)
GATE_COLS = sum(_ORIG_SPLITS[k] for k in _GATE_SPLITS)
C_AQ, C_AG, C_BQ, C_BK, C_BV, C_BG, C_CX, C_CG, C_AK, C_AV = (0, 4, 8, 12, 16, 20, 24, 28, 32, 33)

_NT = (((1,), (1,)), ((), ()))


def _cparams(sem):
    return pltpu.CompilerParams(dimension_semantics=sem, vmem_limit_bytes=VMEM_LIMIT)


def _sigmoid(x):
    return 0.5 + 0.5 * jnp.tanh(0.5 * x)


def _silu(x):
    return x * _sigmoid(x)


def _head_mean_matrix(width):
    idx = np.arange(width) // HEAD_DIM
    return jnp.asarray((idx[:, None] == idx[None, :]).astype(np.float32) / HEAD_DIM, dtype=BF16)


def _heads_rms(x, bd, g):
    x2 = x * x
    hi = x2.astype(BF16)
    lo = (x2 - hi.astype(F32)).astype(BF16)
    width = x.shape[-1]
    step = min(width, MXU_DIM)
    tile = bd[:step, :step]
    ms = jnp.concatenate(
        [jnp.dot(hi[:, c:c + step], tile, preferred_element_type=F32)
         + jnp.dot(lo[:, c:c + step], tile, preferred_element_type=F32) for c in range(0, width, step)], axis=-1)
    return x * lax.rsqrt(ms + EPS) * g


def _head_lane_mask(width, h):
    lane = lax.broadcasted_iota(jnp.int32, (1, width), 1)
    return (lane >= h * HEAD_DIM) & (lane < (h + 1) * HEAD_DIM)


def _mods_kernel(c_ref, w_ref, b_ref, o_ref):
    c = c_ref[...]
    s = _silu(c).astype(BF16)
    o_ref[0] = jnp.dot(s, w_ref[0].astype(BF16), preferred_element_type=F32) + b_ref[0]


def _mods(cond, w_ada, b_ada):
    n = cond.shape[0]
    tn = D_MODEL
    return pl.pallas_call(
        _mods_kernel,
        grid=(DEPTH, 3 * D_MODEL // tn),
        in_specs=[
            pl.BlockSpec((n, D_MODEL), lambda l, j: (0, 0)),
            pl.BlockSpec((1, D_MODEL, tn), lambda l, j: (l, 0, j)),
            pl.BlockSpec((1, 1, tn), lambda l, j: (l, 0, j)),
        ],
        out_specs=pl.BlockSpec((1, n, tn), lambda l, j: (l, 0, j)),
        out_shape=jax.ShapeDtypeStruct((DEPTH, n, 3 * D_MODEL), F32),
        compiler_params=_cparams(("arbitrary", "arbitrary")),
        name="mods",
    )(cond, w_ada, b_ada.reshape(DEPTH, 1, 3 * D_MODEL))


W_BLK = 256
_MIXER_BLKS = IN_COLS // W_BLK
_GATE_BLKS = GATE_COLS // W_BLK
W_MIXER_OFF = IN_COLS
W_PREP_COLS = 2 * IN_COLS


_GAP_BLKS = W_MIXER_OFF // W_BLK - _GATE_BLKS


def _wprep_kernel(w_ref, o_ref):
    n = pl.program_id(0)
    in_gap = (n >= _GATE_BLKS) & (n < _GATE_BLKS + _GAP_BLKS)
    o_ref[...] = jnp.where(in_gap, 0.0, w_ref[...]).astype(BF16)


def _wprep_src(n):
    m = n - _GATE_BLKS - _GAP_BLKS
    mixer = jnp.where(m < 0, 0, jnp.where(m < 2, m, jnp.where(m < _MIXER_BLKS - 1, m + 1, 2)))
    return jnp.where(n < _GATE_BLKS, n + _MIXER_BLKS, mixer)


def _wprep(w_in):
    assert [_ORIG_OFFS[k] // W_BLK for k in _PERM[:-2]] == [0, 3, 5, 7, 9, 11, 13, 15] and _ORIG_OFFS[1] == 2 * W_BLK
    return pl.pallas_call(
        _wprep_kernel,
        grid=(W_PREP_COLS // W_BLK,),
        in_specs=[pl.BlockSpec((DEPTH, D_MODEL, W_BLK), lambda n: (0, 0, _wprep_src(n)))],
        out_specs=pl.BlockSpec((DEPTH, D_MODEL, W_BLK), lambda n: (0, 0, n)),
        out_shape=jax.ShapeDtypeStruct((DEPTH, D_MODEL, W_PREP_COLS), BF16),
        compiler_params=_cparams(("arbitrary",)),
        name="wprep",
    )(w_in)


def _modulated_norm(x, g_ref, mod_ref):
    y = x * lax.rsqrt(jnp.mean(x * x, axis=-1, keepdims=True) + EPS)
    y = y * g_ref[0]
    shift = mod_ref[0, 0, 0:1, :]
    scale = mod_ref[0, 0, 1:2, :]
    return (y * (1.0 + scale) + shift).astype(BF16)


def _inproj_kernel(x_ref, mod_ref, g_ref, w_ref, o_ref):
    h = _modulated_norm(x_ref[...], g_ref, mod_ref)
    o_ref[...] = jnp.dot(h, w_ref[0], preferred_element_type=F32)


def _inproj(x2d, mods, norm_g, w_perm, layer, rows_per_mod, mod_row0):
    tokens = x2d.shape[0]
    tm = 512
    tiles_per_mod = rows_per_mod // tm if rows_per_mod else 0

    def mod_idx(i):
        if rows_per_mod:
            return (layer, mod_row0 + i // tiles_per_mod, 0, 0)
        return (layer, mod_row0, 0, 0)

    return pl.pallas_call(
        _inproj_kernel,
        grid=(tokens // tm,),
        in_specs=[
            pl.BlockSpec((tm, D_MODEL), lambda i: (i, 0)),
            pl.BlockSpec((1, 1, 3, D_MODEL), mod_idx),
            pl.BlockSpec((1, 1, D_MODEL), lambda i: (layer, 0, 0)),
            pl.BlockSpec((1, D_MODEL, IN_COLS), lambda i: (layer, 0, W_MIXER_OFF // IN_COLS)),
        ],
        out_specs=pl.BlockSpec((tm, IN_COLS), lambda i: (i, 0)),
        out_shape=jax.ShapeDtypeStruct((tokens, IN_COLS), F32),
        compiler_params=_cparams(("arbitrary",)),
        name="inproj",
    )(x2d, mods, norm_g.reshape(DEPTH, 1, D_MODEL), w_perm)


CTX_SEQS = 2


def _ctx_mixer(q, k, v, g, gq, gk, bdq, bdk, ones_stack, expand=None, sink_ref=None):
    s_len = q.shape[0]
    n_q = q.shape[1] // HEAD_DIM
    kn = _heads_rms(k, bdk, gk)
    qn = (_heads_rms(q, bdq, gq) * QK_SCALE).astype(BF16)
    knb = kn.astype(BF16)
    vb = v.astype(BF16)
    if expand is not None:
        knb = jnp.dot(knb, expand, preferred_element_type=F32).astype(BF16)
        vb = jnp.dot(vb, expand, preferred_element_type=F32).astype(BF16)
    heads_per_half = MXU_DIM // HEAD_DIM
    masks = [_head_lane_mask(MXU_DIM, hi) for hi in range(heads_per_half)]
    halves = []
    for hh in range(n_q // heads_per_half):
        cols = slice(hh * MXU_DIM, (hh + 1) * MXU_DIM)
        qh, kh, vh = qn[:, cols], knb[:, cols], vb[:, cols]
        kstack = jnp.concatenate([jnp.where(mk, kh, jnp.zeros_like(kh)) for mk in masks], axis=0)
        vstack = jnp.concatenate([jnp.where(mk, vh, jnp.zeros_like(vh)) for mk in masks], axis=0)
        s = lax.dot_general(qh, kstack, _NT, preferred_element_type=F32)
        ps, sink_terms = [], []
        for hi in range(heads_per_half):
            si = s[:, hi * s_len:(hi + 1) * s_len]
            m = jnp.max(si, axis=-1, keepdims=True)
            if sink_ref is not None:
                h = hh * heads_per_half + hi
                snk = sink_ref[0:1, h:h + 1]
                m = jnp.maximum(m, snk)
                sink_terms.append(jnp.where(masks[hi], jnp.exp(snk - m), 0.0))
            ps.append(jnp.exp(si - m).astype(BF16))
        p = jnp.concatenate(ps, axis=1)
        rhs = jnp.concatenate([vstack, ones_stack], axis=1)
        oe = jnp.dot(p, rhs, preferred_element_type=F32)
        l = oe[:, MXU_DIM:]
        for term in sink_terms:
            l = l + term
        halves.append(oe[:, :MXU_DIM] / l)
    return jnp.concatenate(halves, axis=-1) * _silu(g), kn


_CTX_INPUTS = 17


def _ctx_attn_kernel(*refs, s_len, n_alias):
    (aq_ref, ak_ref, av_ref, ag_ref, bq_ref, bk_ref, bv_ref, bg_ref,
     gqa_ref, gka_ref, sink_ref, gqb_ref, gkb_ref, bdw_ref, bdn_ref, exp_ref, ones_ref) = refs[:_CTX_INPUTS]
    ya_ref, yb_ref, ka_ref, va_ref, kb_ref, vb_ref = refs[_CTX_INPUTS + n_alias:]
    ones_stack = ones_ref[...]
    bdw = bdw_ref[...]
    for n in range(aq_ref.shape[0] // s_len):
        rows = slice(n * s_len, (n + 1) * s_len)
        va = av_ref[rows, :]
        vb = bv_ref[rows, :]
        va_ref[n, 0] = va.T
        vb_ref[n, 0] = vb.T
        ya, kna = _ctx_mixer(aq_ref[rows, :], ak_ref[rows, :], va, ag_ref[rows, :], gqa_ref[...], gka_ref[...],
                             bdw, bdn_ref[...], ones_stack, expand=exp_ref[...], sink_ref=sink_ref)
        yb, knb = _ctx_mixer(bq_ref[rows, :], bk_ref[rows, :], vb, bg_ref[rows, :], gqb_ref[...], gkb_ref[...],
                             bdw, bdw, ones_stack)
        ka_ref[n, 0] = kna.T
        kb_ref[n, 0] = knb.T
        ya_ref[rows, :] = ya.astype(ya_ref.dtype)
        yb_ref[rows, :] = yb.astype(yb_ref.dtype)


def _ctx_attn(proj, bsz, s_len, a_gq, a_gk, a_sink, b_gq, b_gk, layer, caches):
    kvw = A_KV_HEADS * HEAD_DIM
    grp = A_HEADS // A_KV_HEADS
    expand = np.zeros((kvw, BRANCH_W), np.float32)
    for h in range(A_HEADS):
        for d in range(HEAD_DIM):
            expand[(h // grp) * HEAD_DIM + d, h * HEAD_DIM + d] = 1.0
    heads_per_half = MXU_DIM // HEAD_DIM
    ones_stack = np.zeros((heads_per_half * s_len, MXU_DIM), np.float32)
    for hi in range(heads_per_half):
        ones_stack[hi * s_len:(hi + 1) * s_len, hi * HEAD_DIM:(hi + 1) * HEAD_DIM] = 1.0
    const = lambda b: (0, 0)
    rows = CTX_SEQS * s_len
    wide = lambda c: pl.BlockSpec((rows, BRANCH_W), lambda b: (b, c // 4))
    narrow = lambda c: pl.BlockSpec((rows, kvw), lambda b: (b, c))
    prev = () if caches is None else tuple(caches)
    n_alias = len(prev)
    cache_spec = lambda width: pl.BlockSpec((CTX_SEQS, 1, width, s_len), lambda b: (b, layer, 0, 0))
    cache_shape = lambda width: jax.ShapeDtypeStruct((bsz, DEPTH, width, s_len), F32)
    outs = pl.pallas_call(
        functools.partial(_ctx_attn_kernel, s_len=s_len, n_alias=n_alias),
        grid=(bsz // CTX_SEQS,),
        in_specs=[
            wide(C_AQ), narrow(C_AK), narrow(C_AV), wide(C_AG), wide(C_BQ), wide(C_BK), wide(C_BV), wide(C_BG),
            pl.BlockSpec((1, BRANCH_W), const),
            pl.BlockSpec((1, kvw), const),
            pl.BlockSpec((1, A_HEADS), const),
            pl.BlockSpec((1, BRANCH_W), const),
            pl.BlockSpec((1, BRANCH_W), const),
            pl.BlockSpec((BRANCH_W, BRANCH_W), const),
            pl.BlockSpec((kvw, kvw), const),
            pl.BlockSpec((kvw, BRANCH_W), const),
            pl.BlockSpec((heads_per_half * s_len, MXU_DIM), const),
        ] + [pl.BlockSpec(memory_space=pl.ANY)] * n_alias,
        out_specs=[
            pl.BlockSpec((rows, BRANCH_W), lambda b: (b, 0)),
            pl.BlockSpec((rows, BRANCH_W), lambda b: (b, 0)),
            cache_spec(kvw), cache_spec(kvw), cache_spec(BRANCH_W), cache_spec(BRANCH_W),
        ],
        out_shape=[
            jax.ShapeDtypeStruct((bsz * s_len, BRANCH_W), BF16),
            jax.ShapeDtypeStruct((bsz * s_len, BRANCH_W), BF16),
            cache_shape(kvw), cache_shape(kvw), cache_shape(BRANCH_W), cache_shape(BRANCH_W),
        ],
        input_output_aliases={_CTX_INPUTS + i: 2 + i for i in range(n_alias)},
        compiler_params=_cparams(("arbitrary",)),
        name="ctx_attn",
    )(proj, proj, proj, proj, proj, proj, proj, proj,
      jnp.tile(a_gq, A_HEADS).reshape(1, BRANCH_W), jnp.tile(a_gk, A_KV_HEADS).reshape(1, kvw),
      a_sink.reshape(1, A_HEADS),
      jnp.tile(b_gq, B_HEADS).reshape(1, BRANCH_W), jnp.tile(b_gk, B_HEADS).reshape(1, BRANCH_W),
      _head_mean_matrix(BRANCH_W), _head_mean_matrix(kvw), jnp.asarray(expand, dtype=BF16),
      jnp.asarray(ones_stack, dtype=BF16), *prev)
    return outs[0], outs[1], tuple(outs[2:])


def _rope(x, cos, sin_signed):
    w = x.shape[-1]
    lane = lax.broadcasted_iota(jnp.int32, x.shape, 1)
    up = pltpu.roll(x, w - 16, axis=1)
    dn = pltpu.roll(x, 16, axis=1)
    partner = jnp.where((lane & 16) == 0, up, dn)
    return x * cos + partner * sin_signed


WIN_SUB = 2


def _win_attn_kernel(q_ref, k_ref, v_ref, g_ref, kc_ref, vc_ref, cos_ref, sin_ref, gq_ref, gk_ref, sink_ref,
                     bdq_ref, bdk_ref, dup_ref, mask_ref, y_ref, kpad, vpad, kcx, vcx):
    j = pl.program_id(1)
    nb = pl.num_programs(1)
    t_len = k_ref.shape[0]
    grp = A_HEADS // A_KV_HEADS
    kvw = A_KV_HEADS * HEAD_DIM
    xw = 2 * kvw

    @pl.when(j == 0)
    def _():
        dup = dup_ref[...]
        kn = _heads_rms(k_ref[...], bdk_ref[...], gk_ref[...])
        kn = _rope(kn, cos_ref[:, 0:kvw], sin_ref[:, 0:kvw]).astype(BF16)
        zeros = jnp.zeros((A_BLOCK, xw), BF16)
        kpad[0:A_BLOCK, :] = zeros
        kpad[A_BLOCK + t_len:2 * A_BLOCK + t_len, :] = zeros
        vpad[0:A_BLOCK, :] = zeros
        vpad[A_BLOCK + t_len:2 * A_BLOCK + t_len, :] = zeros
        kpad[A_BLOCK:A_BLOCK + t_len, :] = jnp.dot(kn, dup, preferred_element_type=F32).astype(BF16)
        vpad[A_BLOCK:A_BLOCK + t_len, :] = jnp.dot(
            v_ref[...].astype(BF16), dup, preferred_element_type=F32).astype(BF16)
        kcx[...] = jnp.dot(kc_ref[0, 0].T.astype(BF16), dup, preferred_element_type=F32).astype(BF16)
        vcx[...] = jnp.dot(vc_ref[0, 0].T.astype(BF16), dup, preferred_element_type=F32).astype(BF16)

    nloc = 3 * A_BLOCK
    low_half = lax.broadcasted_iota(jnp.int32, (1, LANE), 1) < HEAD_DIM
    for sub in range(WIN_SUB):
        jj = j * WIN_SUB + sub
        qrows = slice(sub * A_BLOCK, (sub + 1) * A_BLOCK)
        r0 = pl.multiple_of(jj * A_BLOCK, A_BLOCK)
        qn = _heads_rms(q_ref[qrows, :], bdq_ref[...], gq_ref[...])
        qb = (_rope(qn, cos_ref[pl.ds(r0, A_BLOCK), :], sin_ref[pl.ds(r0, A_BLOCK), :]) * QK_SCALE).astype(BF16)
        maskadd = mask_ref[jnp.where(jj == 0, 0, jnp.where(jj == nb * WIN_SUB - 1, 2, 1))]
        kband = kpad[pl.ds(r0, nloc), :]
        vband = vpad[pl.ds(r0, nloc), :]
        pairs = []
        for kv in range(A_KV_HEADS):
            cols = slice(kv * LANE, (kv + 1) * LANE)
            qparts, sinks = [], []
            for gi in range(grp):
                h = kv * grp + gi
                qpair = qb[:, (h // 2) * LANE:(h // 2 + 1) * LANE]
                keep = low_half if h % 2 == 0 else jnp.logical_not(low_half)
                qparts.append(jnp.where(keep, qpair, jnp.zeros_like(qpair)))
                sinks.append(jnp.broadcast_to(sink_ref[0:1, h:h + 1], (A_BLOCK, 1)))
            qst = jnp.concatenate(qparts, axis=0)
            snk = jnp.concatenate(sinks, axis=0)
            s_loc = lax.dot_general(qst, kband[:, cols], _NT, preferred_element_type=F32) + maskadd
            s_ctx = lax.dot_general(qst, kcx[:, cols], _NT, preferred_element_type=F32)
            m = jnp.maximum(jnp.maximum(jnp.max(s_loc, axis=-1, keepdims=True),
                                        jnp.max(s_ctx, axis=-1, keepdims=True)), snk)
            p_loc = jnp.exp(s_loc - m)
            p_ctx = jnp.exp(s_ctx - m)
            l = (jnp.sum(p_loc, axis=-1, keepdims=True) + jnp.sum(p_ctx, axis=-1, keepdims=True)
                 + jnp.exp(snk - m))
            o = (jnp.dot(p_loc.astype(BF16), vband[:, cols], preferred_element_type=F32)
                 + jnp.dot(p_ctx.astype(BF16), vcx[:, cols], preferred_element_type=F32)) / l
            for k2 in range(grp // 2):
                even = o[(2 * k2) * A_BLOCK:(2 * k2 + 1) * A_BLOCK]
                odd = o[(2 * k2 + 1) * A_BLOCK:(2 * k2 + 2) * A_BLOCK]
                pairs.append(jnp.where(low_half, even, odd))
        y = jnp.concatenate(pairs, axis=-1) * _silu(g_ref[qrows, :])
        y_ref[qrows, :] = y.astype(y_ref.dtype)


def _win_mask(grp, nb):
    assert nb >= 2
    r = np.arange(A_BLOCK)[:, None]
    c = np.arange(3 * A_BLOCK)[None, :]
    band = np.abs(r + A_BLOCK - c) <= A_WINDOW
    variants = [band & (c >= A_BLOCK), band, band & (c < 2 * A_BLOCK)]
    return np.stack([np.tile(np.where(v, 0.0, NEG_INF).astype(np.float32), (grp, 1)) for v in variants])


def _win_attn(proj, cache_k, cache_v, layer, cos, sin_signed, gq, gk, sink, bsz, t_len):
    nb = t_len // A_BLOCK
    assert nb % WIN_SUB == 0
    nsteps = nb // WIN_SUB
    kvw = A_KV_HEADS * HEAD_DIM
    grp = A_HEADS // A_KV_HEADS
    p_len = cache_k.shape[3]
    dup = np.zeros((kvw, 2 * kvw), np.float32)
    for kv in range(A_KV_HEADS):
        for half in range(2):
            for d in range(HEAD_DIM):
                dup[kv * HEAD_DIM + d, kv * LANE + half * HEAD_DIM + d] = 1.0
    const2 = lambda b, j: (0, 0)
    return pl.pallas_call(
        _win_attn_kernel,
        grid=(bsz, nsteps),
        in_specs=[
            pl.BlockSpec((WIN_SUB * A_BLOCK, BRANCH_W), lambda b, j: (b * nsteps + j, C_AQ // 4)),
            pl.BlockSpec((t_len, kvw), lambda b, j: (b, C_AK)),
            pl.BlockSpec((t_len, kvw), lambda b, j: (b, C_AV)),
            pl.BlockSpec((WIN_SUB * A_BLOCK, BRANCH_W), lambda b, j: (b * nsteps + j, C_AG // 4)),
            pl.BlockSpec((1, 1, kvw, p_len), lambda b, j: (b, layer, 0, 0)),
            pl.BlockSpec((1, 1, kvw, p_len), lambda b, j: (b, layer, 0, 0)),
            pl.BlockSpec((t_len, BRANCH_W), const2),
            pl.BlockSpec((t_len, BRANCH_W), const2),
            pl.BlockSpec((1, BRANCH_W), const2),
            pl.BlockSpec((1, kvw), const2),
            pl.BlockSpec((1, A_HEADS), const2),
            pl.BlockSpec((BRANCH_W, BRANCH_W), const2),
            pl.BlockSpec((kvw, kvw), const2),
            pl.BlockSpec((kvw, 2 * kvw), const2),
            pl.BlockSpec((3, grp * A_BLOCK, 3 * A_BLOCK), lambda b, j: (0, 0, 0)),
        ],
        out_specs=pl.BlockSpec((WIN_SUB * A_BLOCK, BRANCH_W), lambda b, j: (b * nsteps + j, 0)),
        out_shape=jax.ShapeDtypeStruct((bsz * t_len, BRANCH_W), BF16),
        scratch_shapes=[pltpu.VMEM((t_len + 2 * A_BLOCK, 2 * kvw), BF16),
                        pltpu.VMEM((t_len + 2 * A_BLOCK, 2 * kvw), BF16),
                        pltpu.VMEM((p_len, 2 * kvw), BF16),
                        pltpu.VMEM((p_len, 2 * kvw), BF16)],
        compiler_params=_cparams(("arbitrary", "arbitrary")),
        name="win_attn",
    )(proj, proj, proj, proj, cache_k, cache_v, cos, sin_signed,
      jnp.tile(gq, A_HEADS).reshape(1, BRANCH_W), jnp.tile(gk, A_KV_HEADS).reshape(1, kvw),
      sink.reshape(1, A_HEADS), _head_mean_matrix(BRANCH_W), _head_mean_matrix(kvw),
      jnp.asarray(dup, dtype=BF16), jnp.asarray(_win_mask(grp, nb)))


NBR_QB = 2 * GRID_W
NBR_BAND = 5


def _nbr_plan(rows):
    kh = min(NB_ROWS, rows)
    nblk = rows // 2
    specs, plan, starts = {}, [], []
    for i in range(nblk):
        s0 = min(max(i - 2, 0), nblk - NBR_BAND)
        starts.append(s0)
        blk = []
        for a in range(2):
            qr = 2 * i + a
            rs = min(max(qr - kh // 2, 0), rows - kh)
            assert 2 * s0 <= rs and rs + kh <= 2 * (s0 + NBR_BAND)
            row = []
            for p in range(NBR_BAND):
                pair = tuple(kr - qr + NB_ROWS - 1 if rs <= kr < rs + kh else None
                             for kr in (2 * (s0 + p), 2 * (s0 + p) + 1))
                row.append(specs.setdefault(pair, len(specs)))
            blk.append(row)
        plan.append(blk)
    return tuple(starts), plan, list(specs)


def _nbr_table(rpb, specs):
    heads = rpb.shape[0]
    c = np.arange(GRID_W)
    cs = np.clip(c - NB_COLS // 2, 0, GRID_W - NB_COLS)
    col_ok = (c[None, :] >= cs[:, None]) & (c[None, :] < cs[:, None] + NB_COLS)
    edge = GRID_W - NB_COLS
    period = 2 * GRID_W - 1
    padded = jnp.pad(rpb.astype(F32), ((0, 0), (0, 0), (edge, edge)))
    tiled = jnp.tile(padded, (1, 1, GRID_W + 1))[:, :, :GRID_W * (period + 1)]
    skew = tiled.reshape(heads, 2 * NB_ROWS - 1, GRID_W, period + 1)
    shifted = skew[:, :, ::-1, :GRID_W]
    by_col = jnp.where(col_ok[None, None], shifted, NEG_INF)
    neg = jnp.full((heads, GRID_W, GRID_W), NEG_INF, F32)
    blocks = [jnp.concatenate([neg if d is None else by_col[:, d] for d in spec], axis=-1) for spec in specs]
    return jnp.stack(blocks, axis=1)


def _nbr_attn_kernel(q_ref, k_ref, v_ref, g_ref, kc_ref, vc_ref, tb_ref, gq_ref, gk_ref, bd_ref, y_ref,
                     *, starts, plan):
    nband = NBR_BAND * NBR_QB
    bd = bd_ref[...]
    qn = (_heads_rms(q_ref[...], bd, gq_ref[...]) * QK_SCALE).astype(BF16)
    kn = _heads_rms(k_ref[...], bd, gk_ref[...]).astype(BF16)
    vb = v_ref[...].astype(BF16)
    kcb = kc_ref[0, 0].T.astype(BF16)
    vcb = vc_ref[0, 0].T.astype(BF16)
    acc = [jnp.zeros((NBR_QB, LANE), F32) for _ in starts]
    for h in range(LANE // HEAD_DIM):
        hm = _head_lane_mask(LANE, h)
        km = jnp.where(hm, kn, jnp.zeros_like(kn))
        kcm = jnp.where(hm, kcb, jnp.zeros_like(kcb))
        vm = jnp.concatenate([jnp.where(hm, vb, jnp.zeros_like(vb)), jnp.ones_like(vb)], axis=1)
        vcm = jnp.concatenate([jnp.where(hm, vcb, jnp.zeros_like(vcb)), jnp.ones_like(vcb)], axis=1)
        for i, s0 in enumerate(starts):
            qi = qn[i * NBR_QB:(i + 1) * NBR_QB]
            ks = slice(s0 * NBR_QB, s0 * NBR_QB + nband)
            s_raw = lax.dot_general(qi, km[ks], _NT, preferred_element_type=F32)
            s_loc = jnp.concatenate(
                [jnp.concatenate([s_raw[a * GRID_W:(a + 1) * GRID_W, p * LANE:(p + 1) * LANE]
                                  + tb_ref[h, plan[i][a][p]] for p in range(NBR_BAND)], axis=1)
                 for a in range(2)], axis=0)
            s_ctx = lax.dot_general(qi, kcm, _NT, preferred_element_type=F32)
            m = jnp.maximum(jnp.max(s_loc, axis=-1, keepdims=True), jnp.max(s_ctx, axis=-1, keepdims=True))
            p_loc = jnp.exp(s_loc - m).astype(BF16)
            p_ctx = jnp.exp(s_ctx - m).astype(BF16)
            oe = (jnp.dot(p_loc, vm[ks], preferred_element_type=F32)
                  + jnp.dot(p_ctx, vcm, preferred_element_type=F32))
            acc[i] = acc[i] + oe[:, :LANE] / oe[:, LANE:]
    y = jnp.concatenate(acc, axis=0) * _silu(g_ref[...])
    y_ref[...] = y.astype(y_ref.dtype)


def _nbr_attn(proj, cache_k, cache_v, layer, rpb, gq, gk, bsz, t_len):
    hp = LANE // HEAD_DIM
    nhp = B_HEADS // hp
    p_len = cache_k.shape[3]
    rows = t_len // GRID_W
    assert rows % 2 == 0 and rows // 2 >= NBR_BAND
    starts, plan, specs = _nbr_plan(rows)
    table = _nbr_table(rpb, specs)
    const = lambda h, b: (0, 0)
    kern = functools.partial(_nbr_attn_kernel, starts=starts, plan=plan)
    return pl.pallas_call(
        kern,
        grid=(nhp, bsz),
        in_specs=[
            pl.BlockSpec((t_len, LANE), lambda h, b: (b, C_BQ + h)),
            pl.BlockSpec((t_len, LANE), lambda h, b: (b, C_BK + h)),
            pl.BlockSpec((t_len, LANE), lambda h, b: (b, C_BV + h)),
            pl.BlockSpec((t_len, LANE), lambda h, b: (b, C_BG + h)),
            pl.BlockSpec((1, 1, LANE, p_len), lambda h, b: (b, layer, h, 0)),
            pl.BlockSpec((1, 1, LANE, p_len), lambda h, b: (b, layer, h, 0)),
            pl.BlockSpec((hp, len(specs), GRID_W, 2 * GRID_W), lambda h, b: (h, 0, 0, 0)),
            pl.BlockSpec((1, LANE), const),
            pl.BlockSpec((1, LANE), const),
            pl.BlockSpec((LANE, LANE), const),
        ],
        out_specs=pl.BlockSpec((t_len, LANE), lambda h, b: (b, h)),
        out_shape=jax.ShapeDtypeStruct((bsz * t_len, BRANCH_W), BF16),
        compiler_params=_cparams(("arbitrary", "arbitrary")),
        name="nbr_attn",
    )(proj, proj, proj, proj, cache_k, cache_v, table,
      jnp.tile(gq, hp).reshape(1, LANE), jnp.tile(gk, hp).reshape(1, LANE), _head_mean_matrix(LANE))


LRU_CHUNKS = SUBLANE
LRU_PITCH_PAD = 4


def _lru_kernel(cx_ref, cg_ref, h0_ref, cw_ref, cb_ref, wg_ref, bg_ref, lam_ref,
                y_ref, st_ref, a_s, u_s, h_s, p_s, y_s):
    t_len = cx_ref.shape[0]
    w = LRU_WIDTH
    cx = cx_ref[...]
    row = lax.broadcasted_iota(jnp.int32, (t_len, w), 0)
    xc = cb_ref[...] + cx * cw_ref[2:3, :]
    xc = xc + jnp.where(row >= 2, pltpu.roll(cx, 2, axis=0), 0.0) * cw_ref[0:1, :]
    xc = xc + jnp.where(row >= 1, pltpu.roll(cx, 1, axis=0), 0.0) * cw_ref[1:2, :]
    xc = xc + jnp.where(row < t_len - 1, pltpu.roll(cx, t_len - 1, axis=0), 0.0) * cw_ref[3:4, :]

    gates = jnp.dot(xc.astype(BF16), wg_ref[...], preferred_element_type=F32) + bg_ref[...]
    coeffs = []
    for d in range(2):
        th_r = jnp.tanh(gates[:, d * w:(d + 1) * w])
        th_i = jnp.tanh(gates[:, (2 + d) * w:(3 + d) * w])
        nl = -lam_ref[d:d + 1, :]
        softplus = jnp.maximum(nl, 0.0) + jnp.log1p(jnp.exp(-jnp.abs(nl)))
        quarter_c = (-0.25 * LRU_C) * softplus
        half_log_a = quarter_c * th_r + quarter_c
        t = jnp.tanh(half_log_a)
        rc = 1.0 / (1.0 - t)
        coeffs.append(((1.0 + t) * rc,
                       jnp.sqrt(-t) * rc * (1.0 + th_i) * xc))

    chunk_len = t_len // LRU_CHUNKS
    pitch = chunk_len + LRU_PITCH_PAD
    nt = w // LANE
    for d, (a_val, u_val) in enumerate(coeffs):
        for c in range(LRU_CHUNKS):
            for k in range(nt):
                dst = slice(c * pitch, c * pitch + chunk_len)
                src = (slice(c * chunk_len, (c + 1) * chunk_len), slice(k * LANE, (k + 1) * LANE))
                a_s[d, k, dst, :] = a_val[src]
                u_s[d, k, dst, :] = u_val[src]

    def scan_body(s, carry):
        hs, ps = carry
        new_h, new_p = [], []
        for d in range(2):
            pos = s if d == 0 else chunk_len - 1 - s
            idx = pl.ds(pos, LRU_CHUNKS, stride=pitch)
            for k in range(nt):
                av = a_s[d, k, idx, :]
                h = av * hs[d * nt + k] + u_s[d, k, idx, :]
                p = av * ps[d * nt + k]
                h_s[d, k, idx, :] = h
                p_s[d, k, idx, :] = p
                new_h.append(h)
                new_p.append(p)
        return tuple(new_h), tuple(new_p)

    zero = jnp.zeros((LRU_CHUNKS, LANE), F32)
    one = jnp.ones((LRU_CHUNKS, LANE), F32)
    h_end, p_end = lax.fori_loop(0, chunk_len, scan_body, ((zero,) * (2 * nt), (one,) * (2 * nt)), unroll=4)

    h0 = h0_ref[0]
    enter, finals = [], []
    for d in range(2):
        order = range(LRU_CHUNKS) if d == 0 else range(LRU_CHUNKS - 1, -1, -1)
        for k in range(nt):
            he, pe = h_end[d * nt + k], p_end[d * nt + k]
            e = h0[d:d + 1, k * LANE:(k + 1) * LANE]
            rows = [None] * LRU_CHUNKS
            for c in order:
                rows[c] = e
                e = pe[c:c + 1, :] * e + he[c:c + 1, :]
            enter.append(jnp.concatenate(rows, axis=0))
            finals.append(e)
    st_ref[0] = jnp.concatenate([jnp.concatenate(finals[d * nt:(d + 1) * nt], axis=1) for d in range(2)], axis=0)

    def fix_body(s, carry):
        idx = pl.ds(s, LRU_CHUNKS, stride=pitch)
        for k in range(nt):
            hf = h_s[0, k, idx, :] + p_s[0, k, idx, :] * enter[k]
            hb = h_s[1, k, idx, :] + p_s[1, k, idx, :] * enter[nt + k]
            y_s[k, idx, :] = hf + hb
        return carry

    lax.fori_loop(0, chunk_len, fix_body, 0, unroll=4)
    hsum = jnp.concatenate(
        [jnp.concatenate([y_s[k, c * pitch:c * pitch + chunk_len, :] for c in range(LRU_CHUNKS)], axis=0)
         for k in range(nt)], axis=1)
    y_ref[...] = (hsum * _silu(cg_ref[...])).astype(y_ref.dtype)


def _lru(proj, h0, conv_w, conv_b, w_gates, b_gates, lam, bsz, t_len):
    w = LRU_WIDTH
    assert t_len % (LRU_CHUNKS * SUBLANE) == 0
    rows_pad = LRU_CHUNKS * (t_len // LRU_CHUNKS + LRU_PITCH_PAD)
    return pl.pallas_call(
        _lru_kernel,
        grid=(bsz,),
        in_specs=[
            pl.BlockSpec((t_len, w), lambda b: (b, C_CX // 4)),
            pl.BlockSpec((t_len, w), lambda b: (b, C_CG // 4)),
            pl.BlockSpec((1, 2, w), lambda b: (b, 0, 0)),
            pl.BlockSpec((CONV_W, w), lambda b: (0, 0)),
            pl.BlockSpec((1, w), lambda b: (0, 0)),
            pl.BlockSpec((w, 4 * w), lambda b: (0, 0)),
            pl.BlockSpec((1, 4 * w), lambda b: (0, 0)),
            pl.BlockSpec((2, w), lambda b: (0, 0)),
        ],
        out_specs=[
            pl.BlockSpec((t_len, w), lambda b: (b, 0)),
            pl.BlockSpec((1, 2, w), lambda b: (b, 0, 0)),
        ],
        out_shape=[
            jax.ShapeDtypeStruct((bsz * t_len, w), BF16),
            jax.ShapeDtypeStruct((bsz, 2, w), F32),
        ],
        scratch_shapes=[pltpu.VMEM((2, w // LANE, rows_pad, LANE), F32) for _ in range(4)]
        + [pltpu.VMEM((w // LANE, rows_pad, LANE), F32)],
        compiler_params=_cparams(("arbitrary",)),
        name="lru",
    )(proj, proj, h0, conv_w, conv_b.reshape(1, w), w_gates, b_gates, lam)


def _lru_gate_weights(wa, ba, wx, bx):
    def dense(wblk):
        eye = jnp.eye(LRU_BLOCKS, dtype=wblk.dtype)
        return jnp.einsum('nkj,nm->nkmj', wblk, eye).reshape(LRU_WIDTH, LRU_WIDTH)
    wg = jnp.concatenate([dense(wa[0]), dense(wa[1]), dense(wx[0]), dense(wx[1])], axis=1)
    bg = jnp.concatenate([ba[0], ba[1], bx[0], bx[1]]).reshape(1, 4 * LRU_WIDTH)
    return (0.5 * wg).astype(BF16), 0.5 * bg


def _merge_kernel(x_ref, ya_ref, yb_ref, yc_ref, mod_ref, g_ref, wgate_ref, wbr_ref, wout_ref, o_ref):
    x = x_ref[...]
    h = _modulated_norm(x, g_ref, mod_ref)
    gates = jnp.dot(h, wgate_ref[0], preferred_element_type=F32)
    z = None
    for k, y_ref in enumerate((ya_ref, yb_ref, yc_ref)):
        term = (_sigmoid(gates[:, k * D_MODEL:(k + 1) * D_MODEL])
                * jnp.dot(y_ref[...], wbr_ref[0, k], preferred_element_type=F32))
        z = term if z is None else z + term
    out = jnp.dot(z.astype(BF16), wout_ref[0], preferred_element_type=F32)
    o_ref[...] = x + mod_ref[0, 0, 2:3, :] * out


def _merge(x2d, ya, yb, yc, mods, norm_g, w_gate, w_br, w_out, layer, rows_per_mod, mod_row0):
    tokens = x2d.shape[0]
    tm = 512
    tiles_per_mod = rows_per_mod // tm if rows_per_mod else 0

    def mod_idx(i):
        if rows_per_mod:
            return (layer, mod_row0 + i // tiles_per_mod, 0, 0)
        return (layer, mod_row0, 0, 0)

    return pl.pallas_call(
        _merge_kernel,
        grid=(tokens // tm,),
        in_specs=[
            pl.BlockSpec((tm, D_MODEL), lambda i: (i, 0)),
            pl.BlockSpec((tm, BRANCH_W), lambda i: (i, 0)),
            pl.BlockSpec((tm, BRANCH_W), lambda i: (i, 0)),
            pl.BlockSpec((tm, BRANCH_W), lambda i: (i, 0)),
            pl.BlockSpec((1, 1, 3, D_MODEL), mod_idx),
            pl.BlockSpec((1, 1, D_MODEL), lambda i: (layer, 0, 0)),
            pl.BlockSpec((1, D_MODEL, GATE_COLS), lambda i: (layer, 0, 0)),
            pl.BlockSpec((1, 3, BRANCH_W, D_MODEL), lambda i: (layer, 0, 0, 0)),
            pl.BlockSpec((1, D_MODEL, D_MODEL), lambda i: (layer, 0, 0)),
        ],
        out_specs=pl.BlockSpec((tm, D_MODEL), lambda i: (i, 0)),
        out_shape=jax.ShapeDtypeStruct((tokens, D_MODEL), F32),
        compiler_params=_cparams(("arbitrary",)),
        name="merge",
    )(x2d, ya, yb, yc, mods, norm_g.reshape(DEPTH, 1, D_MODEL), w_gate, w_br, w_out)


def _rope_tables(t_len):
    t = jnp.arange(t_len)
    m = HEAD_DIM // 4
    freqs = ROPE_BASE ** (-jnp.arange(m, dtype=F32) / m)
    ang_r = (t // GRID_W).astype(F32)[:, None] * freqs[None, :]
    ang_c = (t % GRID_W).astype(F32)[:, None] * freqs[None, :]
    cos = jnp.concatenate([jnp.cos(ang_r), jnp.cos(ang_r), jnp.cos(ang_c), jnp.cos(ang_c)], axis=-1)
    sin = jnp.concatenate([-jnp.sin(ang_r), jnp.sin(ang_r), -jnp.sin(ang_c), jnp.sin(ang_c)], axis=-1)
    reps = BRANCH_W // HEAD_DIM
    return jnp.tile(cos, (1, reps)), jnp.tile(sin, (1, reps))


def kernel(x_prompt, x_sample, cache_ka, cache_va, cache_kb, cache_vb, state_lru, c, c_ctx,
           norm_g, w_ada, b_ada, w_in, a_q_norm, a_k_norm, a_sink, b_q_norm, b_k_norm, b_rpb,
           lru_conv_w, lru_conv_b, lru_wa, lru_ba, lru_wx, lru_bx, lru_lambda, w_branch, w_out):
    bsz, s_len, _ = x_prompt.shape
    dbsz, t_len, _ = x_sample.shape
    p_len = cache_ka.shape[2]

    n_mod = 16
    ctx_row = dbsz
    cond = jnp.zeros((n_mod, D_MODEL), F32).at[:dbsz].set(c).at[ctx_row].set(c_ctx)
    mods = _mods(cond, w_ada, b_ada).reshape(DEPTH, n_mod, 3, D_MODEL)

    w_perm = w_gate = _wprep(w_in)
    w_br = w_branch.astype(BF16)
    w_o = w_out.astype(BF16)
    cos, sin_signed = _rope_tables(t_len)

    def cache_t(x):
        return x.transpose(0, 1, 3, 4, 2).reshape(x.shape[0], DEPTH, x.shape[3] * HEAD_DIM, p_len)
    cka, cva, ckb, cvb = cache_t(cache_ka), cache_t(cache_va), cache_t(cache_kb), cache_t(cache_vb)

    yp = x_prompt.reshape(bsz * s_len, D_MODEL)
    ys = x_sample.reshape(dbsz * t_len, D_MODEL)
    zero_state = jnp.zeros((bsz, 2, LRU_WIDTH), F32)
    new_caches, new_lru = None, []
    for l in range(DEPTH):
        wg, bg = _lru_gate_weights(lru_wa[l], lru_ba[l], lru_wx[l], lru_bx[l])

        proj = _inproj(yp, mods, norm_g, w_perm, l, 0, ctx_row)
        ya, yb, new_caches = _ctx_attn(proj, bsz, s_len, a_q_norm[l], a_k_norm[l], a_sink[l],
                                       b_q_norm[l], b_k_norm[l], l, new_caches)
        yc, st = _lru(proj, zero_state, lru_conv_w[l], lru_conv_b[l], wg, bg, lru_lambda[l], bsz, s_len)
        yp = _merge(yp, ya, yb, yc, mods, norm_g, w_gate, w_br, w_o, l, 0, ctx_row)
        new_lru.append(st)

        proj = _inproj(ys, mods, norm_g, w_perm, l, t_len, 0)
        ya = _win_attn(proj, cka, cva, l, cos, sin_signed, a_q_norm[l], a_k_norm[l], a_sink[l], dbsz, t_len)
        yb = _nbr_attn(proj, ckb, cvb, l, b_rpb[l], b_q_norm[l], b_k_norm[l], dbsz, t_len)
        yc, _ = _lru(proj, state_lru[:, l], lru_conv_w[l], lru_conv_b[l], wg, bg, lru_lambda[l], dbsz, t_len)
        ys = _merge(ys, ya, yb, yc, mods, norm_g, w_gate, w_br, w_o, l, t_len, 0)

    def cache_out(x):
        return x.reshape(bsz, DEPTH, x.shape[2] // HEAD_DIM, HEAD_DIM, s_len).transpose(0, 1, 4, 2, 3)

    return (yp.reshape(bsz, s_len, D_MODEL), ys.reshape(dbsz, t_len, D_MODEL),
            *(cache_out(x) for x in new_caches), jnp.stack(new_lru, axis=1))
```

```python
import functools

import numpy as np
import jax
import jax.numpy as jnp
from jax import lax
from jax.experimental import pallas as pl
from jax.experimental.pallas import tpu as pltpu

F32 = jnp.float32
BF16 = jnp.bfloat16

D_MODEL = 1024
DEPTH = 2
GRID_W = 64
HEAD_DIM = 64
BRANCH_W = 512
A_HEADS = 8
A_KV_HEADS = 2
A_WINDOW = 128
A_BLOCK = 128
B_HEADS = 8
NB_ROWS = 8
NB_COLS = 16
LRU_WIDTH = 512
LRU_BLOCKS = 8
LRU_BW = LRU_WIDTH // LRU_BLOCKS
LRU_C = 8.0
CONV_W = 4
ROPE_BASE = 10000.0
EPS = 1e-6
NEG_INF = -1e30
QK_SCALE = HEAD_DIM ** -0.5

LANE = 128
SUBLANE = 8
MXU_DIM = 256
VMEM_LIMIT = 56 * 1024 * 1024

_ORIG_SPLITS = (512, 128, 128, 512, 512, 512, 512, 512, 512, 512, 1024, 1024, 1024)
_ORIG_OFFS = tuple(int(v) for v in np.cumsum((0,) + _ORIG_SPLITS)[:-1])
_PERM = (0, 3, 4, 5, 6, 7, 8, 9, 1, 2)
_GATE_SPLITS = (10, 11, 12)
IN_COLS = sum(_ORIG_SPLITS[k] for k in _PERM)
GATE_COLS = sum(_ORIG_SPLITS[k] for k in _GATE_SPLITS)
C_AQ, C_AG, C_BQ, C_BK, C_BV, C_BG, C_CX, C_CG, C_AK, C_AV = (0, 4, 8, 12, 16, 20, 24, 28, 32, 33)

_NT = (((1,), (1,)), ((), ()))


def _cparams(sem):
    return pltpu.CompilerParams(dimension_semantics=sem, vmem_limit_bytes=VMEM_LIMIT)


def _sigmoid(x):
    return 0.5 + 0.5 * jnp.tanh(0.5 * x)


def _silu(x):
    return x * _sigmoid(x)


def _head_mean_matrix(width):
    idx = np.arange(width) // HEAD_DIM
    return jnp.asarray((idx[:, None] == idx[None, :]).astype(np.float32) / HEAD_DIM, dtype=BF16)


def _heads_rms(x, bd, g):
    x2 = x * x
    hi = x2.astype(BF16)
    lo = (x2 - hi.astype(F32)).astype(BF16)
    width = x.shape[-1]
    step = min(width, MXU_DIM)
    tile = bd[:step, :step]
    ms = jnp.concatenate(
        [jnp.dot(hi[:, c:c + step], tile, preferred_element_type=F32)
         + jnp.dot(lo[:, c:c + step], tile, preferred_element_type=F32) for c in range(0, width, step)], axis=-1)
    return x * lax.rsqrt(ms + EPS) * g


def _head_lane_mask(width, h):
    lane = lax.broadcasted_iota(jnp.int32, (1, width), 1)
    return (lane >= h * HEAD_DIM) & (lane < (h + 1) * HEAD_DIM)


def _mods_kernel(c_ref, w_ref, b_ref, o_ref):
    c = c_ref[...]
    s = _silu(c).astype(BF16)
    o_ref[0] = jnp.dot(s, w_ref[0].astype(BF16), preferred_element_type=F32) + b_ref[0]


def _mods(cond, w_ada, b_ada):
    n = cond.shape[0]
    tn = D_MODEL
    return pl.pallas_call(
        _mods_kernel,
        grid=(DEPTH, 3 * D_MODEL // tn),
        in_specs=[
            pl.BlockSpec((n, D_MODEL), lambda l, j: (0, 0)),
            pl.BlockSpec((1, D_MODEL, tn), lambda l, j: (l, 0, j)),
            pl.BlockSpec((1, 1, tn), lambda l, j: (l, 0, j)),
        ],
        out_specs=pl.BlockSpec((1, n, tn), lambda l, j: (l, 0, j)),
        out_shape=jax.ShapeDtypeStruct((DEPTH, n, 3 * D_MODEL), F32),
        compiler_params=_cparams(("arbitrary", "arbitrary")),
        name="mods",
    )(cond, w_ada, b_ada.reshape(DEPTH, 1, 3 * D_MODEL))


W_BLK = 256
_MIXER_BLKS = IN_COLS // W_BLK
_GATE_BLKS = GATE_COLS // W_BLK
W_MIXER_OFF = IN_COLS
W_PREP_COLS = 2 * IN_COLS


_GAP_BLKS = W_MIXER_OFF // W_BLK - _GATE_BLKS


def _wprep_kernel(w_ref, o_ref):
    n = pl.program_id(0)
    in_gap = (n >= _GATE_BLKS) & (n < _GATE_BLKS + _GAP_BLKS)
    o_ref[...] = jnp.where(in_gap, 0.0, w_ref[...]).astype(BF16)


def _wprep_src(n):
    m = n - _GATE_BLKS - _GAP_BLKS
    mixer = jnp.where(m < 0, 0, jnp.where(m < 2, m, jnp.where(m < _MIXER_BLKS - 1, m + 1, 2)))
    return jnp.where(n < _GATE_BLKS, n + _MIXER_BLKS, mixer)


def _wprep(w_in):
    assert [_ORIG_OFFS[k] // W_BLK for k in _PERM[:-2]] == [0, 3, 5, 7, 9, 11, 13, 15] and _ORIG_OFFS[1] == 2 * W_BLK
    return pl.pallas_call(
        _wprep_kernel,
        grid=(W_PREP_COLS // W_BLK,),
        in_specs=[pl.BlockSpec((DEPTH, D_MODEL, W_BLK), lambda n: (0, 0, _wprep_src(n)))],
        out_specs=pl.BlockSpec((DEPTH, D_MODEL, W_BLK), lambda n: (0, 0, n)),
        out_shape=jax.ShapeDtypeStruct((DEPTH, D_MODEL, W_PREP_COLS), BF16),
        compiler_params=_cparams(("arbitrary",)),
        name="wprep",
    )(w_in)


def _modulated_norm(x, g_ref, mod_ref):
    y = x * lax.rsqrt(jnp.mean(x * x, axis=-1, keepdims=True) + EPS)
    y = y * g_ref[0]
    shift = mod_ref[0, 0, 0:1, :]
    scale = mod_ref[0, 0, 1:2, :]
    return (y * (1.0 + scale) + shift).astype(BF16)


def _inproj_kernel(x_ref, mod_ref, g_ref, w_ref, o_ref):
    h = _modulated_norm(x_ref[...], g_ref, mod_ref)
    o_ref[...] = jnp.dot(h, w_ref[0], preferred_element_type=F32)


def _inproj(x2d, mods, norm_g, w_perm, layer, rows_per_mod, mod_row0):
    tokens = x2d.shape[0]
    tm = 512
    tiles_per_mod = rows_per_mod // tm if rows_per_mod else 0

    def mod_idx(i):
        if rows_per_mod:
            return (layer, mod_row0 + i // tiles_per_mod, 0, 0)
        return (layer, mod_row0, 0, 0)

    return pl.pallas_call(
        _inproj_kernel,
        grid=(tokens // tm,),
        in_specs=[
            pl.BlockSpec((tm, D_MODEL), lambda i: (i, 0)),
            pl.BlockSpec((1, 1, 3, D_MODEL), mod_idx),
            pl.BlockSpec((1, 1, D_MODEL), lambda i: (layer, 0, 0)),
            pl.BlockSpec((1, D_MODEL, IN_COLS), lambda i: (layer, 0, W_MIXER_OFF // IN_COLS)),
        ],
        out_specs=pl.BlockSpec((tm, IN_COLS), lambda i: (i, 0)),
        out_shape=jax.ShapeDtypeStruct((tokens, IN_COLS), F32),
        compiler_params=_cparams(("arbitrary",)),
        name="inproj",
    )(x2d, mods, norm_g.reshape(DEPTH, 1, D_MODEL), w_perm)


CTX_SEQS = 2
CTX_GROUP = MXU_DIM // HEAD_DIM


def _ctx_mixer(q, k, v, g, gq, gk, bdq, bdk, ones_stack, expand=None, sink_ref=None):
    s_len = q.shape[0]
    n_q = q.shape[1] // HEAD_DIM
    kn = _heads_rms(k, bdk, gk)
    qn = (_heads_rms(q, bdq, gq) * QK_SCALE).astype(BF16)
    knb = kn.astype(BF16)
    vb = v.astype(BF16)
    if expand is not None:
        knb = jnp.dot(knb, expand, preferred_element_type=F32).astype(BF16)
        vb = jnp.dot(vb, expand, preferred_element_type=F32).astype(BF16)
    gw = CTX_GROUP * HEAD_DIM
    masks = [_head_lane_mask(gw, hi) for hi in range(CTX_GROUP)]
    groups = []
    for gp in range(n_q // CTX_GROUP):
        cols = slice(gp * gw, (gp + 1) * gw)
        qh, kh, vh = qn[:, cols], knb[:, cols], vb[:, cols]
        kstack = jnp.concatenate([jnp.where(mk, kh, jnp.zeros_like(kh)) for mk in masks], axis=0)
        vstack = jnp.concatenate([jnp.where(mk, vh, jnp.zeros_like(vh)) for mk in masks], axis=0)
        s = lax.dot_general(qh, kstack, _NT, preferred_element_type=F32)
        ps, sink_terms = [], []
        for hi in range(CTX_GROUP):
            si = s[:, hi * s_len:(hi + 1) * s_len]
            m = jnp.max(si, axis=-1, keepdims=True)
            if sink_ref is not None:
                h = gp * CTX_GROUP + hi
                snk = sink_ref[0:1, h:h + 1]
                m = jnp.maximum(m, snk)
                sink_terms.append(jnp.where(masks[hi], jnp.exp(snk - m), 0.0))
            ps.append(jnp.exp(si - m).astype(BF16))
        p = jnp.concatenate(ps, axis=1)
        rhs = jnp.concatenate([vstack, ones_stack], axis=1)
        oe = jnp.dot(p, rhs, preferred_element_type=F32)
        l = oe[:, gw:]
        for term in sink_terms:
            l = l + term
        groups.append(oe[:, :gw] / l)
    return jnp.concatenate(groups, axis=-1) * _silu(g), kn


_CTX_INPUTS = 17


def _ctx_attn_kernel(*refs, s_len, n_alias):
    (aq_ref, ak_ref, av_ref, ag_ref, bq_ref, bk_ref, bv_ref, bg_ref,
     gqa_ref, gka_ref, sink_ref, gqb_ref, gkb_ref, bdw_ref, bdn_ref, exp_ref, ones_ref) = refs[:_CTX_INPUTS]
    ya_ref, yb_ref, ka_ref, va_ref, kb_ref, vb_ref = refs[_CTX_INPUTS + n_alias:]
    ones_stack = ones_ref[...]
    bdw = bdw_ref[...]
    for n in range(aq_ref.shape[0] // s_len):
        rows = slice(n * s_len, (n + 1) * s_len)
        va = av_ref[rows, :]
        vb = bv_ref[rows, :]
        va_ref[n, 0] = va.T
        vb_ref[n, 0] = vb.T
        ya, kna = _ctx_mixer(aq_ref[rows, :], ak_ref[rows, :], va, ag_ref[rows, :], gqa_ref[...], gka_ref[...],
                             bdw, bdn_ref[...], ones_stack, expand=exp_ref[...], sink_ref=sink_ref)
        yb, knb = _ctx_mixer(bq_ref[rows, :], bk_ref[rows, :], vb, bg_ref[rows, :], gqb_ref[...], gkb_ref[...],
                             bdw, bdw, ones_stack)
        ka_ref[n, 0] = kna.T
        kb_ref[n, 0] = knb.T
        ya_ref[rows, :] = ya.astype(ya_ref.dtype)
        yb_ref[rows, :] = yb.astype(yb_ref.dtype)


def _ctx_attn(proj, bsz, s_len, a_gq, a_gk, a_sink, b_gq, b_gk, layer, caches):
    kvw = A_KV_HEADS * HEAD_DIM
    grp = A_HEADS // A_KV_HEADS
    expand = np.zeros((kvw, BRANCH_W), np.float32)
    for h in range(A_HEADS):
        for d in range(HEAD_DIM):
            expand[(h // grp) * HEAD_DIM + d, h * HEAD_DIM + d] = 1.0
    ones_stack = np.zeros((CTX_GROUP * s_len, CTX_GROUP * HEAD_DIM), np.float32)
    for hi in range(CTX_GROUP):
        ones_stack[hi * s_len:(hi + 1) * s_len, hi * HEAD_DIM:(hi + 1) * HEAD_DIM] = 1.0
    const = lambda b: (0, 0)
    rows = CTX_SEQS * s_len
    wide = lambda c: pl.BlockSpec((rows, BRANCH_W), lambda b: (b, c // 4))
    narrow = lambda c: pl.BlockSpec((rows, kvw), lambda b: (b, c))
    prev = () if caches is None else tuple(caches)
    n_alias = len(prev)
    cache_spec = lambda width: pl.BlockSpec((CTX_SEQS, 1, width, s_len), lambda b: (b, layer, 0, 0))
    cache_shape = lambda width: jax.ShapeDtypeStruct((bsz, DEPTH, width, s_len), F32)
    outs = pl.pallas_call(
        functools.partial(_ctx_attn_kernel, s_len=s_len, n_alias=n_alias),
        grid=(bsz // CTX_SEQS,),
        in_specs=[
            wide(C_AQ), narrow(C_AK), narrow(C_AV), wide(C_AG), wide(C_BQ), wide(C_BK), wide(C_BV), wide(C_BG),
            pl.BlockSpec((1, BRANCH_W), const),
            pl.BlockSpec((1, kvw), const),
            pl.BlockSpec((1, A_HEADS), const),
            pl.BlockSpec((1, BRANCH_W), const),
            pl.BlockSpec((1, BRANCH_W), const),
            pl.BlockSpec((BRANCH_W, BRANCH_W), const),
            pl.BlockSpec((kvw, kvw), const),
            pl.BlockSpec((kvw, BRANCH_W), const),
            pl.BlockSpec((CTX_GROUP * s_len, CTX_GROUP * HEAD_DIM), const),
        ] + [pl.BlockSpec(memory_space=pl.ANY)] * n_alias,
        out_specs=[
            pl.BlockSpec((rows, BRANCH_W), lambda b: (b, 0)),
            pl.BlockSpec((rows, BRANCH_W), lambda b: (b, 0)),
            cache_spec(kvw), cache_spec(kvw), cache_spec(BRANCH_W), cache_spec(BRANCH_W),
        ],
        out_shape=[
            jax.ShapeDtypeStruct((bsz * s_len, BRANCH_W), BF16),
            jax.ShapeDtypeStruct((bsz * s_len, BRANCH_W), BF16),
            cache_shape(kvw), cache_shape(kvw), cache_shape(BRANCH_W), cache_shape(BRANCH_W),
        ],
        input_output_aliases={_CTX_INPUTS + i: 2 + i for i in range(n_alias)},
        compiler_params=_cparams(("arbitrary",)),
        name="ctx_attn",
    )(proj, proj, proj, proj, proj, proj, proj, proj,
      jnp.tile(a_gq, A_HEADS).reshape(1, BRANCH_W), jnp.tile(a_gk, A_KV_HEADS).reshape(1, kvw),
      a_sink.reshape(1, A_HEADS),
      jnp.tile(b_gq, B_HEADS).reshape(1, BRANCH_W), jnp.tile(b_gk, B_HEADS).reshape(1, BRANCH_W),
      _head_mean_matrix(BRANCH_W), _head_mean_matrix(kvw), jnp.asarray(expand, dtype=BF16),
      jnp.asarray(ones_stack, dtype=BF16), *prev)
    return outs[0], outs[1], tuple(outs[2:])


def _rope(x, cos, sin_signed):
    w = x.shape[-1]
    lane = lax.broadcasted_iota(jnp.int32, x.shape, 1)
    up = pltpu.roll(x, w - 16, axis=1)
    dn = pltpu.roll(x, 16, axis=1)
    partner = jnp.where((lane & 16) == 0, up, dn)
    return x * cos + partner * sin_signed


WIN_SUB = 2


def _win_attn_kernel(q_ref, k_ref, v_ref, g_ref, kc_ref, vc_ref, cos_ref, sin_ref, gq_ref, gk_ref, sink_ref,
                     bdq_ref, bdk_ref, dup_ref, mask_ref, y_ref, kpad, vpad, kcx, vcx):
    j = pl.program_id(1)
    nb = pl.num_programs(1)
    t_len = k_ref.shape[0]
    grp = A_HEADS // A_KV_HEADS
    kvw = A_KV_HEADS * HEAD_DIM
    xw = 2 * kvw

    @pl.when(j == 0)
    def _():
        dup = dup_ref[...]
        kn = _heads_rms(k_ref[...], bdk_ref[...], gk_ref[...])
        kn = _rope(kn, cos_ref[:, 0:kvw], sin_ref[:, 0:kvw]).astype(BF16)
        def with_ones(vx):
            ones = jnp.ones((vx.shape[0], LANE), BF16)
            return jnp.concatenate(
                [part for kv in range(A_KV_HEADS) for part in (vx[:, kv * LANE:(kv + 1) * LANE], ones)], axis=1)

        kpad[0:A_BLOCK, :] = jnp.zeros((A_BLOCK, xw), BF16)
        kpad[A_BLOCK + t_len:2 * A_BLOCK + t_len, :] = jnp.zeros((A_BLOCK, xw), BF16)
        vpad[0:A_BLOCK, :] = jnp.zeros((A_BLOCK, 2 * xw), BF16)
        vpad[A_BLOCK + t_len:2 * A_BLOCK + t_len, :] = jnp.zeros((A_BLOCK, 2 * xw), BF16)
        kpad[A_BLOCK:A_BLOCK + t_len, :] = jnp.dot(kn, dup, preferred_element_type=F32).astype(BF16)
        vpad[A_BLOCK:A_BLOCK + t_len, :] = with_ones(
            jnp.dot(v_ref[...].astype(BF16), dup, preferred_element_type=F32).astype(BF16))
        kcx[...] = jnp.dot(kc_ref[0, 0].T.astype(BF16), dup, preferred_element_type=F32).astype(BF16)
        vcx[...] = with_ones(jnp.dot(vc_ref[0, 0].T.astype(BF16), dup, preferred_element_type=F32).astype(BF16))

    nloc = 3 * A_BLOCK
    low_half = lax.broadcasted_iota(jnp.int32, (1, LANE), 1) < HEAD_DIM
    for sub in range(WIN_SUB):
        jj = j * WIN_SUB + sub
        qrows = slice(sub * A_BLOCK, (sub + 1) * A_BLOCK)
        r0 = pl.multiple_of(jj * A_BLOCK, A_BLOCK)
        qn = _heads_rms(q_ref[qrows, :], bdq_ref[...], gq_ref[...])
        qb = (_rope(qn, cos_ref[pl.ds(r0, A_BLOCK), :], sin_ref[pl.ds(r0, A_BLOCK), :]) * QK_SCALE).astype(BF16)
        maskadd = mask_ref[jnp.where(jj == 0, 0, jnp.where(jj == nb * WIN_SUB - 1, 2, 1))]
        kband = kpad[pl.ds(r0, nloc), :]
        vband = vpad[pl.ds(r0, nloc), :]
        pairs = []
        for kv in range(A_KV_HEADS):
            cols = slice(kv * LANE, (kv + 1) * LANE)
            qparts, sinks = [], []
            for gi in range(grp):
                h = kv * grp + gi
                qpair = qb[:, (h // 2) * LANE:(h // 2 + 1) * LANE]
                keep = low_half if h % 2 == 0 else jnp.logical_not(low_half)
                qparts.append(jnp.where(keep, qpair, jnp.zeros_like(qpair)))
                sinks.append(jnp.broadcast_to(sink_ref[0:1, h:h + 1], (A_BLOCK, 1)))
            qst = jnp.concatenate(qparts, axis=0)
            snk = jnp.concatenate(sinks, axis=0)
            s_loc = lax.dot_general(qst, kband[:, cols], _NT, preferred_element_type=F32) + maskadd
            s_ctx = lax.dot_general(qst, kcx[:, cols], _NT, preferred_element_type=F32)
            m = jnp.maximum(jnp.maximum(jnp.max(s_loc, axis=-1, keepdims=True),
                                        jnp.max(s_ctx, axis=-1, keepdims=True)), snk)
            p_loc = jnp.exp(s_loc - m).astype(BF16)
            p_ctx = jnp.exp(s_ctx - m).astype(BF16)
            wide = slice(kv * MXU_DIM, (kv + 1) * MXU_DIM)
            oe = (jnp.dot(p_loc, vband[:, wide], preferred_element_type=F32)
                  + jnp.dot(p_ctx, vcx[:, wide], preferred_element_type=F32))
            o = oe[:, :LANE] / (oe[:, LANE:] + jnp.exp(snk - m))
            for k2 in range(grp // 2):
                even = o[(2 * k2) * A_BLOCK:(2 * k2 + 1) * A_BLOCK]
                odd = o[(2 * k2 + 1) * A_BLOCK:(2 * k2 + 2) * A_BLOCK]
                pairs.append(jnp.where(low_half, even, odd))
        y = jnp.concatenate(pairs, axis=-1) * _silu(g_ref[qrows, :])
        y_ref[qrows, :] = y.astype(y_ref.dtype)


def _win_mask(grp, nb):
    assert nb >= 2
    r = np.arange(A_BLOCK)[:, None]
    c = np.arange(3 * A_BLOCK)[None, :]
    band = np.abs(r + A_BLOCK - c) <= A_WINDOW
    variants = [band & (c >= A_BLOCK), band, band & (c < 2 * A_BLOCK)]
    return np.stack([np.tile(np.where(v, 0.0, NEG_INF).astype(np.float32), (grp, 1)) for v in variants])


def _win_attn(proj, cache_k, cache_v, layer, cos, sin_signed, gq, gk, sink, bsz, t_len):
    nb = t_len // A_BLOCK
    assert nb % WIN_SUB == 0
    nsteps = nb // WIN_SUB
    kvw = A_KV_HEADS * HEAD_DIM
    grp = A_HEADS // A_KV_HEADS
    p_len = cache_k.shape[3]
    dup = np.zeros((kvw, 2 * kvw), np.float32)
    for kv in range(A_KV_HEADS):
        for half in range(2):
            for d in range(HEAD_DIM):
                dup[kv * HEAD_DIM + d, kv * LANE + half * HEAD_DIM + d] = 1.0
    const2 = lambda b, j: (0, 0)
    return pl.pallas_call(
        _win_attn_kernel,
        grid=(bsz, nsteps),
        in_specs=[
            pl.BlockSpec((WIN_SUB * A_BLOCK, BRANCH_W), lambda b, j: (b * nsteps + j, C_AQ // 4)),
            pl.BlockSpec((t_len, kvw), lambda b, j: (b, C_AK)),
            pl.BlockSpec((t_len, kvw), lambda b, j: (b, C_AV)),
            pl.BlockSpec((WIN_SUB * A_BLOCK, BRANCH_W), lambda b, j: (b * nsteps + j, C_AG // 4)),
            pl.BlockSpec((1, 1, kvw, p_len), lambda b, j: (b, layer, 0, 0)),
            pl.BlockSpec((1, 1, kvw, p_len), lambda b, j: (b, layer, 0, 0)),
            pl.BlockSpec((t_len, BRANCH_W), const2),
            pl.BlockSpec((t_len, BRANCH_W), const2),
            pl.BlockSpec((1, BRANCH_W), const2),
            pl.BlockSpec((1, kvw), const2),
            pl.BlockSpec((1, A_HEADS), const2),
            pl.BlockSpec((BRANCH_W, BRANCH_W), const2),
            pl.BlockSpec((kvw, kvw), const2),
            pl.BlockSpec((kvw, 2 * kvw), const2),
            pl.BlockSpec((3, grp * A_BLOCK, 3 * A_BLOCK), lambda b, j: (0, 0, 0)),
        ],
        out_specs=pl.BlockSpec((WIN_SUB * A_BLOCK, BRANCH_W), lambda b, j: (b * nsteps + j, 0)),
        out_shape=jax.ShapeDtypeStruct((bsz * t_len, BRANCH_W), BF16),
        scratch_shapes=[pltpu.VMEM((t_len + 2 * A_BLOCK, 2 * kvw), BF16),
                        pltpu.VMEM((t_len + 2 * A_BLOCK, 4 * kvw), BF16),
                        pltpu.VMEM((p_len, 2 * kvw), BF16),
                        pltpu.VMEM((p_len, 4 * kvw), BF16)],
        compiler_params=_cparams(("arbitrary", "arbitrary")),
        name="win_attn",
    )(proj, proj, proj, proj, cache_k, cache_v, cos, sin_signed,
      jnp.tile(gq, A_HEADS).reshape(1, BRANCH_W), jnp.tile(gk, A_KV_HEADS).reshape(1, kvw),
      sink.reshape(1, A_HEADS), _head_mean_matrix(BRANCH_W), _head_mean_matrix(kvw),
      jnp.asarray(dup, dtype=BF16), jnp.asarray(_win_mask(grp, nb)))


NBR_QB = 2 * GRID_W
NBR_BAND = 5


def _nbr_plan(rows):
    kh = min(NB_ROWS, rows)
    nblk = rows // 2
    specs, plan, starts = {}, [], []
    for i in range(nblk):
        s0 = min(max(i - 2, 0), nblk - NBR_BAND)
        starts.append(s0)
        blk = []
        for a in range(2):
            qr = 2 * i + a
            rs = min(max(qr - kh // 2, 0), rows - kh)
            assert 2 * s0 <= rs and rs + kh <= 2 * (s0 + NBR_BAND)
            row = []
            for p in range(NBR_BAND):
                pair = tuple(kr - qr + NB_ROWS - 1 if rs <= kr < rs + kh else None
                             for kr in (2 * (s0 + p), 2 * (s0 + p) + 1))
                row.append(specs.setdefault(pair, len(specs)))
            blk.append(row)
        plan.append(blk)
    return tuple(starts), plan, list(specs)


def _nbr_table(rpb, specs):
    heads = rpb.shape[0]
    c = np.arange(GRID_W)
    cs = np.clip(c - NB_COLS // 2, 0, GRID_W - NB_COLS)
    col_ok = (c[None, :] >= cs[:, None]) & (c[None, :] < cs[:, None] + NB_COLS)
    edge = GRID_W - NB_COLS
    period = 2 * GRID_W - 1
    padded = jnp.pad(rpb.astype(F32), ((0, 0), (0, 0), (edge, edge)))
    tiled = jnp.tile(padded, (1, 1, GRID_W + 1))[:, :, :GRID_W * (period + 1)]
    skew = tiled.reshape(heads, 2 * NB_ROWS - 1, GRID_W, period + 1)
    shifted = skew[:, :, ::-1, :GRID_W]
    by_col = jnp.where(col_ok[None, None], shifted, NEG_INF)
    neg = jnp.full((heads, GRID_W, GRID_W), NEG_INF, F32)
    blocks = [jnp.concatenate([neg if d is None else by_col[:, d] for d in spec], axis=-1) for spec in specs]
    return jnp.stack(blocks, axis=1)


def _nbr_attn_kernel(q_ref, k_ref, v_ref, g_ref, kc_ref, vc_ref, tb_ref, gq_ref, gk_ref, bd_ref, y_ref,
                     *, starts, plan):
    nband = NBR_BAND * NBR_QB
    bd = bd_ref[...]
    qn = (_heads_rms(q_ref[...], bd, gq_ref[...]) * QK_SCALE).astype(BF16)
    kn = _heads_rms(k_ref[...], bd, gk_ref[...]).astype(BF16)
    vb = v_ref[...].astype(BF16)
    kcb = kc_ref[0, 0].T.astype(BF16)
    vcb = vc_ref[0, 0].T.astype(BF16)
    acc = [jnp.zeros((NBR_QB, LANE), F32) for _ in starts]
    for h in range(LANE // HEAD_DIM):
        hm = _head_lane_mask(LANE, h)
        km = jnp.where(hm, kn, jnp.zeros_like(kn))
        kcm = jnp.where(hm, kcb, jnp.zeros_like(kcb))
        vm = jnp.concatenate([jnp.where(hm, vb, jnp.zeros_like(vb)), jnp.ones_like(vb)], axis=1)
        vcm = jnp.concatenate([jnp.where(hm, vcb, jnp.zeros_like(vcb)), jnp.ones_like(vcb)], axis=1)
        for i, s0 in enumerate(starts):
            qi = qn[i * NBR_QB:(i + 1) * NBR_QB]
            ks = slice(s0 * NBR_QB, s0 * NBR_QB + nband)
            s_raw = lax.dot_general(qi, km[ks], _NT, preferred_element_type=F32)
            s_loc = jnp.concatenate(
                [jnp.concatenate([s_raw[a * GRID_W:(a + 1) * GRID_W, p * LANE:(p + 1) * LANE]
                                  + tb_ref[h, plan[i][a][p]] for p in range(NBR_BAND)], axis=1)
                 for a in range(2)], axis=0)
            s_ctx = lax.dot_general(qi, kcm, _NT, preferred_element_type=F32)
            m = jnp.maximum(jnp.max(s_loc, axis=-1, keepdims=True), jnp.max(s_ctx, axis=-1, keepdims=True))
            p_loc = jnp.exp(s_loc - m).astype(BF16)
            p_ctx = jnp.exp(s_ctx - m).astype(BF16)
            oe = (jnp.dot(p_loc, vm[ks], preferred_element_type=F32)
                  + jnp.dot(p_ctx, vcm, preferred_element_type=F32))
            acc[i] = acc[i] + oe[:, :LANE] / oe[:, LANE:]
    y = jnp.concatenate(acc, axis=0) * _silu(g_ref[...])
    y_ref[...] = y.astype(y_ref.dtype)


def _nbr_attn(proj, cache_k, cache_v, layer, rpb, gq, gk, bsz, t_len):
    hp = LANE // HEAD_DIM
    nhp = B_HEADS // hp
    p_len = cache_k.shape[3]
    rows = t_len // GRID_W
    assert rows % 2 == 0 and rows // 2 >= NBR_BAND
    starts, plan, specs = _nbr_plan(rows)
    table = _nbr_table(rpb, specs)
    const = lambda h, b: (0, 0)
    kern = functools.partial(_nbr_attn_kernel, starts=starts, plan=plan)
    return pl.pallas_call(
        kern,
        grid=(nhp, bsz),
        in_specs=[
            pl.BlockSpec((t_len, LANE), lambda h, b: (b, C_BQ + h)),
            pl.BlockSpec((t_len, LANE), lambda h, b: (b, C_BK + h)),
            pl.BlockSpec((t_len, LANE), lambda h, b: (b, C_BV + h)),
            pl.BlockSpec((t_len, LANE), lambda h, b: (b, C_BG + h)),
            pl.BlockSpec((1, 1, LANE, p_len), lambda h, b: (b, layer, h, 0)),
            pl.BlockSpec((1, 1, LANE, p_len), lambda h, b: (b, layer, h, 0)),
            pl.BlockSpec((hp, len(specs), GRID_W, 2 * GRID_W), lambda h, b: (h, 0, 0, 0)),
            pl.BlockSpec((1, LANE), const),
            pl.BlockSpec((1, LANE), const),
            pl.BlockSpec((LANE, LANE), const),
        ],
        out_specs=pl.BlockSpec((t_len, LANE), lambda h, b: (b, h)),
        out_shape=jax.ShapeDtypeStruct((bsz * t_len, BRANCH_W), BF16),
        compiler_params=_cparams(("arbitrary", "arbitrary")),
        name="nbr_attn",
    )(proj, proj, proj, proj, cache_k, cache_v, table,
      jnp.tile(gq, hp).reshape(1, LANE), jnp.tile(gk, hp).reshape(1, LANE), _head_mean_matrix(LANE))


LRU_CHUNKS = SUBLANE
LRU_PITCH_PAD = 4


def _lru_kernel(cx_ref, cg_ref, h0_ref, cw_ref, cb_ref, wg_ref, bg_ref, lam_ref,
                y_ref, st_ref, a_s, u_s, h_s, p_s, y_s):
    t_len = cx_ref.shape[0]
    w = LRU_WIDTH
    cx = cx_ref[...]
    row = lax.broadcasted_iota(jnp.int32, (t_len, w), 0)
    xc = cb_ref[...] + cx * cw_ref[2:3, :]
    xc = xc + jnp.where(row >= 2, pltpu.roll(cx, 2, axis=0), 0.0) * cw_ref[0:1, :]
    xc = xc + jnp.where(row >= 1, pltpu.roll(cx, 1, axis=0), 0.0) * cw_ref[1:2, :]
    xc = xc + jnp.where(row < t_len - 1, pltpu.roll(cx, t_len - 1, axis=0), 0.0) * cw_ref[3:4, :]

    gates = jnp.dot(xc.astype(BF16), wg_ref[...], preferred_element_type=F32) + bg_ref[...]
    coeffs = []
    for d in range(2):
        th_r = jnp.tanh(gates[:, d * w:(d + 1) * w])
        th_i = jnp.tanh(gates[:, (2 + d) * w:(3 + d) * w])
        nl = -lam_ref[d:d + 1, :]
        softplus = jnp.maximum(nl, 0.0) + jnp.log1p(jnp.exp(-jnp.abs(nl)))
        quarter_c = (-0.25 * LRU_C) * softplus
        half_log_a = quarter_c * th_r + quarter_c
        t = jnp.tanh(half_log_a)
        rc = 1.0 / (1.0 - t)
        coeffs.append(((1.0 + t) * rc,
                       jnp.sqrt(-t) * rc * (1.0 + th_i) * xc))

    chunk_len = t_len // LRU_CHUNKS
    pitch = chunk_len + LRU_PITCH_PAD
    nt = w // LANE
    for d, (a_val, u_val) in enumerate(coeffs):
        for c in range(LRU_CHUNKS):
            for k in range(nt):
                dst = slice(c * pitch, c * pitch + chunk_len)
                src = (slice(c * chunk_len, (c + 1) * chunk_len), slice(k * LANE, (k + 1) * LANE))
                a_s[d, k, dst, :] = a_val[src]
                u_s[d, k, dst, :] = u_val[src]

    def scan_body(s, carry):
        hs, ps = carry
        new_h, new_p = [], []
        for d in range(2):
            pos = s if d == 0 else chunk_len - 1 - s
            idx = pl.ds(pos, LRU_CHUNKS, stride=pitch)
            for k in range(nt):
                av = a_s[d, k, idx, :]
                h = av * hs[d * nt + k] + u_s[d, k, idx, :]
                p = av * ps[d * nt + k]
                h_s[d, k, idx, :] = h
                p_s[d, k, idx, :] = p
                new_h.append(h)
                new_p.append(p)
        return tuple(new_h), tuple(new_p)

    zero = jnp.zeros((LRU_CHUNKS, LANE), F32)
    one = jnp.ones((LRU_CHUNKS, LANE), F32)
    h_end, p_end = lax.fori_loop(0, chunk_len, scan_body, ((zero,) * (2 * nt), (one,) * (2 * nt)), unroll=4)

    h0 = h0_ref[0]
    enter, finals = [], []
    for d in range(2):
        order = range(LRU_CHUNKS) if d == 0 else range(LRU_CHUNKS - 1, -1, -1)
        for k in range(nt):
            he, pe = h_end[d * nt + k], p_end[d * nt + k]
            e = h0[d:d + 1, k * LANE:(k + 1) * LANE]
            rows = [None] * LRU_CHUNKS
            for c in order:
                rows[c] = e
                e = pe[c:c + 1, :] * e + he[c:c + 1, :]
            enter.append(jnp.concatenate(rows, axis=0))
            finals.append(e)
    st_ref[0] = jnp.concatenate([jnp.concatenate(finals[d * nt:(d + 1) * nt], axis=1) for d in range(2)], axis=0)

    def fix_body(s, carry):
        idx = pl.ds(s, LRU_CHUNKS, stride=pitch)
        for k in range(nt):
            hf = h_s[0, k, idx, :] + p_s[0, k, idx, :] * enter[k]
            hb = h_s[1, k, idx, :] + p_s[1, k, idx, :] * enter[nt + k]
            y_s[k, idx, :] = hf + hb
        return carry

    lax.fori_loop(0, chunk_len, fix_body, 0, unroll=4)
    hsum = jnp.concatenate(
        [jnp.concatenate([y_s[k, c * pitch:c * pitch + chunk_len, :] for c in range(LRU_CHUNKS)], axis=0)
         for k in range(nt)], axis=1)
    y_ref[...] = (hsum * _silu(cg_ref[...])).astype(y_ref.dtype)


def _lru(proj, h0, conv_w, conv_b, w_gates, b_gates, lam, bsz, t_len):
    w = LRU_WIDTH
    assert t_len % (LRU_CHUNKS * SUBLANE) == 0
    rows_pad = LRU_CHUNKS * (t_len // LRU_CHUNKS + LRU_PITCH_PAD)
    return pl.pallas_call(
        _lru_kernel,
        grid=(bsz,),
        in_specs=[
            pl.BlockSpec((t_len, w), lambda b: (b, C_CX // 4)),
            pl.BlockSpec((t_len, w), lambda b: (b, C_CG // 4)),
            pl.BlockSpec((1, 2, w), lambda b: (b, 0, 0)),
            pl.BlockSpec((CONV_W, w), lambda b: (0, 0)),
            pl.BlockSpec((1, w), lambda b: (0, 0)),
            pl.BlockSpec((w, 4 * w), lambda b: (0, 0)),
            pl.BlockSpec((1, 4 * w), lambda b: (0, 0)),
            pl.BlockSpec((2, w), lambda b: (0, 0)),
        ],
        out_specs=[
            pl.BlockSpec((t_len, w), lambda b: (b, 0)),
            pl.BlockSpec((1, 2, w), lambda b: (b, 0, 0)),
        ],
        out_shape=[
            jax.ShapeDtypeStruct((bsz * t_len, w), BF16),
            jax.ShapeDtypeStruct((bsz, 2, w), F32),
        ],
        scratch_shapes=[pltpu.VMEM((2, w // LANE, rows_pad, LANE), F32) for _ in range(4)]
        + [pltpu.VMEM((w // LANE, rows_pad, LANE), F32)],
        compiler_params=_cparams(("arbitrary",)),
        name="lru",
    )(proj, proj, h0, conv_w, conv_b.reshape(1, w), w_gates, b_gates, lam)


def _lru_gate_weights(wa, ba, wx, bx):
    def dense(wblk):
        eye = jnp.eye(LRU_BLOCKS, dtype=wblk.dtype)
        return jnp.einsum('nkj,nm->nkmj', wblk, eye).reshape(LRU_WIDTH, LRU_WIDTH)
    wg = jnp.concatenate([dense(wa[0]), dense(wa[1]), dense(wx[0]), dense(wx[1])], axis=1)
    bg = jnp.concatenate([ba[0], ba[1], bx[0], bx[1]]).reshape(1, 4 * LRU_WIDTH)
    return (0.5 * wg).astype(BF16), 0.5 * bg


def _merge_kernel(x_ref, ya_ref, yb_ref, yc_ref, mod_ref, g_ref, wgate_ref, wbr_ref, wout_ref, o_ref):
    x = x_ref[...]
    h = _modulated_norm(x, g_ref, mod_ref)
    gates = jnp.dot(h, wgate_ref[0], preferred_element_type=F32)
    z = None
    for k, y_ref in enumerate((ya_ref, yb_ref, yc_ref)):
        term = (_sigmoid(gates[:, k * D_MODEL:(k + 1) * D_MODEL])
                * jnp.dot(y_ref[...], wbr_ref[0, k], preferred_element_type=F32))
        z = term if z is None else z + term
    out = jnp.dot(z.astype(BF16), wout_ref[0], preferred_element_type=F32)
    o_ref[...] = x + mod_ref[0, 0, 2:3, :] * out


def _merge(x2d, ya, yb, yc, mods, norm_g, w_gate, w_br, w_out, layer, rows_per_mod, mod_row0):
    tokens = x2d.shape[0]
    tm = 512
    tiles_per_mod = rows_per_mod // tm if rows_per_mod else 0

    def mod_idx(i):
        if rows_per_mod:
            return (layer, mod_row0 + i // tiles_per_mod, 0, 0)
        return (layer, mod_row0, 0, 0)

    return pl.pallas_call(
        _merge_kernel,
        grid=(tokens // tm,),
        in_specs=[
            pl.BlockSpec((tm, D_MODEL), lambda i: (i, 0)),
            pl.BlockSpec((tm, BRANCH_W), lambda i: (i, 0)),
            pl.BlockSpec((tm, BRANCH_W), lambda i: (i, 0)),
            pl.BlockSpec((tm, BRANCH_W), lambda i: (i, 0)),
            pl.BlockSpec((1, 1, 3, D_MODEL), mod_idx),
            pl.BlockSpec((1, 1, D_MODEL), lambda i: (layer, 0, 0)),
            pl.BlockSpec((1, D_MODEL, GATE_COLS), lambda i: (layer, 0, 0)),
            pl.BlockSpec((1, 3, BRANCH_W, D_MODEL), lambda i: (layer, 0, 0, 0)),
            pl.BlockSpec((1, D_MODEL, D_MODEL), lambda i: (layer, 0, 0)),
        ],
        out_specs=pl.BlockSpec((tm, D_MODEL), lambda i: (i, 0)),
        out_shape=jax.ShapeDtypeStruct((tokens, D_MODEL), F32),
        compiler_params=_cparams(("arbitrary",)),
        name="merge",
    )(x2d, ya, yb, yc, mods, norm_g.reshape(DEPTH, 1, D_MODEL), w_gate, w_br, w_out)


def _rope_tables(t_len):
    t = jnp.arange(t_len)
    m = HEAD_DIM // 4
    freqs = ROPE_BASE ** (-jnp.arange(m, dtype=F32) / m)
    ang_r = (t // GRID_W).astype(F32)[:, None] * freqs[None, :]
    ang_c = (t % GRID_W).astype(F32)[:, None] * freqs[None, :]
    cos = jnp.concatenate([jnp.cos(ang_r), jnp.cos(ang_r), jnp.cos(ang_c), jnp.cos(ang_c)], axis=-1)
    sin = jnp.concatenate([-jnp.sin(ang_r), jnp.sin(ang_r), -jnp.sin(ang_c), jnp.sin(ang_c)], axis=-1)
    reps = BRANCH_W // HEAD_DIM
    return jnp.tile(cos, (1, reps)), jnp.tile(sin, (1, reps))


def kernel(x_prompt, x_sample, cache_ka, cache_va, cache_kb, cache_vb, state_lru, c, c_ctx,
           norm_g, w_ada, b_ada, w_in, a_q_norm, a_k_norm, a_sink, b_q_norm, b_k_norm, b_rpb,
           lru_conv_w, lru_conv_b, lru_wa, lru_ba, lru_wx, lru_bx, lru_lambda, w_branch, w_out):
    bsz, s_len, _ = x_prompt.shape
    dbsz, t_len, _ = x_sample.shape
    p_len = cache_ka.shape[2]

    n_mod = 16
    ctx_row = dbsz
    cond = jnp.zeros((n_mod, D_MODEL), F32).at[:dbsz].set(c).at[ctx_row].set(c_ctx)
    mods = _mods(cond, w_ada, b_ada).reshape(DEPTH, n_mod, 3, D_MODEL)

    w_perm = w_gate = _wprep(w_in)
    w_br = w_branch.astype(BF16)
    w_o = w_out.astype(BF16)
    cos, sin_signed = _rope_tables(t_len)

    def cache_t(x):
        return x.transpose(0, 1, 3, 4, 2).reshape(x.shape[0], DEPTH, x.shape[3] * HEAD_DIM, p_len)
    cka, cva, ckb, cvb = cache_t(cache_ka), cache_t(cache_va), cache_t(cache_kb), cache_t(cache_vb)

    yp = x_prompt.reshape(bsz * s_len, D_MODEL)
    ys = x_sample.reshape(dbsz * t_len, D_MODEL)
    zero_state = jnp.zeros((bsz, 2, LRU_WIDTH), F32)
    new_caches, new_lru = None, []
    for l in range(DEPTH):
        wg, bg = _lru_gate_weights(lru_wa[l], lru_ba[l], lru_wx[l], lru_bx[l])

        proj = _inproj(yp, mods, norm_g, w_perm, l, 0, ctx_row)
        ya, yb, new_caches = _ctx_attn(proj, bsz, s_len, a_q_norm[l], a_k_norm[l], a_sink[l],
                                       b_q_norm[l], b_k_norm[l], l, new_caches)
        yc, st = _lru(proj, zero_state, lru_conv_w[l], lru_conv_b[l], wg, bg, lru_lambda[l], bsz, s_len)
        yp = _merge(yp, ya, yb, yc, mods, norm_g, w_gate, w_br, w_o, l, 0, ctx_row)
        new_lru.append(st)

        proj = _inproj(ys, mods, norm_g, w_perm, l, t_len, 0)
        ya = _win_attn(proj, cka, cva, l, cos, sin_signed, a_q_norm[l], a_k_norm[l], a_sink[l], dbsz, t_len)
        yb = _nbr_attn(proj, ckb, cvb, l, b_rpb[l], b_q_norm[l], b_k_norm[l], dbsz, t_len)
        yc, _ = _lru(proj, state_lru[:, l], lru_conv_w[l], lru_conv_b[l], wg, bg, lru_lambda[l], dbsz, t_len)
        ys = _merge(ys, ya, yb, yc, mods, norm_g, w_gate, w_br, w_o, l, t_len, 0)

    def cache_out(x):
        return x.reshape(bsz, DEPTH, x.shape[2] // HEAD_DIM, HEAD_DIM, s_len).transpose(0, 1, 4, 2, 3)

    return (yp.reshape(bsz, s_len, D_MODEL), ys.reshape(dbsz, t_len, D_MODEL),
            *(cache_out(x) for x in new_caches), jnp.stack(new_lru, axis=1))
```

```python
import functools

import numpy as np
import jax
import jax.numpy as jnp
from jax import lax
from jax.experimental import pallas as pl
from jax.experimental.pallas import tpu as pltpu

F32 = jnp.float32
BF16 = jnp.bfloat16

D_MODEL = 1024
DEPTH = 2
GRID_W = 64
HEAD_DIM = 64
BRANCH_W = 512
A_HEADS = 8
A_KV_HEADS = 2
A_WINDOW = 128
A_BLOCK = 128
B_HEADS = 8
NB_ROWS = 8
NB_COLS = 16
LRU_WIDTH = 512
LRU_BLOCKS = 8
LRU_BW = LRU_WIDTH // LRU_BLOCKS
LRU_C = 8.0
CONV_W = 4
ROPE_BASE = 10000.0
EPS = 1e-6
NEG_INF = -1e30
QK_SCALE = HEAD_DIM ** -0.5

LANE = 128
SUBLANE = 8
MXU_DIM = 256
VMEM_LIMIT = 56 * 1024 * 1024

_ORIG_SPLITS = (512, 128, 128, 512, 512, 512, 512, 512, 512, 512, 1024, 1024, 1024)
_ORIG_OFFS = tuple(int(v) for v in np.cumsum((0,) + _ORIG_SPLITS)[:-1])
_PERM = (0, 3, 4, 5, 6, 7, 8, 9, 1, 2)
_GATE_SPLITS = (10, 11, 12)
IN_COLS = sum(_ORIG_SPLITS[k] for k in _PERM)
GATE_COLS = sum(_ORIG_SPLITS[k] for k in _GATE_SPLITS)
C_AQ, C_AG, C_BQ, C_BK, C_BV, C_BG, C_CX, C_CG, C_AK, C_AV = (0, 4, 8, 12, 16, 20, 24, 28, 32, 33)

_NT = (((1,), (1,)), ((), ()))


def _cparams(sem):
    return pltpu.CompilerParams(dimension_semantics=sem, vmem_limit_bytes=VMEM_LIMIT)


def _sigmoid(x):
    return 0.5 + 0.5 * jnp.tanh(0.5 * x)


def _silu(x):
    return x * _sigmoid(x)


def _head_mean_matrix(width):
    idx = np.arange(width) // HEAD_DIM
    return jnp.asarray((idx[:, None] == idx[None, :]).astype(np.float32) / HEAD_DIM, dtype=BF16)


def _heads_rms(x, bd, g):
    x2 = x * x
    hi = x2.astype(BF16)
    lo = (x2 - hi.astype(F32)).astype(BF16)
    width = x.shape[-1]
    step = min(width, MXU_DIM)
    tile = bd[:step, :step]
    ms = jnp.concatenate(
        [jnp.dot(hi[:, c:c + step], tile, preferred_element_type=F32)
         + jnp.dot(lo[:, c:c + step], tile, preferred_element_type=F32) for c in range(0, width, step)], axis=-1)
    return x * lax.rsqrt(ms + EPS) * g


def _head_lane_mask(width, h):
    lane = lax.broadcasted_iota(jnp.int32, (1, width), 1)
    return (lane >= h * HEAD_DIM) & (lane < (h + 1) * HEAD_DIM)


def _mods_kernel(c_ref, w_ref, b_ref, o_ref):
    c = c_ref[...]
    s = _silu(c).astype(BF16)
    o_ref[0] = jnp.dot(s, w_ref[0].astype(BF16), preferred_element_type=F32) + b_ref[0]


def _mods(cond, w_ada, b_ada):
    n = cond.shape[0]
    tn = D_MODEL
    return pl.pallas_call(
        _mods_kernel,
        grid=(DEPTH, 3 * D_MODEL // tn),
        in_specs=[
            pl.BlockSpec((n, D_MODEL), lambda l, j: (0, 0)),
            pl.BlockSpec((1, D_MODEL, tn), lambda l, j: (l, 0, j)),
            pl.BlockSpec((1, 1, tn), lambda l, j: (l, 0, j)),
        ],
        out_specs=pl.BlockSpec((1, n, tn), lambda l, j: (l, 0, j)),
        out_shape=jax.ShapeDtypeStruct((DEPTH, n, 3 * D_MODEL), F32),
        compiler_params=_cparams(("arbitrary", "arbitrary")),
        name="mods",
    )(cond, w_ada, b_ada.reshape(DEPTH, 1, 3 * D_MODEL))


W_BLK = 256
_MIXER_BLKS = IN_COLS // W_BLK
_GATE_BLKS = GATE_COLS // W_BLK
W_MIXER_OFF = IN_COLS
W_PREP_COLS = 2 * IN_COLS


_GAP_BLKS = W_MIXER_OFF // W_BLK - _GATE_BLKS


def _wprep_kernel(w_ref, o_ref):
    n = pl.program_id(0)
    in_gap = (n >= _GATE_BLKS) & (n < _GATE_BLKS + _GAP_BLKS)
    o_ref[...] = jnp.where(in_gap, 0.0, w_ref[...]).astype(BF16)


def _wprep_src(n):
    m = n - _GATE_BLKS - _GAP_BLKS
    mixer = jnp.where(m < 0, 0, jnp.where(m < 2, m, jnp.where(m < _MIXER_BLKS - 1, m + 1, 2)))
    return jnp.where(n < _GATE_BLKS, n + _MIXER_BLKS, mixer)


def _wprep(w_in):
    assert [_ORIG_OFFS[k] // W_BLK for k in _PERM[:-2]] == [0, 3, 5, 7, 9, 11, 13, 15] and _ORIG_OFFS[1] == 2 * W_BLK
    return pl.pallas_call(
        _wprep_kernel,
        grid=(W_PREP_COLS // W_BLK,),
        in_specs=[pl.BlockSpec((DEPTH, D_MODEL, W_BLK), lambda n: (0, 0, _wprep_src(n)))],
        out_specs=pl.BlockSpec((DEPTH, D_MODEL, W_BLK), lambda n: (0, 0, n)),
        out_shape=jax.ShapeDtypeStruct((DEPTH, D_MODEL, W_PREP_COLS), BF16),
        compiler_params=_cparams(("arbitrary",)),
        name="wprep",
    )(w_in)


def _modulated_norm(x, g_ref, mod_ref):
    y = x * lax.rsqrt(jnp.mean(x * x, axis=-1, keepdims=True) + EPS)
    y = y * g_ref[0]
    shift = mod_ref[0, 0, 0:1, :]
    scale = mod_ref[0, 0, 1:2, :]
    return (y * (1.0 + scale) + shift).astype(BF16)


def _inproj_kernel(x_ref, mod_ref, g_ref, w_ref, o_ref):
    h = _modulated_norm(x_ref[...], g_ref, mod_ref)
    o_ref[...] = jnp.dot(h, w_ref[0], preferred_element_type=F32)


def _inproj(x2d, mods, norm_g, w_perm, layer, rows_per_mod, mod_row0):
    tokens = x2d.shape[0]
    tm = 512
    tiles_per_mod = rows_per_mod // tm if rows_per_mod else 0

    def mod_idx(i):
        if rows_per_mod:
            return (layer, mod_row0 + i // tiles_per_mod, 0, 0)
        return (layer, mod_row0, 0, 0)

    return pl.pallas_call(
        _inproj_kernel,
        grid=(tokens // tm,),
        in_specs=[
            pl.BlockSpec((tm, D_MODEL), lambda i: (i, 0)),
            pl.BlockSpec((1, 1, 3, D_MODEL), mod_idx),
            pl.BlockSpec((1, 1, D_MODEL), lambda i: (layer, 0, 0)),
            pl.BlockSpec((1, D_MODEL, IN_COLS), lambda i: (layer, 0, W_MIXER_OFF // IN_COLS)),
        ],
        out_specs=pl.BlockSpec((tm, IN_COLS), lambda i: (i, 0)),
        out_shape=jax.ShapeDtypeStruct((tokens, IN_COLS), F32),
        compiler_params=_cparams(("arbitrary",)),
        name="inproj",
    )(x2d, mods, norm_g.reshape(DEPTH, 1, D_MODEL), w_perm)


CTX_SEQS = 2
CTX_GROUP = MXU_DIM // HEAD_DIM


def _ctx_mixer(q, k, v, g, gq, gk, bdq, bdk, ones_stack, expand=None, sink_ref=None):
    s_len = q.shape[0]
    n_q = q.shape[1] // HEAD_DIM
    kn = _heads_rms(k, bdk, gk)
    qn = (_heads_rms(q, bdq, gq) * QK_SCALE).astype(BF16)
    knb = kn.astype(BF16)
    vb = v.astype(BF16)
    if expand is not None:
        knb = jnp.dot(knb, expand, preferred_element_type=F32).astype(BF16)
        vb = jnp.dot(vb, expand, preferred_element_type=F32).astype(BF16)
    gw = CTX_GROUP * HEAD_DIM
    masks = [_head_lane_mask(gw, hi) for hi in range(CTX_GROUP)]
    groups = []
    for gp in range(n_q // CTX_GROUP):
        cols = slice(gp * gw, (gp + 1) * gw)
        qh, kh, vh = qn[:, cols], knb[:, cols], vb[:, cols]
        kstack = jnp.concatenate([jnp.where(mk, kh, jnp.zeros_like(kh)) for mk in masks], axis=0)
        vstack = jnp.concatenate([jnp.where(mk, vh, jnp.zeros_like(vh)) for mk in masks], axis=0)
        s = lax.dot_general(qh, kstack, _NT, preferred_element_type=F32)
        ps, sink_terms = [], []
        for hi in range(CTX_GROUP):
            si = s[:, hi * s_len:(hi + 1) * s_len]
            m = jnp.max(si, axis=-1, keepdims=True)
            if sink_ref is not None:
                h = gp * CTX_GROUP + hi
                snk = sink_ref[0:1, h:h + 1]
                m = jnp.maximum(m, snk)
                sink_terms.append(jnp.where(masks[hi], jnp.exp(snk - m), 0.0))
            ps.append(jnp.exp(si - m).astype(BF16))
        p = jnp.concatenate(ps, axis=1)
        rhs = jnp.concatenate([vstack, ones_stack], axis=1)
        oe = jnp.dot(p, rhs, preferred_element_type=F32)
        l = oe[:, gw:]
        for term in sink_terms:
            l = l + term
        groups.append(oe[:, :gw] / l)
    return jnp.concatenate(groups, axis=-1) * _silu(g), kn


_CTX_INPUTS = 17


def _ctx_attn_kernel(*refs, s_len, n_alias):
    (aq_ref, ak_ref, av_ref, ag_ref, bq_ref, bk_ref, bv_ref, bg_ref,
     gqa_ref, gka_ref, sink_ref, gqb_ref, gkb_ref, bdw_ref, bdn_ref, exp_ref, ones_ref) = refs[:_CTX_INPUTS]
    ya_ref, yb_ref, ka_ref, va_ref, kb_ref, vb_ref = refs[_CTX_INPUTS + n_alias:]
    ones_stack = ones_ref[...]
    bdw = bdw_ref[...]
    for n in range(aq_ref.shape[0] // s_len):
        rows = slice(n * s_len, (n + 1) * s_len)
        va = av_ref[rows, :]
        vb = bv_ref[rows, :]
        va_ref[n, 0] = va.T
        vb_ref[n, 0] = vb.T
        ya, kna = _ctx_mixer(aq_ref[rows, :], ak_ref[rows, :], va, ag_ref[rows, :], gqa_ref[...], gka_ref[...],
                             bdw, bdn_ref[...], ones_stack, expand=exp_ref[...], sink_ref=sink_ref)
        yb, knb = _ctx_mixer(bq_ref[rows, :], bk_ref[rows, :], vb, bg_ref[rows, :], gqb_ref[...], gkb_ref[...],
                             bdw, bdw, ones_stack)
        ka_ref[n, 0] = kna.T
        kb_ref[n, 0] = knb.T
        ya_ref[rows, :] = ya.astype(ya_ref.dtype)
        yb_ref[rows, :] = yb.astype(yb_ref.dtype)


def _ctx_attn(proj, bsz, s_len, a_gq, a_gk, a_sink, b_gq, b_gk, layer, caches):
    per_layer = lambda width: pl.BlockSpec((None, 1, width), lambda b: (layer, 0, 0))
    kvw = A_KV_HEADS * HEAD_DIM
    grp = A_HEADS // A_KV_HEADS
    expand = np.zeros((kvw, BRANCH_W), np.float32)
    for h in range(A_HEADS):
        for d in range(HEAD_DIM):
            expand[(h // grp) * HEAD_DIM + d, h * HEAD_DIM + d] = 1.0
    ones_stack = np.zeros((CTX_GROUP * s_len, CTX_GROUP * HEAD_DIM), np.float32)
    for hi in range(CTX_GROUP):
        ones_stack[hi * s_len:(hi + 1) * s_len, hi * HEAD_DIM:(hi + 1) * HEAD_DIM] = 1.0
    const = lambda b: (0, 0)
    rows = CTX_SEQS * s_len
    wide = lambda c: pl.BlockSpec((rows, BRANCH_W), lambda b: (b, c // 4))
    narrow = lambda c: pl.BlockSpec((rows, kvw), lambda b: (b, c))
    prev = () if caches is None else tuple(caches)
    n_alias = len(prev)
    cache_spec = lambda width: pl.BlockSpec((CTX_SEQS, 1, width, s_len), lambda b: (b, layer, 0, 0))
    cache_shape = lambda width: jax.ShapeDtypeStruct((bsz, DEPTH, width, s_len), F32)
    outs = pl.pallas_call(
        functools.partial(_ctx_attn_kernel, s_len=s_len, n_alias=n_alias),
        grid=(bsz // CTX_SEQS,),
        in_specs=[
            wide(C_AQ), narrow(C_AK), narrow(C_AV), wide(C_AG), wide(C_BQ), wide(C_BK), wide(C_BV), wide(C_BG),
            per_layer(BRANCH_W), per_layer(kvw), per_layer(A_HEADS), per_layer(BRANCH_W), per_layer(BRANCH_W),
            pl.BlockSpec((BRANCH_W, BRANCH_W), const),
            pl.BlockSpec((kvw, kvw), const),
            pl.BlockSpec((kvw, BRANCH_W), const),
            pl.BlockSpec((CTX_GROUP * s_len, CTX_GROUP * HEAD_DIM), const),
        ] + [pl.BlockSpec(memory_space=pl.ANY)] * n_alias,
        out_specs=[
            pl.BlockSpec((rows, BRANCH_W), lambda b: (b, 0)),
            pl.BlockSpec((rows, BRANCH_W), lambda b: (b, 0)),
            cache_spec(kvw), cache_spec(kvw), cache_spec(BRANCH_W), cache_spec(BRANCH_W),
        ],
        out_shape=[
            jax.ShapeDtypeStruct((bsz * s_len, BRANCH_W), BF16),
            jax.ShapeDtypeStruct((bsz * s_len, BRANCH_W), BF16),
            cache_shape(kvw), cache_shape(kvw), cache_shape(BRANCH_W), cache_shape(BRANCH_W),
        ],
        input_output_aliases={_CTX_INPUTS + i: 2 + i for i in range(n_alias)},
        compiler_params=_cparams(("arbitrary",)),
        name="ctx_attn",
    )(proj, proj, proj, proj, proj, proj, proj, proj,
      a_gq, a_gk, a_sink, b_gq, b_gk, _head_mean_matrix(BRANCH_W), _head_mean_matrix(kvw), jnp.asarray(expand, dtype=BF16),
      jnp.asarray(ones_stack, dtype=BF16), *prev)
    return outs[0], outs[1], tuple(outs[2:])


def _rope(x, cos, sin_signed):
    w = x.shape[-1]
    lane = lax.broadcasted_iota(jnp.int32, x.shape, 1)
    up = pltpu.roll(x, w - 16, axis=1)
    dn = pltpu.roll(x, 16, axis=1)
    partner = jnp.where((lane & 16) == 0, up, dn)
    return x * cos + partner * sin_signed


WIN_SUB = 8


def _win_attn_kernel(q_ref, k_ref, v_ref, g_ref, kc_ref, vc_ref, cos_ref, sin_ref, gq_ref, gk_ref, sink_ref,
                     bdq_ref, bdk_ref, dup_ref, mask_ref, y_ref, kpad, vpad, kcx, vcx):
    j = pl.program_id(1)
    nb = pl.num_programs(1)
    t_len = k_ref.shape[0]
    grp = A_HEADS // A_KV_HEADS
    kvw = A_KV_HEADS * HEAD_DIM
    xw = 2 * kvw

    @pl.when(j == 0)
    def _():
        dup = dup_ref[...]
        kn = _heads_rms(k_ref[...], bdk_ref[...], gk_ref[...])
        kn = _rope(kn, cos_ref[:, 0:kvw], sin_ref[:, 0:kvw]).astype(BF16)
        def with_ones(vx):
            ones = jnp.ones((vx.shape[0], LANE), BF16)
            return jnp.concatenate(
                [part for kv in range(A_KV_HEADS) for part in (vx[:, kv * LANE:(kv + 1) * LANE], ones)], axis=1)

        kpad[0:A_BLOCK, :] = jnp.zeros((A_BLOCK, xw), BF16)
        kpad[A_BLOCK + t_len:2 * A_BLOCK + t_len, :] = jnp.zeros((A_BLOCK, xw), BF16)
        vpad[0:A_BLOCK, :] = jnp.zeros((A_BLOCK, 2 * xw), BF16)
        vpad[A_BLOCK + t_len:2 * A_BLOCK + t_len, :] = jnp.zeros((A_BLOCK, 2 * xw), BF16)
        kpad[A_BLOCK:A_BLOCK + t_len, :] = jnp.dot(kn, dup, preferred_element_type=F32).astype(BF16)
        vpad[A_BLOCK:A_BLOCK + t_len, :] = with_ones(
            jnp.dot(v_ref[...].astype(BF16), dup, preferred_element_type=F32).astype(BF16))
        kcx[...] = jnp.dot(kc_ref[0, 0].T.astype(BF16), dup, preferred_element_type=F32).astype(BF16)
        vcx[...] = with_ones(jnp.dot(vc_ref[0, 0].T.astype(BF16), dup, preferred_element_type=F32).astype(BF16))

    nloc = 3 * A_BLOCK
    low_half = lax.broadcasted_iota(jnp.int32, (1, LANE), 1) < HEAD_DIM
    for sub in range(WIN_SUB):
        jj = j * WIN_SUB + sub
        qrows = slice(sub * A_BLOCK, (sub + 1) * A_BLOCK)
        r0 = pl.multiple_of(jj * A_BLOCK, A_BLOCK)
        qn = _heads_rms(q_ref[qrows, :], bdq_ref[...], gq_ref[...])
        qb = (_rope(qn, cos_ref[pl.ds(r0, A_BLOCK), :], sin_ref[pl.ds(r0, A_BLOCK), :]) * QK_SCALE).astype(BF16)
        maskadd = mask_ref[jnp.where(jj == 0, 0, jnp.where(jj == nb * WIN_SUB - 1, 2, 1))]
        kband = kpad[pl.ds(r0, nloc), :]
        vband = vpad[pl.ds(r0, nloc), :]
        pairs = []
        for kv in range(A_KV_HEADS):
            cols = slice(kv * LANE, (kv + 1) * LANE)
            qparts, sinks = [], []
            for gi in range(grp):
                h = kv * grp + gi
                qpair = qb[:, (h // 2) * LANE:(h // 2 + 1) * LANE]
                keep = low_half if h % 2 == 0 else jnp.logical_not(low_half)
                qparts.append(jnp.where(keep, qpair, jnp.zeros_like(qpair)))
                sinks.append(jnp.broadcast_to(sink_ref[0:1, h:h + 1], (A_BLOCK, 1)))
            qst = jnp.concatenate(qparts, axis=0)
            snk = jnp.concatenate(sinks, axis=0)
            s_loc = lax.dot_general(qst, kband[:, cols], _NT, preferred_element_type=F32) + maskadd
            s_ctx = lax.dot_general(qst, kcx[:, cols], _NT, preferred_element_type=F32)
            m = jnp.maximum(jnp.maximum(jnp.max(s_loc, axis=-1, keepdims=True),
                                        jnp.max(s_ctx, axis=-1, keepdims=True)), snk)
            p_loc = jnp.exp(s_loc - m).astype(BF16)
            p_ctx = jnp.exp(s_ctx - m).astype(BF16)
            wide = slice(kv * MXU_DIM, (kv + 1) * MXU_DIM)
            oe = (jnp.dot(p_loc, vband[:, wide], preferred_element_type=F32)
                  + jnp.dot(p_ctx, vcx[:, wide], preferred_element_type=F32))
            o = oe[:, :LANE] / (oe[:, LANE:] + jnp.exp(snk - m))
            for k2 in range(grp // 2):
                even = o[(2 * k2) * A_BLOCK:(2 * k2 + 1) * A_BLOCK]
                odd = o[(2 * k2 + 1) * A_BLOCK:(2 * k2 + 2) * A_BLOCK]
                pairs.append(jnp.where(low_half, even, odd))
        y = jnp.concatenate(pairs, axis=-1) * _silu(g_ref[qrows, :])
        y_ref[qrows, :] = y.astype(y_ref.dtype)


def _win_mask(grp, nb):
    assert nb >= 2
    r = np.arange(A_BLOCK)[:, None]
    c = np.arange(3 * A_BLOCK)[None, :]
    band = np.abs(r + A_BLOCK - c) <= A_WINDOW
    variants = [band & (c >= A_BLOCK), band, band & (c < 2 * A_BLOCK)]
    return np.stack([np.tile(np.where(v, 0.0, NEG_INF).astype(np.float32), (grp, 1)) for v in variants])


def _win_attn(proj, cache_k, cache_v, layer, cos, sin_signed, gq, gk, sink, bsz, t_len):
    nb = t_len // A_BLOCK
    assert nb % WIN_SUB == 0
    nsteps = nb // WIN_SUB
    kvw = A_KV_HEADS * HEAD_DIM
    grp = A_HEADS // A_KV_HEADS
    p_len = cache_k.shape[3]
    dup = np.zeros((kvw, 2 * kvw), np.float32)
    for kv in range(A_KV_HEADS):
        for half in range(2):
            for d in range(HEAD_DIM):
                dup[kv * HEAD_DIM + d, kv * LANE + half * HEAD_DIM + d] = 1.0
    const2 = lambda b, j: (0, 0)
    return pl.pallas_call(
        _win_attn_kernel,
        grid=(bsz, nsteps),
        in_specs=[
            pl.BlockSpec((WIN_SUB * A_BLOCK, BRANCH_W), lambda b, j: (b * nsteps + j, C_AQ // 4)),
            pl.BlockSpec((t_len, kvw), lambda b, j: (b, C_AK)),
            pl.BlockSpec((t_len, kvw), lambda b, j: (b, C_AV)),
            pl.BlockSpec((WIN_SUB * A_BLOCK, BRANCH_W), lambda b, j: (b * nsteps + j, C_AG // 4)),
            pl.BlockSpec((1, 1, kvw, p_len), lambda b, j: (b, layer, 0, 0)),
            pl.BlockSpec((1, 1, kvw, p_len), lambda b, j: (b, layer, 0, 0)),
            pl.BlockSpec((t_len, BRANCH_W), const2),
            pl.BlockSpec((t_len, BRANCH_W), const2),
            pl.BlockSpec((None, 1, BRANCH_W), lambda b, j: (layer, 0, 0)),
            pl.BlockSpec((None, 1, kvw), lambda b, j: (layer, 0, 0)),
            pl.BlockSpec((None, 1, A_HEADS), lambda b, j: (layer, 0, 0)),
            pl.BlockSpec((BRANCH_W, BRANCH_W), const2),
            pl.BlockSpec((kvw, kvw), const2),
            pl.BlockSpec((kvw, 2 * kvw), const2),
            pl.BlockSpec((3, grp * A_BLOCK, 3 * A_BLOCK), lambda b, j: (0, 0, 0)),
        ],
        out_specs=pl.BlockSpec((WIN_SUB * A_BLOCK, BRANCH_W), lambda b, j: (b * nsteps + j, 0)),
        out_shape=jax.ShapeDtypeStruct((bsz * t_len, BRANCH_W), BF16),
        scratch_shapes=[pltpu.VMEM((t_len + 2 * A_BLOCK, 2 * kvw), BF16),
                        pltpu.VMEM((t_len + 2 * A_BLOCK, 4 * kvw), BF16),
                        pltpu.VMEM((p_len, 2 * kvw), BF16),
                        pltpu.VMEM((p_len, 4 * kvw), BF16)],
        compiler_params=_cparams(("arbitrary", "arbitrary")),
        name="win_attn",
    )(proj, proj, proj, proj, cache_k, cache_v, cos, sin_signed,
      gq, gk, sink, _head_mean_matrix(BRANCH_W), _head_mean_matrix(kvw),
      jnp.asarray(dup, dtype=BF16), jnp.asarray(_win_mask(grp, nb)))


NBR_QB = 2 * GRID_W
NBR_BAND = 5


def _nbr_plan(rows):
    kh = min(NB_ROWS, rows)
    nblk = rows // 2
    specs, plan, starts = {}, [], []
    for i in range(nblk):
        s0 = min(max(i - 2, 0), nblk - NBR_BAND)
        starts.append(s0)
        blk = []
        for a in range(2):
            qr = 2 * i + a
            rs = min(max(qr - kh // 2, 0), rows - kh)
            assert 2 * s0 <= rs and rs + kh <= 2 * (s0 + NBR_BAND)
            row = []
            for p in range(NBR_BAND):
                pair = tuple(kr - qr + NB_ROWS - 1 if rs <= kr < rs + kh else None
                             for kr in (2 * (s0 + p), 2 * (s0 + p) + 1))
                row.append(specs.setdefault(pair, len(specs)))
            blk.append(row)
        plan.append(blk)
    return tuple(starts), plan, list(specs)


def _nbr_table(rpb, specs):
    heads = rpb.shape[0]
    c = np.arange(GRID_W)
    cs = np.clip(c - NB_COLS // 2, 0, GRID_W - NB_COLS)
    col_ok = (c[None, :] >= cs[:, None]) & (c[None, :] < cs[:, None] + NB_COLS)
    edge = GRID_W - NB_COLS
    period = 2 * GRID_W - 1
    padded = jnp.pad(rpb.astype(F32), ((0, 0), (0, 0), (edge, edge)))
    tiled = jnp.tile(padded, (1, 1, GRID_W + 1))[:, :, :GRID_W * (period + 1)]
    skew = tiled.reshape(heads, 2 * NB_ROWS - 1, GRID_W, period + 1)
    shifted = skew[:, :, ::-1, :GRID_W]
    by_col = jnp.where(col_ok[None, None], shifted, NEG_INF)
    neg = jnp.full((heads, GRID_W, GRID_W), NEG_INF, F32)
    blocks = [jnp.concatenate([neg if d is None else by_col[:, d] for d in spec], axis=-1) for spec in specs]
    return jnp.stack(blocks, axis=1)


def _nbr_attn_kernel(q_ref, k_ref, v_ref, g_ref, kc_ref, vc_ref, tb_ref, gq_ref, gk_ref, bd_ref, y_ref,
                     *, starts, plan):
    nband = NBR_BAND * NBR_QB
    bd = bd_ref[...]
    qn = (_heads_rms(q_ref[...], bd, gq_ref[...]) * QK_SCALE).astype(BF16)
    kn = _heads_rms(k_ref[...], bd, gk_ref[...]).astype(BF16)
    vb = v_ref[...].astype(BF16)
    kcb = kc_ref[0, 0].T.astype(BF16)
    vcb = vc_ref[0, 0].T.astype(BF16)
    acc = [jnp.zeros((NBR_QB, LANE), F32) for _ in starts]
    for h in range(LANE // HEAD_DIM):
        hm = _head_lane_mask(LANE, h)
        km = jnp.where(hm, kn, jnp.zeros_like(kn))
        kcm = jnp.where(hm, kcb, jnp.zeros_like(kcb))
        vm = jnp.concatenate([jnp.where(hm, vb, jnp.zeros_like(vb)), jnp.ones_like(vb)], axis=1)
        vcm = jnp.concatenate([jnp.where(hm, vcb, jnp.zeros_like(vcb)), jnp.ones_like(vcb)], axis=1)
        for i, s0 in enumerate(starts):
            qi = qn[i * NBR_QB:(i + 1) * NBR_QB]
            ks = slice(s0 * NBR_QB, s0 * NBR_QB + nband)
            s_raw = lax.dot_general(qi, km[ks], _NT, preferred_element_type=F32)
            s_loc = jnp.concatenate(
                [jnp.concatenate([s_raw[a * GRID_W:(a + 1) * GRID_W, p * LANE:(p + 1) * LANE]
                                  + tb_ref[h, plan[i][a][p]] for p in range(NBR_BAND)], axis=1)
                 for a in range(2)], axis=0)
            s_ctx = lax.dot_general(qi, kcm, _NT, preferred_element_type=F32)
            m = jnp.maximum(jnp.max(s_loc, axis=-1, keepdims=True), jnp.max(s_ctx, axis=-1, keepdims=True))
            p_loc = jnp.exp(s_loc - m).astype(BF16)
            p_ctx = jnp.exp(s_ctx - m).astype(BF16)
            oe = (jnp.dot(p_loc, vm[ks], preferred_element_type=F32)
                  + jnp.dot(p_ctx, vcm, preferred_element_type=F32))
            acc[i] = acc[i] + oe[:, :LANE] / oe[:, LANE:]
    y = jnp.concatenate(acc, axis=0) * _silu(g_ref[...])
    y_ref[...] = y.astype(y_ref.dtype)


def _nbr_attn(proj, cache_k, cache_v, layer, table, starts, plan, gq, gk, bsz, t_len):
    hp = LANE // HEAD_DIM
    nhp = B_HEADS // hp
    p_len = cache_k.shape[3]
    const = lambda h, b: (0, 0)
    kern = functools.partial(_nbr_attn_kernel, starts=starts, plan=plan)
    return pl.pallas_call(
        kern,
        grid=(nhp, bsz),
        in_specs=[
            pl.BlockSpec((t_len, LANE), lambda h, b: (b, C_BQ + h)),
            pl.BlockSpec((t_len, LANE), lambda h, b: (b, C_BK + h)),
            pl.BlockSpec((t_len, LANE), lambda h, b: (b, C_BV + h)),
            pl.BlockSpec((t_len, LANE), lambda h, b: (b, C_BG + h)),
            pl.BlockSpec((1, 1, LANE, p_len), lambda h, b: (b, layer, h, 0)),
            pl.BlockSpec((1, 1, LANE, p_len), lambda h, b: (b, layer, h, 0)),
            pl.BlockSpec((hp,) + table.shape[1:], lambda h, b: (layer * nhp + h, 0, 0, 0)),
            pl.BlockSpec((None, 1, LANE), lambda h, b: (layer, 0, 0)),
            pl.BlockSpec((None, 1, LANE), lambda h, b: (layer, 0, 0)),
            pl.BlockSpec((LANE, LANE), const),
        ],
        out_specs=pl.BlockSpec((t_len, LANE), lambda h, b: (b, h)),
        out_shape=jax.ShapeDtypeStruct((bsz * t_len, BRANCH_W), BF16),
        compiler_params=_cparams(("arbitrary", "arbitrary")),
        name="nbr_attn",
    )(proj, proj, proj, proj, cache_k, cache_v, table, gq, gk, _head_mean_matrix(LANE))


LRU_CHUNKS = SUBLANE
LRU_PITCH_PAD = 4


def _lru_kernel(cx_ref, cg_ref, h0_ref, cw_ref, cb_ref, wg_ref, bg_ref, lam_ref,
                y_ref, st_ref, a_s, u_s, h_s, p_s, y_s):
    t_len = cx_ref.shape[0]
    w = LRU_WIDTH
    cx = cx_ref[...]
    row = lax.broadcasted_iota(jnp.int32, (t_len, w), 0)
    xc = cb_ref[...] + cx * cw_ref[2:3, :]
    xc = xc + jnp.where(row >= 2, pltpu.roll(cx, 2, axis=0), 0.0) * cw_ref[0:1, :]
    xc = xc + jnp.where(row >= 1, pltpu.roll(cx, 1, axis=0), 0.0) * cw_ref[1:2, :]
    xc = xc + jnp.where(row < t_len - 1, pltpu.roll(cx, t_len - 1, axis=0), 0.0) * cw_ref[3:4, :]

    gates = jnp.dot(xc.astype(BF16), wg_ref[...], preferred_element_type=F32) + bg_ref[...]
    coeffs = []
    for d in range(2):
        th_r = jnp.tanh(gates[:, d * w:(d + 1) * w])
        th_i = jnp.tanh(gates[:, (2 + d) * w:(3 + d) * w])
        nl = -lam_ref[d:d + 1, :]
        softplus = jnp.maximum(nl, 0.0) + jnp.log1p(jnp.exp(-jnp.abs(nl)))
        quarter_c = (-0.25 * LRU_C) * softplus
        half_log_a = quarter_c * th_r + quarter_c
        t = jnp.tanh(half_log_a)
        rc = 1.0 / (1.0 - t)
        coeffs.append(((1.0 + t) * rc,
                       jnp.sqrt(-t) * rc * (1.0 + th_i) * xc))

    chunk_len = t_len // LRU_CHUNKS
    pitch = chunk_len + LRU_PITCH_PAD
    nt = w // LANE
    for d, (a_val, u_val) in enumerate(coeffs):
        for c in range(LRU_CHUNKS):
            for k in range(nt):
                dst = slice(c * pitch, c * pitch + chunk_len)
                src = (slice(c * chunk_len, (c + 1) * chunk_len), slice(k * LANE, (k + 1) * LANE))
                a_s[d, k, dst, :] = a_val[src]
                u_s[d, k, dst, :] = u_val[src]

    def scan_body(s, carry):
        hs, ps = carry
        new_h, new_p = [], []
        for d in range(2):
            pos = s if d == 0 else chunk_len - 1 - s
            idx = pl.ds(pos, LRU_CHUNKS, stride=pitch)
            for k in range(nt):
                av = a_s[d, k, idx, :]
                h = av * hs[d * nt + k] + u_s[d, k, idx, :]
                p = av * ps[d * nt + k]
                h_s[d, k, idx, :] = h
                p_s[d, k, idx, :] = p
                new_h.append(h)
                new_p.append(p)
        return tuple(new_h), tuple(new_p)

    zero = jnp.zeros((LRU_CHUNKS, LANE), F32)
    one = jnp.ones((LRU_CHUNKS, LANE), F32)
    h_end, p_end = lax.fori_loop(0, chunk_len, scan_body, ((zero,) * (2 * nt), (one,) * (2 * nt)), unroll=4)

    h0 = h0_ref[0]
    enter, finals = [], []
    for d in range(2):
        order = range(LRU_CHUNKS) if d == 0 else range(LRU_CHUNKS - 1, -1, -1)
        for k in range(nt):
            he, pe = h_end[d * nt + k], p_end[d * nt + k]
            e = h0[d:d + 1, k * LANE:(k + 1) * LANE]
            rows = [None] * LRU_CHUNKS
            for c in order:
                rows[c] = e
                e = pe[c:c + 1, :] * e + he[c:c + 1, :]
            enter.append(jnp.concatenate(rows, axis=0))
            finals.append(e)
    st_ref[0] = jnp.concatenate([jnp.concatenate(finals[d * nt:(d + 1) * nt], axis=1) for d in range(2)], axis=0)

    def fix_body(s, carry):
        idx = pl.ds(s, LRU_CHUNKS, stride=pitch)
        for k in range(nt):
            hf = h_s[0, k, idx, :] + p_s[0, k, idx, :] * enter[k]
            hb = h_s[1, k, idx, :] + p_s[1, k, idx, :] * enter[nt + k]
            y_s[k, idx, :] = hf + hb
        return carry

    lax.fori_loop(0, chunk_len, fix_body, 0, unroll=4)
    hsum = jnp.concatenate(
        [jnp.concatenate([y_s[k, c * pitch:c * pitch + chunk_len, :] for c in range(LRU_CHUNKS)], axis=0)
         for k in range(nt)], axis=1)
    y_ref[...] = (hsum * _silu(cg_ref[...])).astype(y_ref.dtype)


def _lru(proj, h0, h0_layer, conv_w, conv_b, w_gates, b_gates, lam, layer, bsz, t_len):
    w = LRU_WIDTH
    assert t_len % (LRU_CHUNKS * SUBLANE) == 0
    rows_pad = LRU_CHUNKS * (t_len // LRU_CHUNKS + LRU_PITCH_PAD)
    return pl.pallas_call(
        _lru_kernel,
        grid=(bsz,),
        in_specs=[
            pl.BlockSpec((t_len, w), lambda b: (b, C_CX // 4)),
            pl.BlockSpec((t_len, w), lambda b: (b, C_CG // 4)),
            pl.BlockSpec((1, None, 2, w), lambda b: (b, h0_layer, 0, 0)),
            pl.BlockSpec((None, CONV_W, w), lambda b: (layer, 0, 0)),
            pl.BlockSpec((None, 1, w), lambda b: (layer, 0, 0)),
            pl.BlockSpec((None, w, 4 * w), lambda b: (layer, 0, 0)),
            pl.BlockSpec((None, 1, 4 * w), lambda b: (layer, 0, 0)),
            pl.BlockSpec((None, 2, w), lambda b: (layer, 0, 0)),
        ],
        out_specs=[
            pl.BlockSpec((t_len, w), lambda b: (b, 0)),
            pl.BlockSpec((1, 2, w), lambda b: (b, 0, 0)),
        ],
        out_shape=[
            jax.ShapeDtypeStruct((bsz * t_len, w), BF16),
            jax.ShapeDtypeStruct((bsz, 2, w), F32),
        ],
        scratch_shapes=[pltpu.VMEM((2, w // LANE, rows_pad, LANE), F32) for _ in range(4)]
        + [pltpu.VMEM((w // LANE, rows_pad, LANE), F32)],
        compiler_params=_cparams(("arbitrary",)),
        name="lru",
    )(proj, proj, h0, conv_w, conv_b, w_gates, b_gates, lam)


def _lru_gate_weights(wa, ba, wx, bx):
    def dense(wblk):
        eye = jnp.eye(LRU_BLOCKS, dtype=wblk.dtype)
        full = jnp.einsum('ldnkj,nm->lnkdmj', wblk, eye)
        return full.reshape(DEPTH, LRU_WIDTH, 2 * LRU_WIDTH)
    wg = jnp.concatenate([dense(wa), dense(wx)], axis=2)
    bg = jnp.concatenate([ba.reshape(DEPTH, 1, 2 * LRU_WIDTH), bx.reshape(DEPTH, 1, 2 * LRU_WIDTH)], axis=2)
    return (0.5 * wg).astype(BF16), 0.5 * bg


def _merge_kernel(x_ref, ya_ref, yb_ref, yc_ref, mod_ref, g_ref, wgate_ref, wbr_ref, wout_ref, o_ref):
    x = x_ref[...]
    h = _modulated_norm(x, g_ref, mod_ref)
    gates = jnp.dot(h, wgate_ref[0], preferred_element_type=F32)
    z = None
    for k, y_ref in enumerate((ya_ref, yb_ref, yc_ref)):
        term = (_sigmoid(gates[:, k * D_MODEL:(k + 1) * D_MODEL])
                * jnp.dot(y_ref[...], wbr_ref[0, k], preferred_element_type=F32))
        z = term if z is None else z + term
    out = jnp.dot(z.astype(BF16), wout_ref[0], preferred_element_type=F32)
    o_ref[...] = x + mod_ref[0, 0, 2:3, :] * out


def _merge(x2d, ya, yb, yc, mods, norm_g, w_gate, w_br, w_out, layer, rows_per_mod, mod_row0):
    tokens = x2d.shape[0]
    tm = 512
    tiles_per_mod = rows_per_mod // tm if rows_per_mod else 0

    def mod_idx(i):
        if rows_per_mod:
            return (layer, mod_row0 + i // tiles_per_mod, 0, 0)
        return (layer, mod_row0, 0, 0)

    return pl.pallas_call(
        _merge_kernel,
        grid=(tokens // tm,),
        in_specs=[
            pl.BlockSpec((tm, D_MODEL), lambda i: (i, 0)),
            pl.BlockSpec((tm, BRANCH_W), lambda i: (i, 0)),
            pl.BlockSpec((tm, BRANCH_W), lambda i: (i, 0)),
            pl.BlockSpec((tm, BRANCH_W), lambda i: (i, 0)),
            pl.BlockSpec((1, 1, 3, D_MODEL), mod_idx),
            pl.BlockSpec((1, 1, D_MODEL), lambda i: (layer, 0, 0)),
            pl.BlockSpec((1, D_MODEL, GATE_COLS), lambda i: (layer, 0, 0)),
            pl.BlockSpec((1, 3, BRANCH_W, D_MODEL), lambda i: (layer, 0, 0, 0)),
            pl.BlockSpec((1, D_MODEL, D_MODEL), lambda i: (layer, 0, 0)),
        ],
        out_specs=pl.BlockSpec((tm, D_MODEL), lambda i: (i, 0)),
        out_shape=jax.ShapeDtypeStruct((tokens, D_MODEL), F32),
        compiler_params=_cparams(("arbitrary",)),
        name="merge",
    )(x2d, ya, yb, yc, mods, norm_g.reshape(DEPTH, 1, D_MODEL), w_gate, w_br, w_out)


def _rope_tables(t_len):
    t = jnp.arange(t_len)
    m = HEAD_DIM // 4
    freqs = ROPE_BASE ** (-jnp.arange(m, dtype=F32) / m)
    ang_r = (t // GRID_W).astype(F32)[:, None] * freqs[None, :]
    ang_c = (t % GRID_W).astype(F32)[:, None] * freqs[None, :]
    cos = jnp.concatenate([jnp.cos(ang_r), jnp.cos(ang_r), jnp.cos(ang_c), jnp.cos(ang_c)], axis=-1)
    sin = jnp.concatenate([-jnp.sin(ang_r), jnp.sin(ang_r), -jnp.sin(ang_c), jnp.sin(ang_c)], axis=-1)
    reps = BRANCH_W // HEAD_DIM
    return jnp.tile(cos, (1, reps)), jnp.tile(sin, (1, reps))


def kernel(x_prompt, x_sample, cache_ka, cache_va, cache_kb, cache_vb, state_lru, c, c_ctx,
           norm_g, w_ada, b_ada, w_in, a_q_norm, a_k_norm, a_sink, b_q_norm, b_k_norm, b_rpb,
           lru_conv_w, lru_conv_b, lru_wa, lru_ba, lru_wx, lru_bx, lru_lambda, w_branch, w_out):
    bsz, s_len, _ = x_prompt.shape
    dbsz, t_len, _ = x_sample.shape
    p_len = cache_ka.shape[2]

    n_mod = 16
    ctx_row = dbsz
    cond = jnp.zeros((n_mod, D_MODEL), F32).at[:dbsz].set(c).at[ctx_row].set(c_ctx)
    mods = _mods(cond, w_ada, b_ada).reshape(DEPTH, n_mod, 3, D_MODEL)

    w_perm = w_gate = _wprep(w_in)
    w_br = w_branch.astype(BF16)
    w_o = w_out.astype(BF16)
    cos, sin_signed = _rope_tables(t_len)

    def cache_t(x):
        return x.transpose(0, 1, 3, 4, 2).reshape(x.shape[0], DEPTH, x.shape[3] * HEAD_DIM, p_len)
    cka, cva, ckb, cvb = cache_t(cache_ka), cache_t(cache_va), cache_t(cache_kb), cache_t(cache_vb)

    def gain_rows(g, reps):
        return jnp.tile(g, (1, reps)).reshape(DEPTH, 1, reps * HEAD_DIM)
    a_gq, a_gk = gain_rows(a_q_norm, A_HEADS), gain_rows(a_k_norm, A_KV_HEADS)
    b_gq, b_gk = gain_rows(b_q_norm, B_HEADS), gain_rows(b_k_norm, B_HEADS)
    b_gq2, b_gk2 = gain_rows(b_q_norm, LANE // HEAD_DIM), gain_rows(b_k_norm, LANE // HEAD_DIM)
    sinks = a_sink.reshape(DEPTH, 1, A_HEADS)
    wg, bg = _lru_gate_weights(lru_wa, lru_ba, lru_wx, lru_bx)
    conv_b = lru_conv_b.reshape(DEPTH, 1, LRU_WIDTH)
    grid_rows = t_len // GRID_W
    assert grid_rows % 2 == 0 and grid_rows // 2 >= NBR_BAND
    nbr_starts, nbr_plan, nbr_specs = _nbr_plan(grid_rows)
    nbr_table = _nbr_table(b_rpb.reshape((DEPTH * B_HEADS,) + b_rpb.shape[2:]), nbr_specs)

    yp = x_prompt.reshape(bsz * s_len, D_MODEL)
    ys = x_sample.reshape(dbsz * t_len, D_MODEL)
    zero_state = jnp.zeros((bsz, 1, 2, LRU_WIDTH), F32)
    new_caches, new_lru = None, []
    for l in range(DEPTH):
        proj = _inproj(yp, mods, norm_g, w_perm, l, 0, ctx_row)
        ya, yb, new_caches = _ctx_attn(proj, bsz, s_len, a_gq, a_gk, sinks, b_gq, b_gk, l, new_caches)
        yc, st = _lru(proj, zero_state, 0, lru_conv_w, conv_b, wg, bg, lru_lambda, l, bsz, s_len)
        yp = _merge(yp, ya, yb, yc, mods, norm_g, w_gate, w_br, w_o, l, 0, ctx_row)
        new_lru.append(st)

        proj = _inproj(ys, mods, norm_g, w_perm, l, t_len, 0)
        ya = _win_attn(proj, cka, cva, l, cos, sin_signed, a_gq, a_gk, sinks, dbsz, t_len)
        yb = _nbr_attn(proj, ckb, cvb, l, nbr_table, nbr_starts, nbr_plan, b_gq2, b_gk2, dbsz, t_len)
        yc, _ = _lru(proj, state_lru, l, lru_conv_w, conv_b, wg, bg, lru_lambda, l, dbsz, t_len)
        ys = _merge(ys, ya, yb, yc, mods, norm_g, w_gate, w_br, w_o, l, t_len, 0)

    def cache_out(x):
        return x.reshape(bsz, DEPTH, x.shape[2] // HEAD_DIM, HEAD_DIM, s_len).transpose(0, 1, 4, 2, 3)

    return (yp.reshape(bsz, s_len, D_MODEL), ys.reshape(dbsz, t_len, D_MODEL),
            *(cache_out(x) for x in new_caches), jnp.stack(new_lru, axis=1))
```

```python
import functools

import numpy as np
import jax
import jax.numpy as jnp
from jax import lax
from jax.experimental import pallas as pl
from jax.experimental.pallas import tpu as pltpu

F32 = jnp.float32
BF16 = jnp.bfloat16

D_MODEL = 1024
DEPTH = 2
GRID_W = 64
HEAD_DIM = 64
BRANCH_W = 512
A_HEADS = 8
A_KV_HEADS = 2
A_WINDOW = 128
A_BLOCK = 128
B_HEADS = 8
NB_ROWS = 8
NB_COLS = 16
LRU_WIDTH = 512
LRU_BLOCKS = 8
LRU_BW = LRU_WIDTH // LRU_BLOCKS
LRU_C = 8.0
CONV_W = 4
ROPE_BASE = 10000.0
EPS = 1e-6
NEG_INF = -1e30
QK_SCALE = HEAD_DIM ** -0.5

LANE = 128
SUBLANE = 8
MXU_DIM = 256
VMEM_LIMIT = 56 * 1024 * 1024

_ORIG_SPLITS = (512, 128, 128, 512, 512, 512, 512, 512, 512, 512, 1024, 1024, 1024)
_ORIG_OFFS = tuple(int(v) for v in np.cumsum((0,) + _ORIG_SPLITS)[:-1])
_PERM = (0, 3, 4, 5, 6, 7, 8, 9, 1, 2)
_GATE_SPLITS = (10, 11, 12)
IN_COLS = sum(_ORIG_SPLITS[k] for k in _PERM)
GATE_COLS = sum(_ORIG_SPLITS[k] for k in _GATE_SPLITS)
C_AQ, C_AG, C_BQ, C_BK, C_BV, C_BG, C_CX, C_CG, C_AK, C_AV = (0, 4, 8, 12, 16, 20, 24, 28, 32, 33)

_NT = (((1,), (1,)), ((), ()))


def _cparams(sem):
    return pltpu.CompilerParams(dimension_semantics=sem, vmem_limit_bytes=VMEM_LIMIT)


def _sigmoid(x):
    return 0.5 + 0.5 * jnp.tanh(0.5 * x)


def _silu(x):
    return x * _sigmoid(x)


def _head_mean_matrix(width):
    idx = np.arange(width) // HEAD_DIM
    return jnp.asarray((idx[:, None] == idx[None, :]).astype(np.float32) / HEAD_DIM, dtype=BF16)


def _heads_rms(x, bd, g):
    x2 = x * x
    hi = x2.astype(BF16)
    lo = (x2 - hi.astype(F32)).astype(BF16)
    width = x.shape[-1]
    step = min(width, MXU_DIM)
    tile = bd[:step, :step]
    ms = jnp.concatenate(
        [jnp.dot(hi[:, c:c + step], tile, preferred_element_type=F32)
         + jnp.dot(lo[:, c:c + step], tile, preferred_element_type=F32) for c in range(0, width, step)], axis=-1)
    return x * lax.rsqrt(ms + EPS) * g


def _head_lane_mask(width, h):
    lane = lax.broadcasted_iota(jnp.int32, (1, width), 1)
    return (lane >= h * HEAD_DIM) & (lane < (h + 1) * HEAD_DIM)


def _mods_kernel(c_ref, w_ref, b_ref, o_ref):
    c = c_ref[...]
    s = _silu(c).astype(BF16)
    o_ref[0] = jnp.dot(s, w_ref[0].astype(BF16), preferred_element_type=F32) + b_ref[0]


def _mods(cond, w_ada, b_ada):
    n = cond.shape[0]
    tn = D_MODEL
    return pl.pallas_call(
        _mods_kernel,
        grid=(DEPTH, 3 * D_MODEL // tn),
        in_specs=[
            pl.BlockSpec((n, D_MODEL), lambda l, j: (0, 0)),
            pl.BlockSpec((1, D_MODEL, tn), lambda l, j: (l, 0, j)),
            pl.BlockSpec((1, 1, tn), lambda l, j: (l, 0, j)),
        ],
        out_specs=pl.BlockSpec((1, n, tn), lambda l, j: (l, 0, j)),
        out_shape=jax.ShapeDtypeStruct((DEPTH, n, 3 * D_MODEL), F32),
        compiler_params=_cparams(("arbitrary", "arbitrary")),
        name="mods",
    )(cond, w_ada, b_ada.reshape(DEPTH, 1, 3 * D_MODEL))


W_BLK = 256
_MIXER_BLKS = IN_COLS // W_BLK
_GATE_BLKS = GATE_COLS // W_BLK
W_MIXER_OFF = IN_COLS
W_PREP_COLS = 2 * IN_COLS


_GAP_BLKS = W_MIXER_OFF // W_BLK - _GATE_BLKS


def _wprep_kernel(w_ref, o_ref):
    n = pl.program_id(0)
    in_gap = (n >= _GATE_BLKS) & (n < _GATE_BLKS + _GAP_BLKS)
    o_ref[...] = jnp.where(in_gap, 0.0, w_ref[...]).astype(BF16)


def _wprep_src(n):
    m = n - _GATE_BLKS - _GAP_BLKS
    mixer = jnp.where(m < 0, 0, jnp.where(m < 2, m, jnp.where(m < _MIXER_BLKS - 1, m + 1, 2)))
    return jnp.where(n < _GATE_BLKS, n + _MIXER_BLKS, mixer)


def _wprep(w_in):
    assert [_ORIG_OFFS[k] // W_BLK for k in _PERM[:-2]] == [0, 3, 5, 7, 9, 11, 13, 15] and _ORIG_OFFS[1] == 2 * W_BLK
    return pl.pallas_call(
        _wprep_kernel,
        grid=(W_PREP_COLS // W_BLK,),
        in_specs=[pl.BlockSpec((DEPTH, D_MODEL, W_BLK), lambda n: (0, 0, _wprep_src(n)))],
        out_specs=pl.BlockSpec((DEPTH, D_MODEL, W_BLK), lambda n: (0, 0, n)),
        out_shape=jax.ShapeDtypeStruct((DEPTH, D_MODEL, W_PREP_COLS), BF16),
        compiler_params=_cparams(("arbitrary",)),
        name="wprep",
    )(w_in)


def _modulated_norm(x, g_ref, mod_ref):
    y = x * lax.rsqrt(jnp.mean(x * x, axis=-1, keepdims=True) + EPS)
    y = y * g_ref[0]
    shift = mod_ref[0, 0, 0:1, :]
    scale = mod_ref[0, 0, 1:2, :]
    return (y * (1.0 + scale) + shift).astype(BF16)


def _inproj_kernel(x_ref, mod_ref, g_ref, w_ref, o_ref):
    h = _modulated_norm(x_ref[...], g_ref, mod_ref)
    o_ref[...] = jnp.dot(h, w_ref[0], preferred_element_type=F32)


def _inproj(x2d, mods, norm_g, w_perm, layer, rows_per_mod, mod_row0):
    tokens = x2d.shape[0]
    tm = 512
    tiles_per_mod = rows_per_mod // tm if rows_per_mod else 0

    def mod_idx(i):
        if rows_per_mod:
            return (layer, mod_row0 + i // tiles_per_mod, 0, 0)
        return (layer, mod_row0, 0, 0)

    return pl.pallas_call(
        _inproj_kernel,
        grid=(tokens // tm,),
        in_specs=[
            pl.BlockSpec((tm, D_MODEL), lambda i: (i, 0)),
            pl.BlockSpec((1, 1, 3, D_MODEL), mod_idx),
            pl.BlockSpec((1, 1, D_MODEL), lambda i: (layer, 0, 0)),
            pl.BlockSpec((1, D_MODEL, IN_COLS), lambda i: (layer, 0, W_MIXER_OFF // IN_COLS)),
        ],
        out_specs=pl.BlockSpec((tm, IN_COLS), lambda i: (i, 0)),
        out_shape=jax.ShapeDtypeStruct((tokens, IN_COLS), F32),
        compiler_params=_cparams(("arbitrary",)),
        name="inproj",
    )(x2d, mods, norm_g.reshape(DEPTH, 1, D_MODEL), w_perm)


CTX_SEQS = 2
CTX_GROUP = MXU_DIM // HEAD_DIM


def _ctx_mixer(q, k, v, g, gq, gk, bdq, bdk, ones_stack, expand=None, sink_ref=None):
    s_len = q.shape[0]
    n_q = q.shape[1] // HEAD_DIM
    kn = _heads_rms(k, bdk, gk)
    qn = (_heads_rms(q, bdq, gq) * QK_SCALE).astype(BF16)
    knb = kn.astype(BF16)
    vb = v.astype(BF16)
    if expand is not None:
        knb = jnp.dot(knb, expand, preferred_element_type=F32).astype(BF16)
        vb = jnp.dot(vb, expand, preferred_element_type=F32).astype(BF16)
    gw = CTX_GROUP * HEAD_DIM
    masks = [_head_lane_mask(gw, hi) for hi in range(CTX_GROUP)]
    groups = []
    for gp in range(n_q // CTX_GROUP):
        cols = slice(gp * gw, (gp + 1) * gw)
        qh, kh, vh = qn[:, cols], knb[:, cols], vb[:, cols]
        kstack = jnp.concatenate([jnp.where(mk, kh, jnp.zeros_like(kh)) for mk in masks], axis=0)
        vstack = jnp.concatenate([jnp.where(mk, vh, jnp.zeros_like(vh)) for mk in masks], axis=0)
        s = lax.dot_general(qh, kstack, _NT, preferred_element_type=F32)
        ps, sink_terms = [], []
        for hi in range(CTX_GROUP):
            si = s[:, hi * s_len:(hi + 1) * s_len]
            m = jnp.max(si, axis=-1, keepdims=True)
            if sink_ref is not None:
                h = gp * CTX_GROUP + hi
                snk = sink_ref[0:1, h:h + 1]
                m = jnp.maximum(m, snk)
                sink_terms.append(jnp.where(masks[hi], jnp.exp(snk - m), 0.0))
            ps.append(jnp.exp(si - m).astype(BF16))
        p = jnp.concatenate(ps, axis=1)
        rhs = jnp.concatenate([vstack, ones_stack], axis=1)
        oe = jnp.dot(p, rhs, preferred_element_type=F32)
        l = oe[:, gw:]
        for term in sink_terms:
            l = l + term
        groups.append(oe[:, :gw] / l)
    return jnp.concatenate(groups, axis=-1) * _silu(g), kn


_CTX_INPUTS = 17


def _ctx_attn_kernel(*refs, s_len, n_alias):
    (aq_ref, ak_ref, av_ref, ag_ref, bq_ref, bk_ref, bv_ref, bg_ref,
     gqa_ref, gka_ref, sink_ref, gqb_ref, gkb_ref, bdw_ref, bdn_ref, exp_ref, ones_ref) = refs[:_CTX_INPUTS]
    ya_ref, yb_ref, ka_ref, va_ref, kb_ref, vb_ref = refs[_CTX_INPUTS + n_alias:]
    ones_stack = ones_ref[...]
    bdw = bdw_ref[...]
    for n in range(aq_ref.shape[0] // s_len):
        rows = slice(n * s_len, (n + 1) * s_len)
        va = av_ref[rows, :]
        vb = bv_ref[rows, :]
        va_ref[n, 0] = va.T
        vb_ref[n, 0] = vb.T
        ya, kna = _ctx_mixer(aq_ref[rows, :], ak_ref[rows, :], va, ag_ref[rows, :], gqa_ref[...], gka_ref[...],
                             bdw, bdn_ref[...], ones_stack, expand=exp_ref[...], sink_ref=sink_ref)
        yb, knb = _ctx_mixer(bq_ref[rows, :], bk_ref[rows, :], vb, bg_ref[rows, :], gqb_ref[...], gkb_ref[...],
                             bdw, bdw, ones_stack)
        ka_ref[n, 0] = kna.T
        kb_ref[n, 0] = knb.T
        ya_ref[rows, :] = ya.astype(ya_ref.dtype)
        yb_ref[rows, :] = yb.astype(yb_ref.dtype)


def _ctx_attn(proj, bsz, s_len, a_gq, a_gk, a_sink, b_gq, b_gk, layer, caches):
    per_layer = lambda width: pl.BlockSpec((None, 1, width), lambda b: (layer, 0, 0))
    kvw = A_KV_HEADS * HEAD_DIM
    grp = A_HEADS // A_KV_HEADS
    expand = np.zeros((kvw, BRANCH_W), np.float32)
    for h in range(A_HEADS):
        for d in range(HEAD_DIM):
            expand[(h // grp) * HEAD_DIM + d, h * HEAD_DIM + d] = 1.0
    ones_stack = np.zeros((CTX_GROUP * s_len, CTX_GROUP * HEAD_DIM), np.float32)
    for hi in range(CTX_GROUP):
        ones_stack[hi * s_len:(hi + 1) * s_len, hi * HEAD_DIM:(hi + 1) * HEAD_DIM] = 1.0
    const = lambda b: (0, 0)
    rows = CTX_SEQS * s_len
    wide = lambda c: pl.BlockSpec((rows, BRANCH_W), lambda b: (b, c // 4))
    narrow = lambda c: pl.BlockSpec((rows, kvw), lambda b: (b, c))
    prev = () if caches is None else tuple(caches)
    n_alias = len(prev)
    cache_spec = lambda width: pl.BlockSpec((CTX_SEQS, 1, width, s_len), lambda b: (b, layer, 0, 0))
    cache_shape = lambda width: jax.ShapeDtypeStruct((bsz, DEPTH, width, s_len), F32)
    outs = pl.pallas_call(
        functools.partial(_ctx_attn_kernel, s_len=s_len, n_alias=n_alias),
        grid=(bsz // CTX_SEQS,),
        in_specs=[
            wide(C_AQ), narrow(C_AK), narrow(C_AV), wide(C_AG), wide(C_BQ), wide(C_BK), wide(C_BV), wide(C_BG),
            per_layer(BRANCH_W), per_layer(kvw), per_layer(A_HEADS), per_layer(BRANCH_W), per_layer(BRANCH_W),
            pl.BlockSpec((BRANCH_W, BRANCH_W), const),
            pl.BlockSpec((kvw, kvw), const),
            pl.BlockSpec((kvw, BRANCH_W), const),
            pl.BlockSpec((CTX_GROUP * s_len, CTX_GROUP * HEAD_DIM), const),
        ] + [pl.BlockSpec(memory_space=pl.ANY)] * n_alias,
        out_specs=[
            pl.BlockSpec((rows, BRANCH_W), lambda b: (b, 0)),
            pl.BlockSpec((rows, BRANCH_W), lambda b: (b, 0)),
            cache_spec(kvw), cache_spec(kvw), cache_spec(BRANCH_W), cache_spec(BRANCH_W),
        ],
        out_shape=[
            jax.ShapeDtypeStruct((bsz * s_len, BRANCH_W), BF16),
            jax.ShapeDtypeStruct((bsz * s_len, BRANCH_W), BF16),
            cache_shape(kvw), cache_shape(kvw), cache_shape(BRANCH_W), cache_shape(BRANCH_W),
        ],
        input_output_aliases={_CTX_INPUTS + i: 2 + i for i in range(n_alias)},
        compiler_params=_cparams(("arbitrary",)),
        name="ctx_attn",
    )(proj, proj, proj, proj, proj, proj, proj, proj,
      a_gq, a_gk, a_sink, b_gq, b_gk, _head_mean_matrix(BRANCH_W), _head_mean_matrix(kvw), jnp.asarray(expand, dtype=BF16),
      jnp.asarray(ones_stack, dtype=BF16), *prev)
    return outs[0], outs[1], tuple(outs[2:])


def _rope(x, cos, sin_signed):
    w = x.shape[-1]
    lane = lax.broadcasted_iota(jnp.int32, x.shape, 1)
    up = pltpu.roll(x, w - 16, axis=1)
    dn = pltpu.roll(x, 16, axis=1)
    partner = jnp.where((lane & 16) == 0, up, dn)
    return x * cos + partner * sin_signed


WIN_SUB = 8


def _win_attn_kernel(q_ref, k_ref, v_ref, g_ref, kc_ref, vc_ref, cos_ref, sin_ref, gq_ref, gk_ref, sink_ref,
                     bdq_ref, bdk_ref, dup_ref, mask_ref, y_ref, kpad, vpad, kcx, vcx):
    j = pl.program_id(1)
    nb = pl.num_programs(1)
    t_len = k_ref.shape[0]
    grp = A_HEADS // A_KV_HEADS
    kvw = A_KV_HEADS * HEAD_DIM
    xw = 2 * kvw

    @pl.when(j == 0)
    def _():
        dup = dup_ref[...]
        kn = _heads_rms(k_ref[...], bdk_ref[...], gk_ref[...])
        kn = _rope(kn, cos_ref[:, 0:kvw], sin_ref[:, 0:kvw]).astype(BF16)
        def with_ones(vx):
            ones = jnp.ones((vx.shape[0], LANE), BF16)
            return jnp.concatenate(
                [part for kv in range(A_KV_HEADS) for part in (vx[:, kv * LANE:(kv + 1) * LANE], ones)], axis=1)

        kpad[0:A_BLOCK, :] = jnp.zeros((A_BLOCK, xw), BF16)
        kpad[A_BLOCK + t_len:2 * A_BLOCK + t_len, :] = jnp.zeros((A_BLOCK, xw), BF16)
        vpad[0:A_BLOCK, :] = jnp.zeros((A_BLOCK, 2 * xw), BF16)
        vpad[A_BLOCK + t_len:2 * A_BLOCK + t_len, :] = jnp.zeros((A_BLOCK, 2 * xw), BF16)
        kpad[A_BLOCK:A_BLOCK + t_len, :] = jnp.dot(kn, dup, preferred_element_type=F32).astype(BF16)
        vpad[A_BLOCK:A_BLOCK + t_len, :] = with_ones(
            jnp.dot(v_ref[...].astype(BF16), dup, preferred_element_type=F32).astype(BF16))
        kcx[...] = jnp.dot(kc_ref[0, 0].T.astype(BF16), dup, preferred_element_type=F32).astype(BF16)
        vcx[...] = with_ones(jnp.dot(vc_ref[0, 0].T.astype(BF16), dup, preferred_element_type=F32).astype(BF16))

    nloc = 3 * A_BLOCK
    low_half = lax.broadcasted_iota(jnp.int32, (1, LANE), 1) < HEAD_DIM
    for sub in range(WIN_SUB):
        jj = j * WIN_SUB + sub
        qrows = slice(sub * A_BLOCK, (sub + 1) * A_BLOCK)
        r0 = pl.multiple_of(jj * A_BLOCK, A_BLOCK)
        qn = _heads_rms(q_ref[qrows, :], bdq_ref[...], gq_ref[...])
        qb = (_rope(qn, cos_ref[pl.ds(r0, A_BLOCK), :], sin_ref[pl.ds(r0, A_BLOCK), :]) * QK_SCALE).astype(BF16)
        maskadd = mask_ref[jnp.where(jj == 0, 0, jnp.where(jj == nb * WIN_SUB - 1, 2, 1))]
        kband = kpad[pl.ds(r0, nloc), :]
        vband = vpad[pl.ds(r0, nloc), :]
        pairs = []
        for kv in range(A_KV_HEADS):
            cols = slice(kv * LANE, (kv + 1) * LANE)
            qparts, sinks = [], []
            for gi in range(grp):
                h = kv * grp + gi
                qpair = qb[:, (h // 2) * LANE:(h // 2 + 1) * LANE]
                keep = low_half if h % 2 == 0 else jnp.logical_not(low_half)
                qparts.append(jnp.where(keep, qpair, jnp.zeros_like(qpair)))
                sinks.append(jnp.broadcast_to(sink_ref[0:1, h:h + 1], (A_BLOCK, 1)))
            qst = jnp.concatenate(qparts, axis=0)
            snk = jnp.concatenate(sinks, axis=0)
            s_loc = lax.dot_general(qst, kband[:, cols], _NT, preferred_element_type=F32) + maskadd
            s_ctx = lax.dot_general(qst, kcx[:, cols], _NT, preferred_element_type=F32)
            m = jnp.maximum(jnp.maximum(jnp.max(s_loc, axis=-1, keepdims=True),
                                        jnp.max(s_ctx, axis=-1, keepdims=True)), snk)
            p_loc = jnp.exp(s_loc - m).astype(BF16)
            p_ctx = jnp.exp(s_ctx - m).astype(BF16)
            wide = slice(kv * MXU_DIM, (kv + 1) * MXU_DIM)
            oe = (jnp.dot(p_loc, vband[:, wide], preferred_element_type=F32)
                  + jnp.dot(p_ctx, vcx[:, wide], preferred_element_type=F32))
            o = oe[:, :LANE] / (oe[:, LANE:] + jnp.exp(snk - m))
            for k2 in range(grp // 2):
                even = o[(2 * k2) * A_BLOCK:(2 * k2 + 1) * A_BLOCK]
                odd = o[(2 * k2 + 1) * A_BLOCK:(2 * k2 + 2) * A_BLOCK]
                pairs.append(jnp.where(low_half, even, odd))
        y = jnp.concatenate(pairs, axis=-1) * _silu(g_ref[qrows, :])
        y_ref[qrows, :] = y.astype(y_ref.dtype)


def _win_mask(grp, nb):
    assert nb >= 2
    r = np.arange(A_BLOCK)[:, None]
    c = np.arange(3 * A_BLOCK)[None, :]
    band = np.abs(r + A_BLOCK - c) <= A_WINDOW
    variants = [band & (c >= A_BLOCK), band, band & (c < 2 * A_BLOCK)]
    return np.stack([np.tile(np.where(v, 0.0, NEG_INF).astype(np.float32), (grp, 1)) for v in variants])


def _win_attn(proj, cache_k, cache_v, layer, cos, sin_signed, gq, gk, sink, bsz, t_len):
    nb = t_len // A_BLOCK
    assert nb % WIN_SUB == 0
    nsteps = nb // WIN_SUB
    kvw = A_KV_HEADS * HEAD_DIM
    grp = A_HEADS // A_KV_HEADS
    p_len = cache_k.shape[3]
    dup = np.zeros((kvw, 2 * kvw), np.float32)
    for kv in range(A_KV_HEADS):
        for half in range(2):
            for d in range(HEAD_DIM):
                dup[kv * HEAD_DIM + d, kv * LANE + half * HEAD_DIM + d] = 1.0
    const2 = lambda b, j: (0, 0)
    return pl.pallas_call(
        _win_attn_kernel,
        grid=(bsz, nsteps),
        in_specs=[
            pl.BlockSpec((WIN_SUB * A_BLOCK, BRANCH_W), lambda b, j: (b * nsteps + j, C_AQ // 4)),
            pl.BlockSpec((t_len, kvw), lambda b, j: (b, C_AK)),
            pl.BlockSpec((t_len, kvw), lambda b, j: (b, C_AV)),
            pl.BlockSpec((WIN_SUB * A_BLOCK, BRANCH_W), lambda b, j: (b * nsteps + j, C_AG // 4)),
            pl.BlockSpec((1, 1, kvw, p_len), lambda b, j: (b, layer, 0, 0)),
            pl.BlockSpec((1, 1, kvw, p_len), lambda b, j: (b, layer, 0, 0)),
            pl.BlockSpec((t_len, BRANCH_W), const2),
            pl.BlockSpec((t_len, BRANCH_W), const2),
            pl.BlockSpec((None, 1, BRANCH_W), lambda b, j: (layer, 0, 0)),
            pl.BlockSpec((None, 1, kvw), lambda b, j: (layer, 0, 0)),
            pl.BlockSpec((None, 1, A_HEADS), lambda b, j: (layer, 0, 0)),
            pl.BlockSpec((BRANCH_W, BRANCH_W), const2),
            pl.BlockSpec((kvw, kvw), const2),
            pl.BlockSpec((kvw, 2 * kvw), const2),
            pl.BlockSpec((3, grp * A_BLOCK, 3 * A_BLOCK), lambda b, j: (0, 0, 0)),
        ],
        out_specs=pl.BlockSpec((WIN_SUB * A_BLOCK, BRANCH_W), lambda b, j: (b * nsteps + j, 0)),
        out_shape=jax.ShapeDtypeStruct((bsz * t_len, BRANCH_W), BF16),
        scratch_shapes=[pltpu.VMEM((t_len + 2 * A_BLOCK, 2 * kvw), BF16),
                        pltpu.VMEM((t_len + 2 * A_BLOCK, 4 * kvw), BF16),
                        pltpu.VMEM((p_len, 2 * kvw), BF16),
                        pltpu.VMEM((p_len, 4 * kvw), BF16)],
        compiler_params=_cparams(("arbitrary", "arbitrary")),
        name="win_attn",
    )(proj, proj, proj, proj, cache_k, cache_v, cos, sin_signed,
      gq, gk, sink, _head_mean_matrix(BRANCH_W), _head_mean_matrix(kvw),
      jnp.asarray(dup, dtype=BF16), jnp.asarray(_win_mask(grp, nb)))


NBR_QB = 2 * GRID_W
NBR_BAND = 5


def _nbr_plan(rows):
    kh = min(NB_ROWS, rows)
    nblk = rows // 2
    specs, plan, starts = {}, [], []
    for i in range(nblk):
        s0 = min(max(i - 2, 0), nblk - NBR_BAND)
        starts.append(s0)
        blk = []
        for a in range(2):
            qr = 2 * i + a
            rs = min(max(qr - kh // 2, 0), rows - kh)
            assert 2 * s0 <= rs and rs + kh <= 2 * (s0 + NBR_BAND)
            row = []
            for p in range(NBR_BAND):
                pair = tuple(kr - qr + NB_ROWS - 1 if rs <= kr < rs + kh else None
                             for kr in (2 * (s0 + p), 2 * (s0 + p) + 1))
                row.append(specs.setdefault(pair, len(specs)))
            blk.append(row)
        plan.append(blk)
    return tuple(starts), plan, list(specs)


def _nbr_table(rpb, specs):
    heads = rpb.shape[0]
    c = np.arange(GRID_W)
    cs = np.clip(c - NB_COLS // 2, 0, GRID_W - NB_COLS)
    col_ok = (c[None, :] >= cs[:, None]) & (c[None, :] < cs[:, None] + NB_COLS)
    dc = c[None, :] - c[:, None] + NB_COLS - 1
    onehot = ((dc[None] == np.arange(2 * NB_COLS - 1)[:, None, None]) & col_ok[None]).astype(np.float32)
    shifted = jnp.einsum('hdj,jqk->hdqk', rpb.astype(F32), jnp.asarray(onehot), precision=lax.Precision.HIGHEST)
    by_col = jnp.where(col_ok[None, None], shifted, NEG_INF)
    neg = jnp.full((heads, GRID_W, GRID_W), NEG_INF, F32)
    blocks = [jnp.concatenate([neg if d is None else by_col[:, d] for d in spec], axis=-1) for spec in specs]
    return jnp.stack(blocks, axis=1)


def _nbr_attn_kernel(q_ref, k_ref, v_ref, g_ref, kc_ref, vc_ref, tb_ref, gq_ref, gk_ref, bd_ref, y_ref,
                     *, starts, plan):
    nband = NBR_BAND * NBR_QB
    bd = bd_ref[...]
    qn = (_heads_rms(q_ref[...], bd, gq_ref[...]) * QK_SCALE).astype(BF16)
    kn = _heads_rms(k_ref[...], bd, gk_ref[...]).astype(BF16)
    vb = v_ref[...].astype(BF16)
    kcb = kc_ref[0, 0].T.astype(BF16)
    vcb = vc_ref[0, 0].T.astype(BF16)
    acc = [jnp.zeros((NBR_QB, LANE), F32) for _ in starts]
    for h in range(LANE // HEAD_DIM):
        hm = _head_lane_mask(LANE, h)
        km = jnp.where(hm, kn, jnp.zeros_like(kn))
        kcm = jnp.where(hm, kcb, jnp.zeros_like(kcb))
        vm = jnp.concatenate([jnp.where(hm, vb, jnp.zeros_like(vb)), jnp.ones_like(vb)], axis=1)
        vcm = jnp.concatenate([jnp.where(hm, vcb, jnp.zeros_like(vcb)), jnp.ones_like(vcb)], axis=1)
        for i, s0 in enumerate(starts):
            qi = qn[i * NBR_QB:(i + 1) * NBR_QB]
            ks = slice(s0 * NBR_QB, s0 * NBR_QB + nband)
            s_raw = lax.dot_general(qi, km[ks], _NT, preferred_element_type=F32)
            s_loc = jnp.concatenate(
                [jnp.concatenate([s_raw[a * GRID_W:(a + 1) * GRID_W, p * LANE:(p + 1) * LANE]
                                  + tb_ref[h, plan[i][a][p]] for p in range(NBR_BAND)], axis=1)
                 for a in range(2)], axis=0)
            s_ctx = lax.dot_general(qi, kcm, _NT, preferred_element_type=F32)
            m = jnp.maximum(jnp.max(s_loc, axis=-1, keepdims=True), jnp.max(s_ctx, axis=-1, keepdims=True))
            p_loc = jnp.exp(s_loc - m).astype(BF16)
            p_ctx = jnp.exp(s_ctx - m).astype(BF16)
            oe = (jnp.dot(p_loc, vm[ks], preferred_element_type=F32)
                  + jnp.dot(p_ctx, vcm, preferred_element_type=F32))
            acc[i] = acc[i] + oe[:, :LANE] / oe[:, LANE:]
    y = jnp.concatenate(acc, axis=0) * _silu(g_ref[...])
    y_ref[...] = y.astype(y_ref.dtype)


def _nbr_attn(proj, cache_k, cache_v, layer, table, starts, plan, gq, gk, bsz, t_len):
    hp = LANE // HEAD_DIM
    nhp = B_HEADS // hp
    p_len = cache_k.shape[3]
    const = lambda h, b: (0, 0)
    kern = functools.partial(_nbr_attn_kernel, starts=starts, plan=plan)
    return pl.pallas_call(
        kern,
        grid=(nhp, bsz),
        in_specs=[
            pl.BlockSpec((t_len, LANE), lambda h, b: (b, C_BQ + h)),
            pl.BlockSpec((t_len, LANE), lambda h, b: (b, C_BK + h)),
            pl.BlockSpec((t_len, LANE), lambda h, b: (b, C_BV + h)),
            pl.BlockSpec((t_len, LANE), lambda h, b: (b, C_BG + h)),
            pl.BlockSpec((1, 1, LANE, p_len), lambda h, b: (b, layer, h, 0)),
            pl.BlockSpec((1, 1, LANE, p_len), lambda h, b: (b, layer, h, 0)),
            pl.BlockSpec((hp,) + table.shape[1:], lambda h, b: (layer * nhp + h, 0, 0, 0)),
            pl.BlockSpec((None, 1, LANE), lambda h, b: (layer, 0, 0)),
            pl.BlockSpec((None, 1, LANE), lambda h, b: (layer, 0, 0)),
            pl.BlockSpec((LANE, LANE), const),
        ],
        out_specs=pl.BlockSpec((t_len, LANE), lambda h, b: (b, h)),
        out_shape=jax.ShapeDtypeStruct((bsz * t_len, BRANCH_W), BF16),
        compiler_params=_cparams(("arbitrary", "arbitrary")),
        name="nbr_attn",
    )(proj, proj, proj, proj, cache_k, cache_v, table, gq, gk, _head_mean_matrix(LANE))


LRU_CHUNKS = SUBLANE
LRU_PITCH_PAD = 4


def _lru_kernel(cx_ref, cg_ref, h0_ref, cw_ref, cb_ref, wg_ref, bg_ref, lam_ref,
                y_ref, st_ref, a_s, u_s, h_s, p_s, y_s):
    t_len = cx_ref.shape[0]
    w = LRU_WIDTH
    cx = cx_ref[...]
    row = lax.broadcasted_iota(jnp.int32, (t_len, w), 0)
    xc = cb_ref[...] + cx * cw_ref[2:3, :]
    xc = xc + jnp.where(row >= 2, pltpu.roll(cx, 2, axis=0), 0.0) * cw_ref[0:1, :]
    xc = xc + jnp.where(row >= 1, pltpu.roll(cx, 1, axis=0), 0.0) * cw_ref[1:2, :]
    xc = xc + jnp.where(row < t_len - 1, pltpu.roll(cx, t_len - 1, axis=0), 0.0) * cw_ref[3:4, :]

    gates = jnp.dot(xc.astype(BF16), wg_ref[...], preferred_element_type=F32) + bg_ref[...]
    coeffs = []
    for d in range(2):
        th_r = jnp.tanh(gates[:, d * w:(d + 1) * w])
        th_i = jnp.tanh(gates[:, (2 + d) * w:(3 + d) * w])
        nl = -lam_ref[d:d + 1, :]
        softplus = jnp.maximum(nl, 0.0) + jnp.log1p(jnp.exp(-jnp.abs(nl)))
        quarter_c = (-0.25 * LRU_C) * softplus
        half_log_a = quarter_c * th_r + quarter_c
        t = jnp.tanh(half_log_a)
        rc = 1.0 / (1.0 - t)
        coeffs.append(((1.0 + t) * rc,
                       jnp.sqrt(-t) * rc * (1.0 + th_i) * xc))

    chunk_len = t_len // LRU_CHUNKS
    pitch = chunk_len + LRU_PITCH_PAD
    nt = w // LANE
    for d, (a_val, u_val) in enumerate(coeffs):
        for c in range(LRU_CHUNKS):
            for k in range(nt):
                dst = slice(c * pitch, c * pitch + chunk_len)
                src = (slice(c * chunk_len, (c + 1) * chunk_len), slice(k * LANE, (k + 1) * LANE))
                a_s[d, k, dst, :] = a_val[src]
                u_s[d, k, dst, :] = u_val[src]

    def scan_body(s, carry):
        hs, ps = carry
        new_h, new_p = [], []
        for d in range(2):
            pos = s if d == 0 else chunk_len - 1 - s
            idx = pl.ds(pos, LRU_CHUNKS, stride=pitch)
            for k in range(nt):
                av = a_s[d, k, idx, :]
                h = av * hs[d * nt + k] + u_s[d, k, idx, :]
                p = av * ps[d * nt + k]
                h_s[d, k, idx, :] = h
                p_s[d, k, idx, :] = p
                new_h.append(h)
                new_p.append(p)
        return tuple(new_h), tuple(new_p)

    zero = jnp.zeros((LRU_CHUNKS, LANE), F32)
    one = jnp.ones((LRU_CHUNKS, LANE), F32)
    h_end, p_end = lax.fori_loop(0, chunk_len, scan_body, ((zero,) * (2 * nt), (one,) * (2 * nt)), unroll=4)

    h0 = h0_ref[0]
    enter, finals = [], []
    for d in range(2):
        order = range(LRU_CHUNKS) if d == 0 else range(LRU_CHUNKS - 1, -1, -1)
        for k in range(nt):
            he, pe = h_end[d * nt + k], p_end[d * nt + k]
            e = h0[d:d + 1, k * LANE:(k + 1) * LANE]
            rows = [None] * LRU_CHUNKS
            for c in order:
                rows[c] = e
                e = pe[c:c + 1, :] * e + he[c:c + 1, :]
            enter.append(jnp.concatenate(rows, axis=0))
            finals.append(e)
    st_ref[0] = jnp.concatenate([jnp.concatenate(finals[d * nt:(d + 1) * nt], axis=1) for d in range(2)], axis=0)

    def fix_body(s, carry):
        idx = pl.ds(s, LRU_CHUNKS, stride=pitch)
        for k in range(nt):
            hf = h_s[0, k, idx, :] + p_s[0, k, idx, :] * enter[k]
            hb = h_s[1, k, idx, :] + p_s[1, k, idx, :] * enter[nt + k]
            y_s[k, idx, :] = hf + hb
        return carry

    lax.fori_loop(0, chunk_len, fix_body, 0, unroll=4)
    hsum = jnp.concatenate(
        [jnp.concatenate([y_s[k, c * pitch:c * pitch + chunk_len, :] for c in range(LRU_CHUNKS)], axis=0)
         for k in range(nt)], axis=1)
    y_ref[...] = (hsum * _silu(cg_ref[...])).astype(y_ref.dtype)


def _lru(proj, h0, h0_layer, conv_w, conv_b, w_gates, b_gates, lam, layer, bsz, t_len):
    w = LRU_WIDTH
    assert t_len % (LRU_CHUNKS * SUBLANE) == 0
    rows_pad = LRU_CHUNKS * (t_len // LRU_CHUNKS + LRU_PITCH_PAD)
    return pl.pallas_call(
        _lru_kernel,
        grid=(bsz,),
        in_specs=[
            pl.BlockSpec((t_len, w), lambda b: (b, C_CX // 4)),
            pl.BlockSpec((t_len, w), lambda b: (b, C_CG // 4)),
            pl.BlockSpec((1, None, 2, w), lambda b: (b, h0_layer, 0, 0)),
            pl.BlockSpec((None, CONV_W, w), lambda b: (layer, 0, 0)),
            pl.BlockSpec((None, 1, w), lambda b: (layer, 0, 0)),
            pl.BlockSpec((None, w, 4 * w), lambda b: (layer, 0, 0)),
            pl.BlockSpec((None, 1, 4 * w), lambda b: (layer, 0, 0)),
            pl.BlockSpec((None, 2, w), lambda b: (layer, 0, 0)),
        ],
        out_specs=[
            pl.BlockSpec((t_len, w), lambda b: (b, 0)),
            pl.BlockSpec((1, 2, w), lambda b: (b, 0, 0)),
        ],
        out_shape=[
            jax.ShapeDtypeStruct((bsz * t_len, w), BF16),
            jax.ShapeDtypeStruct((bsz, 2, w), F32),
        ],
        scratch_shapes=[pltpu.VMEM((2, w // LANE, rows_pad, LANE), F32) for _ in range(4)]
        + [pltpu.VMEM((w // LANE, rows_pad, LANE), F32)],
        compiler_params=_cparams(("arbitrary",)),
        name="lru",
    )(proj, proj, h0, conv_w, conv_b, w_gates, b_gates, lam)


def _lru_gate_weights(wa, ba, wx, bx):
    def dense(wblk):
        eye = jnp.eye(LRU_BLOCKS, dtype=wblk.dtype)
        full = jnp.einsum('ldnkj,nm->lnkdmj', wblk, eye)
        return full.reshape(DEPTH, LRU_WIDTH, 2 * LRU_WIDTH)
    wg = jnp.concatenate([dense(wa), dense(wx)], axis=2)
    bg = jnp.concatenate([ba.reshape(DEPTH, 1, 2 * LRU_WIDTH), bx.reshape(DEPTH, 1, 2 * LRU_WIDTH)], axis=2)
    return (0.5 * wg).astype(BF16), 0.5 * bg


def _merge_kernel(x_ref, ya_ref, yb_ref, yc_ref, mod_ref, g_ref, wgate_ref, wbr_ref, wout_ref, o_ref):
    x = x_ref[...]
    h = _modulated_norm(x, g_ref, mod_ref)
    gates = jnp.dot(h, wgate_ref[0], preferred_element_type=F32)
    z = None
    for k, y_ref in enumerate((ya_ref, yb_ref, yc_ref)):
        term = (_sigmoid(gates[:, k * D_MODEL:(k + 1) * D_MODEL])
                * jnp.dot(y_ref[...], wbr_ref[0, k], preferred_element_type=F32))
        z = term if z is None else z + term
    out = jnp.dot(z.astype(BF16), wout_ref[0], preferred_element_type=F32)
    o_ref[...] = x + mod_ref[0, 0, 2:3, :] * out


def _merge(x2d, ya, yb, yc, mods, norm_g, w_gate, w_br, w_out, layer, rows_per_mod, mod_row0):
    tokens = x2d.shape[0]
    tm = 1024
    assert rows_per_mod % tm == 0
    tiles_per_mod = rows_per_mod // tm if rows_per_mod else 0

    def mod_idx(i):
        if rows_per_mod:
            return (layer, mod_row0 + i // tiles_per_mod, 0, 0)
        return (layer, mod_row0, 0, 0)

    return pl.pallas_call(
        _merge_kernel,
        grid=(tokens // tm,),
        in_specs=[
            pl.BlockSpec((tm, D_MODEL), lambda i: (i, 0)),
            pl.BlockSpec((tm, BRANCH_W), lambda i: (i, 0)),
            pl.BlockSpec((tm, BRANCH_W), lambda i: (i, 0)),
            pl.BlockSpec((tm, BRANCH_W), lambda i: (i, 0)),
            pl.BlockSpec((1, 1, 3, D_MODEL), mod_idx),
            pl.BlockSpec((1, 1, D_MODEL), lambda i: (layer, 0, 0)),
            pl.BlockSpec((1, D_MODEL, GATE_COLS), lambda i: (layer, 0, 0)),
            pl.BlockSpec((1, 3, BRANCH_W, D_MODEL), lambda i: (layer, 0, 0, 0)),
            pl.BlockSpec((1, D_MODEL, D_MODEL), lambda i: (layer, 0, 0)),
        ],
        out_specs=pl.BlockSpec((tm, D_MODEL), lambda i: (i, 0)),
        out_shape=jax.ShapeDtypeStruct((tokens, D_MODEL), F32),
        compiler_params=_cparams(("arbitrary",)),
        name="merge",
    )(x2d, ya, yb, yc, mods, norm_g.reshape(DEPTH, 1, D_MODEL), w_gate, w_br, w_out)


def _rope_tables(t_len):
    t = jnp.arange(t_len)
    m = HEAD_DIM // 4
    freqs = ROPE_BASE ** (-jnp.arange(m, dtype=F32) / m)
    ang_r = (t // GRID_W).astype(F32)[:, None] * freqs[None, :]
    ang_c = (t % GRID_W).astype(F32)[:, None] * freqs[None, :]
    cos = jnp.concatenate([jnp.cos(ang_r), jnp.cos(ang_r), jnp.cos(ang_c), jnp.cos(ang_c)], axis=-1)
    sin = jnp.concatenate([-jnp.sin(ang_r), jnp.sin(ang_r), -jnp.sin(ang_c), jnp.sin(ang_c)], axis=-1)
    reps = BRANCH_W // HEAD_DIM
    return jnp.tile(cos, (1, reps)), jnp.tile(sin, (1, reps))


def kernel(x_prompt, x_sample, cache_ka, cache_va, cache_kb, cache_vb, state_lru, c, c_ctx,
           norm_g, w_ada, b_ada, w_in, a_q_norm, a_k_norm, a_sink, b_q_norm, b_k_norm, b_rpb,
           lru_conv_w, lru_conv_b, lru_wa, lru_ba, lru_wx, lru_bx, lru_lambda, w_branch, w_out):
    bsz, s_len, _ = x_prompt.shape
    dbsz, t_len, _ = x_sample.shape
    p_len = cache_ka.shape[2]

    n_mod = 16
    ctx_row = dbsz
    cond = jnp.zeros((n_mod, D_MODEL), F32).at[:dbsz].set(c).at[ctx_row].set(c_ctx)
    mods = _mods(cond, w_ada, b_ada).reshape(DEPTH, n_mod, 3, D_MODEL)

    w_perm = w_gate = _wprep(w_in)
    w_br = w_branch.astype(BF16)
    w_o = w_out.astype(BF16)
    cos, sin_signed = _rope_tables(t_len)

    def cache_t(x):
        return x.transpose(0, 1, 3, 4, 2).reshape(x.shape[0], DEPTH, x.shape[3] * HEAD_DIM, p_len)
    cka, cva, ckb, cvb = cache_t(cache_ka), cache_t(cache_va), cache_t(cache_kb), cache_t(cache_vb)

    def gain_rows(g, reps):
        return jnp.tile(g, (1, reps)).reshape(DEPTH, 1, reps * HEAD_DIM)
    a_gq, a_gk = gain_rows(a_q_norm, A_HEADS), gain_rows(a_k_norm, A_KV_HEADS)
    b_gq, b_gk = gain_rows(b_q_norm, B_HEADS), gain_rows(b_k_norm, B_HEADS)
    b_gq2, b_gk2 = gain_rows(b_q_norm, LANE // HEAD_DIM), gain_rows(b_k_norm, LANE // HEAD_DIM)
    sinks = a_sink.reshape(DEPTH, 1, A_HEADS)
    wg, bg = _lru_gate_weights(lru_wa, lru_ba, lru_wx, lru_bx)
    conv_b = lru_conv_b.reshape(DEPTH, 1, LRU_WIDTH)
    grid_rows = t_len // GRID_W
    assert grid_rows % 2 == 0 and grid_rows // 2 >= NBR_BAND
    nbr_starts, nbr_plan, nbr_specs = _nbr_plan(grid_rows)
    nbr_table = _nbr_table(b_rpb.reshape((DEPTH * B_HEADS,) + b_rpb.shape[2:]), nbr_specs)

    yp = x_prompt.reshape(bsz * s_len, D_MODEL)
    ys = x_sample.reshape(dbsz * t_len, D_MODEL)
    zero_state = jnp.zeros((bsz, 1, 2, LRU_WIDTH), F32)
    new_caches, new_lru = None, []
    for l in range(DEPTH):
        proj = _inproj(yp, mods, norm_g, w_perm, l, 0, ctx_row)
        ya, yb, new_caches = _ctx_attn(proj, bsz, s_len, a_gq, a_gk, sinks, b_gq, b_gk, l, new_caches)
        yc, st = _lru(proj, zero_state, 0, lru_conv_w, conv_b, wg, bg, lru_lambda, l, bsz, s_len)
        yp = _merge(yp, ya, yb, yc, mods, norm_g, w_gate, w_br, w_o, l, 0, ctx_row)
        new_lru.append(st)

        proj = _inproj(ys, mods, norm_g, w_perm, l, t_len, 0)
        ya = _win_attn(proj, cka, cva, l, cos, sin_signed, a_gq, a_gk, sinks, dbsz, t_len)
        yb = _nbr_attn(proj, ckb, cvb, l, nbr_table, nbr_starts, nbr_plan, b_gq2, b_gk2, dbsz, t_len)
        yc, _ = _lru(proj, state_lru, l, lru_conv_w, conv_b, wg, bg, lru_lambda, l, dbsz, t_len)
        ys = _merge(ys, ya, yb, yc, mods, norm_g, w_gate, w_br, w_o, l, t_len, 0)

    def cache_out(x):
        return x.reshape(bsz, DEPTH, x.shape[2] // HEAD_DIM, HEAD_DIM, s_len).transpose(0, 1, 4, 2, 3)

    return (yp.reshape(bsz, s_len, D_MODEL), ys.reshape(dbsz, t_len, D_MODEL),
            *(cache_out(x) for x in new_caches), jnp.stack(new_lru, axis=1))
```

```python
import functools

import numpy as np
import jax
import jax.numpy as jnp
from jax import lax
from jax.experimental import pallas as pl
from jax.experimental.pallas import tpu as pltpu

F32 = jnp.float32
BF16 = jnp.bfloat16

D_MODEL = 1024
DEPTH = 2
GRID_W = 64
HEAD_DIM = 64
BRANCH_W = 512
A_HEADS = 8
A_KV_HEADS = 2
A_WINDOW = 128
A_BLOCK = 128
B_HEADS = 8
NB_ROWS = 8
NB_COLS = 16
LRU_WIDTH = 512
LRU_BLOCKS = 8
LRU_BW = LRU_WIDTH // LRU_BLOCKS
LRU_C = 8.0
CONV_W = 4
ROPE_BASE = 10000.0
EPS = 1e-6
NEG_INF = -1e30
QK_SCALE = HEAD_DIM ** -0.5

LANE = 128
SUBLANE = 8
MXU_DIM = 256
VMEM_LIMIT = 56 * 1024 * 1024

_ORIG_SPLITS = (512, 128, 128, 512, 512, 512, 512, 512, 512, 512, 1024, 1024, 1024)
_ORIG_OFFS = tuple(int(v) for v in np.cumsum((0,) + _ORIG_SPLITS)[:-1])
_PERM = (0, 3, 4, 5, 6, 7, 8, 9, 1, 2)
_GATE_SPLITS = (10, 11, 12)
IN_COLS = sum(_ORIG_SPLITS[k] for k in _PERM)
GATE_COLS = sum(_ORIG_SPLITS[k] for k in _GATE_SPLITS)
C_AQ, C_AG, C_BQ, C_BK, C_BV, C_BG, C_CX, C_CG, C_AK, C_AV = (0, 4, 8, 12, 16, 20, 24, 28, 32, 33)

_NT = (((1,), (1,)), ((), ()))


def _cparams(sem):
    return pltpu.CompilerParams(dimension_semantics=sem, vmem_limit_bytes=VMEM_LIMIT)


def _sigmoid(x):
    return 0.5 + 0.5 * jnp.tanh(0.5 * x)


def _silu(x):
    return x * _sigmoid(x)


def _head_mean_matrix(width):
    idx = np.arange(width) // HEAD_DIM
    return jnp.asarray((idx[:, None] == idx[None, :]).astype(np.float32) / HEAD_DIM, dtype=BF16)


def _heads_rms(x, bd, g):
    x2 = x * x
    hi = x2.astype(BF16)
    lo = (x2 - hi.astype(F32)).astype(BF16)
    width = x.shape[-1]
    step = min(width, MXU_DIM)
    tile = bd[:step, :step]
    ms = jnp.concatenate(
        [jnp.dot(hi[:, c:c + step], tile, preferred_element_type=F32)
         + jnp.dot(lo[:, c:c + step], tile, preferred_element_type=F32) for c in range(0, width, step)], axis=-1)
    return x * lax.rsqrt(ms + EPS) * g


def _head_lane_mask(width, h):
    lane = lax.broadcasted_iota(jnp.int32, (1, width), 1)
    return (lane >= h * HEAD_DIM) & (lane < (h + 1) * HEAD_DIM)


def _mods_kernel(c_ref, w_ref, b_ref, o_ref):
    c = c_ref[...]
    s = _silu(c).astype(BF16)
    o_ref[0] = jnp.dot(s, w_ref[0].astype(BF16), preferred_element_type=F32) + b_ref[0]


def _mods(cond, w_ada, b_ada):
    n = cond.shape[0]
    tn = D_MODEL
    return pl.pallas_call(
        _mods_kernel,
        grid=(DEPTH, 3 * D_MODEL // tn),
        in_specs=[
            pl.BlockSpec((n, D_MODEL), lambda l, j: (0, 0)),
            pl.BlockSpec((1, D_MODEL, tn), lambda l, j: (l, 0, j)),
            pl.BlockSpec((1, 1, tn), lambda l, j: (l, 0, j)),
        ],
        out_specs=pl.BlockSpec((1, n, tn), lambda l, j: (l, 0, j)),
        out_shape=jax.ShapeDtypeStruct((DEPTH, n, 3 * D_MODEL), F32),
        compiler_params=_cparams(("arbitrary", "arbitrary")),
        name="mods",
    )(cond, w_ada, b_ada.reshape(DEPTH, 1, 3 * D_MODEL))


W_BLK = 256
_MIXER_BLKS = IN_COLS // W_BLK
_GATE_BLKS = GATE_COLS // W_BLK
W_MIXER_OFF = IN_COLS
W_PREP_COLS = 2 * IN_COLS


_GAP_BLKS = W_MIXER_OFF // W_BLK - _GATE_BLKS


def _wprep_kernel(w_ref, o_ref):
    n = pl.program_id(0)
    in_gap = (n >= _GATE_BLKS) & (n < _GATE_BLKS + _GAP_BLKS)
    o_ref[...] = jnp.where(in_gap, 0.0, w_ref[...]).astype(BF16)


def _wprep_src(n):
    m = n - _GATE_BLKS - _GAP_BLKS
    mixer = jnp.where(m < 0, 0, jnp.where(m < 2, m, jnp.where(m < _MIXER_BLKS - 1, m + 1, 2)))
    return jnp.where(n < _GATE_BLKS, n + _MIXER_BLKS, mixer)


def _wprep(w_in):
    assert [_ORIG_OFFS[k] // W_BLK for k in _PERM[:-2]] == [0, 3, 5, 7, 9, 11, 13, 15] and _ORIG_OFFS[1] == 2 * W_BLK
    return pl.pallas_call(
        _wprep_kernel,
        grid=(W_PREP_COLS // W_BLK,),
        in_specs=[pl.BlockSpec((DEPTH, D_MODEL, W_BLK), lambda n: (0, 0, _wprep_src(n)))],
        out_specs=pl.BlockSpec((DEPTH, D_MODEL, W_BLK), lambda n: (0, 0, n)),
        out_shape=jax.ShapeDtypeStruct((DEPTH, D_MODEL, W_PREP_COLS), BF16),
        compiler_params=_cparams(("arbitrary",)),
        name="wprep",
    )(w_in)


def _modulated_norm(x, g_ref, mod_ref):
    y = x * lax.rsqrt(jnp.mean(x * x, axis=-1, keepdims=True) + EPS)
    y = y * g_ref[0]
    shift = mod_ref[0, 0, 0:1, :]
    scale = mod_ref[0, 0, 1:2, :]
    return (y * (1.0 + scale) + shift).astype(BF16)


def _inproj_kernel(x_ref, mod_ref, g_ref, w_ref, o_ref):
    h = _modulated_norm(x_ref[...], g_ref, mod_ref)
    o_ref[...] = jnp.dot(h, w_ref[0], preferred_element_type=F32)


def _inproj(x2d, mods, norm_g, w_perm, layer, rows_per_mod, mod_row0):
    tokens = x2d.shape[0]
    tm = 512
    tiles_per_mod = rows_per_mod // tm if rows_per_mod else 0

    def mod_idx(i):
        if rows_per_mod:
            return (layer, mod_row0 + i // tiles_per_mod, 0, 0)
        return (layer, mod_row0, 0, 0)

    return pl.pallas_call(
        _inproj_kernel,
        grid=(tokens // tm,),
        in_specs=[
            pl.BlockSpec((tm, D_MODEL), lambda i: (i, 0)),
            pl.BlockSpec((1, 1, 3, D_MODEL), mod_idx),
            pl.BlockSpec((1, 1, D_MODEL), lambda i: (layer, 0, 0)),
            pl.BlockSpec((1, D_MODEL, IN_COLS), lambda i: (layer, 0, W_MIXER_OFF // IN_COLS)),
        ],
        out_specs=pl.BlockSpec((tm, IN_COLS), lambda i: (i, 0)),
        out_shape=jax.ShapeDtypeStruct((tokens, IN_COLS), F32),
        compiler_params=_cparams(("arbitrary",)),
        name="inproj",
    )(x2d, mods, norm_g.reshape(DEPTH, 1, D_MODEL), w_perm)


CTX_SEQS = 2
CTX_GROUP = MXU_DIM // HEAD_DIM


def _ctx_mixer(q, k, v, g, gq, gk, bdq, bdk, ones_stack, expand=None, sink_ref=None):
    s_len = q.shape[0]
    n_q = q.shape[1] // HEAD_DIM
    kn = _heads_rms(k, bdk, gk)
    qn = (_heads_rms(q, bdq, gq) * QK_SCALE).astype(BF16)
    knb = kn.astype(BF16)
    vb = v.astype(BF16)
    if expand is not None:
        knb = jnp.dot(knb, expand, preferred_element_type=F32).astype(BF16)
        vb = jnp.dot(vb, expand, preferred_element_type=F32).astype(BF16)
    gw = CTX_GROUP * HEAD_DIM
    masks = [_head_lane_mask(gw, hi) for hi in range(CTX_GROUP)]
    groups = []
    for gp in range(n_q // CTX_GROUP):
        cols = slice(gp * gw, (gp + 1) * gw)
        qh, kh, vh = qn[:, cols], knb[:, cols], vb[:, cols]
        kstack = jnp.concatenate([jnp.where(mk, kh, jnp.zeros_like(kh)) for mk in masks], axis=0)
        vstack = jnp.concatenate([jnp.where(mk, vh, jnp.zeros_like(vh)) for mk in masks], axis=0)
        s = lax.dot_general(qh, kstack, _NT, preferred_element_type=F32)
        ps, sink_terms = [], []
        for hi in range(CTX_GROUP):
            si = s[:, hi * s_len:(hi + 1) * s_len]
            m = jnp.max(si, axis=-1, keepdims=True)
            if sink_ref is not None:
                h = gp * CTX_GROUP + hi
                snk = sink_ref[0:1, h:h + 1]
                m = jnp.maximum(m, snk)
                sink_terms.append(jnp.where(masks[hi], jnp.exp(snk - m), 0.0))
            ps.append(jnp.exp(si - m).astype(BF16))
        p = jnp.concatenate(ps, axis=1)
        rhs = jnp.concatenate([vstack, ones_stack], axis=1)
        oe = jnp.dot(p, rhs, preferred_element_type=F32)
        l = oe[:, gw:]
        for term in sink_terms:
            l = l + term
        groups.append(oe[:, :gw] / l)
    return jnp.concatenate(groups, axis=-1) * _silu(g), kn


_CTX_INPUTS = 17


def _ctx_attn_kernel(*refs, s_len, n_alias):
    (aq_ref, ak_ref, av_ref, ag_ref, bq_ref, bk_ref, bv_ref, bg_ref,
     gqa_ref, gka_ref, sink_ref, gqb_ref, gkb_ref, bdw_ref, bdn_ref, exp_ref, ones_ref) = refs[:_CTX_INPUTS]
    ya_ref, yb_ref, ka_ref, va_ref, kb_ref, vb_ref = refs[_CTX_INPUTS + n_alias:]
    ones_stack = ones_ref[...]
    bdw = bdw_ref[...]
    for n in range(aq_ref.shape[0] // s_len):
        rows = slice(n * s_len, (n + 1) * s_len)
        va = av_ref[rows, :]
        vb = bv_ref[rows, :]
        va_ref[n, 0] = va.T
        vb_ref[n, 0] = vb.T
        ya, kna = _ctx_mixer(aq_ref[rows, :], ak_ref[rows, :], va, ag_ref[rows, :], gqa_ref[...], gka_ref[...],
                             bdw, bdn_ref[...], ones_stack, expand=exp_ref[...], sink_ref=sink_ref)
        yb, knb = _ctx_mixer(bq_ref[rows, :], bk_ref[rows, :], vb, bg_ref[rows, :], gqb_ref[...], gkb_ref[...],
                             bdw, bdw, ones_stack)
        ka_ref[n, 0] = kna.T
        kb_ref[n, 0] = knb.T
        ya_ref[rows, :] = ya.astype(ya_ref.dtype)
        yb_ref[rows, :] = yb.astype(yb_ref.dtype)


def _ctx_attn(proj, bsz, s_len, a_gq, a_gk, a_sink, b_gq, b_gk, layer, caches):
    per_layer = lambda width: pl.BlockSpec((None, 1, width), lambda b: (layer, 0, 0))
    kvw = A_KV_HEADS * HEAD_DIM
    grp = A_HEADS // A_KV_HEADS
    expand = np.zeros((kvw, BRANCH_W), np.float32)
    for h in range(A_HEADS):
        for d in range(HEAD_DIM):
            expand[(h // grp) * HEAD_DIM + d, h * HEAD_DIM + d] = 1.0
    ones_stack = np.zeros((CTX_GROUP * s_len, CTX_GROUP * HEAD_DIM), np.float32)
    for hi in range(CTX_GROUP):
        ones_stack[hi * s_len:(hi + 1) * s_len, hi * HEAD_DIM:(hi + 1) * HEAD_DIM] = 1.0
    const = lambda b: (0, 0)
    rows = CTX_SEQS * s_len
    wide = lambda c: pl.BlockSpec((rows, BRANCH_W), lambda b: (b, c // 4))
    narrow = lambda c: pl.BlockSpec((rows, kvw), lambda b: (b, c))
    prev = () if caches is None else tuple(caches)
    n_alias = len(prev)
    cache_spec = lambda width: pl.BlockSpec((CTX_SEQS, 1, width, s_len), lambda b: (b, layer, 0, 0))
    cache_shape = lambda width: jax.ShapeDtypeStruct((bsz, DEPTH, width, s_len), F32)
    outs = pl.pallas_call(
        functools.partial(_ctx_attn_kernel, s_len=s_len, n_alias=n_alias),
        grid=(bsz // CTX_SEQS,),
        in_specs=[
            wide(C_AQ), narrow(C_AK), narrow(C_AV), wide(C_AG), wide(C_BQ), wide(C_BK), wide(C_BV), wide(C_BG),
            per_layer(BRANCH_W), per_layer(kvw), per_layer(A_HEADS), per_layer(BRANCH_W), per_layer(BRANCH_W),
            pl.BlockSpec((BRANCH_W, BRANCH_W), const),
            pl.BlockSpec((kvw, kvw), const),
            pl.BlockSpec((kvw, BRANCH_W), const),
            pl.BlockSpec((CTX_GROUP * s_len, CTX_GROUP * HEAD_DIM), const),
        ] + [pl.BlockSpec(memory_space=pl.ANY)] * n_alias,
        out_specs=[
            pl.BlockSpec((rows, BRANCH_W), lambda b: (b, 0)),
            pl.BlockSpec((rows, BRANCH_W), lambda b: (b, 0)),
            cache_spec(kvw), cache_spec(kvw), cache_spec(BRANCH_W), cache_spec(BRANCH_W),
        ],
        out_shape=[
            jax.ShapeDtypeStruct((bsz * s_len, BRANCH_W), BF16),
            jax.ShapeDtypeStruct((bsz * s_len, BRANCH_W), BF16),
            cache_shape(kvw), cache_shape(kvw), cache_shape(BRANCH_W), cache_shape(BRANCH_W),
        ],
        input_output_aliases={_CTX_INPUTS + i: 2 + i for i in range(n_alias)},
        compiler_params=_cparams(("arbitrary",)),
        name="ctx_attn",
    )(proj, proj, proj, proj, proj, proj, proj, proj,
      a_gq, a_gk, a_sink, b_gq, b_gk, _head_mean_matrix(BRANCH_W), _head_mean_matrix(kvw), jnp.asarray(expand, dtype=BF16),
      jnp.asarray(ones_stack, dtype=BF16), *prev)
    return outs[0], outs[1], tuple(outs[2:])


def _rope(x, cos, sin_signed):
    w = x.shape[-1]
    lane = lax.broadcasted_iota(jnp.int32, x.shape, 1)
    up = pltpu.roll(x, w - 16, axis=1)
    dn = pltpu.roll(x, 16, axis=1)
    partner = jnp.where((lane & 16) == 0, up, dn)
    return x * cos + partner * sin_signed


WIN_SUB = 8


def _win_attn_kernel(q_ref, k_ref, v_ref, g_ref, kc_ref, vc_ref, cos_ref, sin_ref, gq_ref, gk_ref, sink_ref,
                     bdq_ref, bdk_ref, dup_ref, mask_ref, y_ref, kpad, vpad, kcx, vcx):
    j = pl.program_id(1)
    nb = pl.num_programs(1)
    t_len = k_ref.shape[0]
    grp = A_HEADS // A_KV_HEADS
    kvw = A_KV_HEADS * HEAD_DIM
    xw = 2 * kvw

    @pl.when(j == 0)
    def _():
        dup = dup_ref[...]
        kn = _heads_rms(k_ref[...], bdk_ref[...], gk_ref[...])
        kn = _rope(kn, cos_ref[:, 0:kvw], sin_ref[:, 0:kvw]).astype(BF16)
        def with_ones(vx):
            ones = jnp.ones((vx.shape[0], LANE), BF16)
            return jnp.concatenate(
                [part for kv in range(A_KV_HEADS) for part in (vx[:, kv * LANE:(kv + 1) * LANE], ones)], axis=1)

        kpad[0:A_BLOCK, :] = jnp.zeros((A_BLOCK, xw), BF16)
        kpad[A_BLOCK + t_len:2 * A_BLOCK + t_len, :] = jnp.zeros((A_BLOCK, xw), BF16)
        vpad[0:A_BLOCK, :] = jnp.zeros((A_BLOCK, 2 * xw), BF16)
        vpad[A_BLOCK + t_len:2 * A_BLOCK + t_len, :] = jnp.zeros((A_BLOCK, 2 * xw), BF16)
        kpad[A_BLOCK:A_BLOCK + t_len, :] = jnp.dot(kn, dup, preferred_element_type=F32).astype(BF16)
        vpad[A_BLOCK:A_BLOCK + t_len, :] = with_ones(
            jnp.dot(v_ref[...].astype(BF16), dup, preferred_element_type=F32).astype(BF16))
        kcx[...] = jnp.dot(kc_ref[0, 0].T.astype(BF16), dup, preferred_element_type=F32).astype(BF16)
        vcx[...] = with_ones(jnp.dot(vc_ref[0, 0].T.astype(BF16), dup, preferred_element_type=F32).astype(BF16))

    nloc = 3 * A_BLOCK
    low_half = lax.broadcasted_iota(jnp.int32, (1, LANE), 1) < HEAD_DIM
    for sub in range(WIN_SUB):
        jj = j * WIN_SUB + sub
        qrows = slice(sub * A_BLOCK, (sub + 1) * A_BLOCK)
        r0 = pl.multiple_of(jj * A_BLOCK, A_BLOCK)
        qn = _heads_rms(q_ref[qrows, :], bdq_ref[...], gq_ref[...])
        qb = (_rope(qn, cos_ref[pl.ds(r0, A_BLOCK), :], sin_ref[pl.ds(r0, A_BLOCK), :]) * QK_SCALE).astype(BF16)
        maskadd = mask_ref[jnp.where(jj == 0, 0, jnp.where(jj == nb * WIN_SUB - 1, 2, 1))]
        kband = kpad[pl.ds(r0, nloc), :]
        vband = vpad[pl.ds(r0, nloc), :]
        pairs = []
        for kv in range(A_KV_HEADS):
            cols = slice(kv * LANE, (kv + 1) * LANE)
            qparts, sinks = [], []
            for gi in range(grp):
                h = kv * grp + gi
                qpair = qb[:, (h // 2) * LANE:(h // 2 + 1) * LANE]
                keep = low_half if h % 2 == 0 else jnp.logical_not(low_half)
                qparts.append(jnp.where(keep, qpair, jnp.zeros_like(qpair)))
                sinks.append(jnp.broadcast_to(sink_ref[0:1, h:h + 1], (A_BLOCK, 1)))
            qst = jnp.concatenate(qparts, axis=0)
            snk = jnp.concatenate(sinks, axis=0)
            s_loc = lax.dot_general(qst, kband[:, cols], _NT, preferred_element_type=F32) + maskadd
            s_ctx = lax.dot_general(qst, kcx[:, cols], _NT, preferred_element_type=F32)
            m = jnp.maximum(jnp.maximum(jnp.max(s_loc, axis=-1, keepdims=True),
                                        jnp.max(s_ctx, axis=-1, keepdims=True)), snk)
            p_loc = jnp.exp(s_loc - m).astype(BF16)
            p_ctx = jnp.exp(s_ctx - m).astype(BF16)
            wide = slice(kv * MXU_DIM, (kv + 1) * MXU_DIM)
            oe = (jnp.dot(p_loc, vband[:, wide], preferred_element_type=F32)
                  + jnp.dot(p_ctx, vcx[:, wide], preferred_element_type=F32))
            o = oe[:, :LANE] / (oe[:, LANE:] + jnp.exp(snk - m))
            for k2 in range(grp // 2):
                even = o[(2 * k2) * A_BLOCK:(2 * k2 + 1) * A_BLOCK]
                odd = o[(2 * k2 + 1) * A_BLOCK:(2 * k2 + 2) * A_BLOCK]
                pairs.append(jnp.where(low_half, even, odd))
        y = jnp.concatenate(pairs, axis=-1) * _silu(g_ref[qrows, :])
        y_ref[qrows, :] = y.astype(y_ref.dtype)


def _win_mask(grp, nb):
    assert nb >= 2
    r = np.arange(A_BLOCK)[:, None]
    c = np.arange(3 * A_BLOCK)[None, :]
    band = np.abs(r + A_BLOCK - c) <= A_WINDOW
    variants = [band & (c >= A_BLOCK), band, band & (c < 2 * A_BLOCK)]
    return np.stack([np.tile(np.where(v, 0.0, NEG_INF).astype(np.float32), (grp, 1)) for v in variants])


def _win_attn(proj, cache_k, cache_v, layer, cos, sin_signed, gq, gk, sink, bsz, t_len):
    nb = t_len // A_BLOCK
    assert nb % WIN_SUB == 0
    nsteps = nb // WIN_SUB
    kvw = A_KV_HEADS * HEAD_DIM
    grp = A_HEADS // A_KV_HEADS
    p_len = cache_k.shape[3]
    dup = np.zeros((kvw, 2 * kvw), np.float32)
    for kv in range(A_KV_HEADS):
        for half in range(2):
            for d in range(HEAD_DIM):
                dup[kv * HEAD_DIM + d, kv * LANE + half * HEAD_DIM + d] = 1.0
    const2 = lambda b, j: (0, 0)
    return pl.pallas_call(
        _win_attn_kernel,
        grid=(bsz, nsteps),
        in_specs=[
            pl.BlockSpec((WIN_SUB * A_BLOCK, BRANCH_W), lambda b, j: (b * nsteps + j, C_AQ // 4)),
            pl.BlockSpec((t_len, kvw), lambda b, j: (b, C_AK)),
            pl.BlockSpec((t_len, kvw), lambda b, j: (b, C_AV)),
            pl.BlockSpec((WIN_SUB * A_BLOCK, BRANCH_W), lambda b, j: (b * nsteps + j, C_AG // 4)),
            pl.BlockSpec((1, 1, kvw, p_len), lambda b, j: (b, layer, 0, 0)),
            pl.BlockSpec((1, 1, kvw, p_len), lambda b, j: (b, layer, 0, 0)),
            pl.BlockSpec((t_len, BRANCH_W), const2),
            pl.BlockSpec((t_len, BRANCH_W), const2),
            pl.BlockSpec((None, 1, BRANCH_W), lambda b, j: (layer, 0, 0)),
            pl.BlockSpec((None, 1, kvw), lambda b, j: (layer, 0, 0)),
            pl.BlockSpec((None, 1, A_HEADS), lambda b, j: (layer, 0, 0)),
            pl.BlockSpec((BRANCH_W, BRANCH_W), const2),
            pl.BlockSpec((kvw, kvw), const2),
            pl.BlockSpec((kvw, 2 * kvw), const2),
            pl.BlockSpec((3, grp * A_BLOCK, 3 * A_BLOCK), lambda b, j: (0, 0, 0)),
        ],
        out_specs=pl.BlockSpec((WIN_SUB * A_BLOCK, BRANCH_W), lambda b, j: (b * nsteps + j, 0)),
        out_shape=jax.ShapeDtypeStruct((bsz * t_len, BRANCH_W), BF16),
        scratch_shapes=[pltpu.VMEM((t_len + 2 * A_BLOCK, 2 * kvw), BF16),
                        pltpu.VMEM((t_len + 2 * A_BLOCK, 4 * kvw), BF16),
                        pltpu.VMEM((p_len, 2 * kvw), BF16),
                        pltpu.VMEM((p_len, 4 * kvw), BF16)],
        compiler_params=_cparams(("arbitrary", "arbitrary")),
        name="win_attn",
    )(proj, proj, proj, proj, cache_k, cache_v, cos, sin_signed,
      gq, gk, sink, _head_mean_matrix(BRANCH_W), _head_mean_matrix(kvw),
      jnp.asarray(dup, dtype=BF16), jnp.asarray(_win_mask(grp, nb)))


NBR_QB = 2 * GRID_W
NBR_BAND = 5


def _nbr_plan(rows):
    kh = min(NB_ROWS, rows)
    nblk = rows // 2
    specs, plan, starts = {}, [], []
    for i in range(nblk):
        s0 = min(max(i - 2, 0), nblk - NBR_BAND)
        starts.append(s0)
        blk = []
        for a in range(2):
            qr = 2 * i + a
            rs = min(max(qr - kh // 2, 0), rows - kh)
            assert 2 * s0 <= rs and rs + kh <= 2 * (s0 + NBR_BAND)
            row = []
            for p in range(NBR_BAND):
                pair = tuple(kr - qr + NB_ROWS - 1 if rs <= kr < rs + kh else None
                             for kr in (2 * (s0 + p), 2 * (s0 + p) + 1))
                row.append(specs.setdefault(pair, len(specs)))
            blk.append(row)
        plan.append(blk)
    return tuple(starts), plan, list(specs)


def _nbr_table(rpb, specs):
    heads = rpb.shape[0]
    c = np.arange(GRID_W)
    cs = np.clip(c - NB_COLS // 2, 0, GRID_W - NB_COLS)
    col_ok = (c[None, :] >= cs[:, None]) & (c[None, :] < cs[:, None] + NB_COLS)
    dc = c[None, :] - c[:, None] + NB_COLS - 1
    onehot = ((dc[None] == np.arange(2 * NB_COLS - 1)[:, None, None]) & col_ok[None]).astype(np.float32)
    shifted = jnp.einsum('hdj,jqk->hdqk', rpb.astype(F32), jnp.asarray(onehot), precision=lax.Precision.HIGHEST)
    by_col = jnp.where(col_ok[None, None], shifted, NEG_INF)
    neg = jnp.full((heads, GRID_W, GRID_W), NEG_INF, F32)
    blocks = [jnp.concatenate([neg if d is None else by_col[:, d] for d in spec], axis=-1) for spec in specs]
    return jnp.stack(blocks, axis=1)


NBR_GROUPS = 4


def _nbr_attn_kernel(q_ref, k_ref, v_ref, g_ref, kc_ref, vc_ref, tb_ref, gq_ref, gk_ref, bd_ref, y_ref,
                     *, starts, plan):
    nband = NBR_BAND * NBR_QB
    bd = bd_ref[...]
    hpg = LANE // HEAD_DIM
    for gs in range(NBR_GROUPS):
        lanes = slice(gs * LANE, (gs + 1) * LANE)
        qn = (_heads_rms(q_ref[:, lanes], bd, gq_ref[...]) * QK_SCALE).astype(BF16)
        kn = _heads_rms(k_ref[:, lanes], bd, gk_ref[...]).astype(BF16)
        vb = v_ref[:, lanes].astype(BF16)
        kcb = kc_ref[0, 0, lanes, :].T.astype(BF16)
        vcb = vc_ref[0, 0, lanes, :].T.astype(BF16)
        acc = [jnp.zeros((NBR_QB, LANE), F32) for _ in starts]
        for h in range(hpg):
            hm = _head_lane_mask(LANE, h)
            km = jnp.where(hm, kn, jnp.zeros_like(kn))
            kcm = jnp.where(hm, kcb, jnp.zeros_like(kcb))
            vm = jnp.concatenate([jnp.where(hm, vb, jnp.zeros_like(vb)), jnp.ones_like(vb)], axis=1)
            vcm = jnp.concatenate([jnp.where(hm, vcb, jnp.zeros_like(vcb)), jnp.ones_like(vcb)], axis=1)
            for i, s0 in enumerate(starts):
                qi = qn[i * NBR_QB:(i + 1) * NBR_QB]
                ks = slice(s0 * NBR_QB, s0 * NBR_QB + nband)
                s_raw = lax.dot_general(qi, km[ks], _NT, preferred_element_type=F32)
                s_loc = jnp.concatenate(
                    [jnp.concatenate([s_raw[a * GRID_W:(a + 1) * GRID_W, p * LANE:(p + 1) * LANE]
                                      + tb_ref[gs * hpg + h, plan[i][a][p]] for p in range(NBR_BAND)], axis=1)
                     for a in range(2)], axis=0)
                s_ctx = lax.dot_general(qi, kcm, _NT, preferred_element_type=F32)
                m = jnp.maximum(jnp.max(s_loc, axis=-1, keepdims=True), jnp.max(s_ctx, axis=-1, keepdims=True))
                p_loc = jnp.exp(s_loc - m).astype(BF16)
                p_ctx = jnp.exp(s_ctx - m).astype(BF16)
                oe = (jnp.dot(p_loc, vm[ks], preferred_element_type=F32)
                      + jnp.dot(p_ctx, vcm, preferred_element_type=F32))
                acc[i] = acc[i] + oe[:, :LANE] / oe[:, LANE:]
        y = jnp.concatenate(acc, axis=0) * _silu(g_ref[:, lanes])
        y_ref[:, lanes] = y.astype(y_ref.dtype)


def _nbr_attn(proj, cache_k, cache_v, layer, table, starts, plan, gq, gk, bsz, t_len):
    hp = NBR_GROUPS * (LANE // HEAD_DIM)
    gw = NBR_GROUPS * LANE
    nhp = B_HEADS // hp
    p_len = cache_k.shape[3]
    const = lambda h, b: (0, 0)
    kern = functools.partial(_nbr_attn_kernel, starts=starts, plan=plan)
    return pl.pallas_call(
        kern,
        grid=(nhp, bsz),
        in_specs=[
            pl.BlockSpec((t_len, gw), lambda h, b: (b, C_BQ // NBR_GROUPS + h)),
            pl.BlockSpec((t_len, gw), lambda h, b: (b, C_BK // NBR_GROUPS + h)),
            pl.BlockSpec((t_len, gw), lambda h, b: (b, C_BV // NBR_GROUPS + h)),
            pl.BlockSpec((t_len, gw), lambda h, b: (b, C_BG // NBR_GROUPS + h)),
            pl.BlockSpec((1, 1, gw, p_len), lambda h, b: (b, layer, h, 0)),
            pl.BlockSpec((1, 1, gw, p_len), lambda h, b: (b, layer, h, 0)),
            pl.BlockSpec((hp,) + table.shape[1:], lambda h, b: (layer * nhp + h, 0, 0, 0)),
            pl.BlockSpec((None, 1, LANE), lambda h, b: (layer, 0, 0)),
            pl.BlockSpec((None, 1, LANE), lambda h, b: (layer, 0, 0)),
            pl.BlockSpec((LANE, LANE), const),
        ],
        out_specs=pl.BlockSpec((t_len, gw), lambda h, b: (b, h)),
        out_shape=jax.ShapeDtypeStruct((bsz * t_len, BRANCH_W), BF16),
        compiler_params=_cparams(("arbitrary", "arbitrary")),
        name="nbr_attn",
    )(proj, proj, proj, proj, cache_k, cache_v, table, gq, gk, _head_mean_matrix(LANE))


LRU_CHUNKS = SUBLANE
LRU_PITCH_PAD = 4


def _lru_kernel(cx_ref, cg_ref, h0_ref, cw_ref, cb_ref, wg_ref, bg_ref, lam_ref,
                y_ref, st_ref, a_s, u_s, h_s, p_s, y_s):
    t_len = cx_ref.shape[0]
    w = LRU_WIDTH
    cx = cx_ref[...]
    row = lax.broadcasted_iota(jnp.int32, (t_len, w), 0)
    xc = cb_ref[...] + cx * cw_ref[2:3, :]
    xc = xc + jnp.where(row >= 2, pltpu.roll(cx, 2, axis=0), 0.0) * cw_ref[0:1, :]
    xc = xc + jnp.where(row >= 1, pltpu.roll(cx, 1, axis=0), 0.0) * cw_ref[1:2, :]
    xc = xc + jnp.where(row < t_len - 1, pltpu.roll(cx, t_len - 1, axis=0), 0.0) * cw_ref[3:4, :]

    gates = jnp.dot(xc.astype(BF16), wg_ref[...], preferred_element_type=F32) + bg_ref[...]
    coeffs = []
    for d in range(2):
        th_r = jnp.tanh(gates[:, d * w:(d + 1) * w])
        th_i = jnp.tanh(gates[:, (2 + d) * w:(3 + d) * w])
        nl = -lam_ref[d:d + 1, :]
        softplus = jnp.maximum(nl, 0.0) + jnp.log1p(jnp.exp(-jnp.abs(nl)))
        quarter_c = (-0.25 * LRU_C) * softplus
        half_log_a = quarter_c * th_r + quarter_c
        t = jnp.tanh(half_log_a)
        rc = 1.0 / (1.0 - t)
        coeffs.append(((1.0 + t) * rc,
                       jnp.sqrt(-t) * rc * (1.0 + th_i) * xc))

    chunk_len = t_len // LRU_CHUNKS
    pitch = chunk_len + LRU_PITCH_PAD
    nt = w // LANE
    for d, (a_val, u_val) in enumerate(coeffs):
        for c in range(LRU_CHUNKS):
            for k in range(nt):
                dst = slice(c * pitch, c * pitch + chunk_len)
                src = (slice(c * chunk_len, (c + 1) * chunk_len), slice(k * LANE, (k + 1) * LANE))
                a_s[d, k, dst, :] = a_val[src]
                u_s[d, k, dst, :] = u_val[src]

    def scan_body(s, carry):
        hs, ps = carry
        new_h, new_p = [], []
        for d in range(2):
            pos = s if d == 0 else chunk_len - 1 - s
            idx = pl.ds(pos, LRU_CHUNKS, stride=pitch)
            for k in range(nt):
                av = a_s[d, k, idx, :]
                h = av * hs[d * nt + k] + u_s[d, k, idx, :]
                p = av * ps[d * nt + k]
                h_s[d, k, idx, :] = h
                p_s[d, k, idx, :] = p
                new_h.append(h)
                new_p.append(p)
        return tuple(new_h), tuple(new_p)

    zero = jnp.zeros((LRU_CHUNKS, LANE), F32)
    one = jnp.ones((LRU_CHUNKS, LANE), F32)
    h_end, p_end = lax.fori_loop(0, chunk_len, scan_body, ((zero,) * (2 * nt), (one,) * (2 * nt)), unroll=4)

    h0 = h0_ref[0]
    enter, finals = [], []
    for d in range(2):
        order = range(LRU_CHUNKS) if d == 0 else range(LRU_CHUNKS - 1, -1, -1)
        for k in range(nt):
            he, pe = h_end[d * nt + k], p_end[d * nt + k]
            e = h0[d:d + 1, k * LANE:(k + 1) * LANE]
            rows = [None] * LRU_CHUNKS
            for c in order:
                rows[c] = e
                e = pe[c:c + 1, :] * e + he[c:c + 1, :]
            enter.append(jnp.concatenate(rows, axis=0))
            finals.append(e)
    st_ref[0] = jnp.concatenate([jnp.concatenate(finals[d * nt:(d + 1) * nt], axis=1) for d in range(2)], axis=0)

    def fix_body(s, carry):
        idx = pl.ds(s, LRU_CHUNKS, stride=pitch)
        for k in range(nt):
            hf = h_s[0, k, idx, :] + p_s[0, k, idx, :] * enter[k]
            hb = h_s[1, k, idx, :] + p_s[1, k, idx, :] * enter[nt + k]
            y_s[k, idx, :] = hf + hb
        return carry

    lax.fori_loop(0, chunk_len, fix_body, 0, unroll=4)
    hsum = jnp.concatenate(
        [jnp.concatenate([y_s[k, c * pitch:c * pitch + chunk_len, :] for c in range(LRU_CHUNKS)], axis=0)
         for k in range(nt)], axis=1)
    y_ref[...] = (hsum * _silu(cg_ref[...])).astype(y_ref.dtype)


def _lru(proj, h0, h0_layer, conv_w, conv_b, w_gates, b_gates, lam, layer, bsz, t_len):
    w = LRU_WIDTH
    assert t_len % (LRU_CHUNKS * SUBLANE) == 0
    rows_pad = LRU_CHUNKS * (t_len // LRU_CHUNKS + LRU_PITCH_PAD)
    return pl.pallas_call(
        _lru_kernel,
        grid=(bsz,),
        in_specs=[
            pl.BlockSpec((t_len, w), lambda b: (b, C_CX // 4)),
            pl.BlockSpec((t_len, w), lambda b: (b, C_CG // 4)),
            pl.BlockSpec((1, None, 2, w), lambda b: (b, h0_layer, 0, 0)),
            pl.BlockSpec((None, CONV_W, w), lambda b: (layer, 0, 0)),
            pl.BlockSpec((None, 1, w), lambda b: (layer, 0, 0)),
            pl.BlockSpec((None, w, 4 * w), lambda b: (layer, 0, 0)),
            pl.BlockSpec((None, 1, 4 * w), lambda b: (layer, 0, 0)),
            pl.BlockSpec((None, 2, w), lambda b: (layer, 0, 0)),
        ],
        out_specs=[
            pl.BlockSpec((t_len, w), lambda b: (b, 0)),
            pl.BlockSpec((1, 2, w), lambda b: (b, 0, 0)),
        ],
        out_shape=[
            jax.ShapeDtypeStruct((bsz * t_len, w), BF16),
            jax.ShapeDtypeStruct((bsz, 2, w), F32),
        ],
        scratch_shapes=[pltpu.VMEM((2, w // LANE, rows_pad, LANE), F32) for _ in range(4)]
        + [pltpu.VMEM((w // LANE, rows_pad, LANE), F32)],
        compiler_params=_cparams(("arbitrary",)),
        name="lru",
    )(proj, proj, h0, conv_w, conv_b, w_gates, b_gates, lam)


def _lru_gate_weights(wa, ba, wx, bx):
    def dense(wblk):
        eye = jnp.eye(LRU_BLOCKS, dtype=wblk.dtype)
        full = jnp.einsum('ldnkj,nm->lnkdmj', wblk, eye)
        return full.reshape(DEPTH, LRU_WIDTH, 2 * LRU_WIDTH)
    wg = jnp.concatenate([dense(wa), dense(wx)], axis=2)
    bg = jnp.concatenate([ba.reshape(DEPTH, 1, 2 * LRU_WIDTH), bx.reshape(DEPTH, 1, 2 * LRU_WIDTH)], axis=2)
    return (0.5 * wg).astype(BF16), 0.5 * bg


def _merge_kernel(x_ref, ya_ref, yb_ref, yc_ref, mod_ref, g_ref, wgate_ref, wbr_ref, wout_ref, o_ref):
    x = x_ref[...]
    h = _modulated_norm(x, g_ref, mod_ref)
    gates = jnp.dot(h, wgate_ref[0], preferred_element_type=F32)
    z = None
    for k, y_ref in enumerate((ya_ref, yb_ref, yc_ref)):
        term = (_sigmoid(gates[:, k * D_MODEL:(k + 1) * D_MODEL])
                * jnp.dot(y_ref[...], wbr_ref[0, k], preferred_element_type=F32))
        z = term if z is None else z + term
    out = jnp.dot(z.astype(BF16), wout_ref[0], preferred_element_type=F32)
    o_ref[...] = x + mod_ref[0, 0, 2:3, :] * out


def _merge(x2d, ya, yb, yc, mods, norm_g, w_gate, w_br, w_out, layer, rows_per_mod, mod_row0):
    tokens = x2d.shape[0]
    tm = 1024
    assert rows_per_mod % tm == 0
    tiles_per_mod = rows_per_mod // tm if rows_per_mod else 0

    def mod_idx(i):
        if rows_per_mod:
            return (layer, mod_row0 + i // tiles_per_mod, 0, 0)
        return (layer, mod_row0, 0, 0)

    return pl.pallas_call(
        _merge_kernel,
        grid=(tokens // tm,),
        in_specs=[
            pl.BlockSpec((tm, D_MODEL), lambda i: (i, 0)),
            pl.BlockSpec((tm, BRANCH_W), lambda i: (i, 0)),
            pl.BlockSpec((tm, BRANCH_W), lambda i: (i, 0)),
            pl.BlockSpec((tm, BRANCH_W), lambda i: (i, 0)),
            pl.BlockSpec((1, 1, 3, D_MODEL), mod_idx),
            pl.BlockSpec((1, 1, D_MODEL), lambda i: (layer, 0, 0)),
            pl.BlockSpec((1, D_MODEL, GATE_COLS), lambda i: (layer, 0, 0)),
            pl.BlockSpec((1, 3, BRANCH_W, D_MODEL), lambda i: (layer, 0, 0, 0)),
            pl.BlockSpec((1, D_MODEL, D_MODEL), lambda i: (layer, 0, 0)),
        ],
        out_specs=pl.BlockSpec((tm, D_MODEL), lambda i: (i, 0)),
        out_shape=jax.ShapeDtypeStruct((tokens, D_MODEL), F32),
        compiler_params=_cparams(("arbitrary",)),
        name="merge",
    )(x2d, ya, yb, yc, mods, norm_g.reshape(DEPTH, 1, D_MODEL), w_gate, w_br, w_out)


def _rope_tables(t_len):
    t = jnp.arange(t_len)
    m = HEAD_DIM // 4
    freqs = ROPE_BASE ** (-jnp.arange(m, dtype=F32) / m)
    ang_r = (t // GRID_W).astype(F32)[:, None] * freqs[None, :]
    ang_c = (t % GRID_W).astype(F32)[:, None] * freqs[None, :]
    cos = jnp.concatenate([jnp.cos(ang_r), jnp.cos(ang_r), jnp.cos(ang_c), jnp.cos(ang_c)], axis=-1)
    sin = jnp.concatenate([-jnp.sin(ang_r), jnp.sin(ang_r), -jnp.sin(ang_c), jnp.sin(ang_c)], axis=-1)
    reps = BRANCH_W // HEAD_DIM
    return jnp.tile(cos, (1, reps)), jnp.tile(sin, (1, reps))


def kernel(x_prompt, x_sample, cache_ka, cache_va, cache_kb, cache_vb, state_lru, c, c_ctx,
           norm_g, w_ada, b_ada, w_in, a_q_norm, a_k_norm, a_sink, b_q_norm, b_k_norm, b_rpb,
           lru_conv_w, lru_conv_b, lru_wa, lru_ba, lru_wx, lru_bx, lru_lambda, w_branch, w_out):
    bsz, s_len, _ = x_prompt.shape
    dbsz, t_len, _ = x_sample.shape
    p_len = cache_ka.shape[2]

    n_mod = 16
    ctx_row = dbsz
    cond = jnp.zeros((n_mod, D_MODEL), F32).at[:dbsz].set(c).at[ctx_row].set(c_ctx)
    mods = _mods(cond, w_ada, b_ada).reshape(DEPTH, n_mod, 3, D_MODEL)

    w_perm = w_gate = _wprep(w_in)
    w_br = w_branch.astype(BF16)
    w_o = w_out.astype(BF16)
    cos, sin_signed = _rope_tables(t_len)

    def cache_t(x):
        return x.transpose(0, 1, 3, 4, 2).reshape(x.shape[0], DEPTH, x.shape[3] * HEAD_DIM, p_len)
    cka, cva, ckb, cvb = cache_t(cache_ka), cache_t(cache_va), cache_t(cache_kb), cache_t(cache_vb)

    def gain_rows(g, reps):
        return jnp.tile(g, (1, reps)).reshape(DEPTH, 1, reps * HEAD_DIM)
    a_gq, a_gk = gain_rows(a_q_norm, A_HEADS), gain_rows(a_k_norm, A_KV_HEADS)
    b_gq, b_gk = gain_rows(b_q_norm, B_HEADS), gain_rows(b_k_norm, B_HEADS)
    b_gq2, b_gk2 = gain_rows(b_q_norm, LANE // HEAD_DIM), gain_rows(b_k_norm, LANE // HEAD_DIM)
    sinks = a_sink.reshape(DEPTH, 1, A_HEADS)
    wg, bg = _lru_gate_weights(lru_wa, lru_ba, lru_wx, lru_bx)
    conv_b = lru_conv_b.reshape(DEPTH, 1, LRU_WIDTH)
    grid_rows = t_len // GRID_W
    assert grid_rows % 2 == 0 and grid_rows // 2 >= NBR_BAND
    nbr_starts, nbr_plan, nbr_specs = _nbr_plan(grid_rows)
    nbr_table = _nbr_table(b_rpb.reshape((DEPTH * B_HEADS,) + b_rpb.shape[2:]), nbr_specs)

    yp = x_prompt.reshape(bsz * s_len, D_MODEL)
    ys = x_sample.reshape(dbsz * t_len, D_MODEL)
    zero_state = jnp.zeros((bsz, 1, 2, LRU_WIDTH), F32)
    new_caches, new_lru = None, []
    for l in range(DEPTH):
        proj = _inproj(yp, mods, norm_g, w_perm, l, 0, ctx_row)
        ya, yb, new_caches = _ctx_attn(proj, bsz, s_len, a_gq, a_gk, sinks, b_gq, b_gk, l, new_caches)
        yc, st = _lru(proj, zero_state, 0, lru_conv_w, conv_b, wg, bg, lru_lambda, l, bsz, s_len)
        yp = _merge(yp, ya, yb, yc, mods, norm_g, w_gate, w_br, w_o, l, 0, ctx_row)
        new_lru.append(st)

        proj = _inproj(ys, mods, norm_g, w_perm, l, t_len, 0)
        ya = _win_attn(proj, cka, cva, l, cos, sin_signed, a_gq, a_gk, sinks, dbsz, t_len)
        yb = _nbr_attn(proj, ckb, cvb, l, nbr_table, nbr_starts, nbr_plan, b_gq2, b_gk2, dbsz, t_len)
        yc, _ = _lru(proj, state_lru, l, lru_conv_w, conv_b, wg, bg, lru_lambda, l, dbsz, t_len)
        ys = _merge(ys, ya, yb, yc, mods, norm_g, w_gate, w_br, w_o, l, t_len, 0)

    def cache_out(x):
        return x.reshape(bsz, DEPTH, x.shape[2] // HEAD_DIM, HEAD_DIM, s_len).transpose(0, 1, 4, 2, 3)

    return (yp.reshape(bsz, s_len, D_MODEL), ys.reshape(dbsz, t_len, D_MODEL),
            *(cache_out(x) for x in new_caches), jnp.stack(new_lru, axis=1))
```

```python
import functools

import numpy as np
import jax
import jax.numpy as jnp
from jax import lax
from jax.experimental import pallas as pl
from jax.experimental.pallas import tpu as pltpu

F32 = jnp.float32
BF16 = jnp.bfloat16

D_MODEL = 1024
DEPTH = 2
GRID_W = 64
HEAD_DIM = 64
BRANCH_W = 512
A_HEADS = 8
A_KV_HEADS = 2
A_WINDOW = 128
A_BLOCK = 128
B_HEADS = 8
NB_ROWS = 8
NB_COLS = 16
LRU_WIDTH = 512
LRU_BLOCKS = 8
LRU_BW = LRU_WIDTH // LRU_BLOCKS
LRU_C = 8.0
CONV_W = 4
ROPE_BASE = 10000.0
EPS = 1e-6
NEG_INF = -1e30
QK_SCALE = HEAD_DIM ** -0.5

LANE = 128
SUBLANE = 8
MXU_DIM = 256
VMEM_LIMIT = 56 * 1024 * 1024

_ORIG_SPLITS = (512, 128, 128, 512, 512, 512, 512, 512, 512, 512, 1024, 1024, 1024)
_ORIG_OFFS = tuple(int(v) for v in np.cumsum((0,) + _ORIG_SPLITS)[:-1])
_PERM = (0, 3, 4, 5, 6, 7, 8, 9, 1, 2)
_GATE_SPLITS = (10, 11, 12)
IN_COLS = sum(_ORIG_SPLITS[k] for k in _PERM)
GATE_COLS = sum(_ORIG_SPLITS[k] for k in _GATE_SPLITS)
C_AQ, C_AG, C_BQ, C_BK, C_BV, C_BG, C_CX, C_CG, C_AK, C_AV = (0, 4, 8, 12, 16, 20, 24, 28, 32, 33)

_NT = (((1,), (1,)), ((), ()))


def _cparams(sem):
    return pltpu.CompilerParams(dimension_semantics=sem, vmem_limit_bytes=VMEM_LIMIT)


def _sigmoid(x):
    return 0.5 + 0.5 * jnp.tanh(0.5 * x)


def _silu(x):
    return x * _sigmoid(x)


def _head_mean_matrix(width):
    idx = np.arange(width) // HEAD_DIM
    return jnp.asarray((idx[:, None] == idx[None, :]).astype(np.float32) / HEAD_DIM, dtype=BF16)


def _heads_rms(x, bd, g):
    x2 = x * x
    hi = x2.astype(BF16)
    lo = (x2 - hi.astype(F32)).astype(BF16)
    width = x.shape[-1]
    step = min(width, MXU_DIM)
    tile = bd[:step, :step]
    ms = jnp.concatenate(
        [jnp.dot(hi[:, c:c + step], tile, preferred_element_type=F32)
         + jnp.dot(lo[:, c:c + step], tile, preferred_element_type=F32) for c in range(0, width, step)], axis=-1)
    return x * lax.rsqrt(ms + EPS) * g


def _head_lane_mask(width, h):
    lane = lax.broadcasted_iota(jnp.int32, (1, width), 1)
    return (lane >= h * HEAD_DIM) & (lane < (h + 1) * HEAD_DIM)


def _mods_kernel(c_ref, w_ref, b_ref, o_ref):
    c = c_ref[...]
    s = _silu(c).astype(BF16)
    o_ref[0] = jnp.dot(s, w_ref[0].astype(BF16), preferred_element_type=F32) + b_ref[0]


def _mods(cond, w_ada, b_ada):
    n = cond.shape[0]
    tn = D_MODEL
    return pl.pallas_call(
        _mods_kernel,
        grid=(DEPTH, 3 * D_MODEL // tn),
        in_specs=[
            pl.BlockSpec((n, D_MODEL), lambda l, j: (0, 0)),
            pl.BlockSpec((1, D_MODEL, tn), lambda l, j: (l, 0, j)),
            pl.BlockSpec((1, 1, tn), lambda l, j: (l, 0, j)),
        ],
        out_specs=pl.BlockSpec((1, n, tn), lambda l, j: (l, 0, j)),
        out_shape=jax.ShapeDtypeStruct((DEPTH, n, 3 * D_MODEL), F32),
        compiler_params=_cparams(("arbitrary", "arbitrary")),
        name="mods",
    )(cond, w_ada, b_ada.reshape(DEPTH, 1, 3 * D_MODEL))


W_BLK = 256
_MIXER_BLKS = IN_COLS // W_BLK
_GATE_BLKS = GATE_COLS // W_BLK
W_MIXER_OFF = IN_COLS
W_PREP_COLS = 2 * IN_COLS


_GAP_BLKS = W_MIXER_OFF // W_BLK - _GATE_BLKS


def _wprep_kernel(w_ref, o_ref):
    n = pl.program_id(0)
    in_gap = (n >= _GATE_BLKS) & (n < _GATE_BLKS + _GAP_BLKS)
    o_ref[...] = jnp.where(in_gap, 0.0, w_ref[...]).astype(BF16)


def _wprep_src(n):
    m = n - _GATE_BLKS - _GAP_BLKS
    mixer = jnp.where(m < 0, 0, jnp.where(m < 2, m, jnp.where(m < _MIXER_BLKS - 1, m + 1, 2)))
    return jnp.where(n < _GATE_BLKS, n + _MIXER_BLKS, mixer)


def _wprep(w_in):
    assert [_ORIG_OFFS[k] // W_BLK for k in _PERM[:-2]] == [0, 3, 5, 7, 9, 11, 13, 15] and _ORIG_OFFS[1] == 2 * W_BLK
    return pl.pallas_call(
        _wprep_kernel,
        grid=(W_PREP_COLS // W_BLK,),
        in_specs=[pl.BlockSpec((DEPTH, D_MODEL, W_BLK), lambda n: (0, 0, _wprep_src(n)))],
        out_specs=pl.BlockSpec((DEPTH, D_MODEL, W_BLK), lambda n: (0, 0, n)),
        out_shape=jax.ShapeDtypeStruct((DEPTH, D_MODEL, W_PREP_COLS), BF16),
        compiler_params=_cparams(("arbitrary",)),
        name="wprep",
    )(w_in)


def _modulated_norm(x, g_ref, mod_ref):
    y = x * lax.rsqrt(jnp.mean(x * x, axis=-1, keepdims=True) + EPS)
    y = y * g_ref[0]
    shift = mod_ref[0, 0, 0:1, :]
    scale = mod_ref[0, 0, 1:2, :]
    return (y * (1.0 + scale) + shift).astype(BF16)


def _inproj_kernel(x_ref, mod_ref, g_ref, w_ref, o_ref):
    h = _modulated_norm(x_ref[...], g_ref, mod_ref)
    o_ref[...] = jnp.dot(h, w_ref[0], preferred_element_type=F32)


def _inproj(x2d, mods, norm_g, w_perm, layer, rows_per_mod, mod_row0):
    tokens = x2d.shape[0]
    tm = 1024
    assert rows_per_mod % tm == 0
    tiles_per_mod = rows_per_mod // tm if rows_per_mod else 0

    def mod_idx(i):
        if rows_per_mod:
            return (layer, mod_row0 + i // tiles_per_mod, 0, 0)
        return (layer, mod_row0, 0, 0)

    return pl.pallas_call(
        _inproj_kernel,
        grid=(tokens // tm,),
        in_specs=[
            pl.BlockSpec((tm, D_MODEL), lambda i: (i, 0)),
            pl.BlockSpec((1, 1, 3, D_MODEL), mod_idx),
            pl.BlockSpec((1, 1, D_MODEL), lambda i: (layer, 0, 0)),
            pl.BlockSpec((1, D_MODEL, IN_COLS), lambda i: (layer, 0, W_MIXER_OFF // IN_COLS),
                         pipeline_mode=pl.Buffered(1)),
        ],
        out_specs=pl.BlockSpec((tm, IN_COLS), lambda i: (i, 0)),
        out_shape=jax.ShapeDtypeStruct((tokens, IN_COLS), F32),
        compiler_params=_cparams(("arbitrary",)),
        name="inproj",
    )(x2d, mods, norm_g.reshape(DEPTH, 1, D_MODEL), w_perm)


CTX_SEQS = 2
CTX_GROUP = MXU_DIM // HEAD_DIM


def _ctx_mixer(q, k, v, g, gq, gk, bdq, bdk, ones_stack, expand=None, sink_ref=None):
    s_len = q.shape[0]
    n_q = q.shape[1] // HEAD_DIM
    kn = _heads_rms(k, bdk, gk)
    qn = (_heads_rms(q, bdq, gq) * QK_SCALE).astype(BF16)
    knb = kn.astype(BF16)
    vb = v.astype(BF16)
    if expand is not None:
        knb = jnp.dot(knb, expand, preferred_element_type=F32).astype(BF16)
        vb = jnp.dot(vb, expand, preferred_element_type=F32).astype(BF16)
    gw = CTX_GROUP * HEAD_DIM
    masks = [_head_lane_mask(gw, hi) for hi in range(CTX_GROUP)]
    groups = []
    for gp in range(n_q // CTX_GROUP):
        cols = slice(gp * gw, (gp + 1) * gw)
        qh, kh, vh = qn[:, cols], knb[:, cols], vb[:, cols]
        kstack = jnp.concatenate([jnp.where(mk, kh, jnp.zeros_like(kh)) for mk in masks], axis=0)
        vstack = jnp.concatenate([jnp.where(mk, vh, jnp.zeros_like(vh)) for mk in masks], axis=0)
        s = lax.dot_general(qh, kstack, _NT, preferred_element_type=F32)
        ps, sink_terms = [], []
        for hi in range(CTX_GROUP):
            si = s[:, hi * s_len:(hi + 1) * s_len]
            m = jnp.max(si, axis=-1, keepdims=True)
            if sink_ref is not None:
                h = gp * CTX_GROUP + hi
                snk = sink_ref[0:1, h:h + 1]
                m = jnp.maximum(m, snk)
                sink_terms.append(jnp.where(masks[hi], jnp.exp(snk - m), 0.0))
            ps.append(jnp.exp(si - m).astype(BF16))
        p = jnp.concatenate(ps, axis=1)
        rhs = jnp.concatenate([vstack, ones_stack], axis=1)
        oe = jnp.dot(p, rhs, preferred_element_type=F32)
        l = oe[:, gw:]
        for term in sink_terms:
            l = l + term
        groups.append(oe[:, :gw] / l)
    return jnp.concatenate(groups, axis=-1) * _silu(g), kn


_CTX_INPUTS = 17


def _ctx_attn_kernel(*refs, s_len, n_alias):
    (aq_ref, ak_ref, av_ref, ag_ref, bq_ref, bk_ref, bv_ref, bg_ref,
     gqa_ref, gka_ref, sink_ref, gqb_ref, gkb_ref, bdw_ref, bdn_ref, exp_ref, ones_ref) = refs[:_CTX_INPUTS]
    ya_ref, yb_ref, ka_ref, va_ref, kb_ref, vb_ref = refs[_CTX_INPUTS + n_alias:]
    ones_stack = ones_ref[...]
    bdw = bdw_ref[...]
    for n in range(aq_ref.shape[0] // s_len):
        rows = slice(n * s_len, (n + 1) * s_len)
        va = av_ref[rows, :]
        vb = bv_ref[rows, :]
        va_ref[n, 0] = va.T
        vb_ref[n, 0] = vb.T
        ya, kna = _ctx_mixer(aq_ref[rows, :], ak_ref[rows, :], va, ag_ref[rows, :], gqa_ref[...], gka_ref[...],
                             bdw, bdn_ref[...], ones_stack, expand=exp_ref[...], sink_ref=sink_ref)
        yb, knb = _ctx_mixer(bq_ref[rows, :], bk_ref[rows, :], vb, bg_ref[rows, :], gqb_ref[...], gkb_ref[...],
                             bdw, bdw, ones_stack)
        ka_ref[n, 0] = kna.T
        kb_ref[n, 0] = knb.T
        ya_ref[rows, :] = ya.astype(ya_ref.dtype)
        yb_ref[rows, :] = yb.astype(yb_ref.dtype)


def _ctx_attn(proj, bsz, s_len, a_gq, a_gk, a_sink, b_gq, b_gk, layer, caches):
    per_layer = lambda width: pl.BlockSpec((None, 1, width), lambda b: (layer, 0, 0))
    kvw = A_KV_HEADS * HEAD_DIM
    grp = A_HEADS // A_KV_HEADS
    expand = np.zeros((kvw, BRANCH_W), np.float32)
    for h in range(A_HEADS):
        for d in range(HEAD_DIM):
            expand[(h // grp) * HEAD_DIM + d, h * HEAD_DIM + d] = 1.0
    ones_stack = np.zeros((CTX_GROUP * s_len, CTX_GROUP * HEAD_DIM), np.float32)
    for hi in range(CTX_GROUP):
        ones_stack[hi * s_len:(hi + 1) * s_len, hi * HEAD_DIM:(hi + 1) * HEAD_DIM] = 1.0
    const = lambda b: (0, 0)
    rows = CTX_SEQS * s_len
    wide = lambda c: pl.BlockSpec((rows, BRANCH_W), lambda b: (b, c // 4))
    narrow = lambda c: pl.BlockSpec((rows, kvw), lambda b: (b, c))
    prev = () if caches is None else tuple(caches)
    n_alias = len(prev)
    cache_spec = lambda width: pl.BlockSpec((CTX_SEQS, 1, width, s_len), lambda b: (b, layer, 0, 0))
    cache_shape = lambda width: jax.ShapeDtypeStruct((bsz, DEPTH, width, s_len), F32)
    outs = pl.pallas_call(
        functools.partial(_ctx_attn_kernel, s_len=s_len, n_alias=n_alias),
        grid=(bsz // CTX_SEQS,),
        in_specs=[
            wide(C_AQ), narrow(C_AK), narrow(C_AV), wide(C_AG), wide(C_BQ), wide(C_BK), wide(C_BV), wide(C_BG),
            per_layer(BRANCH_W), per_layer(kvw), per_layer(A_HEADS), per_layer(BRANCH_W), per_layer(BRANCH_W),
            pl.BlockSpec((BRANCH_W, BRANCH_W), const),
            pl.BlockSpec((kvw, kvw), const),
            pl.BlockSpec((kvw, BRANCH_W), const),
            pl.BlockSpec((CTX_GROUP * s_len, CTX_GROUP * HEAD_DIM), const),
        ] + [pl.BlockSpec(memory_space=pl.ANY)] * n_alias,
        out_specs=[
            pl.BlockSpec((rows, BRANCH_W), lambda b: (b, 0)),
            pl.BlockSpec((rows, BRANCH_W), lambda b: (b, 0)),
            cache_spec(kvw), cache_spec(kvw), cache_spec(BRANCH_W), cache_spec(BRANCH_W),
        ],
        out_shape=[
            jax.ShapeDtypeStruct((bsz * s_len, BRANCH_W), BF16),
            jax.ShapeDtypeStruct((bsz * s_len, BRANCH_W), BF16),
            cache_shape(kvw), cache_shape(kvw), cache_shape(BRANCH_W), cache_shape(BRANCH_W),
        ],
        input_output_aliases={_CTX_INPUTS + i: 2 + i for i in range(n_alias)},
        compiler_params=_cparams(("arbitrary",)),
        name="ctx_attn",
    )(proj, proj, proj, proj, proj, proj, proj, proj,
      a_gq, a_gk, a_sink, b_gq, b_gk, _head_mean_matrix(BRANCH_W), _head_mean_matrix(kvw), jnp.asarray(expand, dtype=BF16),
      jnp.asarray(ones_stack, dtype=BF16), *prev)
    return outs[0], outs[1], tuple(outs[2:])


def _rope(x, cos, sin_signed):
    w = x.shape[-1]
    lane = lax.broadcasted_iota(jnp.int32, x.shape, 1)
    up = pltpu.roll(x, w - 16, axis=1)
    dn = pltpu.roll(x, 16, axis=1)
    partner = jnp.where((lane & 16) == 0, up, dn)
    return x * cos + partner * sin_signed


WIN_SUB = 8


def _win_attn_kernel(q_ref, k_ref, v_ref, g_ref, kc_ref, vc_ref, cos_ref, sin_ref, gq_ref, gk_ref, sink_ref,
                     bdq_ref, bdk_ref, dup_ref, mask_ref, y_ref, kpad, vpad, kcx, vcx):
    j = pl.program_id(1)
    nb = pl.num_programs(1)
    t_len = k_ref.shape[0]
    grp = A_HEADS // A_KV_HEADS
    kvw = A_KV_HEADS * HEAD_DIM
    xw = 2 * kvw

    @pl.when(j == 0)
    def _():
        dup = dup_ref[...]
        kn = _heads_rms(k_ref[...], bdk_ref[...], gk_ref[...])
        kn = _rope(kn, cos_ref[:, 0:kvw], sin_ref[:, 0:kvw]).astype(BF16)
        def with_ones(vx):
            ones = jnp.ones((vx.shape[0], LANE), BF16)
            return jnp.concatenate(
                [part for kv in range(A_KV_HEADS) for part in (vx[:, kv * LANE:(kv + 1) * LANE], ones)], axis=1)

        kpad[0:A_BLOCK, :] = jnp.zeros((A_BLOCK, xw), BF16)
        kpad[A_BLOCK + t_len:2 * A_BLOCK + t_len, :] = jnp.zeros((A_BLOCK, xw), BF16)
        vpad[0:A_BLOCK, :] = jnp.zeros((A_BLOCK, 2 * xw), BF16)
        vpad[A_BLOCK + t_len:2 * A_BLOCK + t_len, :] = jnp.zeros((A_BLOCK, 2 * xw), BF16)
        kpad[A_BLOCK:A_BLOCK + t_len, :] = jnp.dot(kn, dup, preferred_element_type=F32).astype(BF16)
        vpad[A_BLOCK:A_BLOCK + t_len, :] = with_ones(
            jnp.dot(v_ref[...].astype(BF16), dup, preferred_element_type=F32).astype(BF16))
        kcx[...] = jnp.dot(kc_ref[0, 0].T.astype(BF16), dup, preferred_element_type=F32).astype(BF16)
        vcx[...] = with_ones(jnp.dot(vc_ref[0, 0].T.astype(BF16), dup, preferred_element_type=F32).astype(BF16))

    nloc = 3 * A_BLOCK
    low_half = lax.broadcasted_iota(jnp.int32, (1, LANE), 1) < HEAD_DIM
    for sub in range(WIN_SUB):
        jj = j * WIN_SUB + sub
        qrows = slice(sub * A_BLOCK, (sub + 1) * A_BLOCK)
        r0 = pl.multiple_of(jj * A_BLOCK, A_BLOCK)
        qn = _heads_rms(q_ref[qrows, :], bdq_ref[...], gq_ref[...])
        qb = (_rope(qn, cos_ref[pl.ds(r0, A_BLOCK), :], sin_ref[pl.ds(r0, A_BLOCK), :]) * QK_SCALE).astype(BF16)
        maskadd = mask_ref[jnp.where(jj == 0, 0, jnp.where(jj == nb * WIN_SUB - 1, 2, 1))]
        kband = kpad[pl.ds(r0, nloc), :]
        vband = vpad[pl.ds(r0, nloc), :]
        pairs = []
        for kv in range(A_KV_HEADS):
            cols = slice(kv * LANE, (kv + 1) * LANE)
            qparts, sinks = [], []
            for gi in range(grp):
                h = kv * grp + gi
                qpair = qb[:, (h // 2) * LANE:(h // 2 + 1) * LANE]
                keep = low_half if h % 2 == 0 else jnp.logical_not(low_half)
                qparts.append(jnp.where(keep, qpair, jnp.zeros_like(qpair)))
                sinks.append(jnp.broadcast_to(sink_ref[0:1, h:h + 1], (A_BLOCK, 1)))
            qst = jnp.concatenate(qparts, axis=0)
            snk = jnp.concatenate(sinks, axis=0)
            s_loc = lax.dot_general(qst, kband[:, cols], _NT, preferred_element_type=F32) + maskadd
            s_ctx = lax.dot_general(qst, kcx[:, cols], _NT, preferred_element_type=F32)
            m = jnp.maximum(jnp.maximum(jnp.max(s_loc, axis=-1, keepdims=True),
                                        jnp.max(s_ctx, axis=-1, keepdims=True)), snk)
            p_loc = jnp.exp(s_loc - m).astype(BF16)
            p_ctx = jnp.exp(s_ctx - m).astype(BF16)
            wide = slice(kv * MXU_DIM, (kv + 1) * MXU_DIM)
            oe = (jnp.dot(p_loc, vband[:, wide], preferred_element_type=F32)
                  + jnp.dot(p_ctx, vcx[:, wide], preferred_element_type=F32))
            o = oe[:, :LANE] / (oe[:, LANE:] + jnp.exp(snk - m))
            for k2 in range(grp // 2):
                even = o[(2 * k2) * A_BLOCK:(2 * k2 + 1) * A_BLOCK]
                odd = o[(2 * k2 + 1) * A_BLOCK:(2 * k2 + 2) * A_BLOCK]
                pairs.append(jnp.where(low_half, even, odd))
        y = jnp.concatenate(pairs, axis=-1) * _silu(g_ref[qrows, :])
        y_ref[qrows, :] = y.astype(y_ref.dtype)


def _win_mask(grp, nb):
    assert nb >= 2
    r = np.arange(A_BLOCK)[:, None]
    c = np.arange(3 * A_BLOCK)[None, :]
    band = np.abs(r + A_BLOCK - c) <= A_WINDOW
    variants = [band & (c >= A_BLOCK), band, band & (c < 2 * A_BLOCK)]
    return np.stack([np.tile(np.where(v, 0.0, NEG_INF).astype(np.float32), (grp, 1)) for v in variants])


def _win_attn(proj, cache_k, cache_v, layer, cos, sin_signed, gq, gk, sink, bsz, t_len):
    nb = t_len // A_BLOCK
    assert nb % WIN_SUB == 0
    nsteps = nb // WIN_SUB
    kvw = A_KV_HEADS * HEAD_DIM
    grp = A_HEADS // A_KV_HEADS
    p_len = cache_k.shape[3]
    dup = np.zeros((kvw, 2 * kvw), np.float32)
    for kv in range(A_KV_HEADS):
        for half in range(2):
            for d in range(HEAD_DIM):
                dup[kv * HEAD_DIM + d, kv * LANE + half * HEAD_DIM + d] = 1.0
    const2 = lambda b, j: (0, 0)
    return pl.pallas_call(
        _win_attn_kernel,
        grid=(bsz, nsteps),
        in_specs=[
            pl.BlockSpec((WIN_SUB * A_BLOCK, BRANCH_W), lambda b, j: (b * nsteps + j, C_AQ // 4)),
            pl.BlockSpec((t_len, kvw), lambda b, j: (b, C_AK)),
            pl.BlockSpec((t_len, kvw), lambda b, j: (b, C_AV)),
            pl.BlockSpec((WIN_SUB * A_BLOCK, BRANCH_W), lambda b, j: (b * nsteps + j, C_AG // 4)),
            pl.BlockSpec((1, 1, kvw, p_len), lambda b, j: (b, layer, 0, 0)),
            pl.BlockSpec((1, 1, kvw, p_len), lambda b, j: (b, layer, 0, 0)),
            pl.BlockSpec((t_len, BRANCH_W), const2),
            pl.BlockSpec((t_len, BRANCH_W), const2),
            pl.BlockSpec((None, 1, BRANCH_W), lambda b, j: (layer, 0, 0)),
            pl.BlockSpec((None, 1, kvw), lambda b, j: (layer, 0, 0)),
            pl.BlockSpec((None, 1, A_HEADS), lambda b, j: (layer, 0, 0)),
            pl.BlockSpec((BRANCH_W, BRANCH_W), const2),
            pl.BlockSpec((kvw, kvw), const2),
            pl.BlockSpec((kvw, 2 * kvw), const2),
            pl.BlockSpec((3, grp * A_BLOCK, 3 * A_BLOCK), lambda b, j: (0, 0, 0)),
        ],
        out_specs=pl.BlockSpec((WIN_SUB * A_BLOCK, BRANCH_W), lambda b, j: (b * nsteps + j, 0)),
        out_shape=jax.ShapeDtypeStruct((bsz * t_len, BRANCH_W), BF16),
        scratch_shapes=[pltpu.VMEM((t_len + 2 * A_BLOCK, 2 * kvw), BF16),
                        pltpu.VMEM((t_len + 2 * A_BLOCK, 4 * kvw), BF16),
                        pltpu.VMEM((p_len, 2 * kvw), BF16),
                        pltpu.VMEM((p_len, 4 * kvw), BF16)],
        compiler_params=_cparams(("arbitrary", "arbitrary")),
        name="win_attn",
    )(proj, proj, proj, proj, cache_k, cache_v, cos, sin_signed,
      gq, gk, sink, _head_mean_matrix(BRANCH_W), _head_mean_matrix(kvw),
      jnp.asarray(dup, dtype=BF16), jnp.asarray(_win_mask(grp, nb)))


NBR_QB = 2 * GRID_W
NBR_BAND = 5


def _nbr_plan(rows):
    kh = min(NB_ROWS, rows)
    nblk = rows // 2
    specs, plan, starts = {}, [], []
    for i in range(nblk):
        s0 = min(max(i - 2, 0), nblk - NBR_BAND)
        starts.append(s0)
        blk = []
        for a in range(2):
            qr = 2 * i + a
            rs = min(max(qr - kh // 2, 0), rows - kh)
            assert 2 * s0 <= rs and rs + kh <= 2 * (s0 + NBR_BAND)
            row = []
            for p in range(NBR_BAND):
                pair = tuple(kr - qr + NB_ROWS - 1 if rs <= kr < rs + kh else None
                             for kr in (2 * (s0 + p), 2 * (s0 + p) + 1))
                row.append(specs.setdefault(pair, len(specs)))
            blk.append(row)
        plan.append(blk)
    return tuple(starts), plan, list(specs)


def _nbr_table(rpb, specs):
    heads = rpb.shape[0]
    c = np.arange(GRID_W)
    cs = np.clip(c - NB_COLS // 2, 0, GRID_W - NB_COLS)
    col_ok = (c[None, :] >= cs[:, None]) & (c[None, :] < cs[:, None] + NB_COLS)
    dc = c[None, :] - c[:, None] + NB_COLS - 1
    onehot = ((dc[None] == np.arange(2 * NB_COLS - 1)[:, None, None]) & col_ok[None]).astype(np.float32)
    shifted = jnp.einsum('hdj,jqk->hdqk', rpb.astype(F32), jnp.asarray(onehot), precision=lax.Precision.HIGHEST)
    by_col = jnp.where(col_ok[None, None], shifted, NEG_INF)
    neg = jnp.full((heads, GRID_W, GRID_W), NEG_INF, F32)
    blocks = [jnp.concatenate([neg if d is None else by_col[:, d] for d in spec], axis=-1) for spec in specs]
    return jnp.stack(blocks, axis=1)


NBR_GROUPS = 4


def _nbr_attn_kernel(q_ref, k_ref, v_ref, g_ref, kc_ref, vc_ref, tb_ref, gq_ref, gk_ref, bd_ref, y_ref,
                     *, starts, plan):
    nband = NBR_BAND * NBR_QB
    bd = bd_ref[...]
    hpg = LANE // HEAD_DIM
    for gs in range(NBR_GROUPS):
        lanes = slice(gs * LANE, (gs + 1) * LANE)
        qn = (_heads_rms(q_ref[:, lanes], bd, gq_ref[...]) * QK_SCALE).astype(BF16)
        kn = _heads_rms(k_ref[:, lanes], bd, gk_ref[...]).astype(BF16)
        vb = v_ref[:, lanes].astype(BF16)
        kcb = kc_ref[0, 0, lanes, :].T.astype(BF16)
        vcb = vc_ref[0, 0, lanes, :].T.astype(BF16)
        acc = [jnp.zeros((NBR_QB, LANE), F32) for _ in starts]
        for h in range(hpg):
            hm = _head_lane_mask(LANE, h)
            km = jnp.where(hm, kn, jnp.zeros_like(kn))
            kcm = jnp.where(hm, kcb, jnp.zeros_like(kcb))
            vm = jnp.concatenate([jnp.where(hm, vb, jnp.zeros_like(vb)), jnp.ones_like(vb)], axis=1)
            vcm = jnp.concatenate([jnp.where(hm, vcb, jnp.zeros_like(vcb)), jnp.ones_like(vcb)], axis=1)
            for i, s0 in enumerate(starts):
                qi = qn[i * NBR_QB:(i + 1) * NBR_QB]
                ks = slice(s0 * NBR_QB, s0 * NBR_QB + nband)
                s_raw = lax.dot_general(qi, km[ks], _NT, preferred_element_type=F32)
                s_loc = jnp.concatenate(
                    [jnp.concatenate([s_raw[a * GRID_W:(a + 1) * GRID_W, p * LANE:(p + 1) * LANE]
                                      + tb_ref[gs * hpg + h, plan[i][a][p]] for p in range(NBR_BAND)], axis=1)
                     for a in range(2)], axis=0)
                s_ctx = lax.dot_general(qi, kcm, _NT, preferred_element_type=F32)
                m = jnp.maximum(jnp.max(s_loc, axis=-1, keepdims=True), jnp.max(s_ctx, axis=-1, keepdims=True))
                p_loc = jnp.exp(s_loc - m).astype(BF16)
                p_ctx = jnp.exp(s_ctx - m).astype(BF16)
                oe = (jnp.dot(p_loc, vm[ks], preferred_element_type=F32)
                      + jnp.dot(p_ctx, vcm, preferred_element_type=F32))
                acc[i] = acc[i] + oe[:, :LANE] / oe[:, LANE:]
        y = jnp.concatenate(acc, axis=0) * _silu(g_ref[:, lanes])
        y_ref[:, lanes] = y.astype(y_ref.dtype)


def _nbr_attn(proj, cache_k, cache_v, layer, table, starts, plan, gq, gk, bsz, t_len):
    hp = NBR_GROUPS * (LANE // HEAD_DIM)
    gw = NBR_GROUPS * LANE
    nhp = B_HEADS // hp
    p_len = cache_k.shape[3]
    const = lambda h, b: (0, 0)
    kern = functools.partial(_nbr_attn_kernel, starts=starts, plan=plan)
    return pl.pallas_call(
        kern,
        grid=(nhp, bsz),
        in_specs=[
            pl.BlockSpec((t_len, gw), lambda h, b: (b, C_BQ // NBR_GROUPS + h)),
            pl.BlockSpec((t_len, gw), lambda h, b: (b, C_BK // NBR_GROUPS + h)),
            pl.BlockSpec((t_len, gw), lambda h, b: (b, C_BV // NBR_GROUPS + h)),
            pl.BlockSpec((t_len, gw), lambda h, b: (b, C_BG // NBR_GROUPS + h)),
            pl.BlockSpec((1, 1, gw, p_len), lambda h, b: (b, layer, h, 0)),
            pl.BlockSpec((1, 1, gw, p_len), lambda h, b: (b, layer, h, 0)),
            pl.BlockSpec((hp,) + table.shape[1:], lambda h, b: (layer * nhp + h, 0, 0, 0)),
            pl.BlockSpec((None, 1, LANE), lambda h, b: (layer, 0, 0)),
            pl.BlockSpec((None, 1, LANE), lambda h, b: (layer, 0, 0)),
            pl.BlockSpec((LANE, LANE), const),
        ],
        out_specs=pl.BlockSpec((t_len, gw), lambda h, b: (b, h)),
        out_shape=jax.ShapeDtypeStruct((bsz * t_len, BRANCH_W), BF16),
        compiler_params=_cparams(("arbitrary", "arbitrary")),
        name="nbr_attn",
    )(proj, proj, proj, proj, cache_k, cache_v, table, gq, gk, _head_mean_matrix(LANE))


LRU_CHUNKS = SUBLANE
LRU_PITCH_PAD = 4


def _lru_kernel(cx_ref, cg_ref, h0_ref, cw_ref, cb_ref, wg_ref, bg_ref, lam_ref,
                y_ref, st_ref, a_s, u_s, h_s, p_s, y_s):
    t_len = cx_ref.shape[0]
    w = LRU_WIDTH
    cx = cx_ref[...]
    row = lax.broadcasted_iota(jnp.int32, (t_len, w), 0)
    xc = cb_ref[...] + cx * cw_ref[2:3, :]
    xc = xc + jnp.where(row >= 2, pltpu.roll(cx, 2, axis=0), 0.0) * cw_ref[0:1, :]
    xc = xc + jnp.where(row >= 1, pltpu.roll(cx, 1, axis=0), 0.0) * cw_ref[1:2, :]
    xc = xc + jnp.where(row < t_len - 1, pltpu.roll(cx, t_len - 1, axis=0), 0.0) * cw_ref[3:4, :]

    gates = jnp.dot(xc.astype(BF16), wg_ref[...], preferred_element_type=F32) + bg_ref[...]
    coeffs = []
    for d in range(2):
        th_r = jnp.tanh(gates[:, d * w:(d + 1) * w])
        th_i = jnp.tanh(gates[:, (2 + d) * w:(3 + d) * w])
        nl = -lam_ref[d:d + 1, :]
        softplus = jnp.maximum(nl, 0.0) + jnp.log1p(jnp.exp(-jnp.abs(nl)))
        quarter_c = (-0.25 * LRU_C) * softplus
        half_log_a = quarter_c * th_r + quarter_c
        t = jnp.tanh(half_log_a)
        rc = 1.0 / (1.0 - t)
        coeffs.append(((1.0 + t) * rc,
                       jnp.sqrt(-t) * rc * (1.0 + th_i) * xc))

    chunk_len = t_len // LRU_CHUNKS
    pitch = chunk_len + LRU_PITCH_PAD
    nt = w // LANE
    for d, (a_val, u_val) in enumerate(coeffs):
        for c in range(LRU_CHUNKS):
            for k in range(nt):
                dst = slice(c * pitch, c * pitch + chunk_len)
                src = (slice(c * chunk_len, (c + 1) * chunk_len), slice(k * LANE, (k + 1) * LANE))
                a_s[d, k, dst, :] = a_val[src]
                u_s[d, k, dst, :] = u_val[src]

    def scan_body(s, carry):
        hs, ps = carry
        new_h, new_p = [], []
        for d in range(2):
            pos = s if d == 0 else chunk_len - 1 - s
            idx = pl.ds(pos, LRU_CHUNKS, stride=pitch)
            for k in range(nt):
                av = a_s[d, k, idx, :]
                h = av * hs[d * nt + k] + u_s[d, k, idx, :]
                p = av * ps[d * nt + k]
                h_s[d, k, idx, :] = h
                p_s[d, k, idx, :] = p
                new_h.append(h)
                new_p.append(p)
        return tuple(new_h), tuple(new_p)

    zero = jnp.zeros((LRU_CHUNKS, LANE), F32)
    one = jnp.ones((LRU_CHUNKS, LANE), F32)
    h_end, p_end = lax.fori_loop(0, chunk_len, scan_body, ((zero,) * (2 * nt), (one,) * (2 * nt)), unroll=4)

    h0 = h0_ref[0]
    enter, finals = [], []
    for d in range(2):
        order = range(LRU_CHUNKS) if d == 0 else range(LRU_CHUNKS - 1, -1, -1)
        for k in range(nt):
            he, pe = h_end[d * nt + k], p_end[d * nt + k]
            e = h0[d:d + 1, k * LANE:(k + 1) * LANE]
            rows = [None] * LRU_CHUNKS
            for c in order:
                rows[c] = e
                e = pe[c:c + 1, :] * e + he[c:c + 1, :]
            enter.append(jnp.concatenate(rows, axis=0))
            finals.append(e)
    st_ref[0] = jnp.concatenate([jnp.concatenate(finals[d * nt:(d + 1) * nt], axis=1) for d in range(2)], axis=0)

    def fix_body(s, carry):
        idx = pl.ds(s, LRU_CHUNKS, stride=pitch)
        for k in range(nt):
            hf = h_s[0, k, idx, :] + p_s[0, k, idx, :] * enter[k]
            hb = h_s[1, k, idx, :] + p_s[1, k, idx, :] * enter[nt + k]
            y_s[k, idx, :] = hf + hb
        return carry

    lax.fori_loop(0, chunk_len, fix_body, 0, unroll=4)
    hsum = jnp.concatenate(
        [jnp.concatenate([y_s[k, c * pitch:c * pitch + chunk_len, :] for c in range(LRU_CHUNKS)], axis=0)
         for k in range(nt)], axis=1)
    y_ref[...] = (hsum * _silu(cg_ref[...])).astype(y_ref.dtype)


def _lru(proj, h0, h0_layer, conv_w, conv_b, w_gates, b_gates, lam, layer, bsz, t_len):
    w = LRU_WIDTH
    assert t_len % (LRU_CHUNKS * SUBLANE) == 0
    rows_pad = LRU_CHUNKS * (t_len // LRU_CHUNKS + LRU_PITCH_PAD)
    return pl.pallas_call(
        _lru_kernel,
        grid=(bsz,),
        in_specs=[
            pl.BlockSpec((t_len, w), lambda b: (b, C_CX // 4)),
            pl.BlockSpec((t_len, w), lambda b: (b, C_CG // 4)),
            pl.BlockSpec((1, None, 2, w), lambda b: (b, h0_layer, 0, 0)),
            pl.BlockSpec((None, CONV_W, w), lambda b: (layer, 0, 0)),
            pl.BlockSpec((None, 1, w), lambda b: (layer, 0, 0)),
            pl.BlockSpec((None, w, 4 * w), lambda b: (layer, 0, 0)),
            pl.BlockSpec((None, 1, 4 * w), lambda b: (layer, 0, 0)),
            pl.BlockSpec((None, 2, w), lambda b: (layer, 0, 0)),
        ],
        out_specs=[
            pl.BlockSpec((t_len, w), lambda b: (b, 0)),
            pl.BlockSpec((1, 2, w), lambda b: (b, 0, 0)),
        ],
        out_shape=[
            jax.ShapeDtypeStruct((bsz * t_len, w), BF16),
            jax.ShapeDtypeStruct((bsz, 2, w), F32),
        ],
        scratch_shapes=[pltpu.VMEM((2, w // LANE, rows_pad, LANE), F32) for _ in range(4)]
        + [pltpu.VMEM((w // LANE, rows_pad, LANE), F32)],
        compiler_params=_cparams(("arbitrary",)),
        name="lru",
    )(proj, proj, h0, conv_w, conv_b, w_gates, b_gates, lam)


def _lru_gate_weights(wa, ba, wx, bx):
    def dense(wblk):
        eye = jnp.eye(LRU_BLOCKS, dtype=wblk.dtype)
        full = jnp.einsum('ldnkj,nm->lnkdmj', wblk, eye)
        return full.reshape(DEPTH, LRU_WIDTH, 2 * LRU_WIDTH)
    wg = jnp.concatenate([dense(wa), dense(wx)], axis=2)
    bg = jnp.concatenate([ba.reshape(DEPTH, 1, 2 * LRU_WIDTH), bx.reshape(DEPTH, 1, 2 * LRU_WIDTH)], axis=2)
    return (0.5 * wg).astype(BF16), 0.5 * bg


def _merge_kernel(x_ref, ya_ref, yb_ref, yc_ref, mod_ref, g_ref, wgate_ref, wbr_ref, wout_ref, o_ref):
    x = x_ref[...]
    h = _modulated_norm(x, g_ref, mod_ref)
    gates = jnp.dot(h, wgate_ref[0], preferred_element_type=F32)
    z = None
    for k, y_ref in enumerate((ya_ref, yb_ref, yc_ref)):
        term = (_sigmoid(gates[:, k * D_MODEL:(k + 1) * D_MODEL])
                * jnp.dot(y_ref[...], wbr_ref[0, k], preferred_element_type=F32))
        z = term if z is None else z + term
    out = jnp.dot(z.astype(BF16), wout_ref[0], preferred_element_type=F32)
    o_ref[...] = x + mod_ref[0, 0, 2:3, :] * out


def _merge(x2d, ya, yb, yc, mods, norm_g, w_gate, w_br, w_out, layer, rows_per_mod, mod_row0):
    tokens = x2d.shape[0]
    tm = 1024
    assert rows_per_mod % tm == 0
    tiles_per_mod = rows_per_mod // tm if rows_per_mod else 0

    def mod_idx(i):
        if rows_per_mod:
            return (layer, mod_row0 + i // tiles_per_mod, 0, 0)
        return (layer, mod_row0, 0, 0)

    return pl.pallas_call(
        _merge_kernel,
        grid=(tokens // tm,),
        in_specs=[
            pl.BlockSpec((tm, D_MODEL), lambda i: (i, 0)),
            pl.BlockSpec((tm, BRANCH_W), lambda i: (i, 0)),
            pl.BlockSpec((tm, BRANCH_W), lambda i: (i, 0)),
            pl.BlockSpec((tm, BRANCH_W), lambda i: (i, 0)),
            pl.BlockSpec((1, 1, 3, D_MODEL), mod_idx),
            pl.BlockSpec((1, 1, D_MODEL), lambda i: (layer, 0, 0)),
            pl.BlockSpec((1, D_MODEL, GATE_COLS), lambda i: (layer, 0, 0)),
            pl.BlockSpec((1, 3, BRANCH_W, D_MODEL), lambda i: (layer, 0, 0, 0)),
            pl.BlockSpec((1, D_MODEL, D_MODEL), lambda i: (layer, 0, 0)),
        ],
        out_specs=pl.BlockSpec((tm, D_MODEL), lambda i: (i, 0)),
        out_shape=jax.ShapeDtypeStruct((tokens, D_MODEL), F32),
        compiler_params=_cparams(("arbitrary",)),
        name="merge",
    )(x2d, ya, yb, yc, mods, norm_g.reshape(DEPTH, 1, D_MODEL), w_gate, w_br, w_out)


def _rope_tables(t_len):
    t = jnp.arange(t_len)
    m = HEAD_DIM // 4
    freqs = ROPE_BASE ** (-jnp.arange(m, dtype=F32) / m)
    ang_r = (t // GRID_W).astype(F32)[:, None] * freqs[None, :]
    ang_c = (t % GRID_W).astype(F32)[:, None] * freqs[None, :]
    cos = jnp.concatenate([jnp.cos(ang_r), jnp.cos(ang_r), jnp.cos(ang_c), jnp.cos(ang_c)], axis=-1)
    sin = jnp.concatenate([-jnp.sin(ang_r), jnp.sin(ang_r), -jnp.sin(ang_c), jnp.sin(ang_c)], axis=-1)
    reps = BRANCH_W // HEAD_DIM
    return jnp.tile(cos, (1, reps)), jnp.tile(sin, (1, reps))


def kernel(x_prompt, x_sample, cache_ka, cache_va, cache_kb, cache_vb, state_lru, c, c_ctx,
           norm_g, w_ada, b_ada, w_in, a_q_norm, a_k_norm, a_sink, b_q_norm, b_k_norm, b_rpb,
           lru_conv_w, lru_conv_b, lru_wa, lru_ba, lru_wx, lru_bx, lru_lambda, w_branch, w_out):
    bsz, s_len, _ = x_prompt.shape
    dbsz, t_len, _ = x_sample.shape
    p_len = cache_ka.shape[2]

    n_mod = 16
    ctx_row = dbsz
    cond = jnp.zeros((n_mod, D_MODEL), F32).at[:dbsz].set(c).at[ctx_row].set(c_ctx)
    mods = _mods(cond, w_ada, b_ada).reshape(DEPTH, n_mod, 3, D_MODEL)

    w_perm = w_gate = _wprep(w_in)
    w_br = w_branch.astype(BF16)
    w_o = w_out.astype(BF16)
    cos, sin_signed = _rope_tables(t_len)

    def cache_t(x):
        return x.transpose(0, 1, 3, 4, 2).reshape(x.shape[0], DEPTH, x.shape[3] * HEAD_DIM, p_len)
    cka, cva, ckb, cvb = cache_t(cache_ka), cache_t(cache_va), cache_t(cache_kb), cache_t(cache_vb)

    def gain_rows(g, reps):
        return jnp.tile(g, (1, reps)).reshape(DEPTH, 1, reps * HEAD_DIM)
    a_gq, a_gk = gain_rows(a_q_norm, A_HEADS), gain_rows(a_k_norm, A_KV_HEADS)
    b_gq, b_gk = gain_rows(b_q_norm, B_HEADS), gain_rows(b_k_norm, B_HEADS)
    b_gq2, b_gk2 = gain_rows(b_q_norm, LANE // HEAD_DIM), gain_rows(b_k_norm, LANE // HEAD_DIM)
    sinks = a_sink.reshape(DEPTH, 1, A_HEADS)
    wg, bg = _lru_gate_weights(lru_wa, lru_ba, lru_wx, lru_bx)
    conv_b = lru_conv_b.reshape(DEPTH, 1, LRU_WIDTH)
    grid_rows = t_len // GRID_W
    assert grid_rows % 2 == 0 and grid_rows // 2 >= NBR_BAND
    nbr_starts, nbr_plan, nbr_specs = _nbr_plan(grid_rows)
    nbr_table = _nbr_table(b_rpb.reshape((DEPTH * B_HEADS,) + b_rpb.shape[2:]), nbr_specs)

    yp = x_prompt.reshape(bsz * s_len, D_MODEL)
    ys = x_sample.reshape(dbsz * t_len, D_MODEL)
    zero_state = jnp.zeros((bsz, 1, 2, LRU_WIDTH), F32)
    new_caches, new_lru = None, []
    for l in range(DEPTH):
        proj = _inproj(yp, mods, norm_g, w_perm, l, 0, ctx_row)
        ya, yb, new_caches = _ctx_attn(proj, bsz, s_len, a_gq, a_gk, sinks, b_gq, b_gk, l, new_caches)
        yc, st = _lru(proj, zero_state, 0, lru_conv_w, conv_b, wg, bg, lru_lambda, l, bsz, s_len)
        yp = _merge(yp, ya, yb, yc, mods, norm_g, w_gate, w_br, w_o, l, 0, ctx_row)
        new_lru.append(st)

        proj = _inproj(ys, mods, norm_g, w_perm, l, t_len, 0)
        ya = _win_attn(proj, cka, cva, l, cos, sin_signed, a_gq, a_gk, sinks, dbsz, t_len)
        yb = _nbr_attn(proj, ckb, cvb, l, nbr_table, nbr_starts, nbr_plan, b_gq2, b_gk2, dbsz, t_len)
        yc, _ = _lru(proj, state_lru, l, lru_conv_w, conv_b, wg, bg, lru_lambda, l, dbsz, t_len)
        ys = _merge(ys, ya, yb, yc, mods, norm_g, w_gate, w_br, w_o, l, t_len, 0)

    def cache_out(x):
        return x.reshape(bsz, DEPTH, x.shape[2] // HEAD_DIM, HEAD_DIM, s_len).transpose(0, 1, 4, 2, 3)

    return (yp.reshape(bsz, s_len, D_MODEL), ys.reshape(dbsz, t_len, D_MODEL),
            *(cache_out(x) for x in new_caches), jnp.stack(new_lru, axis=1))
```

```python
import functools

import numpy as np
import jax
import jax.numpy as jnp
from jax import lax
from jax.experimental import pallas as pl
from jax.experimental.pallas import tpu as pltpu

F32 = jnp.float32
BF16 = jnp.bfloat16

D_MODEL = 1024
DEPTH = 2
GRID_W = 64
HEAD_DIM = 64
BRANCH_W = 512
A_HEADS = 8
A_KV_HEADS = 2
A_WINDOW = 128
A_BLOCK = 128
B_HEADS = 8
NB_ROWS = 8
NB_COLS = 16
LRU_WIDTH = 512
LRU_BLOCKS = 8
LRU_BW = LRU_WIDTH // LRU_BLOCKS
LRU_C = 8.0
CONV_W = 4
ROPE_BASE = 10000.0
EPS = 1e-6
NEG_INF = -1e30
QK_SCALE = HEAD_DIM ** -0.5

LANE = 128
SUBLANE = 8
MXU_DIM = 256
VMEM_LIMIT = 56 * 1024 * 1024

_ORIG_SPLITS = (512, 128, 128, 512, 512, 512, 512, 512, 512, 512, 1024, 1024, 1024)
_ORIG_OFFS = tuple(int(v) for v in np.cumsum((0,) + _ORIG_SPLITS)[:-1])
_PERM = (0, 3, 4, 5, 6, 7, 8, 9, 1, 2)
_GATE_SPLITS = (10, 11, 12)
IN_COLS = sum(_ORIG_SPLITS[k] for k in _PERM)
GATE_COLS = sum(_ORIG_SPLITS[k] for k in _GATE_SPLITS)
C_AQ, C_AG, C_BQ, C_BK, C_BV, C_BG, C_CX, C_CG, C_AK, C_AV = (0, 4, 8, 12, 16, 20, 24, 28, 32, 33)

_NT = (((1,), (1,)), ((), ()))


def _cparams(sem):
    return pltpu.CompilerParams(dimension_semantics=sem, vmem_limit_bytes=VMEM_LIMIT)


def _sigmoid(x):
    return 0.5 + 0.5 * jnp.tanh(0.5 * x)


def _silu(x):
    return x * _sigmoid(x)


def _head_mean_matrix(width):
    idx = np.arange(width) // HEAD_DIM
    return jnp.asarray((idx[:, None] == idx[None, :]).astype(np.float32) / HEAD_DIM, dtype=BF16)


def _heads_rms(x, bd, g):
    x2 = x * x
    hi = x2.astype(BF16)
    lo = (x2 - hi.astype(F32)).astype(BF16)
    width = x.shape[-1]
    step = min(width, MXU_DIM)
    tile = bd[:step, :step]
    ms = jnp.concatenate(
        [jnp.dot(hi[:, c:c + step], tile, preferred_element_type=F32)
         + jnp.dot(lo[:, c:c + step], tile, preferred_element_type=F32) for c in range(0, width, step)], axis=-1)
    return x * lax.rsqrt(ms + EPS) * g


def _head_lane_mask(width, h):
    lane = lax.broadcasted_iota(jnp.int32, (1, width), 1)
    return (lane >= h * HEAD_DIM) & (lane < (h + 1) * HEAD_DIM)


def _mods_kernel(c_ref, w_ref, b_ref, o_ref):
    c = c_ref[...]
    s = _silu(c).astype(BF16)
    o_ref[0] = jnp.dot(s, w_ref[0].astype(BF16), preferred_element_type=F32) + b_ref[0]


def _mods(cond, w_ada, b_ada):
    n = cond.shape[0]
    tn = D_MODEL
    return pl.pallas_call(
        _mods_kernel,
        grid=(DEPTH, 3 * D_MODEL // tn),
        in_specs=[
            pl.BlockSpec((n, D_MODEL), lambda l, j: (0, 0)),
            pl.BlockSpec((1, D_MODEL, tn), lambda l, j: (l, 0, j)),
            pl.BlockSpec((1, 1, tn), lambda l, j: (l, 0, j)),
        ],
        out_specs=pl.BlockSpec((1, n, tn), lambda l, j: (l, 0, j)),
        out_shape=jax.ShapeDtypeStruct((DEPTH, n, 3 * D_MODEL), F32),
        compiler_params=_cparams(("arbitrary", "arbitrary")),
        name="mods",
    )(cond, w_ada, b_ada.reshape(DEPTH, 1, 3 * D_MODEL))


W_BLK = 256
_MIXER_BLKS = IN_COLS // W_BLK
_GATE_BLKS = GATE_COLS // W_BLK
W_MIXER_OFF = IN_COLS
W_PREP_COLS = 2 * IN_COLS


_GAP_BLKS = W_MIXER_OFF // W_BLK - _GATE_BLKS


def _wprep_kernel(w_ref, o_ref):
    n = pl.program_id(0)
    in_gap = (n >= _GATE_BLKS) & (n < _GATE_BLKS + _GAP_BLKS)
    o_ref[...] = jnp.where(in_gap, 0.0, w_ref[...]).astype(BF16)


def _wprep_src(n):
    m = n - _GATE_BLKS - _GAP_BLKS
    mixer = jnp.where(m < 0, 0, jnp.where(m < 2, m, jnp.where(m < _MIXER_BLKS - 1, m + 1, 2)))
    return jnp.where(n < _GATE_BLKS, n + _MIXER_BLKS, mixer)


def _wprep(w_in):
    assert [_ORIG_OFFS[k] // W_BLK for k in _PERM[:-2]] == [0, 3, 5, 7, 9, 11, 13, 15] and _ORIG_OFFS[1] == 2 * W_BLK
    return pl.pallas_call(
        _wprep_kernel,
        grid=(W_PREP_COLS // W_BLK,),
        in_specs=[pl.BlockSpec((DEPTH, D_MODEL, W_BLK), lambda n: (0, 0, _wprep_src(n)))],
        out_specs=pl.BlockSpec((DEPTH, D_MODEL, W_BLK), lambda n: (0, 0, n)),
        out_shape=jax.ShapeDtypeStruct((DEPTH, D_MODEL, W_PREP_COLS), BF16),
        compiler_params=_cparams(("arbitrary",)),
        name="wprep",
    )(w_in)


def _modulated_norm(x, g_ref, mod_ref):
    y = x * lax.rsqrt(jnp.mean(x * x, axis=-1, keepdims=True) + EPS)
    y = y * g_ref[0]
    shift = mod_ref[0, 0, 0:1, :]
    scale = mod_ref[0, 0, 1:2, :]
    return (y * (1.0 + scale) + shift).astype(BF16)


def _inproj_kernel(x_ref, mod_ref, g_ref, w_ref, o_ref):
    h = _modulated_norm(x_ref[...], g_ref, mod_ref)
    o_ref[...] = jnp.dot(h, w_ref[0], preferred_element_type=F32)


def _inproj(x2d, mods, norm_g, w_perm, layer, rows_per_mod, mod_row0):
    tokens = x2d.shape[0]
    tm = 512
    tiles_per_mod = rows_per_mod // tm if rows_per_mod else 0

    def mod_idx(i):
        if rows_per_mod:
            return (layer, mod_row0 + i // tiles_per_mod, 0, 0)
        return (layer, mod_row0, 0, 0)

    return pl.pallas_call(
        _inproj_kernel,
        grid=(tokens // tm,),
        in_specs=[
            pl.BlockSpec((tm, D_MODEL), lambda i: (i, 0)),
            pl.BlockSpec((1, 1, 3, D_MODEL), mod_idx),
            pl.BlockSpec((1, 1, D_MODEL), lambda i: (layer, 0, 0)),
            pl.BlockSpec((1, D_MODEL, IN_COLS), lambda i: (layer, 0, W_MIXER_OFF // IN_COLS)),
        ],
        out_specs=pl.BlockSpec((tm, IN_COLS), lambda i: (i, 0)),
        out_shape=jax.ShapeDtypeStruct((tokens, IN_COLS), F32),
        compiler_params=_cparams(("arbitrary",)),
        name="inproj",
    )(x2d, mods, norm_g.reshape(DEPTH, 1, D_MODEL), w_perm)


CTX_SEQS = 2
CTX_GROUP = MXU_DIM // HEAD_DIM


def _ctx_mixer(q, k, v, g, gq, gk, bdq, bdk, ones_stack, expand=None, sink_ref=None):
    s_len = q.shape[0]
    n_q = q.shape[1] // HEAD_DIM
    kn = _heads_rms(k, bdk, gk)
    qn = (_heads_rms(q, bdq, gq) * QK_SCALE).astype(BF16)
    knb = kn.astype(BF16)
    vb = v.astype(BF16)
    if expand is not None:
        knb = jnp.dot(knb, expand, preferred_element_type=F32).astype(BF16)
        vb = jnp.dot(vb, expand, preferred_element_type=F32).astype(BF16)
    gw = CTX_GROUP * HEAD_DIM
    masks = [_head_lane_mask(gw, hi) for hi in range(CTX_GROUP)]
    groups = []
    for gp in range(n_q // CTX_GROUP):
        cols = slice(gp * gw, (gp + 1) * gw)
        qh, kh, vh = qn[:, cols], knb[:, cols], vb[:, cols]
        kstack = jnp.concatenate([jnp.where(mk, kh, jnp.zeros_like(kh)) for mk in masks], axis=0)
        vstack = jnp.concatenate([jnp.where(mk, vh, jnp.zeros_like(vh)) for mk in masks], axis=0)
        s = lax.dot_general(qh, kstack, _NT, preferred_element_type=F32)
        ps, sink_terms = [], []
        for hi in range(CTX_GROUP):
            si = s[:, hi * s_len:(hi + 1) * s_len]
            m = jnp.max(si, axis=-1, keepdims=True)
            if sink_ref is not None:
                h = gp * CTX_GROUP + hi
                snk = sink_ref[0:1, h:h + 1]
                m = jnp.maximum(m, snk)
                sink_terms.append(jnp.where(masks[hi], jnp.exp(snk - m), 0.0))
            ps.append(jnp.exp(si - m).astype(BF16))
        p = jnp.concatenate(ps, axis=1)
        rhs = jnp.concatenate([vstack, ones_stack], axis=1)
        oe = jnp.dot(p, rhs, preferred_element_type=F32)
        l = oe[:, gw:]
        for term in sink_terms:
            l = l + term
        groups.append(oe[:, :gw] / l)
    return jnp.concatenate(groups, axis=-1) * _silu(g), kn


_CTX_INPUTS = 17


def _ctx_attn_kernel(*refs, s_len, n_alias):
    (aq_ref, ak_ref, av_ref, ag_ref, bq_ref, bk_ref, bv_ref, bg_ref,
     gqa_ref, gka_ref, sink_ref, gqb_ref, gkb_ref, bdw_ref, bdn_ref, exp_ref, ones_ref) = refs[:_CTX_INPUTS]
    ya_ref, yb_ref, ka_ref, va_ref, kb_ref, vb_ref = refs[_CTX_INPUTS + n_alias:]
    ones_stack = ones_ref[...]
    bdw = bdw_ref[...]
    for n in range(aq_ref.shape[0] // s_len):
        rows = slice(n * s_len, (n + 1) * s_len)
        va = av_ref[rows, :]
        vb = bv_ref[rows, :]
        va_ref[n, 0] = va.T
        vb_ref[n, 0] = vb.T
        ya, kna = _ctx_mixer(aq_ref[rows, :], ak_ref[rows, :], va, ag_ref[rows, :], gqa_ref[...], gka_ref[...],
                             bdw, bdn_ref[...], ones_stack, expand=exp_ref[...], sink_ref=sink_ref)
        yb, knb = _ctx_mixer(bq_ref[rows, :], bk_ref[rows, :], vb, bg_ref[rows, :], gqb_ref[...], gkb_ref[...],
                             bdw, bdw, ones_stack)
        ka_ref[n, 0] = kna.T
        kb_ref[n, 0] = knb.T
        ya_ref[rows, :] = ya.astype(ya_ref.dtype)
        yb_ref[rows, :] = yb.astype(yb_ref.dtype)


def _ctx_attn(proj, bsz, s_len, a_gq, a_gk, a_sink, b_gq, b_gk, layer, caches):
    per_layer = lambda width: pl.BlockSpec((None, 1, width), lambda b: (layer, 0, 0))
    kvw = A_KV_HEADS * HEAD_DIM
    grp = A_HEADS // A_KV_HEADS
    expand = np.zeros((kvw, BRANCH_W), np.float32)
    for h in range(A_HEADS):
        for d in range(HEAD_DIM):
            expand[(h // grp) * HEAD_DIM + d, h * HEAD_DIM + d] = 1.0
    ones_stack = np.zeros((CTX_GROUP * s_len, CTX_GROUP * HEAD_DIM), np.float32)
    for hi in range(CTX_GROUP):
        ones_stack[hi * s_len:(hi + 1) * s_len, hi * HEAD_DIM:(hi + 1) * HEAD_DIM] = 1.0
    const = lambda b: (0, 0)
    rows = CTX_SEQS * s_len
    wide = lambda c: pl.BlockSpec((rows, BRANCH_W), lambda b: (b, c // 4))
    narrow = lambda c: pl.BlockSpec((rows, kvw), lambda b: (b, c))
    prev = () if caches is None else tuple(caches)
    n_alias = len(prev)
    cache_spec = lambda width: pl.BlockSpec((CTX_SEQS, 1, width, s_len), lambda b: (b, layer, 0, 0))
    cache_shape = lambda width: jax.ShapeDtypeStruct((bsz, DEPTH, width, s_len), F32)
    outs = pl.pallas_call(
        functools.partial(_ctx_attn_kernel, s_len=s_len, n_alias=n_alias),
        grid=(bsz // CTX_SEQS,),
        in_specs=[
            wide(C_AQ), narrow(C_AK), narrow(C_AV), wide(C_AG), wide(C_BQ), wide(C_BK), wide(C_BV), wide(C_BG),
            per_layer(BRANCH_W), per_layer(kvw), per_layer(A_HEADS), per_layer(BRANCH_W), per_layer(BRANCH_W),
            pl.BlockSpec((BRANCH_W, BRANCH_W), const),
            pl.BlockSpec((kvw, kvw), const),
            pl.BlockSpec((kvw, BRANCH_W), const),
            pl.BlockSpec((CTX_GROUP * s_len, CTX_GROUP * HEAD_DIM), const),
        ] + [pl.BlockSpec(memory_space=pl.ANY)] * n_alias,
        out_specs=[
            pl.BlockSpec((rows, BRANCH_W), lambda b: (b, 0)),
            pl.BlockSpec((rows, BRANCH_W), lambda b: (b, 0)),
            cache_spec(kvw), cache_spec(kvw), cache_spec(BRANCH_W), cache_spec(BRANCH_W),
        ],
        out_shape=[
            jax.ShapeDtypeStruct((bsz * s_len, BRANCH_W), BF16),
            jax.ShapeDtypeStruct((bsz * s_len, BRANCH_W), BF16),
            cache_shape(kvw), cache_shape(kvw), cache_shape(BRANCH_W), cache_shape(BRANCH_W),
        ],
        input_output_aliases={_CTX_INPUTS + i: 2 + i for i in range(n_alias)},
        compiler_params=_cparams(("arbitrary",)),
        name="ctx_attn",
    )(proj, proj, proj, proj, proj, proj, proj, proj,
      a_gq, a_gk, a_sink, b_gq, b_gk, _head_mean_matrix(BRANCH_W), _head_mean_matrix(kvw), jnp.asarray(expand, dtype=BF16),
      jnp.asarray(ones_stack, dtype=BF16), *prev)
    return outs[0], outs[1], tuple(outs[2:])


def _rope(x, cos, sin_signed):
    w = x.shape[-1]
    lane = lax.broadcasted_iota(jnp.int32, x.shape, 1)
    up = pltpu.roll(x, w - 16, axis=1)
    dn = pltpu.roll(x, 16, axis=1)
    partner = jnp.where((lane & 16) == 0, up, dn)
    return x * cos + partner * sin_signed


WIN_SUB = 8


def _win_attn_kernel(q_ref, k_ref, v_ref, g_ref, kc_ref, vc_ref, cos_ref, sin_ref, gq_ref, gk_ref, sink_ref,
                     bdq_ref, bdk_ref, dup_ref, mask_ref, y_ref, kpad, vpad, kcx, vcx):
    j = pl.program_id(1)
    nb = pl.num_programs(1)
    t_len = k_ref.shape[0]
    grp = A_HEADS // A_KV_HEADS
    kvw = A_KV_HEADS * HEAD_DIM
    xw = 2 * kvw

    @pl.when(j == 0)
    def _():
        dup = dup_ref[...]
        kn = _heads_rms(k_ref[...], bdk_ref[...], gk_ref[...])
        kn = _rope(kn, cos_ref[:, 0:kvw], sin_ref[:, 0:kvw]).astype(BF16)
        def with_ones(vx):
            ones = jnp.ones((vx.shape[0], LANE), BF16)
            return jnp.concatenate(
                [part for kv in range(A_KV_HEADS) for part in (vx[:, kv * LANE:(kv + 1) * LANE], ones)], axis=1)

        kpad[0:A_BLOCK, :] = jnp.zeros((A_BLOCK, xw), BF16)
        kpad[A_BLOCK + t_len:2 * A_BLOCK + t_len, :] = jnp.zeros((A_BLOCK, xw), BF16)
        vpad[0:A_BLOCK, :] = jnp.zeros((A_BLOCK, 2 * xw), BF16)
        vpad[A_BLOCK + t_len:2 * A_BLOCK + t_len, :] = jnp.zeros((A_BLOCK, 2 * xw), BF16)
        kpad[A_BLOCK:A_BLOCK + t_len, :] = jnp.dot(kn, dup, preferred_element_type=F32).astype(BF16)
        vpad[A_BLOCK:A_BLOCK + t_len, :] = with_ones(
            jnp.dot(v_ref[...].astype(BF16), dup, preferred_element_type=F32).astype(BF16))
        kcx[...] = jnp.dot(kc_ref[0, 0].T.astype(BF16), dup, preferred_element_type=F32).astype(BF16)
        vcx[...] = with_ones(jnp.dot(vc_ref[0, 0].T.astype(BF16), dup, preferred_element_type=F32).astype(BF16))

    nloc = 3 * A_BLOCK
    low_half = lax.broadcasted_iota(jnp.int32, (1, LANE), 1) < HEAD_DIM
    for sub in range(WIN_SUB):
        jj = j * WIN_SUB + sub
        qrows = slice(sub * A_BLOCK, (sub + 1) * A_BLOCK)
        r0 = pl.multiple_of(jj * A_BLOCK, A_BLOCK)
        qn = _heads_rms(q_ref[qrows, :], bdq_ref[...], gq_ref[...])
        qb = (_rope(qn, cos_ref[pl.ds(r0, A_BLOCK), :], sin_ref[pl.ds(r0, A_BLOCK), :]) * QK_SCALE).astype(BF16)
        maskadd = mask_ref[jnp.where(jj == 0, 0, jnp.where(jj == nb * WIN_SUB - 1, 2, 1))]
        kband = kpad[pl.ds(r0, nloc), :]
        vband = vpad[pl.ds(r0, nloc), :]
        pairs = []
        for kv in range(A_KV_HEADS):
            cols = slice(kv * LANE, (kv + 1) * LANE)
            qparts, sinks = [], []
            for gi in range(grp):
                h = kv * grp + gi
                qpair = qb[:, (h // 2) * LANE:(h // 2 + 1) * LANE]
                keep = low_half if h % 2 == 0 else jnp.logical_not(low_half)
                qparts.append(jnp.where(keep, qpair, jnp.zeros_like(qpair)))
                sinks.append(jnp.broadcast_to(sink_ref[0:1, h:h + 1], (A_BLOCK, 1)))
            qst = jnp.concatenate(qparts, axis=0)
            snk = jnp.concatenate(sinks, axis=0)
            s_loc = lax.dot_general(qst, kband[:, cols], _NT, preferred_element_type=F32) + maskadd
            s_ctx = lax.dot_general(qst, kcx[:, cols], _NT, preferred_element_type=F32)
            m = jnp.maximum(jnp.maximum(jnp.max(s_loc, axis=-1, keepdims=True),
                                        jnp.max(s_ctx, axis=-1, keepdims=True)), snk)
            p_loc = jnp.exp(s_loc - m).astype(BF16)
            p_ctx = jnp.exp(s_ctx - m).astype(BF16)
            wide = slice(kv * MXU_DIM, (kv + 1) * MXU_DIM)
            oe = (jnp.dot(p_loc, vband[:, wide], preferred_element_type=F32)
                  + jnp.dot(p_ctx, vcx[:, wide], preferred_element_type=F32))
            o = oe[:, :LANE] / (oe[:, LANE:] + jnp.exp(snk - m))
            for k2 in range(grp // 2):
                even = o[(2 * k2) * A_BLOCK:(2 * k2 + 1) * A_BLOCK]
                odd = o[(2 * k2 + 1) * A_BLOCK:(2 * k2 + 2) * A_BLOCK]
                pairs.append(jnp.where(low_half, even, odd))
        y = jnp.concatenate(pairs, axis=-1) * _silu(g_ref[qrows, :])
        y_ref[qrows, :] = y.astype(y_ref.dtype)


def _win_mask(grp, nb):
    assert nb >= 2
    r = np.arange(A_BLOCK)[:, None]
    c = np.arange(3 * A_BLOCK)[None, :]
    band = np.abs(r + A_BLOCK - c) <= A_WINDOW
    variants = [band & (c >= A_BLOCK), band, band & (c < 2 * A_BLOCK)]
    return np.stack([np.tile(np.where(v, 0.0, NEG_INF).astype(np.float32), (grp, 1)) for v in variants])


def _win_attn(proj, cache_k, cache_v, layer, cos, sin_signed, gq, gk, sink, bsz, t_len):
    nb = t_len // A_BLOCK
    assert nb % WIN_SUB == 0
    nsteps = nb // WIN_SUB
    kvw = A_KV_HEADS * HEAD_DIM
    grp = A_HEADS // A_KV_HEADS
    p_len = cache_k.shape[3]
    dup = np.zeros((kvw, 2 * kvw), np.float32)
    for kv in range(A_KV_HEADS):
        for half in range(2):
            for d in range(HEAD_DIM):
                dup[kv * HEAD_DIM + d, kv * LANE + half * HEAD_DIM + d] = 1.0
    const2 = lambda b, j: (0, 0)
    return pl.pallas_call(
        _win_attn_kernel,
        grid=(bsz, nsteps),
        in_specs=[
            pl.BlockSpec((WIN_SUB * A_BLOCK, BRANCH_W), lambda b, j: (b * nsteps + j, C_AQ // 4)),
            pl.BlockSpec((t_len, kvw), lambda b, j: (b, C_AK)),
            pl.BlockSpec((t_len, kvw), lambda b, j: (b, C_AV)),
            pl.BlockSpec((WIN_SUB * A_BLOCK, BRANCH_W), lambda b, j: (b * nsteps + j, C_AG // 4)),
            pl.BlockSpec((1, 1, kvw, p_len), lambda b, j: (b, layer, 0, 0)),
            pl.BlockSpec((1, 1, kvw, p_len), lambda b, j: (b, layer, 0, 0)),
            pl.BlockSpec((t_len, BRANCH_W), const2),
            pl.BlockSpec((t_len, BRANCH_W), const2),
            pl.BlockSpec((None, 1, BRANCH_W), lambda b, j: (layer, 0, 0)),
            pl.BlockSpec((None, 1, kvw), lambda b, j: (layer, 0, 0)),
            pl.BlockSpec((None, 1, A_HEADS), lambda b, j: (layer, 0, 0)),
            pl.BlockSpec((BRANCH_W, BRANCH_W), const2),
            pl.BlockSpec((kvw, kvw), const2),
            pl.BlockSpec((kvw, 2 * kvw), const2),
            pl.BlockSpec((3, grp * A_BLOCK, 3 * A_BLOCK), lambda b, j: (0, 0, 0)),
        ],
        out_specs=pl.BlockSpec((WIN_SUB * A_BLOCK, BRANCH_W), lambda b, j: (b * nsteps + j, 0)),
        out_shape=jax.ShapeDtypeStruct((bsz * t_len, BRANCH_W), BF16),
        scratch_shapes=[pltpu.VMEM((t_len + 2 * A_BLOCK, 2 * kvw), BF16),
                        pltpu.VMEM((t_len + 2 * A_BLOCK, 4 * kvw), BF16),
                        pltpu.VMEM((p_len, 2 * kvw), BF16),
                        pltpu.VMEM((p_len, 4 * kvw), BF16)],
        compiler_params=_cparams(("arbitrary", "arbitrary")),
        name="win_attn",
    )(proj, proj, proj, proj, cache_k, cache_v, cos, sin_signed,
      gq, gk, sink, _head_mean_matrix(BRANCH_W), _head_mean_matrix(kvw),
      jnp.asarray(dup, dtype=BF16), jnp.asarray(_win_mask(grp, nb)))


NBR_QB = 2 * GRID_W
NBR_BAND = 5


def _nbr_plan(rows):
    kh = min(NB_ROWS, rows)
    nblk = rows // 2
    specs, plan, starts = {}, [], []
    for i in range(nblk):
        s0 = min(max(i - 2, 0), nblk - NBR_BAND)
        starts.append(s0)
        blk = []
        for a in range(2):
            qr = 2 * i + a
            rs = min(max(qr - kh // 2, 0), rows - kh)
            assert 2 * s0 <= rs and rs + kh <= 2 * (s0 + NBR_BAND)
            row = []
            for p in range(NBR_BAND):
                pair = tuple(kr - qr + NB_ROWS - 1 if rs <= kr < rs + kh else None
                             for kr in (2 * (s0 + p), 2 * (s0 + p) + 1))
                row.append(specs.setdefault(pair, len(specs)))
            blk.append(row)
        plan.append(blk)
    return tuple(starts), plan, list(specs)


def _nbr_table(rpb, specs):
    heads = rpb.shape[0]
    c = np.arange(GRID_W)
    cs = np.clip(c - NB_COLS // 2, 0, GRID_W - NB_COLS)
    col_ok = (c[None, :] >= cs[:, None]) & (c[None, :] < cs[:, None] + NB_COLS)
    dc = c[None, :] - c[:, None] + NB_COLS - 1
    onehot = ((dc[None] == np.arange(2 * NB_COLS - 1)[:, None, None]) & col_ok[None]).astype(np.float32)
    shifted = jnp.einsum('hdj,jqk->hdqk', rpb.astype(F32), jnp.asarray(onehot), precision=lax.Precision.HIGHEST)
    by_col = jnp.where(col_ok[None, None], shifted, NEG_INF)
    neg = jnp.full((heads, GRID_W, GRID_W), NEG_INF, F32)
    blocks = [jnp.concatenate([neg if d is None else by_col[:, d] for d in spec], axis=-1) for spec in specs]
    return jnp.stack(blocks, axis=1)


NBR_GROUPS = 4


def _nbr_attn_kernel(q_ref, k_ref, v_ref, g_ref, kc_ref, vc_ref, tb_ref, gq_ref, gk_ref, bd_ref, y_ref,
                     *, starts, plan):
    nband = NBR_BAND * NBR_QB
    bd = bd_ref[...]
    hpg = LANE // HEAD_DIM
    for gs in range(NBR_GROUPS):
        lanes = slice(gs * LANE, (gs + 1) * LANE)
        qn = (_heads_rms(q_ref[:, lanes], bd, gq_ref[...]) * QK_SCALE).astype(BF16)
        kn = _heads_rms(k_ref[:, lanes], bd, gk_ref[...]).astype(BF16)
        vb = v_ref[:, lanes].astype(BF16)
        kcb = kc_ref[0, 0, lanes, :].T.astype(BF16)
        vcb = vc_ref[0, 0, lanes, :].T.astype(BF16)
        acc = [jnp.zeros((NBR_QB, LANE), F32) for _ in starts]
        for h in range(hpg):
            hm = _head_lane_mask(LANE, h)
            km = jnp.where(hm, kn, jnp.zeros_like(kn))
            kcm = jnp.where(hm, kcb, jnp.zeros_like(kcb))
            vm = jnp.concatenate([jnp.where(hm, vb, jnp.zeros_like(vb)), jnp.ones_like(vb)], axis=1)
            vcm = jnp.concatenate([jnp.where(hm, vcb, jnp.zeros_like(vcb)), jnp.ones_like(vcb)], axis=1)
            for i, s0 in enumerate(starts):
                qi = qn[i * NBR_QB:(i + 1) * NBR_QB]
                ks = slice(s0 * NBR_QB, s0 * NBR_QB + nband)
                s_raw = lax.dot_general(qi, km[ks], _NT, preferred_element_type=F32)
                s_loc = jnp.concatenate(
                    [jnp.concatenate([s_raw[a * GRID_W:(a + 1) * GRID_W, p * LANE:(p + 1) * LANE]
                                      + tb_ref[gs * hpg + h, plan[i][a][p]] for p in range(NBR_BAND)], axis=1)
                     for a in range(2)], axis=0)
                s_ctx = lax.dot_general(qi, kcm, _NT, preferred_element_type=F32)
                m = jnp.maximum(jnp.max(s_loc, axis=-1, keepdims=True), jnp.max(s_ctx, axis=-1, keepdims=True))
                p_loc = jnp.exp(s_loc - m).astype(BF16)
                p_ctx = jnp.exp(s_ctx - m).astype(BF16)
                oe = (jnp.dot(p_loc, vm[ks], preferred_element_type=F32)
                      + jnp.dot(p_ctx, vcm, preferred_element_type=F32))
                acc[i] = acc[i] + oe[:, :LANE] / oe[:, LANE:]
        y = jnp.concatenate(acc, axis=0) * _silu(g_ref[:, lanes])
        y_ref[:, lanes] = y.astype(y_ref.dtype)


def _nbr_attn(proj, cache_k, cache_v, layer, table, starts, plan, gq, gk, bsz, t_len):
    hp = NBR_GROUPS * (LANE // HEAD_DIM)
    gw = NBR_GROUPS * LANE
    nhp = B_HEADS // hp
    p_len = cache_k.shape[3]
    const = lambda h, b: (0, 0)
    kern = functools.partial(_nbr_attn_kernel, starts=starts, plan=plan)
    return pl.pallas_call(
        kern,
        grid=(nhp, bsz),
        in_specs=[
            pl.BlockSpec((t_len, gw), lambda h, b: (b, C_BQ // NBR_GROUPS + h)),
            pl.BlockSpec((t_len, gw), lambda h, b: (b, C_BK // NBR_GROUPS + h)),
            pl.BlockSpec((t_len, gw), lambda h, b: (b, C_BV // NBR_GROUPS + h)),
            pl.BlockSpec((t_len, gw), lambda h, b: (b, C_BG // NBR_GROUPS + h)),
            pl.BlockSpec((1, 1, gw, p_len), lambda h, b: (b, layer, h, 0)),
            pl.BlockSpec((1, 1, gw, p_len), lambda h, b: (b, layer, h, 0)),
            pl.BlockSpec((hp,) + table.shape[1:], lambda h, b: (layer * nhp + h, 0, 0, 0)),
            pl.BlockSpec((None, 1, LANE), lambda h, b: (layer, 0, 0)),
            pl.BlockSpec((None, 1, LANE), lambda h, b: (layer, 0, 0)),
            pl.BlockSpec((LANE, LANE), const),
        ],
        out_specs=pl.BlockSpec((t_len, gw), lambda h, b: (b, h)),
        out_shape=jax.ShapeDtypeStruct((bsz * t_len, BRANCH_W), BF16),
        compiler_params=_cparams(("arbitrary", "arbitrary")),
        name="nbr_attn",
    )(proj, proj, proj, proj, cache_k, cache_v, table, gq, gk, _head_mean_matrix(LANE))


LRU_CHUNKS = SUBLANE
LRU_PITCH_PAD = 4


def _lru_kernel(cx_ref, cg_ref, h0_ref, cw_ref, cb_ref, wg_ref, bg_ref, lam_ref,
                y_ref, st_ref, a_s, u_s, h_s, p_s, y_s):
    t_len = cx_ref.shape[0]
    w = LRU_WIDTH
    cx = cx_ref[...]
    taps = (pltpu.roll(cx, 2, axis=0), pltpu.roll(cx, 1, axis=0), cx, pltpu.roll(cx, t_len - 1, axis=0))
    edge_row = lax.broadcasted_iota(jnp.int32, (SUBLANE, w), 0)

    def conv_rows(rows, keep):
        acc = cb_ref[...] + taps[2][rows] * cw_ref[2:3, :]
        for j in (0, 1, 3):
            tap = taps[j][rows]
            if keep[j] is not None:
                tap = jnp.where(keep[j], tap, 0.0)
            acc = acc + tap * cw_ref[j:j + 1, :]
        return acc

    xc = jnp.concatenate([
        conv_rows(slice(0, SUBLANE), (edge_row >= 2, edge_row >= 1, None, None)),
        conv_rows(slice(SUBLANE, t_len - SUBLANE), (None, None, None, None)),
        conv_rows(slice(t_len - SUBLANE, t_len), (None, None, None, edge_row < SUBLANE - 1)),
    ], axis=0)

    gates = jnp.dot(xc.astype(BF16), wg_ref[...], preferred_element_type=F32) + bg_ref[...]
    coeffs = []
    for d in range(2):
        th_r = jnp.tanh(gates[:, d * w:(d + 1) * w])
        th_i = jnp.tanh(gates[:, (2 + d) * w:(3 + d) * w])
        nl = -lam_ref[d:d + 1, :]
        softplus = jnp.maximum(nl, 0.0) + jnp.log1p(jnp.exp(-jnp.abs(nl)))
        quarter_c = (-0.25 * LRU_C) * softplus
        half_log_a = quarter_c * th_r + quarter_c
        t = jnp.tanh(half_log_a)
        rc = 1.0 / (1.0 - t)
        coeffs.append(((1.0 + t) * rc,
                       jnp.sqrt(-t) * rc * (1.0 + th_i) * xc))

    chunk_len = t_len // LRU_CHUNKS
    pitch = chunk_len + LRU_PITCH_PAD
    nt = w // LANE
    for d, (a_val, u_val) in enumerate(coeffs):
        for c in range(LRU_CHUNKS):
            for k in range(nt):
                dst = slice(c * pitch, c * pitch + chunk_len)
                src = (slice(c * chunk_len, (c + 1) * chunk_len), slice(k * LANE, (k + 1) * LANE))
                a_s[d, k, dst, :] = a_val[src]
                u_s[d, k, dst, :] = u_val[src]

    def scan_body(s, carry):
        hs, ps = carry
        new_h, new_p = [], []
        for d in range(2):
            pos = s if d == 0 else chunk_len - 1 - s
            idx = pl.ds(pos, LRU_CHUNKS, stride=pitch)
            for k in range(nt):
                av = a_s[d, k, idx, :]
                h = av * hs[d * nt + k] + u_s[d, k, idx, :]
                p = av * ps[d * nt + k]
                h_s[d, k, idx, :] = h
                p_s[d, k, idx, :] = p
                new_h.append(h)
                new_p.append(p)
        return tuple(new_h), tuple(new_p)

    zero = jnp.zeros((LRU_CHUNKS, LANE), F32)
    one = jnp.ones((LRU_CHUNKS, LANE), F32)
    h_end, p_end = lax.fori_loop(0, chunk_len, scan_body, ((zero,) * (2 * nt), (one,) * (2 * nt)), unroll=4)

    h0 = h0_ref[0]
    enter, finals = [], []
    for d in range(2):
        order = range(LRU_CHUNKS) if d == 0 else range(LRU_CHUNKS - 1, -1, -1)
        for k in range(nt):
            he, pe = h_end[d * nt + k], p_end[d * nt + k]
            e = h0[d:d + 1, k * LANE:(k + 1) * LANE]
            rows = [None] * LRU_CHUNKS
            for c in order:
                rows[c] = e
                e = pe[c:c + 1, :] * e + he[c:c + 1, :]
            enter.append(jnp.concatenate(rows, axis=0))
            finals.append(e)
    st_ref[0] = jnp.concatenate([jnp.concatenate(finals[d * nt:(d + 1) * nt], axis=1) for d in range(2)], axis=0)

    def fix_body(s, carry):
        idx = pl.ds(s, LRU_CHUNKS, stride=pitch)
        for k in range(nt):
            hf = h_s[0, k, idx, :] + p_s[0, k, idx, :] * enter[k]
            hb = h_s[1, k, idx, :] + p_s[1, k, idx, :] * enter[nt + k]
            y_s[k, idx, :] = hf + hb
        return carry

    lax.fori_loop(0, chunk_len, fix_body, 0, unroll=4)
    hsum = jnp.concatenate(
        [jnp.concatenate([y_s[k, c * pitch:c * pitch + chunk_len, :] for c in range(LRU_CHUNKS)], axis=0)
         for k in range(nt)], axis=1)
    y_ref[...] = (hsum * _silu(cg_ref[...])).astype(y_ref.dtype)


def _lru(proj, h0, h0_layer, conv_w, conv_b, w_gates, b_gates, lam, layer, bsz, t_len):
    w = LRU_WIDTH
    assert t_len % (LRU_CHUNKS * SUBLANE) == 0
    rows_pad = LRU_CHUNKS * (t_len // LRU_CHUNKS + LRU_PITCH_PAD)
    return pl.pallas_call(
        _lru_kernel,
        grid=(bsz,),
        in_specs=[
            pl.BlockSpec((t_len, w), lambda b: (b, C_CX // 4)),
            pl.BlockSpec((t_len, w), lambda b: (b, C_CG // 4)),
            pl.BlockSpec((1, None, 2, w), lambda b: (b, h0_layer, 0, 0)),
            pl.BlockSpec((None, CONV_W, w), lambda b: (layer, 0, 0)),
            pl.BlockSpec((None, 1, w), lambda b: (layer, 0, 0)),
            pl.BlockSpec((None, w, 4 * w), lambda b: (layer, 0, 0)),
            pl.BlockSpec((None, 1, 4 * w), lambda b: (layer, 0, 0)),
            pl.BlockSpec((None, 2, w), lambda b: (layer, 0, 0)),
        ],
        out_specs=[
            pl.BlockSpec((t_len, w), lambda b: (b, 0)),
            pl.BlockSpec((1, 2, w), lambda b: (b, 0, 0)),
        ],
        out_shape=[
            jax.ShapeDtypeStruct((bsz * t_len, w), BF16),
            jax.ShapeDtypeStruct((bsz, 2, w), F32),
        ],
        scratch_shapes=[pltpu.VMEM((2, w // LANE, rows_pad, LANE), F32) for _ in range(4)]
        + [pltpu.VMEM((w // LANE, rows_pad, LANE), F32)],
        compiler_params=_cparams(("arbitrary",)),
        name="lru",
    )(proj, proj, h0, conv_w, conv_b, w_gates, b_gates, lam)


def _lru_gate_weights(wa, ba, wx, bx):
    def dense(wblk):
        rows = wblk.transpose(0, 2, 3, 1, 4)[:, :, :, :, None, :]
        same_block = jnp.eye(LRU_BLOCKS, dtype=bool)[None, :, None, None, :, None]
        full = jnp.where(same_block, rows, 0.0)
        return full.reshape(DEPTH, LRU_WIDTH, 2 * LRU_WIDTH)
    wg = jnp.concatenate([dense(wa), dense(wx)], axis=2)
    bg = jnp.concatenate([ba.reshape(DEPTH, 1, 2 * LRU_WIDTH), bx.reshape(DEPTH, 1, 2 * LRU_WIDTH)], axis=2)
    return (0.5 * wg).astype(BF16), 0.5 * bg


def _merge_kernel(x_ref, ya_ref, yb_ref, yc_ref, mod_ref, g_ref, wgate_ref, wbr_ref, wout_ref, o_ref):
    x = x_ref[...]
    h = _modulated_norm(x, g_ref, mod_ref)
    gates = jnp.dot(h, wgate_ref[0], preferred_element_type=F32)
    z = None
    for k, y_ref in enumerate((ya_ref, yb_ref, yc_ref)):
        term = (_sigmoid(gates[:, k * D_MODEL:(k + 1) * D_MODEL])
                * jnp.dot(y_ref[...], wbr_ref[0, k], preferred_element_type=F32))
        z = term if z is None else z + term
    out = jnp.dot(z.astype(BF16), wout_ref[0], preferred_element_type=F32)
    o_ref[...] = x + mod_ref[0, 0, 2:3, :] * out


def _merge(x2d, ya, yb, yc, mods, norm_g, w_gate, w_br, w_out, layer, rows_per_mod, mod_row0):
    tokens = x2d.shape[0]
    tm = 1024
    assert rows_per_mod % tm == 0
    tiles_per_mod = rows_per_mod // tm if rows_per_mod else 0

    def mod_idx(i):
        if rows_per_mod:
            return (layer, mod_row0 + i // tiles_per_mod, 0, 0)
        return (layer, mod_row0, 0, 0)

    return pl.pallas_call(
        _merge_kernel,
        grid=(tokens // tm,),
        in_specs=[
            pl.BlockSpec((tm, D_MODEL), lambda i: (i, 0)),
            pl.BlockSpec((tm, BRANCH_W), lambda i: (i, 0)),
            pl.BlockSpec((tm, BRANCH_W), lambda i: (i, 0)),
            pl.BlockSpec((tm, BRANCH_W), lambda i: (i, 0)),
            pl.BlockSpec((1, 1, 3, D_MODEL), mod_idx),
            pl.BlockSpec((1, 1, D_MODEL), lambda i: (layer, 0, 0)),
            pl.BlockSpec((1, D_MODEL, GATE_COLS), lambda i: (layer, 0, 0)),
            pl.BlockSpec((1, 3, BRANCH_W, D_MODEL), lambda i: (layer, 0, 0, 0)),
            pl.BlockSpec((1, D_MODEL, D_MODEL), lambda i: (layer, 0, 0)),
        ],
        out_specs=pl.BlockSpec((tm, D_MODEL), lambda i: (i, 0)),
        out_shape=jax.ShapeDtypeStruct((tokens, D_MODEL), F32),
        compiler_params=_cparams(("arbitrary",)),
        name="merge",
    )(x2d, ya, yb, yc, mods, norm_g.reshape(DEPTH, 1, D_MODEL), w_gate, w_br, w_out)


def _rope_tables(t_len):
    t = jnp.arange(t_len)
    m = HEAD_DIM // 4
    freqs = ROPE_BASE ** (-jnp.arange(m, dtype=F32) / m)
    ang_r = (t // GRID_W).astype(F32)[:, None] * freqs[None, :]
    ang_c = (t % GRID_W).astype(F32)[:, None] * freqs[None, :]
    cos = jnp.concatenate([jnp.cos(ang_r), jnp.cos(ang_r), jnp.cos(ang_c), jnp.cos(ang_c)], axis=-1)
    sin = jnp.concatenate([-jnp.sin(ang_r), jnp.sin(ang_r), -jnp.sin(ang_c), jnp.sin(ang_c)], axis=-1)
    reps = BRANCH_W // HEAD_DIM
    return jnp.tile(cos, (1, reps)), jnp.tile(sin, (1, reps))


def kernel(x_prompt, x_sample, cache_ka, cache_va, cache_kb, cache_vb, state_lru, c, c_ctx,
           norm_g, w_ada, b_ada, w_in, a_q_norm, a_k_norm, a_sink, b_q_norm, b_k_norm, b_rpb,
           lru_conv_w, lru_conv_b, lru_wa, lru_ba, lru_wx, lru_bx, lru_lambda, w_branch, w_out):
    bsz, s_len, _ = x_prompt.shape
    dbsz, t_len, _ = x_sample.shape
    p_len = cache_ka.shape[2]

    n_mod = 16
    ctx_row = dbsz
    cond = jnp.zeros((n_mod, D_MODEL), F32).at[:dbsz].set(c).at[ctx_row].set(c_ctx)
    mods = _mods(cond, w_ada, b_ada).reshape(DEPTH, n_mod, 3, D_MODEL)

    w_perm = w_gate = _wprep(w_in)
    w_br = w_branch.astype(BF16)
    w_o = w_out.astype(BF16)
    cos, sin_signed = _rope_tables(t_len)

    def cache_t(x):
        return x.transpose(0, 1, 3, 4, 2).reshape(x.shape[0], DEPTH, x.shape[3] * HEAD_DIM, p_len)
    cka, cva, ckb, cvb = cache_t(cache_ka), cache_t(cache_va), cache_t(cache_kb), cache_t(cache_vb)

    def gain_rows(g, reps):
        return jnp.tile(g, (1, reps)).reshape(DEPTH, 1, reps * HEAD_DIM)
    a_gq, a_gk = gain_rows(a_q_norm, A_HEADS), gain_rows(a_k_norm, A_KV_HEADS)
    b_gq, b_gk = gain_rows(b_q_norm, B_HEADS), gain_rows(b_k_norm, B_HEADS)
    b_gq2, b_gk2 = gain_rows(b_q_norm, LANE // HEAD_DIM), gain_rows(b_k_norm, LANE // HEAD_DIM)
    sinks = a_sink.reshape(DEPTH, 1, A_HEADS)
    wg, bg = _lru_gate_weights(lru_wa, lru_ba, lru_wx, lru_bx)
    conv_b = lru_conv_b.reshape(DEPTH, 1, LRU_WIDTH)
    grid_rows = t_len // GRID_W
    assert grid_rows % 2 == 0 and grid_rows // 2 >= NBR_BAND
    nbr_starts, nbr_plan, nbr_specs = _nbr_plan(grid_rows)
    nbr_table = _nbr_table(b_rpb.reshape((DEPTH * B_HEADS,) + b_rpb.shape[2:]), nbr_specs)

    yp = x_prompt.reshape(bsz * s_len, D_MODEL)
    ys = x_sample.reshape(dbsz * t_len, D_MODEL)
    zero_state = jnp.zeros((bsz, 1, 2, LRU_WIDTH), F32)
    new_caches, new_lru = None, []
    for l in range(DEPTH):
        proj = _inproj(yp, mods, norm_g, w_perm, l, 0, ctx_row)
        ya, yb, new_caches = _ctx_attn(proj, bsz, s_len, a_gq, a_gk, sinks, b_gq, b_gk, l, new_caches)
        yc, st = _lru(proj, zero_state, 0, lru_conv_w, conv_b, wg, bg, lru_lambda, l, bsz, s_len)
        yp = _merge(yp, ya, yb, yc, mods, norm_g, w_gate, w_br, w_o, l, 0, ctx_row)
        new_lru.append(st)

        proj = _inproj(ys, mods, norm_g, w_perm, l, t_len, 0)
        ya = _win_attn(proj, cka, cva, l, cos, sin_signed, a_gq, a_gk, sinks, dbsz, t_len)
        yb = _nbr_attn(proj, ckb, cvb, l, nbr_table, nbr_starts, nbr_plan, b_gq2, b_gk2, dbsz, t_len)
        yc, _ = _lru(proj, state_lru, l, lru_conv_w, conv_b, wg, bg, lru_lambda, l, dbsz, t_len)
        ys = _merge(ys, ya, yb, yc, mods, norm_g, w_gate, w_br, w_o, l, t_len, 0)

    def cache_out(x):
        return x.reshape(bsz, DEPTH, x.shape[2] // HEAD_DIM, HEAD_DIM, s_len).transpose(0, 1, 4, 2, 3)

    return (yp.reshape(bsz, s_len, D_MODEL), ys.reshape(dbsz, t_len, D_MODEL),
            *(cache_out(x) for x in new_caches), jnp.stack(new_lru, axis=1))
```

```python
import functools

import numpy as np
import jax
import jax.numpy as jnp
from jax import lax
from jax.experimental import pallas as pl
from jax.experimental.pallas import tpu as pltpu

F32 = jnp.float32
BF16 = jnp.bfloat16

D_MODEL = 1024
DEPTH = 2
GRID_W = 64
HEAD_DIM = 64
BRANCH_W = 512
A_HEADS = 8
A_KV_HEADS = 2
A_WINDOW = 128
A_BLOCK = 128
B_HEADS = 8
NB_ROWS = 8
NB_COLS = 16
LRU_WIDTH = 512
LRU_BLOCKS = 8
LRU_BW = LRU_WIDTH // LRU_BLOCKS
LRU_C = 8.0
CONV_W = 4
ROPE_BASE = 10000.0
EPS = 1e-6
NEG_INF = -1e30
QK_SCALE = HEAD_DIM ** -0.5

LANE = 128
SUBLANE = 8
MXU_DIM = 256
VMEM_LIMIT = 56 * 1024 * 1024

_ORIG_SPLITS = (512, 128, 128, 512, 512, 512, 512, 512, 512, 512, 1024, 1024, 1024)
_ORIG_OFFS = tuple(int(v) for v in np.cumsum((0,) + _ORIG_SPLITS)[:-1])
_PERM = (0, 3, 4, 5, 6, 7, 8, 9, 1, 2)
_GATE_SPLITS = (10, 11, 12)
IN_COLS = sum(_ORIG_SPLITS[k] for k in _PERM)
GATE_COLS = sum(_ORIG_SPLITS[k] for k in _GATE_SPLITS)
C_AQ, C_AG, C_BQ, C_BK, C_BV, C_BG, C_CX, C_CG, C_AK, C_AV = (0, 4, 8, 12, 16, 20, 24, 28, 32, 33)

_NT = (((1,), (1,)), ((), ()))


def _cparams(sem):
    return pltpu.CompilerParams(dimension_semantics=sem, vmem_limit_bytes=VMEM_LIMIT)


def _sigmoid(x):
    return 0.5 + 0.5 * jnp.tanh(0.5 * x)


def _silu(x):
    return x * _sigmoid(x)


def _head_mean_matrix(width):
    idx = np.arange(width) // HEAD_DIM
    return jnp.asarray((idx[:, None] == idx[None, :]).astype(np.float32) / HEAD_DIM, dtype=BF16)


def _heads_rms(x, bd, g):
    x2 = x * x
    hi = x2.astype(BF16)
    lo = (x2 - hi.astype(F32)).astype(BF16)
    width = x.shape[-1]
    step = min(width, MXU_DIM)
    tile = bd[:step, :step]
    ms = jnp.concatenate(
        [jnp.dot(hi[:, c:c + step], tile, preferred_element_type=F32)
         + jnp.dot(lo[:, c:c + step], tile, preferred_element_type=F32) for c in range(0, width, step)], axis=-1)
    return x * lax.rsqrt(ms + EPS) * g


def _head_lane_mask(width, h):
    lane = lax.broadcasted_iota(jnp.int32, (1, width), 1)
    return (lane >= h * HEAD_DIM) & (lane < (h + 1) * HEAD_DIM)


def _mods_kernel(c_ref, w_ref, b_ref, o_ref):
    c = c_ref[...]
    s = _silu(c).astype(BF16)
    o_ref[0] = jnp.dot(s, w_ref[0].astype(BF16), preferred_element_type=F32) + b_ref[0]


def _mods(cond, w_ada, b_ada):
    n = cond.shape[0]
    tn = D_MODEL
    return pl.pallas_call(
        _mods_kernel,
        grid=(DEPTH, 3 * D_MODEL // tn),
        in_specs=[
            pl.BlockSpec((n, D_MODEL), lambda l, j: (0, 0)),
            pl.BlockSpec((1, D_MODEL, tn), lambda l, j: (l, 0, j)),
            pl.BlockSpec((1, 1, tn), lambda l, j: (l, 0, j)),
        ],
        out_specs=pl.BlockSpec((1, n, tn), lambda l, j: (l, 0, j)),
        out_shape=jax.ShapeDtypeStruct((DEPTH, n, 3 * D_MODEL), F32),
        compiler_params=_cparams(("arbitrary", "arbitrary")),
        name="mods",
    )(cond, w_ada, b_ada.reshape(DEPTH, 1, 3 * D_MODEL))


W_MIXER_OFF = IN_COLS
W_PREP_COLS = 2 * IN_COLS
W_PREP_ROWS = 256
_AK0, _AG0, _GATE0 = _ORIG_OFFS[1], _ORIG_OFFS[3], _ORIG_OFFS[_GATE_SPLITS[0]]


def _wprep_kernel(w_ref, o_ref):
    w = w_ref[0]
    rows = w.shape[0]
    pieces = (w[:, _GATE0:_GATE0 + GATE_COLS],
              jnp.zeros((rows, W_MIXER_OFF - GATE_COLS), F32),
              w[:, 0:_AK0],
              w[:, _AG0:_GATE0],
              w[:, _AK0:_AG0])
    col = 0
    for piece in pieces:
        o_ref[0, :, col:col + piece.shape[1]] = piece.astype(BF16)
        col += piece.shape[1]


def _wprep(w_in):
    assert _PERM == (0, 3, 4, 5, 6, 7, 8, 9, 1, 2) and _AK0 % LANE == 0 and _AG0 % LANE == 0 and _GATE0 == IN_COLS
    return pl.pallas_call(
        _wprep_kernel,
        grid=(DEPTH, D_MODEL // W_PREP_ROWS),
        in_specs=[pl.BlockSpec((1, W_PREP_ROWS, w_in.shape[2]), lambda l, r: (l, r, 0))],
        out_specs=pl.BlockSpec((1, W_PREP_ROWS, W_PREP_COLS), lambda l, r: (l, r, 0)),
        out_shape=jax.ShapeDtypeStruct((DEPTH, D_MODEL, W_PREP_COLS), BF16),
        compiler_params=_cparams(("arbitrary", "arbitrary")),
        name="wprep",
    )(w_in)


def _modulated_norm(x, g_ref, mod_ref):
    y = x * lax.rsqrt(jnp.mean(x * x, axis=-1, keepdims=True) + EPS)
    y = y * g_ref[0]
    shift = mod_ref[0, 0, 0:1, :]
    scale = mod_ref[0, 0, 1:2, :]
    return (y * (1.0 + scale) + shift).astype(BF16)


def _inproj_kernel(x_ref, mod_ref, g_ref, w_ref, o_ref):
    h = _modulated_norm(x_ref[...], g_ref, mod_ref)
    o_ref[...] = jnp.dot(h, w_ref[0], preferred_element_type=F32)


def _inproj(x2d, mods, norm_g, w_perm, layer, rows_per_mod, mod_row0):
    tokens = x2d.shape[0]
    tm = 512
    tiles_per_mod = rows_per_mod // tm if rows_per_mod else 0

    def mod_idx(i):
        if rows_per_mod:
            return (layer, mod_row0 + i // tiles_per_mod, 0, 0)
        return (layer, mod_row0, 0, 0)

    return pl.pallas_call(
        _inproj_kernel,
        grid=(tokens // tm,),
        in_specs=[
            pl.BlockSpec((tm, D_MODEL), lambda i: (i, 0)),
            pl.BlockSpec((1, 1, 3, D_MODEL), mod_idx),
            pl.BlockSpec((1, 1, D_MODEL), lambda i: (layer, 0, 0)),
            pl.BlockSpec((1, D_MODEL, IN_COLS), lambda i: (layer, 0, W_MIXER_OFF // IN_COLS)),
        ],
        out_specs=pl.BlockSpec((tm, IN_COLS), lambda i: (i, 0)),
        out_shape=jax.ShapeDtypeStruct((tokens, IN_COLS), F32),
        compiler_params=_cparams(("arbitrary",)),
        name="inproj",
    )(x2d, mods, norm_g.reshape(DEPTH, 1, D_MODEL), w_perm)


CTX_SEQS = 2
CTX_GROUP = MXU_DIM // HEAD_DIM


def _ctx_mixer(q, k, v, g, gq, gk, bdq, bdk, ones_stack, expand=None, sink_ref=None):
    s_len = q.shape[0]
    n_q = q.shape[1] // HEAD_DIM
    kn = _heads_rms(k, bdk, gk)
    qn = (_heads_rms(q, bdq, gq) * QK_SCALE).astype(BF16)
    knb = kn.astype(BF16)
    vb = v.astype(BF16)
    if expand is not None:
        knb = jnp.dot(knb, expand, preferred_element_type=F32).astype(BF16)
        vb = jnp.dot(vb, expand, preferred_element_type=F32).astype(BF16)
    gw = CTX_GROUP * HEAD_DIM
    masks = [_head_lane_mask(gw, hi) for hi in range(CTX_GROUP)]
    groups = []
    for gp in range(n_q // CTX_GROUP):
        cols = slice(gp * gw, (gp + 1) * gw)
        qh, kh, vh = qn[:, cols], knb[:, cols], vb[:, cols]
        kstack = jnp.concatenate([jnp.where(mk, kh, jnp.zeros_like(kh)) for mk in masks], axis=0)
        vstack = jnp.concatenate([jnp.where(mk, vh, jnp.zeros_like(vh)) for mk in masks], axis=0)
        s = lax.dot_general(qh, kstack, _NT, preferred_element_type=F32)
        ps, sink_terms = [], []
        for hi in range(CTX_GROUP):
            si = s[:, hi * s_len:(hi + 1) * s_len]
            m = jnp.max(si, axis=-1, keepdims=True)
            if sink_ref is not None:
                h = gp * CTX_GROUP + hi
                snk = sink_ref[0:1, h:h + 1]
                m = jnp.maximum(m, snk)
                sink_terms.append(jnp.where(masks[hi], jnp.exp(snk - m), 0.0))
            ps.append(jnp.exp(si - m).astype(BF16))
        p = jnp.concatenate(ps, axis=1)
        rhs = jnp.concatenate([vstack, ones_stack], axis=1)
        oe = jnp.dot(p, rhs, preferred_element_type=F32)
        l = oe[:, gw:]
        for term in sink_terms:
            l = l + term
        groups.append(oe[:, :gw] / l)
    return jnp.concatenate(groups, axis=-1) * _silu(g), kn


_CTX_INPUTS = 17


def _ctx_attn_kernel(*refs, s_len, n_alias):
    (aq_ref, ak_ref, av_ref, ag_ref, bq_ref, bk_ref, bv_ref, bg_ref,
     gqa_ref, gka_ref, sink_ref, gqb_ref, gkb_ref, bdw_ref, bdn_ref, exp_ref, ones_ref) = refs[:_CTX_INPUTS]
    ya_ref, yb_ref, ka_ref, va_ref, kb_ref, vb_ref = refs[_CTX_INPUTS + n_alias:]
    ones_stack = ones_ref[...]
    bdw = bdw_ref[...]
    for n in range(aq_ref.shape[0] // s_len):
        rows = slice(n * s_len, (n + 1) * s_len)
        va = av_ref[rows, :]
        vb = bv_ref[rows, :]
        va_ref[n, 0] = va.T
        vb_ref[n, 0] = vb.T
        ya, kna = _ctx_mixer(aq_ref[rows, :], ak_ref[rows, :], va, ag_ref[rows, :], gqa_ref[...], gka_ref[...],
                             bdw, bdn_ref[...], ones_stack, expand=exp_ref[...], sink_ref=sink_ref)
        yb, knb = _ctx_mixer(bq_ref[rows, :], bk_ref[rows, :], vb, bg_ref[rows, :], gqb_ref[...], gkb_ref[...],
                             bdw, bdw, ones_stack)
        ka_ref[n, 0] = kna.T
        kb_ref[n, 0] = knb.T
        ya_ref[rows, :] = ya.astype(ya_ref.dtype)
        yb_ref[rows, :] = yb.astype(yb_ref.dtype)


def _ctx_attn(proj, bsz, s_len, a_gq, a_gk, a_sink, b_gq, b_gk, layer, caches):
    per_layer = lambda width: pl.BlockSpec((None, 1, width), lambda b: (layer, 0, 0))
    kvw = A_KV_HEADS * HEAD_DIM
    grp = A_HEADS // A_KV_HEADS
    expand = np.zeros((kvw, BRANCH_W), np.float32)
    for h in range(A_HEADS):
        for d in range(HEAD_DIM):
            expand[(h // grp) * HEAD_DIM + d, h * HEAD_DIM + d] = 1.0
    ones_stack = np.zeros((CTX_GROUP * s_len, CTX_GROUP * HEAD_DIM), np.float32)
    for hi in range(CTX_GROUP):
        ones_stack[hi * s_len:(hi + 1) * s_len, hi * HEAD_DIM:(hi + 1) * HEAD_DIM] = 1.0
    const = lambda b: (0, 0)
    rows = CTX_SEQS * s_len
    wide = lambda c: pl.BlockSpec((rows, BRANCH_W), lambda b: (b, c // 4))
    narrow = lambda c: pl.BlockSpec((rows, kvw), lambda b: (b, c))
    prev = () if caches is None else tuple(caches)
    n_alias = len(prev)
    cache_spec = lambda width: pl.BlockSpec((CTX_SEQS, 1, width, s_len), lambda b: (b, layer, 0, 0))
    cache_shape = lambda width: jax.ShapeDtypeStruct((bsz, DEPTH, width, s_len), F32)
    outs = pl.pallas_call(
        functools.partial(_ctx_attn_kernel, s_len=s_len, n_alias=n_alias),
        grid=(bsz // CTX_SEQS,),
        in_specs=[
            wide(C_AQ), narrow(C_AK), narrow(C_AV), wide(C_AG), wide(C_BQ), wide(C_BK), wide(C_BV), wide(C_BG),
            per_layer(BRANCH_W), per_layer(kvw), per_layer(A_HEADS), per_layer(BRANCH_W), per_layer(BRANCH_W),
            pl.BlockSpec((BRANCH_W, BRANCH_W), const),
            pl.BlockSpec((kvw, kvw), const),
            pl.BlockSpec((kvw, BRANCH_W), const),
            pl.BlockSpec((CTX_GROUP * s_len, CTX_GROUP * HEAD_DIM), const),
        ] + [pl.BlockSpec(memory_space=pl.ANY)] * n_alias,
        out_specs=[
            pl.BlockSpec((rows, BRANCH_W), lambda b: (b, 0)),
            pl.BlockSpec((rows, BRANCH_W), lambda b: (b, 0)),
            cache_spec(kvw), cache_spec(kvw), cache_spec(BRANCH_W), cache_spec(BRANCH_W),
        ],
        out_shape=[
            jax.ShapeDtypeStruct((bsz * s_len, BRANCH_W), BF16),
            jax.ShapeDtypeStruct((bsz * s_len, BRANCH_W), BF16),
            cache_shape(kvw), cache_shape(kvw), cache_shape(BRANCH_W), cache_shape(BRANCH_W),
        ],
        input_output_aliases={_CTX_INPUTS + i: 2 + i for i in range(n_alias)},
        compiler_params=_cparams(("arbitrary",)),
        name="ctx_attn",
    )(proj, proj, proj, proj, proj, proj, proj, proj,
      a_gq, a_gk, a_sink, b_gq, b_gk, _head_mean_matrix(BRANCH_W), _head_mean_matrix(kvw), jnp.asarray(expand, dtype=BF16),
      jnp.asarray(ones_stack, dtype=BF16), *prev)
    return outs[0], outs[1], tuple(outs[2:])


def _rope(x, cos, sin_signed):
    w = x.shape[-1]
    lane = lax.broadcasted_iota(jnp.int32, x.shape, 1)
    up = pltpu.roll(x, w - 16, axis=1)
    dn = pltpu.roll(x, 16, axis=1)
    partner = jnp.where((lane & 16) == 0, up, dn)
    return x * cos + partner * sin_signed


WIN_SUB = 8


def _win_attn_kernel(q_ref, k_ref, v_ref, g_ref, kc_ref, vc_ref, cos_ref, sin_ref, gq_ref, gk_ref, sink_ref,
                     bdq_ref, bdk_ref, dup_ref, mask_ref, y_ref, kpad, vpad, kcx, vcx):
    j = pl.program_id(1)
    nb = pl.num_programs(1)
    t_len = k_ref.shape[0]
    grp = A_HEADS // A_KV_HEADS
    kvw = A_KV_HEADS * HEAD_DIM
    xw = 2 * kvw

    @pl.when(j == 0)
    def _():
        dup = dup_ref[...]
        kn = _heads_rms(k_ref[...], bdk_ref[...], gk_ref[...])
        kn = _rope(kn, cos_ref[:, 0:kvw], sin_ref[:, 0:kvw]).astype(BF16)
        def with_ones(vx):
            ones = jnp.ones((vx.shape[0], LANE), BF16)
            return jnp.concatenate(
                [part for kv in range(A_KV_HEADS) for part in (vx[:, kv * LANE:(kv + 1) * LANE], ones)], axis=1)

        kpad[0:A_BLOCK, :] = jnp.zeros((A_BLOCK, xw), BF16)
        kpad[A_BLOCK + t_len:2 * A_BLOCK + t_len, :] = jnp.zeros((A_BLOCK, xw), BF16)
        vpad[0:A_BLOCK, :] = jnp.zeros((A_BLOCK, 2 * xw), BF16)
        vpad[A_BLOCK + t_len:2 * A_BLOCK + t_len, :] = jnp.zeros((A_BLOCK, 2 * xw), BF16)
        kpad[A_BLOCK:A_BLOCK + t_len, :] = jnp.dot(kn, dup, preferred_element_type=F32).astype(BF16)
        vpad[A_BLOCK:A_BLOCK + t_len, :] = with_ones(
            jnp.dot(v_ref[...].astype(BF16), dup, preferred_element_type=F32).astype(BF16))
        kcx[...] = jnp.dot(kc_ref[0, 0].T.astype(BF16), dup, preferred_element_type=F32).astype(BF16)
        vcx[...] = with_ones(jnp.dot(vc_ref[0, 0].T.astype(BF16), dup, preferred_element_type=F32).astype(BF16))

    nloc = 3 * A_BLOCK
    low_half = lax.broadcasted_iota(jnp.int32, (1, LANE), 1) < HEAD_DIM
    for sub in range(WIN_SUB):
        jj = j * WIN_SUB + sub
        qrows = slice(sub * A_BLOCK, (sub + 1) * A_BLOCK)
        r0 = pl.multiple_of(jj * A_BLOCK, A_BLOCK)
        qn = _heads_rms(q_ref[qrows, :], bdq_ref[...], gq_ref[...])
        qb = (_rope(qn, cos_ref[pl.ds(r0, A_BLOCK), :], sin_ref[pl.ds(r0, A_BLOCK), :]) * QK_SCALE).astype(BF16)
        maskadd = mask_ref[jnp.where(jj == 0, 0, jnp.where(jj == nb * WIN_SUB - 1, 2, 1))]
        kband = kpad[pl.ds(r0, nloc), :]
        vband = vpad[pl.ds(r0, nloc), :]
        pairs = []
        for kv in range(A_KV_HEADS):
            cols = slice(kv * LANE, (kv + 1) * LANE)
            qparts, sinks = [], []
            for gi in range(grp):
                h = kv * grp + gi
                qpair = qb[:, (h // 2) * LANE:(h // 2 + 1) * LANE]
                keep = low_half if h % 2 == 0 else jnp.logical_not(low_half)
                qparts.append(jnp.where(keep, qpair, jnp.zeros_like(qpair)))
                sinks.append(jnp.broadcast_to(sink_ref[0:1, h:h + 1], (A_BLOCK, 1)))
            qst = jnp.concatenate(qparts, axis=0)
            snk = jnp.concatenate(sinks, axis=0)
            s_loc = lax.dot_general(qst, kband[:, cols], _NT, preferred_element_type=F32) + maskadd
            s_ctx = lax.dot_general(qst, kcx[:, cols], _NT, preferred_element_type=F32)
            m = jnp.maximum(jnp.maximum(jnp.max(s_loc, axis=-1, keepdims=True),
                                        jnp.max(s_ctx, axis=-1, keepdims=True)), snk)
            p_loc = jnp.exp(s_loc - m).astype(BF16)
            p_ctx = jnp.exp(s_ctx - m).astype(BF16)
            wide = slice(kv * MXU_DIM, (kv + 1) * MXU_DIM)
            oe = (jnp.dot(p_loc, vband[:, wide], preferred_element_type=F32)
                  + jnp.dot(p_ctx, vcx[:, wide], preferred_element_type=F32))
            o = oe[:, :LANE] / (oe[:, LANE:] + jnp.exp(snk - m))
            for k2 in range(grp // 2):
                even = o[(2 * k2) * A_BLOCK:(2 * k2 + 1) * A_BLOCK]
                odd = o[(2 * k2 + 1) * A_BLOCK:(2 * k2 + 2) * A_BLOCK]
                pairs.append(jnp.where(low_half, even, odd))
        y = jnp.concatenate(pairs, axis=-1) * _silu(g_ref[qrows, :])
        y_ref[qrows, :] = y.astype(y_ref.dtype)


def _win_mask(grp, nb):
    assert nb >= 2
    r = np.arange(A_BLOCK)[:, None]
    c = np.arange(3 * A_BLOCK)[None, :]
    band = np.abs(r + A_BLOCK - c) <= A_WINDOW
    variants = [band & (c >= A_BLOCK), band, band & (c < 2 * A_BLOCK)]
    return np.stack([np.tile(np.where(v, 0.0, NEG_INF).astype(np.float32), (grp, 1)) for v in variants])


def _win_attn(proj, cache_k, cache_v, layer, cos, sin_signed, gq, gk, sink, bsz, t_len):
    nb = t_len // A_BLOCK
    assert nb % WIN_SUB == 0
    nsteps = nb // WIN_SUB
    kvw = A_KV_HEADS * HEAD_DIM
    grp = A_HEADS // A_KV_HEADS
    p_len = cache_k.shape[3]
    dup = np.zeros((kvw, 2 * kvw), np.float32)
    for kv in range(A_KV_HEADS):
        for half in range(2):
            for d in range(HEAD_DIM):
                dup[kv * HEAD_DIM + d, kv * LANE + half * HEAD_DIM + d] = 1.0
    const2 = lambda b, j: (0, 0)
    return pl.pallas_call(
        _win_attn_kernel,
        grid=(bsz, nsteps),
        in_specs=[
            pl.BlockSpec((WIN_SUB * A_BLOCK, BRANCH_W), lambda b, j: (b * nsteps + j, C_AQ // 4)),
            pl.BlockSpec((t_len, kvw), lambda b, j: (b, C_AK)),
            pl.BlockSpec((t_len, kvw), lambda b, j: (b, C_AV)),
            pl.BlockSpec((WIN_SUB * A_BLOCK, BRANCH_W), lambda b, j: (b * nsteps + j, C_AG // 4)),
            pl.BlockSpec((1, 1, kvw, p_len), lambda b, j: (b, layer, 0, 0)),
            pl.BlockSpec((1, 1, kvw, p_len), lambda b, j: (b, layer, 0, 0)),
            pl.BlockSpec((t_len, BRANCH_W), const2),
            pl.BlockSpec((t_len, BRANCH_W), const2),
            pl.BlockSpec((None, 1, BRANCH_W), lambda b, j: (layer, 0, 0)),
            pl.BlockSpec((None, 1, kvw), lambda b, j: (layer, 0, 0)),
            pl.BlockSpec((None, 1, A_HEADS), lambda b, j: (layer, 0, 0)),
            pl.BlockSpec((BRANCH_W, BRANCH_W), const2),
            pl.BlockSpec((kvw, kvw), const2),
            pl.BlockSpec((kvw, 2 * kvw), const2),
            pl.BlockSpec((3, grp * A_BLOCK, 3 * A_BLOCK), lambda b, j: (0, 0, 0)),
        ],
        out_specs=pl.BlockSpec((WIN_SUB * A_BLOCK, BRANCH_W), lambda b, j: (b * nsteps + j, 0)),
        out_shape=jax.ShapeDtypeStruct((bsz * t_len, BRANCH_W), BF16),
        scratch_shapes=[pltpu.VMEM((t_len + 2 * A_BLOCK, 2 * kvw), BF16),
                        pltpu.VMEM((t_len + 2 * A_BLOCK, 4 * kvw), BF16),
                        pltpu.VMEM((p_len, 2 * kvw), BF16),
                        pltpu.VMEM((p_len, 4 * kvw), BF16)],
        compiler_params=_cparams(("arbitrary", "arbitrary")),
        name="win_attn",
    )(proj, proj, proj, proj, cache_k, cache_v, cos, sin_signed,
      gq, gk, sink, _head_mean_matrix(BRANCH_W), _head_mean_matrix(kvw),
      jnp.asarray(dup, dtype=BF16), jnp.asarray(_win_mask(grp, nb)))


NBR_QB = 2 * GRID_W
NBR_BAND = 5


def _nbr_plan(rows):
    kh = min(NB_ROWS, rows)
    nblk = rows // 2
    specs, plan, starts = {}, [], []
    for i in range(nblk):
        s0 = min(max(i - 2, 0), nblk - NBR_BAND)
        starts.append(s0)
        blk = []
        for a in range(2):
            qr = 2 * i + a
            rs = min(max(qr - kh // 2, 0), rows - kh)
            assert 2 * s0 <= rs and rs + kh <= 2 * (s0 + NBR_BAND)
            row = []
            for p in range(NBR_BAND):
                pair = tuple(kr - qr + NB_ROWS - 1 if rs <= kr < rs + kh else None
                             for kr in (2 * (s0 + p), 2 * (s0 + p) + 1))
                row.append(specs.setdefault(pair, len(specs)))
            blk.append(row)
        plan.append(blk)
    return tuple(starts), plan, list(specs)


def _nbr_table(rpb, specs):
    heads = rpb.shape[0]
    c = np.arange(GRID_W)
    cs = np.clip(c - NB_COLS // 2, 0, GRID_W - NB_COLS)
    col_ok = (c[None, :] >= cs[:, None]) & (c[None, :] < cs[:, None] + NB_COLS)
    dc = c[None, :] - c[:, None] + NB_COLS - 1
    onehot = ((dc[None] == np.arange(2 * NB_COLS - 1)[:, None, None]) & col_ok[None]).astype(np.float32)
    shifted = jnp.einsum('hdj,jqk->hdqk', rpb.astype(F32), jnp.asarray(onehot), precision=lax.Precision.HIGHEST)
    by_col = jnp.where(col_ok[None, None], shifted, NEG_INF)
    neg = jnp.full((heads, GRID_W, GRID_W), NEG_INF, F32)
    blocks = [jnp.concatenate([neg if d is None else by_col[:, d] for d in spec], axis=-1) for spec in specs]
    return jnp.stack(blocks, axis=1)


NBR_GROUPS = 4


def _nbr_attn_kernel(q_ref, k_ref, v_ref, g_ref, kc_ref, vc_ref, tb_ref, gq_ref, gk_ref, bd_ref, y_ref,
                     *, starts, plan):
    nband = NBR_BAND * NBR_QB
    bd = bd_ref[...]
    hpg = LANE // HEAD_DIM
    for gs in range(NBR_GROUPS):
        lanes = slice(gs * LANE, (gs + 1) * LANE)
        qn = (_heads_rms(q_ref[:, lanes], bd, gq_ref[...]) * QK_SCALE).astype(BF16)
        kn = _heads_rms(k_ref[:, lanes], bd, gk_ref[...]).astype(BF16)
        vb = v_ref[:, lanes].astype(BF16)
        kcb = kc_ref[0, 0, lanes, :].T.astype(BF16)
        vcb = vc_ref[0, 0, lanes, :].T.astype(BF16)
        acc = [jnp.zeros((NBR_QB, LANE), F32) for _ in starts]
        for h in range(hpg):
            hm = _head_lane_mask(LANE, h)
            km = jnp.where(hm, kn, jnp.zeros_like(kn))
            kcm = jnp.where(hm, kcb, jnp.zeros_like(kcb))
            vm = jnp.concatenate([jnp.where(hm, vb, jnp.zeros_like(vb)), jnp.ones_like(vb)], axis=1)
            vcm = jnp.concatenate([jnp.where(hm, vcb, jnp.zeros_like(vcb)), jnp.ones_like(vcb)], axis=1)
            for i, s0 in enumerate(starts):
                qi = qn[i * NBR_QB:(i + 1) * NBR_QB]
                ks = slice(s0 * NBR_QB, s0 * NBR_QB + nband)
                s_raw = lax.dot_general(qi, km[ks], _NT, preferred_element_type=F32)
                s_loc = jnp.concatenate(
                    [jnp.concatenate([s_raw[a * GRID_W:(a + 1) * GRID_W, p * LANE:(p + 1) * LANE]
                                      + tb_ref[gs * hpg + h, plan[i][a][p]] for p in range(NBR_BAND)], axis=1)
                     for a in range(2)], axis=0)
                s_ctx = lax.dot_general(qi, kcm, _NT, preferred_element_type=F32)
                m = jnp.maximum(jnp.max(s_loc, axis=-1, keepdims=True), jnp.max(s_ctx, axis=-1, keepdims=True))
                p_loc = jnp.exp(s_loc - m).astype(BF16)
                p_ctx = jnp.exp(s_ctx - m).astype(BF16)
                oe = (jnp.dot(p_loc, vm[ks], preferred_element_type=F32)
                      + jnp.dot(p_ctx, vcm, preferred_element_type=F32))
                acc[i] = acc[i] + oe[:, :LANE] / oe[:, LANE:]
        y = jnp.concatenate(acc, axis=0) * _silu(g_ref[:, lanes])
        y_ref[:, lanes] = y.astype(y_ref.dtype)


def _nbr_attn(proj, cache_k, cache_v, layer, table, starts, plan, gq, gk, bsz, t_len):
    hp = NBR_GROUPS * (LANE // HEAD_DIM)
    gw = NBR_GROUPS * LANE
    nhp = B_HEADS // hp
    p_len = cache_k.shape[3]
    const = lambda h, b: (0, 0)
    kern = functools.partial(_nbr_attn_kernel, starts=starts, plan=plan)
    return pl.pallas_call(
        kern,
        grid=(nhp, bsz),
        in_specs=[
            pl.BlockSpec((t_len, gw), lambda h, b: (b, C_BQ // NBR_GROUPS + h)),
            pl.BlockSpec((t_len, gw), lambda h, b: (b, C_BK // NBR_GROUPS + h)),
            pl.BlockSpec((t_len, gw), lambda h, b: (b, C_BV // NBR_GROUPS + h)),
            pl.BlockSpec((t_len, gw), lambda h, b: (b, C_BG // NBR_GROUPS + h)),
            pl.BlockSpec((1, 1, gw, p_len), lambda h, b: (b, layer, h, 0)),
            pl.BlockSpec((1, 1, gw, p_len), lambda h, b: (b, layer, h, 0)),
            pl.BlockSpec((hp,) + table.shape[1:], lambda h, b: (layer * nhp + h, 0, 0, 0)),
            pl.BlockSpec((None, 1, LANE), lambda h, b: (layer, 0, 0)),
            pl.BlockSpec((None, 1, LANE), lambda h, b: (layer, 0, 0)),
            pl.BlockSpec((LANE, LANE), const),
        ],
        out_specs=pl.BlockSpec((t_len, gw), lambda h, b: (b, h)),
        out_shape=jax.ShapeDtypeStruct((bsz * t_len, BRANCH_W), BF16),
        compiler_params=_cparams(("arbitrary", "arbitrary")),
        name="nbr_attn",
    )(proj, proj, proj, proj, cache_k, cache_v, table, gq, gk, _head_mean_matrix(LANE))


LRU_CHUNKS = SUBLANE
LRU_PITCH_PAD = 4


def _lru_kernel(cx_ref, cg_ref, h0_ref, cw_ref, cb_ref, wg_ref, bg_ref, lam_ref,
                y_ref, st_ref, a_s, u_s, h_s, p_s, y_s):
    t_len = cx_ref.shape[0]
    w = LRU_WIDTH
    cx = cx_ref[...]
    row = lax.broadcasted_iota(jnp.int32, (t_len, w), 0)
    xc = cb_ref[...] + cx * cw_ref[2:3, :]
    xc = xc + jnp.where(row >= 2, pltpu.roll(cx, 2, axis=0), 0.0) * cw_ref[0:1, :]
    xc = xc + jnp.where(row >= 1, pltpu.roll(cx, 1, axis=0), 0.0) * cw_ref[1:2, :]
    xc = xc + jnp.where(row < t_len - 1, pltpu.roll(cx, t_len - 1, axis=0), 0.0) * cw_ref[3:4, :]

    gates = jnp.dot(xc.astype(BF16), wg_ref[...], preferred_element_type=F32) + bg_ref[...]
    coeffs = []
    for d in range(2):
        th_r = jnp.tanh(gates[:, d * w:(d + 1) * w])
        th_i = jnp.tanh(gates[:, (2 + d) * w:(3 + d) * w])
        nl = -lam_ref[d:d + 1, :]
        softplus = jnp.maximum(nl, 0.0) + jnp.log1p(jnp.exp(-jnp.abs(nl)))
        quarter_c = (-0.25 * LRU_C) * softplus
        half_log_a = quarter_c * th_r + quarter_c
        t = jnp.tanh(half_log_a)
        rc = 1.0 / (1.0 - t)
        coeffs.append(((1.0 + t) * rc,
                       jnp.sqrt(-t) * rc * (1.0 + th_i) * xc))

    chunk_len = t_len // LRU_CHUNKS
    pitch = chunk_len + LRU_PITCH_PAD
    nt = w // LANE
    for d, (a_val, u_val) in enumerate(coeffs):
        for c in range(LRU_CHUNKS):
            for k in range(nt):
                dst = slice(c * pitch, c * pitch + chunk_len)
                src = (slice(c * chunk_len, (c + 1) * chunk_len), slice(k * LANE, (k + 1) * LANE))
                a_s[d, k, dst, :] = a_val[src]
                u_s[d, k, dst, :] = u_val[src]

    def scan_body(s, carry):
        hs, ps = carry
        new_h, new_p = [], []
        for d in range(2):
            pos = s if d == 0 else chunk_len - 1 - s
            idx = pl.ds(pos, LRU_CHUNKS, stride=pitch)
            for k in range(nt):
                av = a_s[d, k, idx, :]
                h = av * hs[d * nt + k] + u_s[d, k, idx, :]
                p = av * ps[d * nt + k]
                h_s[d, k, idx, :] = h
                p_s[d, k, idx, :] = p
                new_h.append(h)
                new_p.append(p)
        return tuple(new_h), tuple(new_p)

    zero = jnp.zeros((LRU_CHUNKS, LANE), F32)
    one = jnp.ones((LRU_CHUNKS, LANE), F32)
    h_end, p_end = lax.fori_loop(0, chunk_len, scan_body, ((zero,) * (2 * nt), (one,) * (2 * nt)), unroll=4)

    h0 = h0_ref[0]
    enter, finals = [], []
    for d in range(2):
        order = range(LRU_CHUNKS) if d == 0 else range(LRU_CHUNKS - 1, -1, -1)
        for k in range(nt):
            he, pe = h_end[d * nt + k], p_end[d * nt + k]
            e = h0[d:d + 1, k * LANE:(k + 1) * LANE]
            rows = [None] * LRU_CHUNKS
            for c in order:
                rows[c] = e
                e = pe[c:c + 1, :] * e + he[c:c + 1, :]
            enter.append(jnp.concatenate(rows, axis=0))
            finals.append(e)
    st_ref[0] = jnp.concatenate([jnp.concatenate(finals[d * nt:(d + 1) * nt], axis=1) for d in range(2)], axis=0)

    def fix_body(s, carry):
        idx = pl.ds(s, LRU_CHUNKS, stride=pitch)
        for k in range(nt):
            hf = h_s[0, k, idx, :] + p_s[0, k, idx, :] * enter[k]
            hb = h_s[1, k, idx, :] + p_s[1, k, idx, :] * enter[nt + k]
            y_s[k, idx, :] = hf + hb
        return carry

    lax.fori_loop(0, chunk_len, fix_body, 0, unroll=4)
    hsum = jnp.concatenate(
        [jnp.concatenate([y_s[k, c * pitch:c * pitch + chunk_len, :] for c in range(LRU_CHUNKS)], axis=0)
         for k in range(nt)], axis=1)
    y_ref[...] = (hsum * _silu(cg_ref[...])).astype(y_ref.dtype)


def _lru(proj, h0, h0_layer, conv_w, conv_b, w_gates, b_gates, lam, layer, bsz, t_len):
    w = LRU_WIDTH
    assert t_len % (LRU_CHUNKS * SUBLANE) == 0
    rows_pad = LRU_CHUNKS * (t_len // LRU_CHUNKS + LRU_PITCH_PAD)
    return pl.pallas_call(
        _lru_kernel,
        grid=(bsz,),
        in_specs=[
            pl.BlockSpec((t_len, w), lambda b: (b, C_CX // 4)),
            pl.BlockSpec((t_len, w), lambda b: (b, C_CG // 4)),
            pl.BlockSpec((1, None, 2, w), lambda b: (b, h0_layer, 0, 0)),
            pl.BlockSpec((None, CONV_W, w), lambda b: (layer, 0, 0)),
            pl.BlockSpec((None, 1, w), lambda b: (layer, 0, 0)),
            pl.BlockSpec((None, w, 4 * w), lambda b: (layer, 0, 0)),
            pl.BlockSpec((None, 1, 4 * w), lambda b: (layer, 0, 0)),
            pl.BlockSpec((None, 2, w), lambda b: (layer, 0, 0)),
        ],
        out_specs=[
            pl.BlockSpec((t_len, w), lambda b: (b, 0)),
            pl.BlockSpec((1, 2, w), lambda b: (b, 0, 0)),
        ],
        out_shape=[
            jax.ShapeDtypeStruct((bsz * t_len, w), BF16),
            jax.ShapeDtypeStruct((bsz, 2, w), F32),
        ],
        scratch_shapes=[pltpu.VMEM((2, w // LANE, rows_pad, LANE), F32) for _ in range(4)]
        + [pltpu.VMEM((w // LANE, rows_pad, LANE), F32)],
        compiler_params=_cparams(("arbitrary",)),
        name="lru",
    )(proj, proj, h0, conv_w, conv_b, w_gates, b_gates, lam)


def _lru_gate_weights(wa, ba, wx, bx):
    def dense(wblk):
        eye = jnp.eye(LRU_BLOCKS, dtype=wblk.dtype)
        full = jnp.einsum('ldnkj,nm->lnkdmj', wblk, eye)
        return full.reshape(DEPTH, LRU_WIDTH, 2 * LRU_WIDTH)
    wg = jnp.concatenate([dense(wa), dense(wx)], axis=2)
    bg = jnp.concatenate([ba.reshape(DEPTH, 1, 2 * LRU_WIDTH), bx.reshape(DEPTH, 1, 2 * LRU_WIDTH)], axis=2)
    return (0.5 * wg).astype(BF16), 0.5 * bg


def _merge_kernel(x_ref, ya_ref, yb_ref, yc_ref, mod_ref, g_ref, wgate_ref, wbr_ref, wout_ref, o_ref):
    x = x_ref[...]
    h = _modulated_norm(x, g_ref, mod_ref)
    gates = jnp.dot(h, wgate_ref[0], preferred_element_type=F32)
    z = None
    for k, y_ref in enumerate((ya_ref, yb_ref, yc_ref)):
        term = (_sigmoid(gates[:, k * D_MODEL:(k + 1) * D_MODEL])
                * jnp.dot(y_ref[...], wbr_ref[0, k], preferred_element_type=F32))
        z = term if z is None else z + term
    out = jnp.dot(z.astype(BF16), wout_ref[0], preferred_element_type=F32)
    o_ref[...] = x + mod_ref[0, 0, 2:3, :] * out


def _merge(x2d, ya, yb, yc, mods, norm_g, w_gate, w_br, w_out, layer, rows_per_mod, mod_row0):
    tokens = x2d.shape[0]
    tm = 1024
    assert rows_per_mod % tm == 0
    tiles_per_mod = rows_per_mod // tm if rows_per_mod else 0

    def mod_idx(i):
        if rows_per_mod:
            return (layer, mod_row0 + i // tiles_per_mod, 0, 0)
        return (layer, mod_row0, 0, 0)

    return pl.pallas_call(
        _merge_kernel,
        grid=(tokens // tm,),
        in_specs=[
            pl.BlockSpec((tm, D_MODEL), lambda i: (i, 0)),
            pl.BlockSpec((tm, BRANCH_W), lambda i: (i, 0)),
            pl.BlockSpec((tm, BRANCH_W), lambda i: (i, 0)),
            pl.BlockSpec((tm, BRANCH_W), lambda i: (i, 0)),
            pl.BlockSpec((1, 1, 3, D_MODEL), mod_idx),
            pl.BlockSpec((1, 1, D_MODEL), lambda i: (layer, 0, 0)),
            pl.BlockSpec((1, D_MODEL, GATE_COLS), lambda i: (layer, 0, 0)),
            pl.BlockSpec((1, 3, BRANCH_W, D_MODEL), lambda i: (layer, 0, 0, 0)),
            pl.BlockSpec((1, D_MODEL, D_MODEL), lambda i: (layer, 0, 0)),
        ],
        out_specs=pl.BlockSpec((tm, D_MODEL), lambda i: (i, 0)),
        out_shape=jax.ShapeDtypeStruct((tokens, D_MODEL), F32),
        compiler_params=_cparams(("arbitrary",)),
        name="merge",
    )(x2d, ya, yb, yc, mods, norm_g.reshape(DEPTH, 1, D_MODEL), w_gate, w_br, w_out)


def _rope_tables(t_len):
    t = jnp.arange(t_len)
    m = HEAD_DIM // 4
    freqs = ROPE_BASE ** (-jnp.arange(m, dtype=F32) / m)
    ang_r = (t // GRID_W).astype(F32)[:, None] * freqs[None, :]
    ang_c = (t % GRID_W).astype(F32)[:, None] * freqs[None, :]
    cos = jnp.concatenate([jnp.cos(ang_r), jnp.cos(ang_r), jnp.cos(ang_c), jnp.cos(ang_c)], axis=-1)
    sin = jnp.concatenate([-jnp.sin(ang_r), jnp.sin(ang_r), -jnp.sin(ang_c), jnp.sin(ang_c)], axis=-1)
    reps = BRANCH_W // HEAD_DIM
    return jnp.tile(cos, (1, reps)), jnp.tile(sin, (1, reps))


def kernel(x_prompt, x_sample, cache_ka, cache_va, cache_kb, cache_vb, state_lru, c, c_ctx,
           norm_g, w_ada, b_ada, w_in, a_q_norm, a_k_norm, a_sink, b_q_norm, b_k_norm, b_rpb,
           lru_conv_w, lru_conv_b, lru_wa, lru_ba, lru_wx, lru_bx, lru_lambda, w_branch, w_out):
    bsz, s_len, _ = x_prompt.shape
    dbsz, t_len, _ = x_sample.shape
    p_len = cache_ka.shape[2]

    n_mod = 16
    ctx_row = dbsz
    cond = jnp.zeros((n_mod, D_MODEL), F32).at[:dbsz].set(c).at[ctx_row].set(c_ctx)
    mods = _mods(cond, w_ada, b_ada).reshape(DEPTH, n_mod, 3, D_MODEL)

    w_perm = w_gate = _wprep(w_in)
    w_br = w_branch.astype(BF16)
    w_o = w_out.astype(BF16)
    cos, sin_signed = _rope_tables(t_len)

    def cache_t(x):
        return x.transpose(0, 1, 3, 4, 2).reshape(x.shape[0], DEPTH, x.shape[3] * HEAD_DIM, p_len)
    cka, cva, ckb, cvb = cache_t(cache_ka), cache_t(cache_va), cache_t(cache_kb), cache_t(cache_vb)

    def gain_rows(g, reps):
        return jnp.tile(g, (1, reps)).reshape(DEPTH, 1, reps * HEAD_DIM)
    a_gq, a_gk = gain_rows(a_q_norm, A_HEADS), gain_rows(a_k_norm, A_KV_HEADS)
    b_gq, b_gk = gain_rows(b_q_norm, B_HEADS), gain_rows(b_k_norm, B_HEADS)
    b_gq2, b_gk2 = gain_rows(b_q_norm, LANE // HEAD_DIM), gain_rows(b_k_norm, LANE // HEAD_DIM)
    sinks = a_sink.reshape(DEPTH, 1, A_HEADS)
    wg, bg = _lru_gate_weights(lru_wa, lru_ba, lru_wx, lru_bx)
    conv_b = lru_conv_b.reshape(DEPTH, 1, LRU_WIDTH)
    grid_rows = t_len // GRID_W
    assert grid_rows % 2 == 0 and grid_rows // 2 >= NBR_BAND
    nbr_starts, nbr_plan, nbr_specs = _nbr_plan(grid_rows)
    nbr_table = _nbr_table(b_rpb.reshape((DEPTH * B_HEADS,) + b_rpb.shape[2:]), nbr_specs)

    yp = x_prompt.reshape(bsz * s_len, D_MODEL)
    ys = x_sample.reshape(dbsz * t_len, D_MODEL)
    zero_state = jnp.zeros((bsz, 1, 2, LRU_WIDTH), F32)
    new_caches, new_lru = None, []
    for l in range(DEPTH):
        proj = _inproj(yp, mods, norm_g, w_perm, l, 0, ctx_row)
        ya, yb, new_caches = _ctx_attn(proj, bsz, s_len, a_gq, a_gk, sinks, b_gq, b_gk, l, new_caches)
        yc, st = _lru(proj, zero_state, 0, lru_conv_w, conv_b, wg, bg, lru_lambda, l, bsz, s_len)
        yp = _merge(yp, ya, yb, yc, mods, norm_g, w_gate, w_br, w_o, l, 0, ctx_row)
        new_lru.append(st)

        proj = _inproj(ys, mods, norm_g, w_perm, l, t_len, 0)
        ya = _win_attn(proj, cka, cva, l, cos, sin_signed, a_gq, a_gk, sinks, dbsz, t_len)
        yb = _nbr_attn(proj, ckb, cvb, l, nbr_table, nbr_starts, nbr_plan, b_gq2, b_gk2, dbsz, t_len)
        yc, _ = _lru(proj, state_lru, l, lru_conv_w, conv_b, wg, bg, lru_lambda, l, dbsz, t_len)
        ys = _merge(ys, ya, yb, yc, mods, norm_g, w_gate, w_br, w_o, l, t_len, 0)

    def cache_out(x):
        return x.reshape(bsz, DEPTH, x.shape[2] // HEAD_DIM, HEAD_DIM, s_len).transpose(0, 1, 4, 2, 3)

    return (yp.reshape(bsz, s_len, D_MODEL), ys.reshape(dbsz, t_len, D_MODEL),
            *(cache_out(x) for x in new_caches), jnp.stack(new_lru, axis=1))
```

```python
import functools

import numpy as np
import jax
import jax.numpy as jnp
from jax import lax
from jax.experimental import pallas as pl
from jax.experimental.pallas import tpu as pltpu

F32 = jnp.float32
BF16 = jnp.bfloat16

D_MODEL = 1024
DEPTH = 2
GRID_W = 64
HEAD_DIM = 64
BRANCH_W = 512
A_HEADS = 8
A_KV_HEADS = 2
A_WINDOW = 128
A_BLOCK = 128
B_HEADS = 8
NB_ROWS = 8
NB_COLS = 16
LRU_WIDTH = 512
LRU_BLOCKS = 8
LRU_BW = LRU_WIDTH // LRU_BLOCKS
LRU_C = 8.0
CONV_W = 4
ROPE_BASE = 10000.0
EPS = 1e-6
NEG_INF = -1e30
QK_SCALE = HEAD_DIM ** -0.5

LANE = 128
SUBLANE = 8
MXU_DIM = 256
VMEM_LIMIT = 56 * 1024 * 1024

_ORIG_SPLITS = (512, 128, 128, 512, 512, 512, 512, 512, 512, 512, 1024, 1024, 1024)
_ORIG_OFFS = tuple(int(v) for v in np.cumsum((0,) + _ORIG_SPLITS)[:-1])
_PERM = (0, 3, 4, 5, 6, 7, 8, 9, 1, 2)
_GATE_SPLITS = (10, 11, 12)
IN_COLS = sum(_ORIG_SPLITS[k] for k in _PERM)
GATE_COLS = sum(_ORIG_SPLITS[k] for k in _GATE_SPLITS)
C_AQ, C_AG, C_BQ, C_BK, C_BV, C_BG, C_CX, C_CG, C_AK, C_AV = (0, 4, 8, 12, 16, 20, 24, 28, 32, 33)

_NT = (((1,), (1,)), ((), ()))


def _cparams(sem):
    return pltpu.CompilerParams(dimension_semantics=sem, vmem_limit_bytes=VMEM_LIMIT)


def _sigmoid(x):
    return 0.5 + 0.5 * jnp.tanh(0.5 * x)


def _silu(x):
    return x * _sigmoid(x)


def _head_mean_matrix(width):
    idx = np.arange(width) // HEAD_DIM
    return jnp.asarray((idx[:, None] == idx[None, :]).astype(np.float32) / HEAD_DIM, dtype=BF16)


def _heads_rms(x, bd, g):
    x2 = x * x
    hi = x2.astype(BF16)
    lo = (x2 - hi.astype(F32)).astype(BF16)
    width = x.shape[-1]
    step = min(width, MXU_DIM)
    tile = bd[:step, :step]
    ms = jnp.concatenate(
        [jnp.dot(hi[:, c:c + step], tile, preferred_element_type=F32)
         + jnp.dot(lo[:, c:c + step], tile, preferred_element_type=F32) for c in range(0, width, step)], axis=-1)
    return x * lax.rsqrt(ms + EPS) * g


def _head_lane_mask(width, h):
    lane = lax.broadcasted_iota(jnp.int32, (1, width), 1)
    return (lane >= h * HEAD_DIM) & (lane < (h + 1) * HEAD_DIM)


def _mods_kernel(c_ref, w_ref, b_ref, o_ref):
    c = c_ref[...]
    s = _silu(c).astype(BF16)
    o_ref[0] = jnp.dot(s, w_ref[0].astype(BF16), preferred_element_type=F32) + b_ref[0]


def _mods(cond, w_ada, b_ada):
    n = cond.shape[0]
    tn = D_MODEL
    return pl.pallas_call(
        _mods_kernel,
        grid=(DEPTH, 3 * D_MODEL // tn),
        in_specs=[
            pl.BlockSpec((n, D_MODEL), lambda l, j: (0, 0)),
            pl.BlockSpec((1, D_MODEL, tn), lambda l, j: (l, 0, j)),
            pl.BlockSpec((1, 1, tn), lambda l, j: (l, 0, j)),
        ],
        out_specs=pl.BlockSpec((1, n, tn), lambda l, j: (l, 0, j)),
        out_shape=jax.ShapeDtypeStruct((DEPTH, n, 3 * D_MODEL), F32),
        compiler_params=_cparams(("arbitrary", "arbitrary")),
        name="mods",
    )(cond, w_ada, b_ada.reshape(DEPTH, 1, 3 * D_MODEL))


W_MIXER_OFF = IN_COLS
W_PREP_COLS = 2 * IN_COLS
W_PREP_ROWS = 256
_AK0, _AG0, _GATE0 = _ORIG_OFFS[1], _ORIG_OFFS[3], _ORIG_OFFS[_GATE_SPLITS[0]]


def _wprep_kernel(w_ref, o_ref):
    w = w_ref[0]
    rows = w.shape[0]
    pieces = (w[:, _GATE0:_GATE0 + GATE_COLS],
              jnp.zeros((rows, W_MIXER_OFF - GATE_COLS), F32),
              w[:, 0:_AK0],
              w[:, _AG0:_GATE0],
              w[:, _AK0:_AG0])
    col = 0
    for piece in pieces:
        o_ref[0, :, col:col + piece.shape[1]] = piece.astype(BF16)
        col += piece.shape[1]


def _wprep(w_in):
    assert _PERM == (0, 3, 4, 5, 6, 7, 8, 9, 1, 2) and _AK0 % LANE == 0 and _AG0 % LANE == 0 and _GATE0 == IN_COLS
    return pl.pallas_call(
        _wprep_kernel,
        grid=(DEPTH, D_MODEL // W_PREP_ROWS),
        in_specs=[pl.BlockSpec((1, W_PREP_ROWS, w_in.shape[2]), lambda l, r: (l, r, 0))],
        out_specs=pl.BlockSpec((1, W_PREP_ROWS, W_PREP_COLS), lambda l, r: (l, r, 0)),
        out_shape=jax.ShapeDtypeStruct((DEPTH, D_MODEL, W_PREP_COLS), BF16),
        compiler_params=_cparams(("arbitrary", "arbitrary")),
        name="wprep",
    )(w_in)


def _modulated_norm(x, g_ref, mod_ref):
    y = x * lax.rsqrt(jnp.mean(x * x, axis=-1, keepdims=True) + EPS)
    y = y * g_ref[0]
    shift = mod_ref[0, 0, 0:1, :]
    scale = mod_ref[0, 0, 1:2, :]
    return (y * (1.0 + scale) + shift).astype(BF16)


def _inproj_kernel(x_ref, mod_ref, g_ref, w_ref, o_ref):
    h = _modulated_norm(x_ref[...], g_ref, mod_ref)
    o_ref[...] = jnp.dot(h, w_ref[0], preferred_element_type=F32)


def _inproj(x2d, mods, norm_g, w_perm, layer, rows_per_mod, mod_row0):
    tokens = x2d.shape[0]
    tm = 512
    tiles_per_mod = rows_per_mod // tm if rows_per_mod else 0

    def mod_idx(i):
        if rows_per_mod:
            return (layer, mod_row0 + i // tiles_per_mod, 0, 0)
        return (layer, mod_row0, 0, 0)

    return pl.pallas_call(
        _inproj_kernel,
        grid=(tokens // tm,),
        in_specs=[
            pl.BlockSpec((tm, D_MODEL), lambda i: (i, 0)),
            pl.BlockSpec((1, 1, 3, D_MODEL), mod_idx),
            pl.BlockSpec((1, 1, D_MODEL), lambda i: (layer, 0, 0)),
            pl.BlockSpec((1, D_MODEL, IN_COLS), lambda i: (layer, 0, W_MIXER_OFF // IN_COLS)),
        ],
        out_specs=pl.BlockSpec((tm, IN_COLS), lambda i: (i, 0)),
        out_shape=jax.ShapeDtypeStruct((tokens, IN_COLS), F32),
        compiler_params=_cparams(("arbitrary",)),
        name="inproj",
    )(x2d, mods, norm_g.reshape(DEPTH, 1, D_MODEL), w_perm)


CTX_SEQS = 2
CTX_GROUP = MXU_DIM // HEAD_DIM


def _ctx_mixer(q, k, v, g, gq, gk, bdq, bdk, ones_stack, expand=None, sink_ref=None):
    s_len = q.shape[0]
    n_q = q.shape[1] // HEAD_DIM
    kn = _heads_rms(k, bdk, gk)
    qn = (_heads_rms(q, bdq, gq) * QK_SCALE).astype(BF16)
    knb = kn.astype(BF16)
    vb = v.astype(BF16)
    if expand is not None:
        knb = jnp.dot(knb, expand, preferred_element_type=F32).astype(BF16)
        vb = jnp.dot(vb, expand, preferred_element_type=F32).astype(BF16)
    gw = CTX_GROUP * HEAD_DIM
    masks = [_head_lane_mask(gw, hi) for hi in range(CTX_GROUP)]
    groups = []
    for gp in range(n_q // CTX_GROUP):
        cols = slice(gp * gw, (gp + 1) * gw)
        qh, kh, vh = qn[:, cols], knb[:, cols], vb[:, cols]
        kstack = jnp.concatenate([jnp.where(mk, kh, jnp.zeros_like(kh)) for mk in masks], axis=0)
        vstack = jnp.concatenate([jnp.where(mk, vh, jnp.zeros_like(vh)) for mk in masks], axis=0)
        s = lax.dot_general(qh, kstack, _NT, preferred_element_type=F32)
        ps, sink_terms = [], []
        for hi in range(CTX_GROUP):
            si = s[:, hi * s_len:(hi + 1) * s_len]
            m = jnp.max(si, axis=-1, keepdims=True)
            if sink_ref is not None:
                h = gp * CTX_GROUP + hi
                snk = sink_ref[0:1, h:h + 1]
                m = jnp.maximum(m, snk)
                sink_terms.append(jnp.where(masks[hi], jnp.exp(snk - m), 0.0))
            ps.append(jnp.exp(si - m).astype(BF16))
        p = jnp.concatenate(ps, axis=1)
        rhs = jnp.concatenate([vstack, ones_stack], axis=1)
        oe = jnp.dot(p, rhs, preferred_element_type=F32)
        l = oe[:, gw:]
        for term in sink_terms:
            l = l + term
        groups.append(oe[:, :gw] / l)
    return jnp.concatenate(groups, axis=-1) * _silu(g), kn


_CTX_INPUTS = 17


def _ctx_attn_kernel(*refs, s_len, n_alias):
    (aq_ref, ak_ref, av_ref, ag_ref, bq_ref, bk_ref, bv_ref, bg_ref,
     gqa_ref, gka_ref, sink_ref, gqb_ref, gkb_ref, bdw_ref, bdn_ref, exp_ref, ones_ref) = refs[:_CTX_INPUTS]
    ya_ref, yb_ref, ka_ref, va_ref, kb_ref, vb_ref = refs[_CTX_INPUTS + n_alias:]
    ones_stack = ones_ref[...]
    bdw = bdw_ref[...]
    for n in range(aq_ref.shape[0] // s_len):
        rows = slice(n * s_len, (n + 1) * s_len)
        va = av_ref[rows, :]
        vb = bv_ref[rows, :]
        va_ref[n, 0] = va.T
        vb_ref[n, 0] = vb.T
        ya, kna = _ctx_mixer(aq_ref[rows, :], ak_ref[rows, :], va, ag_ref[rows, :], gqa_ref[...], gka_ref[...],
                             bdw, bdn_ref[...], ones_stack, expand=exp_ref[...], sink_ref=sink_ref)
        yb, knb = _ctx_mixer(bq_ref[rows, :], bk_ref[rows, :], vb, bg_ref[rows, :], gqb_ref[...], gkb_ref[...],
                             bdw, bdw, ones_stack)
        ka_ref[n, 0] = kna.T
        kb_ref[n, 0] = knb.T
        ya_ref[rows, :] = ya.astype(ya_ref.dtype)
        yb_ref[rows, :] = yb.astype(yb_ref.dtype)


def _ctx_attn(proj, bsz, s_len, a_gq, a_gk, a_sink, b_gq, b_gk, layer, caches):
    per_layer = lambda width: pl.BlockSpec((None, 1, width), lambda b: (layer, 0, 0))
    kvw = A_KV_HEADS * HEAD_DIM
    grp = A_HEADS // A_KV_HEADS
    expand = np.zeros((kvw, BRANCH_W), np.float32)
    for h in range(A_HEADS):
        for d in range(HEAD_DIM):
            expand[(h // grp) * HEAD_DIM + d, h * HEAD_DIM + d] = 1.0
    ones_stack = np.zeros((CTX_GROUP * s_len, CTX_GROUP * HEAD_DIM), np.float32)
    for hi in range(CTX_GROUP):
        ones_stack[hi * s_len:(hi + 1) * s_len, hi * HEAD_DIM:(hi + 1) * HEAD_DIM] = 1.0
    const = lambda b: (0, 0)
    rows = CTX_SEQS * s_len
    wide = lambda c: pl.BlockSpec((rows, BRANCH_W), lambda b: (b, c // 4))
    narrow = lambda c: pl.BlockSpec((rows, kvw), lambda b: (b, c))
    prev = () if caches is None else tuple(caches)
    n_alias = len(prev)
    cache_spec = lambda width: pl.BlockSpec((CTX_SEQS, 1, width, s_len), lambda b: (b, layer, 0, 0))
    cache_shape = lambda width: jax.ShapeDtypeStruct((bsz, DEPTH, width, s_len), F32)
    outs = pl.pallas_call(
        functools.partial(_ctx_attn_kernel, s_len=s_len, n_alias=n_alias),
        grid=(bsz // CTX_SEQS,),
        in_specs=[
            wide(C_AQ), narrow(C_AK), narrow(C_AV), wide(C_AG), wide(C_BQ), wide(C_BK), wide(C_BV), wide(C_BG),
            per_layer(BRANCH_W), per_layer(kvw), per_layer(A_HEADS), per_layer(BRANCH_W), per_layer(BRANCH_W),
            pl.BlockSpec((BRANCH_W, BRANCH_W), const),
            pl.BlockSpec((kvw, kvw), const),
            pl.BlockSpec((kvw, BRANCH_W), const),
            pl.BlockSpec((CTX_GROUP * s_len, CTX_GROUP * HEAD_DIM), const),
        ] + [pl.BlockSpec(memory_space=pl.ANY)] * n_alias,
        out_specs=[
            pl.BlockSpec((rows, BRANCH_W), lambda b: (b, 0)),
            pl.BlockSpec((rows, BRANCH_W), lambda b: (b, 0)),
            cache_spec(kvw), cache_spec(kvw), cache_spec(BRANCH_W), cache_spec(BRANCH_W),
        ],
        out_shape=[
            jax.ShapeDtypeStruct((bsz * s_len, BRANCH_W), BF16),
            jax.ShapeDtypeStruct((bsz * s_len, BRANCH_W), BF16),
            cache_shape(kvw), cache_shape(kvw), cache_shape(BRANCH_W), cache_shape(BRANCH_W),
        ],
        input_output_aliases={_CTX_INPUTS + i: 2 + i for i in range(n_alias)},
        compiler_params=_cparams(("arbitrary",)),
        name="ctx_attn",
    )(proj, proj, proj, proj, proj, proj, proj, proj,
      a_gq, a_gk, a_sink, b_gq, b_gk, _head_mean_matrix(BRANCH_W), _head_mean_matrix(kvw), jnp.asarray(expand, dtype=BF16),
      jnp.asarray(ones_stack, dtype=BF16), *prev)
    return outs[0], outs[1], tuple(outs[2:])


def _rope(x, cos, sin_signed):
    w = x.shape[-1]
    lane = lax.broadcasted_iota(jnp.int32, x.shape, 1)
    up = pltpu.roll(x, w - 16, axis=1)
    dn = pltpu.roll(x, 16, axis=1)
    partner = jnp.where((lane & 16) == 0, up, dn)
    return x * cos + partner * sin_signed


WIN_SUB = 8


def _win_attn_kernel(q_ref, k_ref, v_ref, g_ref, kc_ref, vc_ref, cos_ref, sin_ref, gq_ref, gk_ref, sink_ref,
                     bdq_ref, bdk_ref, dup_ref, mask_ref, y_ref, kpad, vpad, kcx, vcx):
    j = pl.program_id(1)
    nb = pl.num_programs(1)
    t_len = k_ref.shape[0]
    grp = A_HEADS // A_KV_HEADS
    kvw = A_KV_HEADS * HEAD_DIM
    xw = 2 * kvw

    @pl.when(j == 0)
    def _():
        dup = dup_ref[...]
        kn = _heads_rms(k_ref[...], bdk_ref[...], gk_ref[...])
        kn = _rope(kn, cos_ref[:, 0:kvw], sin_ref[:, 0:kvw]).astype(BF16)
        def with_ones(vx):
            ones = jnp.ones((vx.shape[0], LANE), BF16)
            return jnp.concatenate(
                [part for kv in range(A_KV_HEADS) for part in (vx[:, kv * LANE:(kv + 1) * LANE], ones)], axis=1)

        kpad[0:A_BLOCK, :] = jnp.zeros((A_BLOCK, xw), BF16)
        kpad[A_BLOCK + t_len:2 * A_BLOCK + t_len, :] = jnp.zeros((A_BLOCK, xw), BF16)
        vpad[0:A_BLOCK, :] = jnp.zeros((A_BLOCK, 2 * xw), BF16)
        vpad[A_BLOCK + t_len:2 * A_BLOCK + t_len, :] = jnp.zeros((A_BLOCK, 2 * xw), BF16)
        kpad[A_BLOCK:A_BLOCK + t_len, :] = jnp.dot(kn, dup, preferred_element_type=F32).astype(BF16)
        vpad[A_BLOCK:A_BLOCK + t_len, :] = with_ones(
            jnp.dot(v_ref[...].astype(BF16), dup, preferred_element_type=F32).astype(BF16))
        kcx[...] = jnp.dot(kc_ref[0, 0].T.astype(BF16), dup, preferred_element_type=F32).astype(BF16)
        vcx[...] = with_ones(jnp.dot(vc_ref[0, 0].T.astype(BF16), dup, preferred_element_type=F32).astype(BF16))

    nloc = 3 * A_BLOCK
    low_half = lax.broadcasted_iota(jnp.int32, (1, LANE), 1) < HEAD_DIM
    for sub in range(WIN_SUB):
        jj = j * WIN_SUB + sub
        qrows = slice(sub * A_BLOCK, (sub + 1) * A_BLOCK)
        r0 = pl.multiple_of(jj * A_BLOCK, A_BLOCK)
        qn = _heads_rms(q_ref[qrows, :], bdq_ref[...], gq_ref[...])
        qb = (_rope(qn, cos_ref[pl.ds(r0, A_BLOCK), :], sin_ref[pl.ds(r0, A_BLOCK), :]) * QK_SCALE).astype(BF16)
        maskadd = mask_ref[jnp.where(jj == 0, 0, jnp.where(jj == nb * WIN_SUB - 1, 2, 1))]
        kband = kpad[pl.ds(r0, nloc), :]
        vband = vpad[pl.ds(r0, nloc), :]
        pairs = []
        for kv in range(A_KV_HEADS):
            cols = slice(kv * LANE, (kv + 1) * LANE)
            qparts, sinks = [], []
            for gi in range(grp):
                h = kv * grp + gi
                qpair = qb[:, (h // 2) * LANE:(h // 2 + 1) * LANE]
                keep = low_half if h % 2 == 0 else jnp.logical_not(low_half)
                qparts.append(jnp.where(keep, qpair, jnp.zeros_like(qpair)))
                sinks.append(jnp.broadcast_to(sink_ref[0:1, h:h + 1], (A_BLOCK, 1)))
            qst = jnp.concatenate(qparts, axis=0)
            snk = jnp.concatenate(sinks, axis=0)
            s_loc = lax.dot_general(qst, kband[:, cols], _NT, preferred_element_type=F32) + maskadd
            s_ctx = lax.dot_general(qst, kcx[:, cols], _NT, preferred_element_type=F32)
            m = jnp.maximum(jnp.maximum(jnp.max(s_loc, axis=-1, keepdims=True),
                                        jnp.max(s_ctx, axis=-1, keepdims=True)), snk)
            p_loc = jnp.exp(s_loc - m).astype(BF16)
            p_ctx = jnp.exp(s_ctx - m).astype(BF16)
            wide = slice(kv * MXU_DIM, (kv + 1) * MXU_DIM)
            oe = (jnp.dot(p_loc, vband[:, wide], preferred_element_type=F32)
                  + jnp.dot(p_ctx, vcx[:, wide], preferred_element_type=F32))
            o = oe[:, :LANE] / (oe[:, LANE:] + jnp.exp(snk - m))
            for k2 in range(grp // 2):
                even = o[(2 * k2) * A_BLOCK:(2 * k2 + 1) * A_BLOCK]
                odd = o[(2 * k2 + 1) * A_BLOCK:(2 * k2 + 2) * A_BLOCK]
                pairs.append(jnp.where(low_half, even, odd))
        y = jnp.concatenate(pairs, axis=-1) * _silu(g_ref[qrows, :])
        y_ref[qrows, :] = y.astype(y_ref.dtype)


def _win_mask(grp, nb):
    assert nb >= 2
    r = np.arange(A_BLOCK)[:, None]
    c = np.arange(3 * A_BLOCK)[None, :]
    band = np.abs(r + A_BLOCK - c) <= A_WINDOW
    variants = [band & (c >= A_BLOCK), band, band & (c < 2 * A_BLOCK)]
    return np.stack([np.tile(np.where(v, 0.0, NEG_INF).astype(np.float32), (grp, 1)) for v in variants])


def _win_attn(proj, cache_k, cache_v, layer, cos, sin_signed, gq, gk, sink, bsz, t_len):
    nb = t_len // A_BLOCK
    assert nb % WIN_SUB == 0
    nsteps = nb // WIN_SUB
    kvw = A_KV_HEADS * HEAD_DIM
    grp = A_HEADS // A_KV_HEADS
    p_len = cache_k.shape[3]
    dup = np.zeros((kvw, 2 * kvw), np.float32)
    for kv in range(A_KV_HEADS):
        for half in range(2):
            for d in range(HEAD_DIM):
                dup[kv * HEAD_DIM + d, kv * LANE + half * HEAD_DIM + d] = 1.0
    const2 = lambda b, j: (0, 0)
    return pl.pallas_call(
        _win_attn_kernel,
        grid=(bsz, nsteps),
        in_specs=[
            pl.BlockSpec((WIN_SUB * A_BLOCK, BRANCH_W), lambda b, j: (b * nsteps + j, C_AQ // 4)),
            pl.BlockSpec((t_len, kvw), lambda b, j: (b, C_AK)),
            pl.BlockSpec((t_len, kvw), lambda b, j: (b, C_AV)),
            pl.BlockSpec((WIN_SUB * A_BLOCK, BRANCH_W), lambda b, j: (b * nsteps + j, C_AG // 4)),
            pl.BlockSpec((1, 1, kvw, p_len), lambda b, j: (b, layer, 0, 0)),
            pl.BlockSpec((1, 1, kvw, p_len), lambda b, j: (b, layer, 0, 0)),
            pl.BlockSpec((t_len, BRANCH_W), const2),
            pl.BlockSpec((t_len, BRANCH_W), const2),
            pl.BlockSpec((None, 1, BRANCH_W), lambda b, j: (layer, 0, 0)),
            pl.BlockSpec((None, 1, kvw), lambda b, j: (layer, 0, 0)),
            pl.BlockSpec((None, 1, A_HEADS), lambda b, j: (layer, 0, 0)),
            pl.BlockSpec((BRANCH_W, BRANCH_W), const2),
            pl.BlockSpec((kvw, kvw), const2),
            pl.BlockSpec((kvw, 2 * kvw), const2),
            pl.BlockSpec((3, grp * A_BLOCK, 3 * A_BLOCK), lambda b, j: (0, 0, 0)),
        ],
        out_specs=pl.BlockSpec((WIN_SUB * A_BLOCK, BRANCH_W), lambda b, j: (b * nsteps + j, 0)),
        out_shape=jax.ShapeDtypeStruct((bsz * t_len, BRANCH_W), BF16),
        scratch_shapes=[pltpu.VMEM((t_len + 2 * A_BLOCK, 2 * kvw), BF16),
                        pltpu.VMEM((t_len + 2 * A_BLOCK, 4 * kvw), BF16),
                        pltpu.VMEM((p_len, 2 * kvw), BF16),
                        pltpu.VMEM((p_len, 4 * kvw), BF16)],
        compiler_params=_cparams(("arbitrary", "arbitrary")),
        name="win_attn",
    )(proj, proj, proj, proj, cache_k, cache_v, cos, sin_signed,
      gq, gk, sink, _head_mean_matrix(BRANCH_W), _head_mean_matrix(kvw),
      jnp.asarray(dup, dtype=BF16), jnp.asarray(_win_mask(grp, nb)))


NBR_QB = 2 * GRID_W
NBR_BAND = 5


def _nbr_plan(rows):
    kh = min(NB_ROWS, rows)
    nblk = rows // 2
    specs, plan, starts = {}, [], []
    for i in range(nblk):
        s0 = min(max(i - 2, 0), nblk - NBR_BAND)
        starts.append(s0)
        blk = []
        for a in range(2):
            qr = 2 * i + a
            rs = min(max(qr - kh // 2, 0), rows - kh)
            assert 2 * s0 <= rs and rs + kh <= 2 * (s0 + NBR_BAND)
            row = []
            for p in range(NBR_BAND):
                pair = tuple(kr - qr + NB_ROWS - 1 if rs <= kr < rs + kh else None
                             for kr in (2 * (s0 + p), 2 * (s0 + p) + 1))
                row.append(specs.setdefault(pair, len(specs)))
            blk.append(row)
        plan.append(blk)
    return tuple(starts), plan, list(specs)


def _nbr_table(rpb, specs):
    heads = rpb.shape[0]
    c = np.arange(GRID_W)
    cs = np.clip(c - NB_COLS // 2, 0, GRID_W - NB_COLS)
    col_ok = (c[None, :] >= cs[:, None]) & (c[None, :] < cs[:, None] + NB_COLS)
    dc = c[None, :] - c[:, None] + NB_COLS - 1
    onehot = ((dc[None] == np.arange(2 * NB_COLS - 1)[:, None, None]) & col_ok[None]).astype(np.float32)
    shifted = jnp.einsum('hdj,jqk->hdqk', rpb.astype(F32), jnp.asarray(onehot), precision=lax.Precision.HIGHEST)
    by_col = jnp.where(col_ok[None, None], shifted, NEG_INF)
    neg = jnp.full((heads, GRID_W, GRID_W), NEG_INF, F32)
    blocks = [jnp.concatenate([neg if d is None else by_col[:, d] for d in spec], axis=-1) for spec in specs]
    return jnp.stack(blocks, axis=1)


NBR_GROUPS = 4


def _nbr_attn_kernel(q_ref, k_ref, v_ref, g_ref, kc_ref, vc_ref, tb_ref, gq_ref, gk_ref, bd_ref, y_ref,
                     *, starts, plan):
    nband = NBR_BAND * NBR_QB
    bd = bd_ref[...]
    hpg = LANE // HEAD_DIM
    for gs in range(NBR_GROUPS):
        lanes = slice(gs * LANE, (gs + 1) * LANE)
        qn = (_heads_rms(q_ref[:, lanes], bd, gq_ref[...]) * QK_SCALE).astype(BF16)
        kn = _heads_rms(k_ref[:, lanes], bd, gk_ref[...]).astype(BF16)
        vb = v_ref[:, lanes].astype(BF16)
        kcb = kc_ref[0, 0, lanes, :].T.astype(BF16)
        vcb = vc_ref[0, 0, lanes, :].T.astype(BF16)
        acc = [jnp.zeros((NBR_QB, LANE), F32) for _ in starts]
        for h in range(hpg):
            hm = _head_lane_mask(LANE, h)
            km = jnp.where(hm, kn, jnp.zeros_like(kn))
            kcm = jnp.where(hm, kcb, jnp.zeros_like(kcb))
            vm = jnp.concatenate([jnp.where(hm, vb, jnp.zeros_like(vb)), jnp.ones_like(vb)], axis=1)
            vcm = jnp.concatenate([jnp.where(hm, vcb, jnp.zeros_like(vcb)), jnp.ones_like(vcb)], axis=1)
            for i, s0 in enumerate(starts):
                qi = qn[i * NBR_QB:(i + 1) * NBR_QB]
                ks = slice(s0 * NBR_QB, s0 * NBR_QB + nband)
                s_raw = lax.dot_general(qi, km[ks], _NT, preferred_element_type=F32)
                s_loc = jnp.concatenate(
                    [jnp.concatenate([s_raw[a * GRID_W:(a + 1) * GRID_W, p * LANE:(p + 1) * LANE]
                                      + tb_ref[gs * hpg + h, plan[i][a][p]] for p in range(NBR_BAND)], axis=1)
                     for a in range(2)], axis=0)
                s_ctx = lax.dot_general(qi, kcm, _NT, preferred_element_type=F32)
                m = jnp.maximum(jnp.max(s_loc, axis=-1, keepdims=True), jnp.max(s_ctx, axis=-1, keepdims=True))
                p_loc = jnp.exp(s_loc - m).astype(BF16)
                p_ctx = jnp.exp(s_ctx - m).astype(BF16)
                oe = (jnp.dot(p_loc, vm[ks], preferred_element_type=F32)
                      + jnp.dot(p_ctx, vcm, preferred_element_type=F32))
                acc[i] = acc[i] + oe[:, :LANE] / oe[:, LANE:]
        y = jnp.concatenate(acc, axis=0) * _silu(g_ref[:, lanes])
        y_ref[:, lanes] = y.astype(y_ref.dtype)


def _nbr_attn(proj, cache_k, cache_v, layer, table, starts, plan, gq, gk, bsz, t_len):
    hp = NBR_GROUPS * (LANE // HEAD_DIM)
    gw = NBR_GROUPS * LANE
    nhp = B_HEADS // hp
    p_len = cache_k.shape[3]
    const = lambda h, b: (0, 0)
    kern = functools.partial(_nbr_attn_kernel, starts=starts, plan=plan)
    return pl.pallas_call(
        kern,
        grid=(nhp, bsz),
        in_specs=[
            pl.BlockSpec((t_len, gw), lambda h, b: (b, C_BQ // NBR_GROUPS + h)),
            pl.BlockSpec((t_len, gw), lambda h, b: (b, C_BK // NBR_GROUPS + h)),
            pl.BlockSpec((t_len, gw), lambda h, b: (b, C_BV // NBR_GROUPS + h)),
            pl.BlockSpec((t_len, gw), lambda h, b: (b, C_BG // NBR_GROUPS + h)),
            pl.BlockSpec((1, 1, gw, p_len), lambda h, b: (b, layer, h, 0)),
            pl.BlockSpec((1, 1, gw, p_len), lambda h, b: (b, layer, h, 0)),
            pl.BlockSpec((hp,) + table.shape[1:], lambda h, b: (layer * nhp + h, 0, 0, 0)),
            pl.BlockSpec((None, 1, LANE), lambda h, b: (layer, 0, 0)),
            pl.BlockSpec((None, 1, LANE), lambda h, b: (layer, 0, 0)),
            pl.BlockSpec((LANE, LANE), const),
        ],
        out_specs=pl.BlockSpec((t_len, gw), lambda h, b: (b, h)),
        out_shape=jax.ShapeDtypeStruct((bsz * t_len, BRANCH_W), BF16),
        compiler_params=_cparams(("arbitrary", "arbitrary")),
        name="nbr_attn",
    )(proj, proj, proj, proj, cache_k, cache_v, table, gq, gk, _head_mean_matrix(LANE))


LRU_CHUNKS = SUBLANE
LRU_PITCH_PAD = 4


LRU_STEP_ROWS = 1024


def _lru_kernel(cx_ref, cg_ref, h0_ref, cw_ref, cb_ref, wg_ref, bg_ref, lam_ref,
                y_ref, st_ref, a_s, u_s, h_s, p_s, y_s, *, t_len):
    for n in range(cx_ref.shape[0] // t_len):
        rows = pl.ds(n * t_len, t_len)
        _lru_sequence(cx_ref.at[rows, :], cg_ref.at[rows, :], h0_ref.at[pl.ds(n, 1)], cw_ref, cb_ref, wg_ref,
                      bg_ref, lam_ref, y_ref.at[rows, :], st_ref.at[pl.ds(n, 1)], a_s, u_s, h_s, p_s, y_s)


def _lru_sequence(cx_ref, cg_ref, h0_ref, cw_ref, cb_ref, wg_ref, bg_ref, lam_ref,
                  y_ref, st_ref, a_s, u_s, h_s, p_s, y_s):
    t_len = cx_ref.shape[0]
    w = LRU_WIDTH
    cx = cx_ref[...]
    row = lax.broadcasted_iota(jnp.int32, (t_len, w), 0)
    xc = cb_ref[...] + cx * cw_ref[2:3, :]
    xc = xc + jnp.where(row >= 2, pltpu.roll(cx, 2, axis=0), 0.0) * cw_ref[0:1, :]
    xc = xc + jnp.where(row >= 1, pltpu.roll(cx, 1, axis=0), 0.0) * cw_ref[1:2, :]
    xc = xc + jnp.where(row < t_len - 1, pltpu.roll(cx, t_len - 1, axis=0), 0.0) * cw_ref[3:4, :]

    gates = jnp.dot(xc.astype(BF16), wg_ref[...], preferred_element_type=F32) + bg_ref[...]
    coeffs = []
    for d in range(2):
        th_r = jnp.tanh(gates[:, d * w:(d + 1) * w])
        th_i = jnp.tanh(gates[:, (2 + d) * w:(3 + d) * w])
        nl = -lam_ref[d:d + 1, :]
        softplus = jnp.maximum(nl, 0.0) + jnp.log1p(jnp.exp(-jnp.abs(nl)))
        quarter_c = (-0.25 * LRU_C) * softplus
        half_log_a = quarter_c * th_r + quarter_c
        t = jnp.tanh(half_log_a)
        rc = 1.0 / (1.0 - t)
        coeffs.append(((1.0 + t) * rc,
                       jnp.sqrt(-t) * rc * (1.0 + th_i) * xc))

    chunk_len = t_len // LRU_CHUNKS
    pitch = chunk_len + LRU_PITCH_PAD
    nt = w // LANE
    for d, (a_val, u_val) in enumerate(coeffs):
        for c in range(LRU_CHUNKS):
            for k in range(nt):
                dst = slice(c * pitch, c * pitch + chunk_len)
                src = (slice(c * chunk_len, (c + 1) * chunk_len), slice(k * LANE, (k + 1) * LANE))
                a_s[d, k, dst, :] = a_val[src]
                u_s[d, k, dst, :] = u_val[src]

    def scan_body(s, carry):
        hs, ps = carry
        new_h, new_p = [], []
        for d in range(2):
            pos = s if d == 0 else chunk_len - 1 - s
            idx = pl.ds(pos, LRU_CHUNKS, stride=pitch)
            for k in range(nt):
                av = a_s[d, k, idx, :]
                h = av * hs[d * nt + k] + u_s[d, k, idx, :]
                p = av * ps[d * nt + k]
                h_s[d, k, idx, :] = h
                p_s[d, k, idx, :] = p
                new_h.append(h)
                new_p.append(p)
        return tuple(new_h), tuple(new_p)

    zero = jnp.zeros((LRU_CHUNKS, LANE), F32)
    one = jnp.ones((LRU_CHUNKS, LANE), F32)
    h_end, p_end = lax.fori_loop(0, chunk_len, scan_body, ((zero,) * (2 * nt), (one,) * (2 * nt)), unroll=4)

    h0 = h0_ref[0]
    enter, finals = [], []
    for d in range(2):
        order = range(LRU_CHUNKS) if d == 0 else range(LRU_CHUNKS - 1, -1, -1)
        for k in range(nt):
            he, pe = h_end[d * nt + k], p_end[d * nt + k]
            e = h0[d:d + 1, k * LANE:(k + 1) * LANE]
            rows = [None] * LRU_CHUNKS
            for c in order:
                rows[c] = e
                e = pe[c:c + 1, :] * e + he[c:c + 1, :]
            enter.append(jnp.concatenate(rows, axis=0))
            finals.append(e)
    st_ref[0] = jnp.concatenate([jnp.concatenate(finals[d * nt:(d + 1) * nt], axis=1) for d in range(2)], axis=0)

    def fix_body(s, carry):
        idx = pl.ds(s, LRU_CHUNKS, stride=pitch)
        for k in range(nt):
            hf = h_s[0, k, idx, :] + p_s[0, k, idx, :] * enter[k]
            hb = h_s[1, k, idx, :] + p_s[1, k, idx, :] * enter[nt + k]
            y_s[k, idx, :] = hf + hb
        return carry

    lax.fori_loop(0, chunk_len, fix_body, 0, unroll=4)
    hsum = jnp.concatenate(
        [jnp.concatenate([y_s[k, c * pitch:c * pitch + chunk_len, :] for c in range(LRU_CHUNKS)], axis=0)
         for k in range(nt)], axis=1)
    y_ref[...] = (hsum * _silu(cg_ref[...])).astype(y_ref.dtype)


def _lru(proj, h0, h0_layer, conv_w, conv_b, w_gates, b_gates, lam, layer, bsz, t_len):
    w = LRU_WIDTH
    assert t_len % (LRU_CHUNKS * SUBLANE) == 0
    rows_pad = LRU_CHUNKS * (t_len // LRU_CHUNKS + LRU_PITCH_PAD)
    nseq = max(1, LRU_STEP_ROWS // t_len)
    assert bsz % nseq == 0
    step_rows = nseq * t_len
    return pl.pallas_call(
        functools.partial(_lru_kernel, t_len=t_len),
        grid=(bsz // nseq,),
        in_specs=[
            pl.BlockSpec((step_rows, w), lambda b: (b, C_CX // 4)),
            pl.BlockSpec((step_rows, w), lambda b: (b, C_CG // 4)),
            pl.BlockSpec((nseq, None, 2, w), lambda b: (b, h0_layer, 0, 0)),
            pl.BlockSpec((None, CONV_W, w), lambda b: (layer, 0, 0)),
            pl.BlockSpec((None, 1, w), lambda b: (layer, 0, 0)),
            pl.BlockSpec((None, w, 4 * w), lambda b: (layer, 0, 0)),
            pl.BlockSpec((None, 1, 4 * w), lambda b: (layer, 0, 0)),
            pl.BlockSpec((None, 2, w), lambda b: (layer, 0, 0)),
        ],
        out_specs=[
            pl.BlockSpec((step_rows, w), lambda b: (b, 0)),
            pl.BlockSpec((nseq, 2, w), lambda b: (b, 0, 0)),
        ],
        out_shape=[
            jax.ShapeDtypeStruct((bsz * t_len, w), BF16),
            jax.ShapeDtypeStruct((bsz, 2, w), F32),
        ],
        scratch_shapes=[pltpu.VMEM((2, w // LANE, rows_pad, LANE), F32) for _ in range(4)]
        + [pltpu.VMEM((w // LANE, rows_pad, LANE), F32)],
        compiler_params=_cparams(("arbitrary",)),
        name="lru",
    )(proj, proj, h0, conv_w, conv_b, w_gates, b_gates, lam)


def _lru_gate_weights(wa, ba, wx, bx):
    def dense(wblk):
        eye = jnp.eye(LRU_BLOCKS, dtype=wblk.dtype)
        full = jnp.einsum('ldnkj,nm->lnkdmj', wblk, eye)
        return full.reshape(DEPTH, LRU_WIDTH, 2 * LRU_WIDTH)
    wg = jnp.concatenate([dense(wa), dense(wx)], axis=2)
    bg = jnp.concatenate([ba.reshape(DEPTH, 1, 2 * LRU_WIDTH), bx.reshape(DEPTH, 1, 2 * LRU_WIDTH)], axis=2)
    return (0.5 * wg).astype(BF16), 0.5 * bg


def _merge_kernel(x_ref, ya_ref, yb_ref, yc_ref, mod_ref, g_ref, wgate_ref, wbr_ref, wout_ref, o_ref):
    x = x_ref[...]
    h = _modulated_norm(x, g_ref, mod_ref)
    gates = jnp.dot(h, wgate_ref[0], preferred_element_type=F32)
    z = None
    for k, y_ref in enumerate((ya_ref, yb_ref, yc_ref)):
        term = (_sigmoid(gates[:, k * D_MODEL:(k + 1) * D_MODEL])
                * jnp.dot(y_ref[...], wbr_ref[0, k], preferred_element_type=F32))
        z = term if z is None else z + term
    out = jnp.dot(z.astype(BF16), wout_ref[0], preferred_element_type=F32)
    o_ref[...] = x + mod_ref[0, 0, 2:3, :] * out


def _merge(x2d, ya, yb, yc, mods, norm_g, w_gate, w_br, w_out, layer, rows_per_mod, mod_row0):
    tokens = x2d.shape[0]
    tm = 1024
    assert rows_per_mod % tm == 0
    tiles_per_mod = rows_per_mod // tm if rows_per_mod else 0

    def mod_idx(i):
        if rows_per_mod:
            return (layer, mod_row0 + i // tiles_per_mod, 0, 0)
        return (layer, mod_row0, 0, 0)

    return pl.pallas_call(
        _merge_kernel,
        grid=(tokens // tm,),
        in_specs=[
            pl.BlockSpec((tm, D_MODEL), lambda i: (i, 0)),
            pl.BlockSpec((tm, BRANCH_W), lambda i: (i, 0)),
            pl.BlockSpec((tm, BRANCH_W), lambda i: (i, 0)),
            pl.BlockSpec((tm, BRANCH_W), lambda i: (i, 0)),
            pl.BlockSpec((1, 1, 3, D_MODEL), mod_idx),
            pl.BlockSpec((1, 1, D_MODEL), lambda i: (layer, 0, 0)),
            pl.BlockSpec((1, D_MODEL, GATE_COLS), lambda i: (layer, 0, 0)),
            pl.BlockSpec((1, 3, BRANCH_W, D_MODEL), lambda i: (layer, 0, 0, 0)),
            pl.BlockSpec((1, D_MODEL, D_MODEL), lambda i: (layer, 0, 0)),
        ],
        out_specs=pl.BlockSpec((tm, D_MODEL), lambda i: (i, 0)),
        out_shape=jax.ShapeDtypeStruct((tokens, D_MODEL), F32),
        compiler_params=_cparams(("arbitrary",)),
        name="merge",
    )(x2d, ya, yb, yc, mods, norm_g.reshape(DEPTH, 1, D_MODEL), w_gate, w_br, w_out)


def _rope_tables(t_len):
    t = jnp.arange(t_len)
    m = HEAD_DIM // 4
    freqs = ROPE_BASE ** (-jnp.arange(m, dtype=F32) / m)
    ang_r = (t // GRID_W).astype(F32)[:, None] * freqs[None, :]
    ang_c = (t % GRID_W).astype(F32)[:, None] * freqs[None, :]
    cos = jnp.concatenate([jnp.cos(ang_r), jnp.cos(ang_r), jnp.cos(ang_c), jnp.cos(ang_c)], axis=-1)
    sin = jnp.concatenate([-jnp.sin(ang_r), jnp.sin(ang_r), -jnp.sin(ang_c), jnp.sin(ang_c)], axis=-1)
    reps = BRANCH_W // HEAD_DIM
    return jnp.tile(cos, (1, reps)), jnp.tile(sin, (1, reps))


def kernel(x_prompt, x_sample, cache_ka, cache_va, cache_kb, cache_vb, state_lru, c, c_ctx,
           norm_g, w_ada, b_ada, w_in, a_q_norm, a_k_norm, a_sink, b_q_norm, b_k_norm, b_rpb,
           lru_conv_w, lru_conv_b, lru_wa, lru_ba, lru_wx, lru_bx, lru_lambda, w_branch, w_out):
    bsz, s_len, _ = x_prompt.shape
    dbsz, t_len, _ = x_sample.shape
    p_len = cache_ka.shape[2]

    n_mod = 16
    ctx_row = dbsz
    cond = jnp.zeros((n_mod, D_MODEL), F32).at[:dbsz].set(c).at[ctx_row].set(c_ctx)
    mods = _mods(cond, w_ada, b_ada).reshape(DEPTH, n_mod, 3, D_MODEL)

    w_perm = w_gate = _wprep(w_in)
    w_br = w_branch.astype(BF16)
    w_o = w_out.astype(BF16)
    cos, sin_signed = _rope_tables(t_len)

    def cache_t(x):
        return x.transpose(0, 1, 3, 4, 2).reshape(x.shape[0], DEPTH, x.shape[3] * HEAD_DIM, p_len)
    cka, cva, ckb, cvb = cache_t(cache_ka), cache_t(cache_va), cache_t(cache_kb), cache_t(cache_vb)

    def gain_rows(g, reps):
        return jnp.tile(g, (1, reps)).reshape(DEPTH, 1, reps * HEAD_DIM)
    a_gq, a_gk = gain_rows(a_q_norm, A_HEADS), gain_rows(a_k_norm, A_KV_HEADS)
    b_gq, b_gk = gain_rows(b_q_norm, B_HEADS), gain_rows(b_k_norm, B_HEADS)
    b_gq2, b_gk2 = gain_rows(b_q_norm, LANE // HEAD_DIM), gain_rows(b_k_norm, LANE // HEAD_DIM)
    sinks = a_sink.reshape(DEPTH, 1, A_HEADS)
    wg, bg = _lru_gate_weights(lru_wa, lru_ba, lru_wx, lru_bx)
    conv_b = lru_conv_b.reshape(DEPTH, 1, LRU_WIDTH)
    grid_rows = t_len // GRID_W
    assert grid_rows % 2 == 0 and grid_rows // 2 >= NBR_BAND
    nbr_starts, nbr_plan, nbr_specs = _nbr_plan(grid_rows)
    nbr_table = _nbr_table(b_rpb.reshape((DEPTH * B_HEADS,) + b_rpb.shape[2:]), nbr_specs)

    yp = x_prompt.reshape(bsz * s_len, D_MODEL)
    ys = x_sample.reshape(dbsz * t_len, D_MODEL)
    zero_state = jnp.zeros((bsz, 1, 2, LRU_WIDTH), F32)
    new_caches, new_lru = None, []
    for l in range(DEPTH):
        proj = _inproj(yp, mods, norm_g, w_perm, l, 0, ctx_row)
        ya, yb, new_caches = _ctx_attn(proj, bsz, s_len, a_gq, a_gk, sinks, b_gq, b_gk, l, new_caches)
        yc, st = _lru(proj, zero_state, 0, lru_conv_w, conv_b, wg, bg, lru_lambda, l, bsz, s_len)
        yp = _merge(yp, ya, yb, yc, mods, norm_g, w_gate, w_br, w_o, l, 0, ctx_row)
        new_lru.append(st)

        proj = _inproj(ys, mods, norm_g, w_perm, l, t_len, 0)
        ya = _win_attn(proj, cka, cva, l, cos, sin_signed, a_gq, a_gk, sinks, dbsz, t_len)
        yb = _nbr_attn(proj, ckb, cvb, l, nbr_table, nbr_starts, nbr_plan, b_gq2, b_gk2, dbsz, t_len)
        yc, _ = _lru(proj, state_lru, l, lru_conv_w, conv_b, wg, bg, lru_lambda, l, dbsz, t_len)
        ys = _merge(ys, ya, yb, yc, mods, norm_g, w_gate, w_br, w_o, l, t_len, 0)

    def cache_out(x):
        return x.reshape(bsz, DEPTH, x.shape[2] // HEAD_DIM, HEAD_DIM, s_len).transpose(0, 1, 4, 2, 3)

    return (yp.reshape(bsz, s_len, D_MODEL), ys.reshape(dbsz, t_len, D_MODEL),
            *(cache_out(x) for x in new_caches), jnp.stack(new_lru, axis=1))
```
